```python
import jax, jax.numpy as jnp
from jax import lax
import numpy as np

D_MODEL = 1024
BATCH = 8
SEQ = 16384
DEPTH = 4

HEAD_DIM = 64
A_Q_HEADS = 4
A_KV_HEADS = 2
A_WINDOW = 128
B_HEADS = 6
B_BRANCHES = ((128, 1), (512, 4), (2048, 16))
C_WIDTH = 384
C_BLOCKS = 6
C_CONV = 4
C_EXP = 8.0
D_FF = 2816
BLOCK = 128
ROPE_THETA = 10000.0
EPS = 1e-6
SCALE = HEAD_DIM ** -0.5

A_WIDTH = A_Q_HEADS * HEAD_DIM
A_KV_WIDTH = A_KV_HEADS * HEAD_DIM
B_WIDTH = B_HEADS * HEAD_DIM
MIX_WIDTH = A_WIDTH + B_WIDTH + C_WIDTH
IN_SPLIT_SIZES = (A_WIDTH, A_KV_WIDTH, A_KV_WIDTH, B_WIDTH, B_WIDTH, B_WIDTH, C_WIDTH, C_WIDTH)
IN_COLS = A_WIDTH + 2 * A_KV_WIDTH + 3 * B_WIDTH + 2 * C_WIDTH

kernel_name = "hymba_style_swa_dilated_rglru_macaron"


def rms_norm(x, g):
    xf = x.astype(jnp.float32)
    y = xf * lax.rsqrt(jnp.mean(xf * xf, axis=-1, keepdims=True) + EPS)
    return (y * g.astype(jnp.float32)).astype(x.dtype)


def swiglu(x, w_gate, w_up, w_down):
    return (jax.nn.silu(x @ w_gate) * (x @ w_up)) @ w_down


def rope_tables(positions):
    inv = 1.0 / (ROPE_THETA ** (jnp.arange(0, HEAD_DIM, 2, dtype=jnp.float32) / HEAD_DIM))
    ang = positions.astype(jnp.float32)[..., None] * inv
    return jnp.cos(ang), jnp.sin(ang)


def apply_rope(x, cos, sin):
    x1, x2 = jnp.split(x.astype(jnp.float32), 2, axis=-1)
    c = cos[:, :, None, :]
    s = sin[:, :, None, :]
    return jnp.concatenate([x1 * c - x2 * s, x2 * c + x1 * s], axis=-1).astype(x.dtype)


def banded_attention(q, k, v, max_dist):
    n, g, L, hd = q.shape
    n_prev = -(-max_dist // BLOCK)
    nb = -(-L // BLOCK)
    Lp = nb * BLOCK
    qb = jnp.pad(q, ((0, 0), (0, 0), (0, Lp - L), (0, 0))).reshape(n, g, nb, BLOCK, hd)
    pad = ((0, 0), (n_prev * BLOCK, Lp - L), (0, 0))
    kp = jnp.pad(k, pad)
    vp = jnp.pad(v, pad)
    kb = jnp.concatenate([kp[:, j * BLOCK:j * BLOCK + Lp].reshape(n, nb, BLOCK, hd) for j in range(n_prev + 1)], axis=2)
    vb = jnp.concatenate([vp[:, j * BLOCK:j * BLOCK + Lp].reshape(n, nb, BLOCK, hd) for j in range(n_prev + 1)], axis=2)
    q_pos = jnp.arange(Lp).reshape(nb, BLOCK, 1)
    k_pos = (jnp.arange(nb)[:, None] * BLOCK + jnp.arange((n_prev + 1) * BLOCK)[None, :] - n_prev * BLOCK)[:, None, :]
    dist = q_pos - k_pos
    mask = (dist >= 0) & (dist <= max_dist) & (k_pos >= 0)
    s = jnp.einsum('ngbqd,nbkd->ngbqk', qb, kb).astype(jnp.float32) * SCALE
    s = jnp.where(mask, s, -jnp.inf)
    m = jnp.max(s, axis=-1, keepdims=True)
    p = jnp.exp(s - m)
    l = jnp.sum(p, axis=-1, keepdims=True)
    o = jnp.einsum('ngbqk,nbkd->ngbqd', p.astype(v.dtype), vb).astype(jnp.float32) / l
    lse = (m + jnp.log(l))[..., 0]
    o = o.reshape(n, g, Lp, hd)[:, :, :L].astype(v.dtype)
    lse = lse.reshape(n, g, Lp)[:, :, :L]
    return o, lse


def swa_sink_mixer(q, k, v, sinks):
    b, s = q.shape[:2]
    g = A_Q_HEADS // A_KV_HEADS
    qh = q.reshape(b, s, A_KV_HEADS, g, HEAD_DIM).transpose(0, 2, 3, 1, 4).reshape(b * A_KV_HEADS, g, s, HEAD_DIM)
    kh = k.transpose(0, 2, 1, 3).reshape(b * A_KV_HEADS, s, HEAD_DIM)
    vh = v.transpose(0, 2, 1, 3).reshape(b * A_KV_HEADS, s, HEAD_DIM)
    o, lse = banded_attention(qh, kh, vh, A_WINDOW - 1)
    sink = jnp.tile(sinks.astype(jnp.float32).reshape(A_KV_HEADS, g), (b, 1))[:, :, None]
    o = o * jax.nn.sigmoid(lse - sink)[..., None].astype(o.dtype)
    return o.reshape(b, A_KV_HEADS, g, s, HEAD_DIM).transpose(0, 3, 1, 2, 4).reshape(b, s, A_WIDTH)


def dilated_mixer(q, k, v):
    b, s, h, hd = q.shape
    outs, lses = [], []
    for window, d in B_BRANCHES:
        def gather(t):
            return t.reshape(b, s // d, d, h, hd).transpose(0, 3, 2, 1, 4).reshape(b * h * d, s // d, hd)
        o, lse = banded_attention(gather(q)[:, None], gather(k), gather(v), window // d)
        outs.append(o[:, 0].reshape(b, h, d, s // d, hd).transpose(0, 3, 2, 1, 4).reshape(b, s, h, hd))
        lses.append(lse[:, 0].reshape(b, h, d, s // d).transpose(0, 3, 2, 1).reshape(b, s, h))
    w = jax.nn.softmax(jnp.stack(lses, axis=-1), axis=-1)
    o = outs[0] * w[..., 0:1].astype(q.dtype)
    for i in range(1, len(B_BRANCHES)):
        o = o + outs[i] * w[..., i:i + 1].astype(q.dtype)
    return o.reshape(b, s, B_WIDTH)


def _lru_combine(left, right):
    a1, b1 = left
    a2, b2 = right
    return a1 * a2, a2 * b1 + b2


def rglru_mixer(xc, gate, conv_w, conv_b, w_r, b_r, w_i, b_i, lam, positions):
    b, s, c = xc.shape
    xp = jnp.pad(xc, ((0, 0), (C_CONV - 1, 0), (0, 0)))
    y = conv_b + conv_w[0] * xp[:, C_CONV - 1:C_CONV - 1 + s]
    for j in range(1, C_CONV):
        y = y + conv_w[j] * xp[:, C_CONV - 1 - j:C_CONV - 1 - j + s]
    yb = y.reshape(b, s, C_BLOCKS, c // C_BLOCKS)
    r = jax.nn.sigmoid(jnp.einsum('bshi,hij->bshj', yb, w_r) + b_r).reshape(b, s, c)
    ig = jax.nn.sigmoid(jnp.einsum('bshi,hij->bshj', yb, w_i) + b_i).reshape(b, s, c)
    log_a = -C_EXP * r.astype(jnp.float32) * jax.nn.softplus(-lam.astype(jnp.float32))
    reset = (positions == 0)[..., None]
    a = jnp.where(reset, 0.0, jnp.exp(log_a))
    mult = jnp.where(reset, 1.0, jnp.sqrt(-jnp.expm1(2.0 * log_a)))
    bx = mult * (ig * y).astype(jnp.float32)
    _, hs = lax.associative_scan(_lru_combine, (a, bx), axis=1)
    return (hs * jax.nn.gelu(gate.astype(jnp.float32))).astype(xc.dtype)


def _split_cols(proj):
    out, start = [], 0
    for size in IN_SPLIT_SIZES:
        out.append(proj[..., start:start + size])
        start += size
    return out


def _fwd_setup_inputs(seed: int = 0) -> dict:
    key = jax.random.key(seed)
    ks = jax.random.split(key, 24)
    f32 = jnp.float32
    L = DEPTH
    nrm = lambda k, shape, scale: jax.random.normal(k, shape, f32) * scale
    a0 = jax.random.uniform(ks[14], (L, C_WIDTH), f32, minval=0.9, maxval=0.999)
    return {
        "x": jax.random.normal(ks[0], (BATCH, SEQ, D_MODEL), f32),
        "positions": jnp.broadcast_to(jnp.arange(SEQ, dtype=jnp.int32), (BATCH, SEQ)),
        "norm_ffn1": 1.0 + nrm(ks[1], (L, D_MODEL), 0.02),
        "ffn1_gate": nrm(ks[2], (L, D_MODEL, D_FF), D_MODEL ** -0.5),
        "ffn1_up": nrm(ks[3], (L, D_MODEL, D_FF), D_MODEL ** -0.5),
        "ffn1_down": nrm(ks[4], (L, D_FF, D_MODEL), D_FF ** -0.5),
        "norm_mix": 1.0 + nrm(ks[5], (L, D_MODEL), 0.02),
        "w_in": nrm(ks[6], (L, D_MODEL, IN_COLS), D_MODEL ** -0.5),
        "attn_sinks": nrm(ks[7], (L, A_Q_HEADS), 1.0),
        "conv_w": nrm(ks[8], (L, C_CONV, C_WIDTH), C_CONV ** -0.5),
        "conv_b": nrm(ks[9], (L, C_WIDTH), 0.01),
        "rg_w_r": nrm(ks[10], (L, C_BLOCKS, C_WIDTH // C_BLOCKS, C_WIDTH // C_BLOCKS), (C_WIDTH // C_BLOCKS) ** -0.5),
        "rg_b_r": nrm(ks[11], (L, C_BLOCKS, C_WIDTH // C_BLOCKS), 0.01),
        "rg_w_i": nrm(ks[12], (L, C_BLOCKS, C_WIDTH // C_BLOCKS, C_WIDTH // C_BLOCKS), (C_WIDTH // C_BLOCKS) ** -0.5),
        "rg_b_i": nrm(ks[13], (L, C_BLOCKS, C_WIDTH // C_BLOCKS), 0.01),
        "rg_lambda": jnp.log(a0) - jnp.log1p(-a0),
        "w_out": nrm(ks[15], (L, MIX_WIDTH, D_MODEL), MIX_WIDTH ** -0.5),
        "norm_ffn2": 1.0 + nrm(ks[16], (L, D_MODEL), 0.02),
        "ffn2_gate": nrm(ks[17], (L, D_MODEL, D_FF), D_MODEL ** -0.5),
        "ffn2_up": nrm(ks[18], (L, D_MODEL, D_FF), D_MODEL ** -0.5),
        "ffn2_down": nrm(ks[19], (L, D_FF, D_MODEL), D_FF ** -0.5),
        "norm_final": 1.0 + nrm(ks[20], (D_MODEL,), 0.02),
    }


def _fwd_reference(x, positions, norm_ffn1, ffn1_gate, ffn1_up, ffn1_down, norm_mix, w_in, attn_sinks,
              conv_w, conv_b, rg_w_r, rg_b_r, rg_w_i, rg_b_i, rg_lambda, w_out,
              norm_ffn2, ffn2_gate, ffn2_up, ffn2_down, norm_final):
    b, s, _ = x.shape
    cos, sin = rope_tables(positions)
    for l in range(DEPTH):
        x = x + 0.5 * swiglu(rms_norm(x, norm_ffn1[l]), ffn1_gate[l], ffn1_up[l], ffn1_down[l])
        h = rms_norm(x, norm_mix[l])
        qa, ka, va, qb, kb, vb, xc, gc = _split_cols(h @ w_in[l])
        qa = apply_rope(qa.reshape(b, s, A_Q_HEADS, HEAD_DIM), cos, sin)
        ka = apply_rope(ka.reshape(b, s, A_KV_HEADS, HEAD_DIM), cos, sin)
        va = va.reshape(b, s, A_KV_HEADS, HEAD_DIM)
        out_a = swa_sink_mixer(qa, ka, va, attn_sinks[l])
        qb = apply_rope(qb.reshape(b, s, B_HEADS, HEAD_DIM), cos, sin)
        kb = apply_rope(kb.reshape(b, s, B_HEADS, HEAD_DIM), cos, sin)
        vb = vb.reshape(b, s, B_HEADS, HEAD_DIM)
        out_b = dilated_mixer(qb, kb, vb)
        out_c = rglru_mixer(xc, gc, conv_w[l], conv_b[l], rg_w_r[l], rg_b_r[l], rg_w_i[l], rg_b_i[l],
                            rg_lambda[l], positions)
        x = x + jnp.concatenate([out_a, out_b, out_c], axis=-1) @ w_out[l]
        x = x + 0.5 * swiglu(rms_norm(x, norm_ffn2[l]), ffn2_gate[l], ffn2_up[l], ffn2_down[l])
    return rms_norm(x, norm_final)


import jax as _jax
import jax.numpy as _jnp

TWIN_FORMAT = 'train_step'
FWD_PARAMS = ['x', 'positions', 'norm_ffn1', 'ffn1_gate', 'ffn1_up', 'ffn1_down', 'norm_mix', 'w_in', 'attn_sinks', 'conv_w', 'conv_b', 'rg_w_r', 'rg_b_r', 'rg_w_i', 'rg_b_i', 'rg_lambda', 'w_out', 'norm_ffn2', 'ffn2_gate', 'ffn2_up', 'ffn2_down', 'norm_final']
TWIN_WEIGHTS = ['norm_ffn1', 'ffn1_gate', 'ffn1_up', 'ffn1_down', 'norm_mix', 'w_in', 'attn_sinks', 'conv_w', 'conv_b', 'rg_w_r', 'rg_b_r', 'rg_w_i', 'rg_b_i', 'rg_lambda', 'w_out', 'norm_ffn2', 'ffn2_gate', 'ffn2_up', 'ffn2_down', 'norm_final']
TWIN_DIFF_INPUT = 'x'
TWIN_INPUTS = ['x', 'positions', 'norm_ffn1', 'ffn1_gate', 'ffn1_up', 'ffn1_down', 'norm_mix', 'w_in', 'attn_sinks', 'conv_w', 'conv_b', 'rg_w_r', 'rg_b_r', 'rg_w_i', 'rg_b_i', 'rg_lambda', 'w_out', 'norm_ffn2', 'ffn2_gate', 'ffn2_up', 'ffn2_down', 'norm_final', 'loss_target', 'm_norm_ffn1', 'm_ffn1_gate', 'm_ffn1_up', 'm_ffn1_down', 'm_norm_mix', 'm_w_in', 'm_attn_sinks', 'm_conv_w', 'm_conv_b', 'm_rg_w_r', 'm_rg_b_r', 'm_rg_w_i', 'm_rg_b_i', 'm_rg_lambda', 'm_w_out', 'm_norm_ffn2', 'm_ffn2_gate', 'm_ffn2_up', 'm_ffn2_down', 'm_norm_final', 'v_norm_ffn1', 'v_ffn1_gate', 'v_ffn1_up', 'v_ffn1_down', 'v_norm_mix', 'v_w_in', 'v_attn_sinks', 'v_conv_w', 'v_conv_b', 'v_rg_w_r', 'v_rg_b_r', 'v_rg_w_i', 'v_rg_b_i', 'v_rg_lambda', 'v_w_out', 'v_norm_ffn2', 'v_ffn2_gate', 'v_ffn2_up', 'v_ffn2_down', 'v_norm_final']
TWIN_OUTPUTS = ['loss', 'grad_x', 'grad_norm_ffn1', 'grad_ffn1_gate', 'grad_ffn1_up', 'grad_ffn1_down', 'grad_norm_mix', 'grad_w_in', 'grad_attn_sinks', 'grad_conv_w', 'grad_conv_b', 'grad_rg_w_r', 'grad_rg_b_r', 'grad_rg_w_i', 'grad_rg_b_i', 'grad_rg_lambda', 'grad_w_out', 'grad_norm_ffn2', 'grad_ffn2_gate', 'grad_ffn2_up', 'grad_ffn2_down', 'grad_norm_final', 'delta_norm_ffn1', 'delta_ffn1_gate', 'delta_ffn1_up', 'delta_ffn1_down', 'delta_norm_mix', 'delta_w_in', 'delta_attn_sinks', 'delta_conv_w', 'delta_conv_b', 'delta_rg_w_r', 'delta_rg_b_r', 'delta_rg_w_i', 'delta_rg_b_i', 'delta_rg_lambda', 'delta_w_out', 'delta_norm_ffn2', 'delta_ffn2_gate', 'delta_ffn2_up', 'delta_ffn2_down', 'delta_norm_final', 'new_m_norm_ffn1', 'new_m_ffn1_gate', 'new_m_ffn1_up', 'new_m_ffn1_down', 'new_m_norm_mix', 'new_m_w_in', 'new_m_attn_sinks', 'new_m_conv_w', 'new_m_conv_b', 'new_m_rg_w_r', 'new_m_rg_b_r', 'new_m_rg_w_i', 'new_m_rg_b_i', 'new_m_rg_lambda', 'new_m_w_out', 'new_m_norm_ffn2', 'new_m_ffn2_gate', 'new_m_ffn2_up', 'new_m_ffn2_down', 'new_m_norm_final', 'new_v_norm_ffn1', 'new_v_ffn1_gate', 'new_v_ffn1_up', 'new_v_ffn1_down', 'new_v_norm_mix', 'new_v_w_in', 'new_v_attn_sinks', 'new_v_conv_w', 'new_v_conv_b', 'new_v_rg_w_r', 'new_v_rg_b_r', 'new_v_rg_w_i', 'new_v_rg_b_i', 'new_v_rg_lambda', 'new_v_w_out', 'new_v_norm_ffn2', 'new_v_ffn2_gate', 'new_v_ffn2_up', 'new_v_ffn2_down', 'new_v_norm_final']
TWIN_LEAF_KINDS = {'loss': 'loss', 'grad_x': 'grad_x', 'grad_norm_ffn1': 'grad_w', 'grad_ffn1_gate': 'grad_w', 'grad_ffn1_up': 'grad_w', 'grad_ffn1_down': 'grad_w', 'grad_norm_mix': 'grad_w', 'grad_w_in': 'grad_w', 'grad_attn_sinks': 'grad_w', 'grad_conv_w': 'grad_w', 'grad_conv_b': 'grad_w', 'grad_rg_w_r': 'grad_w', 'grad_rg_b_r': 'grad_w', 'grad_rg_w_i': 'grad_w', 'grad_rg_b_i': 'grad_w', 'grad_rg_lambda': 'grad_w', 'grad_w_out': 'grad_w', 'grad_norm_ffn2': 'grad_w', 'grad_ffn2_gate': 'grad_w', 'grad_ffn2_up': 'grad_w', 'grad_ffn2_down': 'grad_w', 'grad_norm_final': 'grad_w', 'delta_norm_ffn1': 'delta_w', 'delta_ffn1_gate': 'delta_w', 'delta_ffn1_up': 'delta_w', 'delta_ffn1_down': 'delta_w', 'delta_norm_mix': 'delta_w', 'delta_w_in': 'delta_w', 'delta_attn_sinks': 'delta_w', 'delta_conv_w': 'delta_w', 'delta_conv_b': 'delta_w', 'delta_rg_w_r': 'delta_w', 'delta_rg_b_r': 'delta_w', 'delta_rg_w_i': 'delta_w', 'delta_rg_b_i': 'delta_w', 'delta_rg_lambda': 'delta_w', 'delta_w_out': 'delta_w', 'delta_norm_ffn2': 'delta_w', 'delta_ffn2_gate': 'delta_w', 'delta_ffn2_up': 'delta_w', 'delta_ffn2_down': 'delta_w', 'delta_norm_final': 'delta_w', 'new_m_norm_ffn1': 'new_m', 'new_m_ffn1_gate': 'new_m', 'new_m_ffn1_up': 'new_m', 'new_m_ffn1_down': 'new_m', 'new_m_norm_mix': 'new_m', 'new_m_w_in': 'new_m', 'new_m_attn_sinks': 'new_m', 'new_m_conv_w': 'new_m', 'new_m_conv_b': 'new_m', 'new_m_rg_w_r': 'new_m', 'new_m_rg_b_r': 'new_m', 'new_m_rg_w_i': 'new_m', 'new_m_rg_b_i': 'new_m', 'new_m_rg_lambda': 'new_m', 'new_m_w_out': 'new_m', 'new_m_norm_ffn2': 'new_m', 'new_m_ffn2_gate': 'new_m', 'new_m_ffn2_up': 'new_m', 'new_m_ffn2_down': 'new_m', 'new_m_norm_final': 'new_m', 'new_v_norm_ffn1': 'new_v', 'new_v_ffn1_gate': 'new_v', 'new_v_ffn1_up': 'new_v', 'new_v_ffn1_down': 'new_v', 'new_v_norm_mix': 'new_v', 'new_v_w_in': 'new_v', 'new_v_attn_sinks': 'new_v', 'new_v_conv_w': 'new_v', 'new_v_conv_b': 'new_v', 'new_v_rg_w_r': 'new_v', 'new_v_rg_b_r': 'new_v', 'new_v_rg_w_i': 'new_v', 'new_v_rg_b_i': 'new_v', 'new_v_rg_lambda': 'new_v', 'new_v_w_out': 'new_v', 'new_v_norm_ffn2': 'new_v', 'new_v_ffn2_gate': 'new_v', 'new_v_ffn2_up': 'new_v', 'new_v_ffn2_down': 'new_v', 'new_v_norm_final': 'new_v'}


def _forward(args):
    return _fwd_reference(*[args[k] for k in FWD_PARAMS])


def _output_shape():
    def fwd():
        inp = _fwd_setup_inputs(0)
        return _fwd_reference(*[inp[k] for k in FWD_PARAMS])
    out = _jax.eval_shape(fwd)
    return out.shape, out.dtype

N_MICROBATCH = 1
ADAM_LR = 0.001
ADAM_B1 = 0.9
ADAM_B2 = 0.999
ADAM_EPS = 1e-08
ADAM_WD = 0.01
ADAM_STEP = 10
PER_EXAMPLE_BATCH_AXIS = {'x': 0, 'positions': 0, 'loss_target': 0}
SHARED_INPUTS = []
_WEIGHT_DTYPES = {'norm_ffn1': _jnp.float32, 'ffn1_gate': _jnp.float32, 'ffn1_up': _jnp.float32, 'ffn1_down': _jnp.float32, 'norm_mix': _jnp.float32, 'w_in': _jnp.float32, 'attn_sinks': _jnp.float32, 'conv_w': _jnp.float32, 'conv_b': _jnp.float32, 'rg_w_r': _jnp.float32, 'rg_b_r': _jnp.float32, 'rg_w_i': _jnp.float32, 'rg_b_i': _jnp.float32, 'rg_lambda': _jnp.float32, 'w_out': _jnp.float32, 'norm_ffn2': _jnp.float32, 'ffn2_gate': _jnp.float32, 'ffn2_up': _jnp.float32, 'ffn2_down': _jnp.float32, 'norm_final': _jnp.float32}
MOMENT_SCALE = {'norm_ffn1': 1.603208e-01, 'ffn1_gate': 6.838276e-02, 'ffn1_up': 6.618510e-02, 'ffn1_down': 1.099292e-01, 'norm_mix': 1.597351e-01, 'w_in': 1.022199e-01, 'attn_sinks': 6.061549e-02, 'conv_w': 1.526984e-01, 'conv_b': 7.997327e-01, 'rg_w_r': 3.406708e-02, 'rg_b_r': 3.088998e-02, 'rg_w_i': 5.943935e-02, 'rg_b_i': 4.980453e-02, 'rg_lambda': 6.241226e-02, 'w_out': 1.127483e-01, 'norm_ffn2': 1.415927e-01, 'ffn2_gate': 6.219328e-02, 'ffn2_up': 6.020333e-02, 'ffn2_down': 9.969881e-02, 'norm_final': 1.278784e+02}


def _to_microbatches(a, axis):
    t = _jnp.moveaxis(a, axis, 0)
    t = t.reshape((N_MICROBATCH, t.shape[0] // N_MICROBATCH) + t.shape[1:])
    return _jnp.moveaxis(t, 1, axis + 1)


def setup_inputs(seed: int = 0) -> dict:
    inp = _fwd_setup_inputs(seed)
    key = _jax.random.fold_in(_jax.random.key(seed), 7919)
    shape, _ = _output_shape()
    out = dict(inp)
    out["loss_target"] = _jax.random.normal(_jax.random.fold_in(key, 0), shape, _jnp.float32)
    for i, name in enumerate(TWIN_WEIGHTS):
        w = inp[name].astype(_jnp.float32)
        if MOMENT_SCALE is None:
            s = _jnp.sqrt(_jnp.mean(_jnp.square(w)) + 1e-30)
        else:
            s = MOMENT_SCALE[name]
        km, kv = _jax.random.split(_jax.random.fold_in(key, i + 1))
        out[name] = w
        out["m_" + name] = s * _jax.random.normal(km, w.shape, _jnp.float32)
        out["v_" + name] = (s * s) * _jax.random.uniform(kv, w.shape, _jnp.float32, 0.5, 1.5)
    if N_MICROBATCH > 1:
        for name, axis in PER_EXAMPLE_BATCH_AXIS.items():
            out[name] = _to_microbatches(out[name], axis)
    return {'x': out['x'], 'positions': out['positions'], 'norm_ffn1': out['norm_ffn1'], 'ffn1_gate': out['ffn1_gate'], 'ffn1_up': out['ffn1_up'], 'ffn1_down': out['ffn1_down'], 'norm_mix': out['norm_mix'], 'w_in': out['w_in'], 'attn_sinks': out['attn_sinks'], 'conv_w': out['conv_w'], 'conv_b': out['conv_b'], 'rg_w_r': out['rg_w_r'], 'rg_b_r': out['rg_b_r'], 'rg_w_i': out['rg_w_i'], 'rg_b_i': out['rg_b_i'], 'rg_lambda': out['rg_lambda'], 'w_out': out['w_out'], 'norm_ffn2': out['norm_ffn2'], 'ffn2_gate': out['ffn2_gate'], 'ffn2_up': out['ffn2_up'], 'ffn2_down': out['ffn2_down'], 'norm_final': out['norm_final'], 'loss_target': out['loss_target'], 'm_norm_ffn1': out['m_norm_ffn1'], 'm_ffn1_gate': out['m_ffn1_gate'], 'm_ffn1_up': out['m_ffn1_up'], 'm_ffn1_down': out['m_ffn1_down'], 'm_norm_mix': out['m_norm_mix'], 'm_w_in': out['m_w_in'], 'm_attn_sinks': out['m_attn_sinks'], 'm_conv_w': out['m_conv_w'], 'm_conv_b': out['m_conv_b'], 'm_rg_w_r': out['m_rg_w_r'], 'm_rg_b_r': out['m_rg_b_r'], 'm_rg_w_i': out['m_rg_w_i'], 'm_rg_b_i': out['m_rg_b_i'], 'm_rg_lambda': out['m_rg_lambda'], 'm_w_out': out['m_w_out'], 'm_norm_ffn2': out['m_norm_ffn2'], 'm_ffn2_gate': out['m_ffn2_gate'], 'm_ffn2_up': out['m_ffn2_up'], 'm_ffn2_down': out['m_ffn2_down'], 'm_norm_final': out['m_norm_final'], 'v_norm_ffn1': out['v_norm_ffn1'], 'v_ffn1_gate': out['v_ffn1_gate'], 'v_ffn1_up': out['v_ffn1_up'], 'v_ffn1_down': out['v_ffn1_down'], 'v_norm_mix': out['v_norm_mix'], 'v_w_in': out['v_w_in'], 'v_attn_sinks': out['v_attn_sinks'], 'v_conv_w': out['v_conv_w'], 'v_conv_b': out['v_conv_b'], 'v_rg_w_r': out['v_rg_w_r'], 'v_rg_b_r': out['v_rg_b_r'], 'v_rg_w_i': out['v_rg_w_i'], 'v_rg_b_i': out['v_rg_b_i'], 'v_rg_lambda': out['v_rg_lambda'], 'v_w_out': out['v_w_out'], 'v_norm_ffn2': out['v_norm_ffn2'], 'v_ffn2_gate': out['v_ffn2_gate'], 'v_ffn2_up': out['v_ffn2_up'], 'v_ffn2_down': out['v_ffn2_down'], 'v_norm_final': out['v_norm_final']}


def _loss(weights, diff, rest, loss_target):
    with _jax.named_scope("forward"):
        args = {**rest, TWIN_DIFF_INPUT: diff, **{k: w.astype(_WEIGHT_DTYPES[k]) for k, w in weights.items()}}
        y = _forward(args)
    with _jax.named_scope("loss_head"):
        err = _jnp.square(y.astype(_jnp.float32) - loss_target)
        return 0.5 * _jnp.sum(_jnp.mean(err, axis=-1)) if err.ndim else 0.5 * err


def _adamw(w, g, m, v):
    m = ADAM_B1 * m + (1.0 - ADAM_B1) * g
    v = ADAM_B2 * v + (1.0 - ADAM_B2) * _jnp.square(g)
    m_hat = m / (1.0 - ADAM_B1 ** ADAM_STEP)
    v_hat = v / (1.0 - ADAM_B2 ** ADAM_STEP)
    delta = -ADAM_LR * (m_hat / (_jnp.sqrt(v_hat) + ADAM_EPS) + ADAM_WD * w)
    return delta, m, v


def reference(x, positions, norm_ffn1, ffn1_gate, ffn1_up, ffn1_down, norm_mix, w_in, attn_sinks, conv_w, conv_b, rg_w_r, rg_b_r, rg_w_i, rg_b_i, rg_lambda, w_out, norm_ffn2, ffn2_gate, ffn2_up, ffn2_down, norm_final, loss_target, m_norm_ffn1, m_ffn1_gate, m_ffn1_up, m_ffn1_down, m_norm_mix, m_w_in, m_attn_sinks, m_conv_w, m_conv_b, m_rg_w_r, m_rg_b_r, m_rg_w_i, m_rg_b_i, m_rg_lambda, m_w_out, m_norm_ffn2, m_ffn2_gate, m_ffn2_up, m_ffn2_down, m_norm_final, v_norm_ffn1, v_ffn1_gate, v_ffn1_up, v_ffn1_down, v_norm_mix, v_w_in, v_attn_sinks, v_conv_w, v_conv_b, v_rg_w_r, v_rg_b_r, v_rg_w_i, v_rg_b_i, v_rg_lambda, v_w_out, v_norm_ffn2, v_ffn2_gate, v_ffn2_up, v_ffn2_down, v_norm_final):
    given = dict(x=x, positions=positions, norm_ffn1=norm_ffn1, ffn1_gate=ffn1_gate, ffn1_up=ffn1_up, ffn1_down=ffn1_down, norm_mix=norm_mix, w_in=w_in, attn_sinks=attn_sinks, conv_w=conv_w, conv_b=conv_b, rg_w_r=rg_w_r, rg_b_r=rg_b_r, rg_w_i=rg_w_i, rg_b_i=rg_b_i, rg_lambda=rg_lambda, w_out=w_out, norm_ffn2=norm_ffn2, ffn2_gate=ffn2_gate, ffn2_up=ffn2_up, ffn2_down=ffn2_down, norm_final=norm_final, loss_target=loss_target, m_norm_ffn1=m_norm_ffn1, m_ffn1_gate=m_ffn1_gate, m_ffn1_up=m_ffn1_up, m_ffn1_down=m_ffn1_down, m_norm_mix=m_norm_mix, m_w_in=m_w_in, m_attn_sinks=m_attn_sinks, m_conv_w=m_conv_w, m_conv_b=m_conv_b, m_rg_w_r=m_rg_w_r, m_rg_b_r=m_rg_b_r, m_rg_w_i=m_rg_w_i, m_rg_b_i=m_rg_b_i, m_rg_lambda=m_rg_lambda, m_w_out=m_w_out, m_norm_ffn2=m_norm_ffn2, m_ffn2_gate=m_ffn2_gate, m_ffn2_up=m_ffn2_up, m_ffn2_down=m_ffn2_down, m_norm_final=m_norm_final, v_norm_ffn1=v_norm_ffn1, v_ffn1_gate=v_ffn1_gate, v_ffn1_up=v_ffn1_up, v_ffn1_down=v_ffn1_down, v_norm_mix=v_norm_mix, v_w_in=v_w_in, v_attn_sinks=v_attn_sinks, v_conv_w=v_conv_w, v_conv_b=v_conv_b, v_rg_w_r=v_rg_w_r, v_rg_b_r=v_rg_b_r, v_rg_w_i=v_rg_w_i, v_rg_b_i=v_rg_b_i, v_rg_lambda=v_rg_lambda, v_w_out=v_w_out, v_norm_ffn2=v_norm_ffn2, v_ffn2_gate=v_ffn2_gate, v_ffn2_up=v_ffn2_up, v_ffn2_down=v_ffn2_down, v_norm_final=v_norm_final)
    weights = {n: given[n] for n in TWIN_WEIGHTS}
    shared = {n: given[n] for n in SHARED_INPUTS}
    per_example = {n: given[n] for n in ['x', 'positions']}
    grad_fn = _jax.value_and_grad(_loss, argnums=(0, 1))

    def one_microbatch(ex, loss_target):
        ex = dict(ex)
        diff = ex.pop(TWIN_DIFF_INPUT)
        return grad_fn(weights, diff, {**shared, **ex}, loss_target)

    if N_MICROBATCH == 1:
        loss, (grad_w, grad_x) = one_microbatch(per_example, given["loss_target"])
    else:
        def body(carry, xs):
            loss_sum, grad_sum = carry
            l_k, (gw_k, gx_k) = one_microbatch(xs[0], xs[1])
            with _jax.named_scope("update"):
                return (loss_sum + l_k, _jax.tree.map(_jnp.add, grad_sum, gw_k)), gx_k

        init = (_jnp.zeros((), _jnp.float32), _jax.tree.map(_jnp.zeros_like, weights))
        (loss, grad_w), grad_x = _jax.lax.scan(body, init, (per_example, given["loss_target"]))
    with _jax.named_scope("update"):
        delta_w, new_m, new_v = {}, {}, {}
        for n in TWIN_WEIGHTS:
            delta_w[n], new_m[n], new_v[n] = _adamw(weights[n], grad_w[n], given["m_" + n], given["v_" + n])
    return (loss, grad_x, *[grad_w[n] for n in TWIN_WEIGHTS], *[delta_w[n] for n in TWIN_WEIGHTS],
            *[new_m[n] for n in TWIN_WEIGHTS], *[new_v[n] for n in TWIN_WEIGHTS])
```

```python
import functools
import math

import jax
import jax.numpy as jnp
from jax import lax
from jax.experimental import pallas as pl
from jax.experimental.pallas import tpu as pltpu

F32 = jnp.float32
BF16 = jnp.bfloat16

N_DEV = 8
DEPTH = 4
HEAD_DIM = 64
LANES = 128
QBLK = 128
A_WIDTH = 256
A_KV_WIDTH = 128
B_WIDTH = 384
C_WIDTH = 384
C_BLOCKS = 6
C_CONV = 4
C_EXP = 8.0
MIX_WIDTH = A_WIDTH + B_WIDTH + C_WIDTH
IN_COLS = A_WIDTH + 2 * A_KV_WIDTH + 3 * B_WIDTH + 2 * C_WIDTH
A_MAX_DIST = 127
B_BRANCHES = ((128, 1), (512, 4), (2048, 16))
ROPE_THETA = 10000.0
EPS = 1e-6
SCALE = HEAD_DIM ** -0.5

ADAM_LR = 0.001
ADAM_B1 = 0.9
ADAM_B2 = 0.999
ADAM_EPS = 1e-08
ADAM_WD = 0.01
ADAM_STEP = 10

ATTN_CHUNK = 1024
VMEM_LIMIT = 56 * 1024 * 1024

NT_DIMS = (((1,), (1,)), ((), ()))
TN_DIMS = (((0,), (0,)), ((), ()))
NN_DIMS = (((1,), (0,)), ((), ()))

WEIGHT_NAMES = ['norm_ffn1', 'ffn1_gate', 'ffn1_up', 'ffn1_down', 'norm_mix', 'w_in', 'attn_sinks', 'conv_w',
                'conv_b', 'rg_w_r', 'rg_b_r', 'rg_w_i', 'rg_b_i', 'rg_lambda', 'w_out', 'norm_ffn2', 'ffn2_gate',
                'ffn2_up', 'ffn2_down', 'norm_final']
BIG_NAMES = ['ffn1_gate', 'ffn1_up', 'ffn1_down', 'w_in', 'w_out', 'ffn2_gate', 'ffn2_up', 'ffn2_down']
SMALL_NAMES = ['norm_ffn1', 'norm_mix', 'norm_ffn2', 'norm_final', 'attn_sinks', 'conv_b', 'rg_w_r', 'rg_b_r',
               'rg_w_i', 'rg_b_i', 'rg_lambda']


def _params(sem, vmem=VMEM_LIMIT):
    return pltpu.CompilerParams(dimension_semantics=sem, vmem_limit_bytes=vmem)


def _dot(a, b, dims=NN_DIMS):
    return lax.dot_general(a, b, dims, preferred_element_type=F32)


def _sigmoid(x):
    return 1.0 / (1.0 + jnp.exp(-x))


def _mm(a, b, mode, *, name, tm=512, tn=512, tk=512, out_dtype=F32, alpha=1.0, res=None):
    if mode == 'nn':
        (M, K), N = a.shape, b.shape[1]
    elif mode == 'nt':
        (M, K), N = a.shape, b.shape[0]
    else:
        (K, M), N = a.shape, b.shape[1]
    tm, tn, tk = min(tm, M), min(tn, N), min(tk, K)
    ni, nj, nk = M // tm, N // tn, K // tk
    assert ni * tm == M and nj * tn == N and nk * tk == K, (name, a.shape, b.shape, tm, tn, tk)
    if mode == 'tn':
        a_spec = pl.BlockSpec((tk, tm), lambda j, i, k: (k, i))
    else:
        a_spec = pl.BlockSpec((tm, tk), lambda j, i, k: (i, k))
    if mode == 'nt':
        b_spec = pl.BlockSpec((tn, tk), lambda j, i, k: (j, k))
    else:
        b_spec = pl.BlockSpec((tk, tn), lambda j, i, k: (k, j))
    dims = {'nn': NN_DIMS, 'nt': NT_DIMS, 'tn': TN_DIMS}[mode]
    o_spec = pl.BlockSpec((tm, tn), lambda j, i, k: (i, j))
    has_res = res is not None

    def body(*refs):
        if has_res:
            a_ref, b_ref, r_ref, o_ref = refs[:4]
        else:
            a_ref, b_ref, o_ref = refs[:3]
        part = _dot(a_ref[...].astype(BF16), b_ref[...].astype(BF16), dims)

        def finish(acc):
            out = acc * alpha if alpha != 1.0 else acc
            if has_res:
                out = r_ref[...] + out
            o_ref[...] = out.astype(out_dtype)

        if nk == 1:
            finish(part)
        else:
            acc_ref = refs[-1]
            k = pl.program_id(2)

            @pl.when(k == 0)
            def _():
                acc_ref[...] = part

            @pl.when(k > 0)
            def _():
                acc_ref[...] += part

            @pl.when(k == nk - 1)
            def _():
                finish(acc_ref[...])

    in_specs = [a_spec, b_spec] + ([o_spec] if has_res else [])
    operands = [a, b] + ([res] if has_res else [])
    return pl.pallas_call(
        body, name=name, grid=(nj, ni, nk), in_specs=in_specs, out_specs=o_spec,
        out_shape=jax.ShapeDtypeStruct((M, N), out_dtype),
        scratch_shapes=[pltpu.VMEM((tm, tn), F32)] if nk > 1 else [],
        compiler_params=_params(("parallel", "parallel", "arbitrary")),
    )(*operands)


def _rms_fwd(x, g, *, name, tm=512):
    T, D = x.shape
    tm = min(tm, T)

    def body(x_ref, g_ref, o_ref):
        xv = x_ref[...]
        rstd = lax.rsqrt(jnp.mean(xv * xv, axis=-1, keepdims=True) + EPS)
        o_ref[...] = (xv * rstd * g_ref[...]).astype(BF16)

    return pl.pallas_call(
        body, name=name, grid=(T // tm,),
        in_specs=[pl.BlockSpec((tm, D), lambda i: (i, 0)), pl.BlockSpec((1, D), lambda i: (0, 0))],
        out_specs=pl.BlockSpec((tm, D), lambda i: (i, 0)),
        out_shape=jax.ShapeDtypeStruct((T, D), BF16),
        compiler_params=_params(("parallel",)),
    )(x, g)


def _rms_bwd_math(xv, g, dn):
    rstd = lax.rsqrt(jnp.mean(xv * xv, axis=-1, keepdims=True) + EPS)
    xhat = xv * rstd
    dxhat = dn * g
    dx = rstd * (dxhat - xhat * jnp.mean(dxhat * xhat, axis=-1, keepdims=True))
    return dx, jnp.sum(dn * xhat, axis=0, keepdims=True)


def _rms_bwd(x, g, dn, dres, *, name, tm=512):
    T, D = x.shape
    tm = min(tm, T)

    def body(x_ref, g_ref, dn_ref, dres_ref, dx_ref, dg_ref):
        dx, dg = _rms_bwd_math(x_ref[...], g_ref[...], dn_ref[...])
        dx_ref[...] = dres_ref[...] + dx

        @pl.when(pl.program_id(0) == 0)
        def _():
            dg_ref[...] = jnp.zeros_like(dg_ref)

        dg_ref[...] += dg

    row = pl.BlockSpec((tm, D), lambda i: (i, 0))
    vec = pl.BlockSpec((1, D), lambda i: (0, 0))
    return pl.pallas_call(
        body, name=name, grid=(T // tm,),
        in_specs=[row, vec, row, row], out_specs=[row, vec],
        out_shape=[jax.ShapeDtypeStruct((T, D), F32), jax.ShapeDtypeStruct((1, D), F32)],
        compiler_params=_params(("arbitrary",)),
    )(x, g, dn, dres)


def _loss_head(x, g, target, *, name, tm=512):
    T, D = x.shape
    tm = min(tm, T)

    def body(x_ref, g_ref, t_ref, loss_ref, dx_ref, dg_ref):
        xv = x_ref[...]
        g = g_ref[...]
        rstd = lax.rsqrt(jnp.mean(xv * xv, axis=-1, keepdims=True) + EPS)
        y = xv * rstd * g
        err = y - t_ref[...]
        part = 0.5 * jnp.sum(jnp.mean(err * err, axis=-1, keepdims=True), axis=0, keepdims=True)
        dx, dg = _rms_bwd_math(xv, g, err * (1.0 / D))
        dx_ref[...] = dx

        @pl.when(pl.program_id(0) == 0)
        def _():
            dg_ref[...] = jnp.zeros_like(dg_ref)
            loss_ref[...] = jnp.zeros_like(loss_ref)

        dg_ref[...] += dg
        loss_ref[...] += jnp.broadcast_to(part, loss_ref.shape)

    row = pl.BlockSpec((tm, D), lambda i: (i, 0))
    vec = pl.BlockSpec((1, D), lambda i: (0, 0))
    lspec = pl.BlockSpec((1, LANES), lambda i: (0, 0))
    return pl.pallas_call(
        body, name=name, grid=(T // tm,),
        in_specs=[row, vec, row], out_specs=[lspec, row, vec],
        out_shape=[jax.ShapeDtypeStruct((1, LANES), F32), jax.ShapeDtypeStruct((T, D), F32),
                   jax.ShapeDtypeStruct((1, D), F32)],
        compiler_params=_params(("arbitrary",)),
    )(x, g, target)


def _resident(shape):
    return pl.BlockSpec(shape, lambda i: (0,) * len(shape), pipeline_mode=pl.Buffered(1))


def _ffn_chunk(F):
    for c in (1408, 1024, 512, 256, 128):
        if F % c == 0:
            return c
    return F


def _ffn_fwd(x, g, wg, wu, wd, *, name, tm=256):
    T, D = x.shape
    F = wg.shape[1]
    tm = min(tm, T)
    fc = _ffn_chunk(F)

    def body(x_ref, g_ref, wg_ref, wu_ref, wd_ref, o_ref, a_ref, u_ref):
        xv = x_ref[...]
        rstd = lax.rsqrt(jnp.mean(xv * xv, axis=-1, keepdims=True) + EPS)
        n = (xv * rstd * g_ref[...]).astype(BF16)
        acc = jnp.zeros((tm, D), F32)
        for c in range(F // fc):
            sl = slice(c * fc, (c + 1) * fc)
            a = _dot(n, wg_ref[:, sl])
            u = _dot(n, wu_ref[:, sl])
            a_ref[:, sl] = a.astype(BF16)
            u_ref[:, sl] = u.astype(BF16)
            act = (a * _sigmoid(a) * u).astype(BF16)
            acc = acc + _dot(act, wd_ref[sl, :])
        o_ref[...] = xv + 0.5 * acc

    row = pl.BlockSpec((tm, D), lambda i: (i, 0))
    hid = pl.BlockSpec((tm, F), lambda i: (i, 0))
    return pl.pallas_call(
        body, name=name, grid=(T // tm,),
        in_specs=[row, pl.BlockSpec((1, D), lambda i: (0, 0)),
                  _resident((D, F)), _resident((D, F)), _resident((F, D))],
        out_specs=[row, hid, hid],
        out_shape=[jax.ShapeDtypeStruct((T, D), F32), jax.ShapeDtypeStruct((T, F), BF16),
                   jax.ShapeDtypeStruct((T, F), BF16)],
        compiler_params=_params(("parallel",)),
    )(x, g, wg, wu, wd)


def _ffn_bwd(x, g, dy, a, u, wg, wu, wd, *, name, tm=256):
    T, D = x.shape
    F = wg.shape[1]
    tm = min(tm, T)
    fc = _ffn_chunk(F)

    def body(x_ref, g_ref, dy_ref, a_ref, u_ref, wg_ref, wu_ref, wd_ref,
             dx_ref, dg_ref, n_ref, act_ref, da_ref, du_ref):
        xv = x_ref[...]
        g = g_ref[...]
        rstd = lax.rsqrt(jnp.mean(xv * xv, axis=-1, keepdims=True) + EPS)
        n_ref[...] = (xv * rstd * g).astype(BF16)
        dy = dy_ref[...]
        dyh = (0.5 * dy).astype(BF16)
        dn = jnp.zeros((tm, D), F32)
        for c in range(F // fc):
            sl = slice(c * fc, (c + 1) * fc)
            av = a_ref[:, sl].astype(F32)
            uv = u_ref[:, sl].astype(F32)
            dact = _dot(dyh, wd_ref[sl, :], NT_DIMS)
            s = _sigmoid(av)
            silu = av * s
            act_ref[:, sl] = (silu * uv).astype(BF16)
            da = (dact * uv * (s * (1.0 + av * (1.0 - s)))).astype(BF16)
            du = (dact * silu).astype(BF16)
            da_ref[:, sl] = da
            du_ref[:, sl] = du
            dn = dn + _dot(da, wg_ref[:, sl], NT_DIMS) + _dot(du, wu_ref[:, sl], NT_DIMS)
        dx, dg = _rms_bwd_math(xv, g, dn)
        dx_ref[...] = dy + dx

        @pl.when(pl.program_id(0) == 0)
        def _():
            dg_ref[...] = jnp.zeros_like(dg_ref)

        dg_ref[...] += dg

    row = pl.BlockSpec((tm, D), lambda i: (i, 0))
    hid = pl.BlockSpec((tm, F), lambda i: (i, 0))
    vec = pl.BlockSpec((1, D), lambda i: (0, 0))
    return pl.pallas_call(
        body, name=name, grid=(T // tm,),
        in_specs=[row, vec, row, hid, hid,
                  _resident((D, F)), _resident((D, F)), _resident((F, D))],
        out_specs=[row, vec, row, hid, hid, hid],
        out_shape=[jax.ShapeDtypeStruct((T, D), F32), jax.ShapeDtypeStruct((1, D), F32),
                   jax.ShapeDtypeStruct((T, D), BF16), jax.ShapeDtypeStruct((T, F), BF16),
                   jax.ShapeDtypeStruct((T, F), BF16), jax.ShapeDtypeStruct((T, F), BF16)],
        compiler_params=_params(("arbitrary",)),
    )(x, g, dy, a, u, wg, wu, wd)


def _lane_iota(shape):
    return lax.broadcasted_iota(jnp.int32, shape, 1)


def _rope_partner(x):
    first_half = (_lane_iota(x.shape) & (HEAD_DIM - 1)) < HEAD_DIM // 2
    return jnp.where(first_half, pltpu.roll(x, LANES - HEAD_DIM // 2, 1), pltpu.roll(x, HEAD_DIM // 2, 1))


def _swap_heads(x):
    return pltpu.roll(x, HEAD_DIM, 1)


def _split_rope(proj, cos_t, sin_t, *, name, tm=256):
    T = proj.shape[0]
    tm = min(tm, T)

    def body(p_ref, c_ref, s_ref, qa_ref, ka_ref, va_ref, qb_ref, kb_ref, vb_ref, xc_ref, gc_ref):
        cos = c_ref[...]
        sin = s_ref[...]

        def rope(x):
            return x * cos + _rope_partner(x) * sin

        lo = _lane_iota((tm, LANES)) < HEAD_DIM
        col = 0
        for j in range(A_WIDTH // LANES):
            qa_ref[:, j * LANES:(j + 1) * LANES] = (rope(p_ref[:, col:col + LANES]) * SCALE).astype(BF16)
            col += LANES
        kr = rope(p_ref[:, col:col + LANES])
        col += LANES
        vr = p_ref[:, col:col + LANES]
        col += LANES
        for src, dst in ((kr, ka_ref), (vr, va_ref)):
            sw = _swap_heads(src)
            dst[:, 0:LANES] = jnp.where(lo, src, sw).astype(BF16)
            dst[:, LANES:2 * LANES] = jnp.where(lo, sw, src).astype(BF16)
        for dst, roped, scale in ((qb_ref, True, SCALE), (kb_ref, True, 1.0), (vb_ref, False, 1.0)):
            for j in range(B_WIDTH // LANES):
                v = p_ref[:, col:col + LANES]
                if roped:
                    v = rope(v) * scale
                dst[:, j * LANES:(j + 1) * LANES] = v.astype(BF16)
                col += LANES
        xc_ref[...] = p_ref[:, col:col + C_WIDTH]
        gc_ref[...] = p_ref[:, col + C_WIDTH:col + 2 * C_WIDTH]

    def row(w):
        return pl.BlockSpec((tm, w), lambda i: (i, 0))

    widths = [A_WIDTH, A_WIDTH, A_WIDTH, B_WIDTH, B_WIDTH, B_WIDTH, C_WIDTH, C_WIDTH]
    dtypes = [BF16] * 6 + [F32] * 2
    return pl.pallas_call(
        body, name=name, grid=(T // tm,),
        in_specs=[row(IN_COLS), row(LANES), row(LANES)],
        out_specs=[row(w) for w in widths],
        out_shape=[jax.ShapeDtypeStruct((T, w), d) for w, d in zip(widths, dtypes)],
        compiler_params=_params(("parallel",)),
    )(proj, cos_t, sin_t)


def _merge_dproj(dqa, dka2, dva2, dqb, dkb, dvb, dxc, dgc, cos_t, sin_t, *, name, tm=256):
    T = dqa.shape[0]
    tm = min(tm, T)
    nb = len(dqb)

    def body(*refs):
        dqa_ref, dka_ref, dva_ref = refs[:3]
        dqb_refs = refs[3:3 + nb]
        dkb_refs = refs[3 + nb:3 + 2 * nb]
        dvb_refs = refs[3 + 2 * nb:3 + 3 * nb]
        dxc_ref, dgc_ref, c_ref, s_ref, o_ref = refs[3 + 3 * nb:]
        cos = c_ref[...]
        sin = s_ref[...]

        def rope_t(dy):
            return dy * cos - _rope_partner(dy) * sin

        lo = _lane_iota((tm, LANES)) < HEAD_DIM
        col = 0
        for j in range(A_WIDTH // LANES):
            o_ref[:, col:col + LANES] = (rope_t(dqa_ref[:, j * LANES:(j + 1) * LANES]) * SCALE).astype(BF16)
            col += LANES
        for src, roped in ((dka_ref, True), (dva_ref, False)):
            b0 = src[:, 0:LANES]
            b1 = src[:, LANES:2 * LANES]
            v = jnp.where(lo, b0 + _swap_heads(b0), b1 + _swap_heads(b1))
            if roped:
                v = rope_t(v)
            o_ref[:, col:col + LANES] = v.astype(BF16)
            col += LANES
        for group, roped, scale in ((dqb_refs, True, SCALE), (dkb_refs, True, 1.0), (dvb_refs, False, 1.0)):
            for j in range(B_WIDTH // LANES):
                sl = slice(j * LANES, (j + 1) * LANES)
                v = group[0][:, sl]
                for r in group[1:]:
                    v = v + r[:, sl]
                if roped:
                    v = rope_t(v) * scale
                o_ref[:, col:col + LANES] = v.astype(BF16)
                col += LANES
        o_ref[:, col:col + C_WIDTH] = dxc_ref[...].astype(BF16)
        o_ref[:, col + C_WIDTH:col + 2 * C_WIDTH] = dgc_ref[...].astype(BF16)

    def row(w):
        return pl.BlockSpec((tm, w), lambda i: (i, 0))

    ins = [dqa, dka2, dva2, *dqb, *dkb, *dvb, dxc, dgc, cos_t, sin_t]
    return pl.pallas_call(
        body, name=name, grid=(T // tm,),
        in_specs=[row(v.shape[1]) for v in ins],
        out_specs=row(IN_COLS),
        out_shape=jax.ShapeDtypeStruct((T, IN_COLS), BF16),
        compiler_params=_params(("parallel",)),
    )(*ins)


def _band_masks(max_dist):
    row = lax.broadcasted_iota(jnp.int32, (QBLK, QBLK), 0)
    key = lax.broadcasted_iota(jnp.int32, (QBLK, QBLK), 1)
    return row >= key, (key - row) >= (QBLK - max_dist)


def _head_masks():
    lo = _lane_iota((QBLK, LANES)) < HEAD_DIM
    return lo, jnp.logical_not(lo)


def _keep(hm, x):
    return x * jnp.where(hm, 1.0, 0.0).astype(x.dtype)


def _head_col(x, hm):
    return jnp.max(jnp.where(hm, x, -jnp.inf), axis=1, keepdims=True)


def _attn_specs(R, C):
    chunk = min(ATTN_CHUNK, R)
    nb = chunk // QBLK
    nch = R // chunk
    main = pl.BlockSpec((chunk, LANES), lambda j, c: (c, j))
    prev = pl.BlockSpec((QBLK, LANES), lambda j, c: (jnp.maximum(c * nb - 1, 0), j))
    nxt = pl.BlockSpec((QBLK, LANES), lambda j, c: (jnp.minimum((c + 1) * nb, R // QBLK - 1), j))
    return chunk, nb, nch, main, prev, nxt


def _attn_fwd(q, k, v, max_dist, *, name):
    R, C = q.shape
    chunk, nb, nch, main, prev, _ = _attn_specs(R, C)

    def body(q_ref, k_ref, v_ref, kp_ref, vp_ref, o_ref, lse_ref):
        c = pl.program_id(1)
        cur_mask, prev_mask = _band_masks(max_dist)
        heads = _head_masks()

        def block(q_blk, k_cur, v_cur, k_prev, v_prev, prev_ok):
            o_h, lse_h = [], []
            for hm in heads:
                qh = _keep(hm, q_blk)
                s_c = jnp.where(cur_mask, _dot(qh, k_cur, NT_DIMS), -jnp.inf)
                s_p = jnp.where(prev_ok, _dot(qh, k_prev, NT_DIMS), -jnp.inf)
                m = jnp.maximum(jnp.max(s_c, axis=1, keepdims=True), jnp.max(s_p, axis=1, keepdims=True))
                p_c = jnp.exp(s_c - m)
                p_p = jnp.exp(s_p - m)
                l = jnp.sum(p_c, axis=1, keepdims=True) + jnp.sum(p_p, axis=1, keepdims=True)
                pv = _dot(p_c.astype(BF16), v_cur) + _dot(p_p.astype(BF16), v_prev)
                o_h.append(pv / l)
                lse_h.append(jnp.broadcast_to(m + jnp.log(l), (QBLK, LANES)))
            return jnp.where(heads[0], o_h[0], o_h[1]), jnp.where(heads[0], lse_h[0], lse_h[1])

        first = pl.ds(0, QBLK)
        o0, l0 = block(q_ref[first, :], k_ref[first, :], v_ref[first, :], kp_ref[...], vp_ref[...],
                       jnp.logical_and(prev_mask, c > 0))
        o_ref[first, :] = o0
        lse_ref[first, :] = l0

        def loop(qb, carry):
            cur = pl.ds(pl.multiple_of(qb * QBLK, QBLK), QBLK)
            prv = pl.ds(pl.multiple_of((qb - 1) * QBLK, QBLK), QBLK)
            o, l = block(q_ref[cur, :], k_ref[cur, :], v_ref[cur, :], k_ref[prv, :], v_ref[prv, :], prev_mask)
            o_ref[cur, :] = o
            lse_ref[cur, :] = l
            return carry

        if nb > 1:
            lax.fori_loop(1, nb, loop, 0)

    return pl.pallas_call(
        body, name=name, grid=(C // LANES, nch),
        in_specs=[main, main, main, prev, prev], out_specs=[main, main],
        out_shape=[jax.ShapeDtypeStruct((R, C), F32), jax.ShapeDtypeStruct((R, C), F32)],
        compiler_params=_params(("parallel", "parallel")),
    )(q, k, v, k, v)


def _attn_bwd(q, k, v, do, lse, delta, max_dist, *, name):
    R, C = q.shape
    chunk, nb, nch, main, prev, nxt = _attn_specs(R, C)

    def body(q_ref, k_ref, v_ref, do_ref, lse_ref, dl_ref, kp_ref, vp_ref, qn_ref, don_ref, lsen_ref, dln_ref,
             dq_ref, dk_ref, dv_ref):
        c = pl.program_id(1)
        cur_mask, prev_mask = _band_masks(max_dist)
        heads = _head_masks()

        def pair(q_blk, do_blk, lse_blk, dl_blk, k_blk, v_blk, mask, want_dq, want_dkv):
            dq = jnp.zeros((QBLK, LANES), F32)
            dk = jnp.zeros((QBLK, LANES), F32)
            dv = jnp.zeros((QBLK, LANES), F32)
            for hm in heads:
                qh = _keep(hm, q_blk)
                doh = _keep(hm, do_blk)
                s = _dot(qh, k_blk, NT_DIMS)
                p = jnp.where(mask, jnp.exp(s - _head_col(lse_blk, hm)), 0.0)
                dp = _dot(doh, v_blk, NT_DIMS)
                ds = (p * (dp - _head_col(dl_blk, hm))).astype(BF16)
                if want_dq:
                    dq = dq + _dot(ds, _keep(hm, k_blk))
                if want_dkv:
                    dk = dk + _dot(ds, qh, TN_DIMS)
                    dv = dv + _dot(p.astype(BF16), doh, TN_DIMS)
            return dq, dk, dv

        dk_ref[...] = jnp.zeros_like(dk_ref)
        dv_ref[...] = jnp.zeros_like(dv_ref)

        first = pl.ds(0, QBLK)
        q0, do0, lse0, dl0 = q_ref[first, :], do_ref[first, :], lse_ref[first, :], dl_ref[first, :]
        dq_c, dk_c, dv_c = pair(q0, do0, lse0, dl0, k_ref[first, :], v_ref[first, :], cur_mask, True, True)
        dq_p, _, _ = pair(q0, do0, lse0, dl0, kp_ref[...], vp_ref[...], jnp.logical_and(prev_mask, c > 0),
                          True, False)
        dq_ref[first, :] = dq_c + dq_p
        dk_ref[first, :] += dk_c
        dv_ref[first, :] += dv_c

        def loop(qb, carry):
            cur = pl.ds(pl.multiple_of(qb * QBLK, QBLK), QBLK)
            prv = pl.ds(pl.multiple_of((qb - 1) * QBLK, QBLK), QBLK)
            qv, dov, lsev, dlv = q_ref[cur, :], do_ref[cur, :], lse_ref[cur, :], dl_ref[cur, :]
            dq1, dk1, dv1 = pair(qv, dov, lsev, dlv, k_ref[cur, :], v_ref[cur, :], cur_mask, True, True)
            dq2, dk2, dv2 = pair(qv, dov, lsev, dlv, k_ref[prv, :], v_ref[prv, :], prev_mask, True, True)
            dq_ref[cur, :] = dq1 + dq2
            dk_ref[cur, :] += dk1
            dv_ref[cur, :] += dv1
            dk_ref[prv, :] += dk2
            dv_ref[prv, :] += dv2
            return carry

        if nb > 1:
            lax.fori_loop(1, nb, loop, 0)

        last = pl.ds((nb - 1) * QBLK, QBLK)
        _, dk_n, dv_n = pair(qn_ref[...], don_ref[...], lsen_ref[...], dln_ref[...], k_ref[last, :], v_ref[last, :],
                             jnp.logical_and(prev_mask, c < nch - 1), False, True)
        dk_ref[last, :] += dk_n
        dv_ref[last, :] += dv_n

    return pl.pallas_call(
        body, name=name, grid=(C // LANES, nch),
        in_specs=[main] * 6 + [prev, prev] + [nxt] * 4, out_specs=[main, main, main],
        out_shape=[jax.ShapeDtypeStruct((R, C), F32)] * 3,
        compiler_params=_params(("parallel", "parallel")),
    )(q, k, v, do, lse, delta, k, v, q, do, lse, delta)


def _head_sum(x):
    r = lax.broadcasted_iota(jnp.int32, (LANES, LANES), 0) // HEAD_DIM
    c = lax.broadcasted_iota(jnp.int32, (LANES, LANES), 1) // HEAD_DIM
    ones = jnp.where(r == c, 1.0, 0.0).astype(BF16)
    outs = []
    for j in range(x.shape[1] // LANES):
        rem = x[:, j * LANES:(j + 1) * LANES]
        acc = jnp.zeros(rem.shape, F32)
        for _ in range(3):
            part = rem.astype(BF16)
            acc = acc + _dot(part, ones)
            rem = rem - part.astype(F32)
        outs.append(acc)
    return outs[0] if len(outs) == 1 else jnp.concatenate(outs, axis=1)


def _branch_weights(lses):
    m = functools.reduce(jnp.maximum, lses)
    es = [jnp.exp(l - m) for l in lses]
    den = functools.reduce(lambda a, b: a + b, es)
    return [e / den for e in es]


def _combine_fwd(oa, lsea, sink, obs, lsebs, oc, *, name, tm=256):
    T = oa.shape[0]
    tm = min(tm, T)
    nb = len(obs)

    def body(*refs):
        oa_ref, lsea_ref, sink_ref = refs[:3]
        ob_refs = refs[3:3 + nb]
        lse_refs = refs[3 + nb:3 + 2 * nb]
        oc_ref, out_ref = refs[3 + 2 * nb:]
        out_ref[:, 0:A_WIDTH] = (oa_ref[...] * _sigmoid(lsea_ref[...] - sink_ref[...])).astype(BF16)
        ws = _branch_weights([r[...] for r in lse_refs])
        ob = ob_refs[0][...] * ws[0]
        for r, w in zip(ob_refs[1:], ws[1:]):
            ob = ob + r[...] * w
        out_ref[:, A_WIDTH:A_WIDTH + B_WIDTH] = ob.astype(BF16)
        out_ref[:, A_WIDTH + B_WIDTH:MIX_WIDTH] = oc_ref[...].astype(BF16)

    def row(w):
        return pl.BlockSpec((tm, w), lambda i: (i, 0))

    ins = [oa, lsea, sink, *obs, *lsebs, oc]
    in_specs = [row(A_WIDTH), row(A_WIDTH), pl.BlockSpec((1, A_WIDTH), lambda i: (0, 0))]
    in_specs += [row(B_WIDTH)] * (2 * nb) + [row(C_WIDTH)]
    return pl.pallas_call(
        body, name=name, grid=(T // tm,), in_specs=in_specs, out_specs=row(MIX_WIDTH),
        out_shape=jax.ShapeDtypeStruct((T, MIX_WIDTH), BF16),
        compiler_params=_params(("parallel",)),
    )(*ins)


def _combine_bwd(dmix, oa, lsea, sink, obs, lsebs, *, name, tm=256):
    T = oa.shape[0]
    tm = min(tm, T)
    nb = len(obs)

    def body(*refs):
        dmix_ref, oa_ref, lsea_ref, sink_ref = refs[:4]
        ob_refs = refs[4:4 + nb]
        lse_refs = refs[4 + nb:4 + 2 * nb]
        outs = refs[4 + 2 * nb:]
        doa_ref, dla_ref = outs[:2]
        dob_refs = outs[2:2 + nb]
        dlb_refs = outs[2 + nb:2 + 2 * nb]
        doc_ref, dsink_ref = outs[2 + 2 * nb:]

        d_a = dmix_ref[:, 0:A_WIDTH]
        d_b = dmix_ref[:, A_WIDTH:A_WIDTH + B_WIDTH]
        doc_ref[...] = dmix_ref[:, A_WIDTH + B_WIDTH:MIX_WIDTH]

        gate = _sigmoid(lsea_ref[...] - sink_ref[...])
        doa_ref[...] = (d_a * gate).astype(BF16)
        dgate = _head_sum(d_a * oa_ref[...])
        dlse = dgate * gate * (1.0 - gate)
        dla_ref[...] = dgate * gate - dlse

        @pl.when(pl.program_id(0) == 0)
        def _():
            dsink_ref[...] = jnp.zeros_like(dsink_ref)

        dsink_ref[...] -= jnp.sum(dlse, axis=0, keepdims=True)

        ws = _branch_weights([r[...] for r in lse_refs])
        dws = [_head_sum(d_b * r[...]) for r in ob_refs]
        sw = ws[0] * dws[0]
        for w, dw in zip(ws[1:], dws[1:]):
            sw = sw + w * dw
        for w, do_ref, dl_ref in zip(ws, dob_refs, dlb_refs):
            do_ref[...] = (w * d_b).astype(BF16)
            dl_ref[...] = w * sw

    def row(w):
        return pl.BlockSpec((tm, w), lambda i: (i, 0))

    vec = pl.BlockSpec((1, A_WIDTH), lambda i: (0, 0))
    ins = [dmix, oa, lsea, sink, *obs, *lsebs]
    in_specs = [row(MIX_WIDTH), row(A_WIDTH), row(A_WIDTH), vec] + [row(B_WIDTH)] * (2 * nb)
    out_specs = [row(A_WIDTH), row(A_WIDTH)] + [row(B_WIDTH)] * (2 * nb) + [row(C_WIDTH), vec]
    out_shape = [jax.ShapeDtypeStruct((T, A_WIDTH), BF16), jax.ShapeDtypeStruct((T, A_WIDTH), F32)]
    out_shape += [jax.ShapeDtypeStruct((T, B_WIDTH), BF16)] * nb + [jax.ShapeDtypeStruct((T, B_WIDTH), F32)] * nb
    out_shape += [jax.ShapeDtypeStruct((T, C_WIDTH), F32), jax.ShapeDtypeStruct((1, A_WIDTH), F32)]
    res = pl.pallas_call(
        body, name=name, grid=(T // tm,), in_specs=in_specs, out_specs=out_specs, out_shape=out_shape,
        compiler_params=_params(("arbitrary",)),
    )(*ins)
    return res[0], res[1], list(res[2:2 + nb]), list(res[2 + nb:2 + 2 * nb]), res[2 + 2 * nb], res[3 + 2 * nb]


HIST = 8


def _softplus_neg(lam):
    e = jnp.exp(-jnp.abs(lam))
    log1p = jnp.where(e < 0.01, e * (1.0 - e * (0.5 - e * (1.0 / 3.0))), jnp.log(1.0 + e))
    return jnp.maximum(-lam, 0.0) + log1p


def _neg_expm1(x):
    series = -x * (1.0 + x * (0.5 + x * (1.0 / 6.0 + x * (1.0 / 24.0 + x * (1.0 / 120.0)))))
    return jnp.where(x > -0.1, series, 1.0 - jnp.exp(x))


def _gelu_parts(x):
    k = math.sqrt(2.0 / math.pi)
    t = jnp.tanh(k * (x + 0.044715 * (x * x * x)))
    cdf = 0.5 * (1.0 + t)
    return x * cdf, cdf + 0.5 * x * (1.0 - t * t) * (k * (1.0 + 3.0 * 0.044715 * (x * x)))


def _rglru_gates(y, pos_ref, wr_ref, br_ref, wi_ref, bi_ref, lam_ref):
    yb = y.astype(BF16)
    r = _sigmoid(_dot(yb, wr_ref[...]) + br_ref[...])
    ig = _sigmoid(_dot(yb, wi_ref[...]) + bi_ref[...])
    sp = _softplus_neg(lam_ref[...])
    log_a = -C_EXP * r * sp
    reset = pos_ref[...] == 0
    a = jnp.where(reset, 0.0, jnp.exp(log_a))
    mult = jnp.where(reset, 1.0, jnp.sqrt(_neg_expm1(2.0 * log_a)))
    return yb, r, ig, sp, reset, a, mult


def _conv_fwd(xs_ref, cw_ref, cb_ref, tm):
    y = cb_ref[...] + cw_ref[0:1, :] * xs_ref[HIST:HIST + tm, :]
    for j in range(1, C_CONV):
        y = y + cw_ref[j:j + 1, :] * xs_ref[HIST - j:HIST - j + tm, :]
    return y


def _rglru_fwd(xc, gc, pos, cw, cb, wr, br, wi, bi, lam, *, name, tm=512):
    T, W = xc.shape
    tm = min(tm, T)

    def body(xc_ref, gc_ref, pos_ref, cw_ref, cb_ref, wr_ref, br_ref, wi_ref, bi_ref, lam_ref,
             out_ref, hs_ref, xs, a_s, b_s, h_s):
        @pl.when(pl.program_id(0) == 0)
        def _():
            xs[0:HIST, :] = jnp.zeros((HIST, W), F32)
            h_s[...] = jnp.zeros_like(h_s)

        xv = xc_ref[...]
        xs[HIST:HIST + tm, :] = xv
        y = _conv_fwd(xs, cw_ref, cb_ref, tm)
        xs[0:HIST, :] = xv[tm - HIST:tm, :]
        _, _, ig, _, _, a, mult = _rglru_gates(y, pos_ref, wr_ref, br_ref, wi_ref, bi_ref, lam_ref)
        a_s[...] = a
        b_s[...] = mult * (ig * y)

        def step(t, h):
            row = pl.ds(t, 1)
            h = a_s[row, :] * h + b_s[row, :]
            b_s[row, :] = h
            return h

        h_s[...] = lax.fori_loop(0, tm, step, h_s[...], unroll=8)
        hs = b_s[...]
        hs_ref[...] = hs
        out_ref[...] = hs * _gelu_parts(gc_ref[...])[0]

    row = pl.BlockSpec((tm, W), lambda i: (i, 0))
    full = lambda shape: pl.BlockSpec(shape, lambda i: (0,) * len(shape))
    return pl.pallas_call(
        body, name=name, grid=(T // tm,),
        in_specs=[row, row, pl.BlockSpec((tm, 1), lambda i: (i, 0)), full((C_CONV, W)), full((1, W)),
                  full((W, W)), full((1, W)), full((W, W)), full((1, W)), full((1, W))],
        out_specs=[row, row],
        out_shape=[jax.ShapeDtypeStruct((T, W), F32)] * 2,
        scratch_shapes=[pltpu.VMEM((tm + HIST, W), F32), pltpu.VMEM((tm, W), F32), pltpu.VMEM((tm, W), F32),
                        pltpu.VMEM((1, W), F32)],
        compiler_params=_params(("arbitrary",)),
    )(xc, gc, pos, cw, cb, wr, br, wi, bi, lam)


def _rglru_bwd(xc, gc, pos, hs, dout, cw, cb, wr, br, wi, bi, lam, *, name, tm=512):
    T, W = xc.shape
    tm = min(tm, T)
    nt = T // tm
    hb = tm // HIST

    def body(xc_ref, gc_ref, pos_ref, hs_ref, dout_ref, xch_ref, hsh_ref,
             cw_ref, cb_ref, wr_ref, br_ref, wi_ref, bi_ref, lam_ref,
             dxc_ref, dgc_ref, dcw_ref, dcb_ref, dwr_ref, dbr_ref, dwi_ref, dbi_ref, dlam_ref,
             xs, hsx, dys, a_s, d_s, carry_s):
        i = pl.program_id(0)

        @pl.when(i == 0)
        def _():
            for r in (dcw_ref, dcb_ref, dwr_ref, dbr_ref, dwi_ref, dbi_ref, dlam_ref, carry_s):
                r[...] = jnp.zeros_like(r)
            dys[tm:tm + HIST, :] = jnp.zeros((HIST, W), F32)

        has_prev = i < nt - 1
        xs[0:HIST, :] = jnp.where(has_prev, xch_ref[...], 0.0)
        hsx[0:HIST, :] = jnp.where(has_prev, hsh_ref[...], 0.0)
        xs[HIST:HIST + tm, :] = xc_ref[...]
        hs = hs_ref[...]
        hsx[HIST:HIST + tm, :] = hs
        y = _conv_fwd(xs, cw_ref, cb_ref, tm)
        yb, r, ig, sp, reset, a, mult = _rglru_gates(y, pos_ref, wr_ref, br_ref, wi_ref, bi_ref, lam_ref)

        gelu, dgelu = _gelu_parts(gc_ref[...])
        dout = dout_ref[...]
        dgc_ref[...] = dout * hs * dgelu
        a_s[...] = a
        d_s[...] = dout * gelu

        def step(k, carry):
            row = pl.ds(tm - 1 - k, 1)
            dh = d_s[row, :] + carry
            d_s[row, :] = dh
            return a_s[row, :] * dh

        carry_s[...] = lax.fori_loop(0, tm, step, carry_s[...], unroll=8)
        dh = d_s[...]
        hprev = hsx[HIST - 1:HIST - 1 + tm, :]
        igy = ig * y
        dmult = dh * igy
        digy = dh * mult
        dlog_a = jnp.where(reset, 0.0, dh * hprev * a - dmult * a * a / mult)
        dlam_ref[...] += jnp.sum(dlog_a * (C_EXP * r) * _sigmoid(-lam_ref[...]), axis=0, keepdims=True)
        dz_r = dlog_a * (-C_EXP * sp) * r * (1.0 - r)
        dz_i = digy * y * ig * (1.0 - ig)
        dzr_b = dz_r.astype(BF16)
        dzi_b = dz_i.astype(BF16)
        dy = digy * ig + _dot(dzr_b, wr_ref[...], NT_DIMS) + _dot(dzi_b, wi_ref[...], NT_DIMS)
        dwr_ref[...] += _dot(yb, dzr_b, TN_DIMS)
        dwi_ref[...] += _dot(yb, dzi_b, TN_DIMS)
        dbr_ref[...] += jnp.sum(dz_r, axis=0, keepdims=True)
        dbi_ref[...] += jnp.sum(dz_i, axis=0, keepdims=True)

        dys[0:tm, :] = dy
        dxc = cw_ref[0:1, :] * dy
        for j in range(1, C_CONV):
            dxc = dxc + cw_ref[j:j + 1, :] * dys[j:j + tm, :]
        dxc_ref[...] = dxc
        dys[tm:tm + HIST, :] = dy[0:HIST, :]
        dcb_ref[...] += jnp.sum(dy, axis=0, keepdims=True)
        for j in range(C_CONV):
            dcw_ref[j:j + 1, :] += jnp.sum(dy * xs[HIST - j:HIST - j + tm, :], axis=0, keepdims=True)

    row = pl.BlockSpec((tm, W), lambda i: (nt - 1 - i, 0))
    halo = pl.BlockSpec((HIST, W), lambda i: (jnp.maximum((nt - 1 - i) * hb - 1, 0), 0))
    full = lambda shape: pl.BlockSpec(shape, lambda i: (0,) * len(shape))
    out_specs = [row, row, full((C_CONV, W)), full((1, W)), full((W, W)), full((1, W)), full((W, W)), full((1, W)),
                 full((1, W))]
    out_shape = [jax.ShapeDtypeStruct((T, W), F32)] * 2
    out_shape += [jax.ShapeDtypeStruct(s, F32) for s in ((C_CONV, W), (1, W), (W, W), (1, W), (W, W), (1, W), (1, W))]
    return pl.pallas_call(
        body, name=name, grid=(nt,),
        in_specs=[row, row, pl.BlockSpec((tm, 1), lambda i: (nt - 1 - i, 0)), row, row, halo, halo,
                  full((C_CONV, W)), full((1, W)), full((W, W)), full((1, W)), full((W, W)), full((1, W)),
                  full((1, W))],
        out_specs=out_specs, out_shape=out_shape,
        scratch_shapes=[pltpu.VMEM((tm + HIST, W), F32), pltpu.VMEM((tm + HIST, W), F32),
                        pltpu.VMEM((tm + HIST, W), F32), pltpu.VMEM((tm, W), F32), pltpu.VMEM((tm, W), F32),
                        pltpu.VMEM((1, W), F32)],
        compiler_params=_params(("arbitrary",)),
    )(xc, gc, pos, hs, dout, xc, hs, cw, cb, wr, br, wi, bi, lam)


def _adam_math(w, g, m, v):
    m = ADAM_B1 * m + (1.0 - ADAM_B1) * g
    v = ADAM_B2 * v + (1.0 - ADAM_B2) * (g * g)
    m_hat = m / (1.0 - ADAM_B1 ** ADAM_STEP)
    v_hat = v / (1.0 - ADAM_B2 ** ADAM_STEP)
    delta = -ADAM_LR * (m_hat / (jnp.sqrt(v_hat) + ADAM_EPS) + ADAM_WD * w)
    return delta, m, v


def _pick_rows(R, cap=512, mult=16):
    for d in range(min(cap, R), 0, -1):
        if R % d == 0 and d % mult == 0:
            return d
    return R


def _adamw(parts, w, m, v, *, name, tr=None):
    R, C = w.shape
    tr = _pick_rows(R) if tr is None else tr
    assert R % tr == 0, (name, R, tr)

    def body(p_ref, w_ref, m_ref, v_ref, g_ref, d_ref, nm_ref, nv_ref):
        g = p_ref[0].astype(F32)
        for d in range(1, N_DEV):
            g = g + p_ref[d].astype(F32)
        delta, nm, nv = _adam_math(w_ref[...], g, m_ref[...], v_ref[...])
        g_ref[...] = g
        d_ref[...] = delta
        nm_ref[...] = nm
        nv_ref[...] = nv

    row = pl.BlockSpec((tr, C), lambda i: (i, 0))
    return pl.pallas_call(
        body, name=name, grid=(R // tr,),
        in_specs=[pl.BlockSpec((N_DEV, tr, C), lambda i: (0, i, 0)), row, row, row],
        out_specs=[row] * 4, out_shape=[jax.ShapeDtypeStruct((R, C), F32)] * 4,
        compiler_params=_params(("parallel",)),
    )(parts, w, m, v)


def _exchange(srcs, gather, *, name):
    n = len(srcs)
    out_shape = [jax.ShapeDtypeStruct((N_DEV,) + s.shape if gather else s.shape, s.dtype) for s in srcs]

    def body(*refs):
        ins, outs = refs[:n], refs[n:2 * n]
        send_sems, recv_sems, local_sems = refs[2 * n:]
        x, y, c = lax.axis_index("x"), lax.axis_index("y"), lax.axis_index("c")
        me = 4 * x + 2 * y + c
        local_copies, sends, arrivals = [], [], []
        for a in range(n):
            mine = ins[a] if gather else ins[a].at[me]
            local = pltpu.make_async_copy(mine, outs[a].at[me], local_sems.at[a])
            local.start()
            local_copies.append(local)
            for k in range(1, N_DEV):
                px, py, pc = x ^ ((k >> 2) & 1), y ^ ((k >> 1) & 1), c ^ (k & 1)
                peer = 4 * px + 2 * py + pc
                send = pltpu.make_async_remote_copy(
                    src_ref=ins[a] if gather else ins[a].at[peer], dst_ref=outs[a].at[me],
                    send_sem=send_sems.at[a * N_DEV + k], recv_sem=recv_sems.at[a * N_DEV + k],
                    device_id=(px, py, pc), device_id_type=pl.DeviceIdType.MESH)
                send.start()
                sends.append(send)
                arrivals.append(pltpu.make_async_remote_copy(
                    src_ref=mine, dst_ref=outs[a].at[peer],
                    send_sem=send_sems.at[a * N_DEV + k], recv_sem=recv_sems.at[a * N_DEV + k],
                    device_id=(px, py, pc), device_id_type=pl.DeviceIdType.MESH))
        for cp in sends:
            cp.wait_send()
        for cp in arrivals:
            cp.wait_recv()
        for cp in local_copies:
            cp.wait()

    return pl.pallas_call(
        body, name=name,
        in_specs=[pl.BlockSpec(memory_space=pl.ANY)] * n, out_specs=[pl.BlockSpec(memory_space=pl.ANY)] * n,
        out_shape=out_shape,
        scratch_shapes=[pltpu.SemaphoreType.DMA((n * N_DEV,)), pltpu.SemaphoreType.DMA((n * N_DEV,)),
                        pltpu.SemaphoreType.DMA((n,))],
    )(*srcs)


def _to_blocks(w, axis):
    shape = w.shape
    k = shape[axis] // N_DEV
    w = w.reshape(shape[:axis] + (N_DEV, k) + shape[axis + 1:])
    return jnp.moveaxis(w, axis, 0)


def _from_blocks(wb, axis):
    w = jnp.moveaxis(wb, 0, axis)
    shape = w.shape
    return w.reshape(shape[:axis] + (shape[axis] * shape[axis + 1],) + shape[axis + 2:])


def _block_diag(w):
    n, k, _ = w.shape
    eye = jnp.eye(n, dtype=w.dtype)
    return (eye[:, None, :, None] * w[:, :, None, :]).reshape(n * k, n * k)


def _diag_blocks(wd):
    k = HEAD_DIM
    return jnp.stack([wd[h * k:(h + 1) * k, h * k:(h + 1) * k] for h in range(C_BLOCKS)])


def _pack(arrays):
    rows = []
    for a in arrays:
        flat = a.reshape(-1).astype(F32)
        pad = (-flat.shape[0]) % LANES
        rows.append(jnp.pad(flat, (0, pad)).reshape(-1, LANES))
    out = jnp.concatenate(rows, axis=0)
    return jnp.pad(out, ((0, (-out.shape[0]) % 8), (0, 0)))


def _unpack(packed, shapes):
    outs, r = [], 0
    for s in shapes:
        size = math.prod(s)
        nrows = -(-size // LANES)
        outs.append(packed[r:r + nrows].reshape(-1)[:size].reshape(s))
        r += nrows
    return outs


def _rope_tables(positions):
    inv = 1.0 / (ROPE_THETA ** (jnp.arange(0, HEAD_DIM, 2, dtype=F32) / HEAD_DIM))
    ang = positions.astype(F32)[:, None] * inv
    cos, sin = jnp.cos(ang), jnp.sin(ang)
    return jnp.tile(cos, (1, 4)), jnp.tile(jnp.concatenate([-sin, sin], axis=1), (1, 2))


def _dilate(t, d):
    return t.reshape(t.shape[0] // d, d * t.shape[1])


def _layer_fwd(l, x, pos, cos_t, sin_t, W):
    tag = f"l{l}"
    saved = {'x0': x}
    x1, a1, u1 = _ffn_fwd(x, W['norm_ffn1'][l], W['ffn1_gate'][l], W['ffn1_up'][l], W['ffn1_down'][l],
                          name=f"ffn1_fwd_{tag}")
    h = _rms_fwd(x1, W['norm_mix'][l], name=f"mixnorm_fwd_{tag}")
    proj = _mm(h, W['w_in'][l], 'nn', name=f"proj_{tag}", tm=512, tn=IN_COLS, tk=h.shape[1])
    qa, ka2, va2, qb, kb, vb, xc, gc = _split_rope(proj, cos_t, sin_t, name=f"split_{tag}")
    oa, lsea = _attn_fwd(qa, ka2, va2, A_MAX_DIST, name=f"attn_a_fwd_{tag}")
    obs, lsebs = [], []
    for bi, (window, d) in enumerate(B_BRANCHES):
        o, lse = _attn_fwd(_dilate(qb, d), _dilate(kb, d), _dilate(vb, d), window // d,
                           name=f"attn_b{bi}_fwd_{tag}")
        obs.append(o.reshape(qb.shape))
        lsebs.append(lse.reshape(qb.shape))
    oc, hs = _rglru_fwd(xc, gc, pos, W['conv_w'][l], W['conv_b'][l], W['rg_w_r'][l], W['rg_b_r'][l],
                        W['rg_w_i'][l], W['rg_b_i'][l], W['rg_lambda'][l], name=f"rglru_fwd_{tag}")
    mix = _combine_fwd(oa, lsea, W['sinks'][l], obs, lsebs, oc, name=f"combine_fwd_{tag}")
    x2 = _mm(mix, W['w_out'][l], 'nn', name=f"outproj_{tag}", tm=512, tn=x.shape[1], tk=MIX_WIDTH, res=x1)
    x3, a2, u2 = _ffn_fwd(x2, W['norm_ffn2'][l], W['ffn2_gate'][l], W['ffn2_up'][l], W['ffn2_down'][l],
                          name=f"ffn2_fwd_{tag}")
    saved.update(a1=a1, u1=u1, x1=x1, h=h, qa=qa, ka2=ka2, va2=va2, qb=qb, kb=kb, vb=vb, xc=xc, gc=gc, oa=oa,
                 lsea=lsea, obs=obs, lsebs=lsebs, hs=hs, mix=mix, x2=x2, a2=a2, u2=u2)
    return x3, saved


def _ffn_grads(tag, which, x, g, dy, a, u, wg, wu, wd):
    T, D = x.shape
    F = wg.shape[1]
    dx, dg, n, act, da, du = _ffn_bwd(x, g, dy, a, u, wg, wu, wd, name=f"{which}_bwd_{tag}")
    fc = _ffn_chunk(F)
    d_gate = _mm(n, da, 'tn', name=f"{which}_dgate_{tag}", tm=D, tn=fc, tk=512)
    d_up = _mm(n, du, 'tn', name=f"{which}_dup_{tag}", tm=D, tn=fc, tk=512)
    d_down = _mm(act, dy, 'tn', name=f"{which}_ddown_{tag}", tm=fc, tn=D, tk=512, alpha=0.5)
    return dx, dg, d_gate, d_up, d_down


def _layer_bwd(l, dx3, pos, cos_t, sin_t, W, S):
    tag = f"l{l}"
    G = {}
    dx2, G['norm_ffn2'], G['ffn2_gate'], G['ffn2_up'], G['ffn2_down'] = _ffn_grads(
        tag, 'ffn2', S['x2'], W['norm_ffn2'][l], dx3, S['a2'], S['u2'], W['ffn2_gate'][l], W['ffn2_up'][l],
        W['ffn2_down'][l])
    D = dx2.shape[1]
    dmix = _mm(dx2, W['w_out'][l], 'nt', name=f"outproj_dx_{tag}", tm=512, tn=MIX_WIDTH, tk=D)
    G['w_out'] = _mm(S['mix'], dx2, 'tn', name=f"outproj_dw_{tag}", tm=MIX_WIDTH, tn=D, tk=512)
    doa, dla, dobs, dlbs, doc, dsink = _combine_bwd(dmix, S['oa'], S['lsea'], W['sinks'][l], S['obs'], S['lsebs'],
                                                    name=f"combine_bwd_{tag}")
    G['attn_sinks'] = dsink.reshape(A_WIDTH // HEAD_DIM, HEAD_DIM)[:, 0]
    dqa, dka2, dva2 = _attn_bwd(S['qa'], S['ka2'], S['va2'], doa, S['lsea'], dla, A_MAX_DIST,
                                name=f"attn_a_bwd_{tag}")
    dqb, dkb, dvb = [], [], []
    shape = S['qb'].shape
    for bi, (window, d) in enumerate(B_BRANCHES):
        dq, dk, dv = _attn_bwd(_dilate(S['qb'], d), _dilate(S['kb'], d), _dilate(S['vb'], d), _dilate(dobs[bi], d),
                               _dilate(S['lsebs'][bi], d), _dilate(dlbs[bi], d), window // d,
                               name=f"attn_b{bi}_bwd_{tag}")
        dqb.append(dq.reshape(shape))
        dkb.append(dk.reshape(shape))
        dvb.append(dv.reshape(shape))
    (dxc, dgc, G['conv_w'], G['conv_b'], dwr, G['rg_b_r'], dwi, G['rg_b_i'], G['rg_lambda']) = _rglru_bwd(
        S['xc'], S['gc'], pos, S['hs'], doc, W['conv_w'][l], W['conv_b'][l], W['rg_w_r'][l], W['rg_b_r'][l],
        W['rg_w_i'][l], W['rg_b_i'][l], W['rg_lambda'][l], name=f"rglru_bwd_{tag}")
    G['rg_w_r'] = _diag_blocks(dwr)
    G['rg_w_i'] = _diag_blocks(dwi)
    dproj = _merge_dproj(dqa, dka2, dva2, dqb, dkb, dvb, dxc, dgc, cos_t, sin_t, name=f"merge_{tag}")
    dh = _mm(dproj, W['w_in'][l], 'nt', name=f"proj_dx_{tag}", tm=512, tn=D, tk=IN_COLS)
    G['w_in'] = _mm(S['h'], dproj, 'tn', name=f"proj_dw_{tag}", tm=D, tn=IN_COLS, tk=512)
    dx1, G['norm_mix'] = _rms_bwd(S['x1'], W['norm_mix'][l], dh, dx2, name=f"mixnorm_bwd_{tag}")
    dx0, G['norm_ffn1'], G['ffn1_gate'], G['ffn1_up'], G['ffn1_down'] = _ffn_grads(
        tag, 'ffn1', S['x0'], W['norm_ffn1'][l], dx1, S['a1'], S['u1'], W['ffn1_gate'][l], W['ffn1_up'][l],
        W['ffn1_down'][l])
    return dx0, G


def _device_step(x, positions, loss_target, W):
    T = x.shape[0]
    pos = positions.reshape(T, 1)
    cos_t, sin_t = _rope_tables(positions)
    saved = []
    for l in range(DEPTH):
        x, S = _layer_fwd(l, x, pos, cos_t, sin_t, W)
        saved.append(S)
    loss, dx, dg_final = _loss_head(x, W['norm_final'], loss_target, name="loss_head")
    grads = [None] * DEPTH
    for l in reversed(range(DEPTH)):
        dx, grads[l] = _layer_bwd(l, dx, pos, cos_t, sin_t, W, saved[l])
    return loss, dx, grads, dg_final


SHARD_AXIS = {'ffn1_gate': 2, 'ffn1_up': 2, 'ffn1_down': 1, 'w_in': 2, 'w_out': 1, 'ffn2_gate': 2, 'ffn2_up': 2,
              'ffn2_down': 1, 'conv_w': 2}


def kernel(x, positions, norm_ffn1, ffn1_gate, ffn1_up, ffn1_down, norm_mix, w_in, attn_sinks, conv_w, conv_b, rg_w_r, rg_b_r, rg_w_i, rg_b_i, rg_lambda, w_out, norm_ffn2, ffn2_gate, ffn2_up, ffn2_down, norm_final, loss_target, m_norm_ffn1, m_ffn1_gate, m_ffn1_up, m_ffn1_down, m_norm_mix, m_w_in, m_attn_sinks, m_conv_w, m_conv_b, m_rg_w_r, m_rg_b_r, m_rg_w_i, m_rg_b_i, m_rg_lambda, m_w_out, m_norm_ffn2, m_ffn2_gate, m_ffn2_up, m_ffn2_down, m_norm_final, v_norm_ffn1, v_ffn1_gate, v_ffn1_up, v_ffn1_down, v_norm_mix, v_w_in, v_attn_sinks, v_conv_w, v_conv_b, v_rg_w_r, v_rg_b_r, v_rg_w_i, v_rg_b_i, v_rg_lambda, v_w_out, v_norm_ffn2, v_ffn2_gate, v_ffn2_up, v_ffn2_down, v_norm_final):
    given = dict(locals())
    me = 4 * lax.axis_index("x") + 2 * lax.axis_index("y") + lax.axis_index("c")

    shards = [given[n].astype(BF16) for n in BIG_NAMES] + [conv_w]
    gathered = _exchange(shards, True, name="gather_weights")
    full = {n: g for n, g in zip(BIG_NAMES + ['conv_w'], gathered)}

    W = {}
    for n in BIG_NAMES:
        ax = SHARD_AXIS[n]
        W[n] = [_from_blocks(full[n][:, l], ax - 1) for l in range(DEPTH)]
    W['conv_w'] = [_from_blocks(full['conv_w'][:, l], 1) for l in range(DEPTH)]
    for n in ('norm_ffn1', 'norm_mix', 'norm_ffn2', 'conv_b', 'rg_lambda'):
        W[n] = [given[n][l][None, :] for l in range(DEPTH)]
    W['norm_final'] = norm_final[None, :]
    W['sinks'] = [jnp.repeat(attn_sinks[l], HEAD_DIM)[None, :] for l in range(DEPTH)]
    for n in ('rg_w_r', 'rg_w_i'):
        W[n] = [_block_diag(given[n][l]).astype(BF16) for l in range(DEPTH)]
    for n in ('rg_b_r', 'rg_b_i'):
        W[n] = [given[n][l].reshape(1, C_WIDTH) for l in range(DEPTH)]

    loss_part, grad_x, grads, dg_final = _device_step(x[0], positions[0], loss_target[0], W)
    loss = lax.psum(loss_part[0, 0], ("x", "y", "c"))

    sends = []
    for n in BIG_NAMES:
        stacked = jnp.stack([grads[l][n] for l in range(DEPTH)])
        sends.append(_to_blocks(stacked, SHARD_AXIS[n]).astype(BF16))
    parts = _exchange(sends, False, name="scatter_grads")

    small_shapes = [given[n].shape for n in SMALL_NAMES] + [(DEPTH, C_CONV, C_WIDTH)]
    small_grads = []
    for n in SMALL_NAMES:
        if n == 'norm_final':
            small_grads.append(dg_final.reshape(-1))
        else:
            small_grads.append(jnp.stack([grads[l][n].reshape(given[n].shape[1:]) for l in range(DEPTH)]))
    small_grads.append(jnp.stack([grads[l]['conv_w'] for l in range(DEPTH)]))
    small_parts = _exchange([_pack(small_grads)], True, name="gather_small_grads")[0]

    out = {}
    for n, p in zip(BIG_NAMES, parts):
        shape = given[n].shape
        two_d = (shape[0] * shape[1], shape[2])
        res = _adamw(p.reshape((N_DEV,) + two_d), given[n].reshape(two_d), given['m_' + n].reshape(two_d),
                     given['v_' + n].reshape(two_d), name=f"adamw_{n}")
        out[n] = [r.reshape(shape) for r in res]

    w_small = [given[n] for n in SMALL_NAMES]
    m_small = [given['m_' + n] for n in SMALL_NAMES]
    v_small = [given['v_' + n] for n in SMALL_NAMES]
    zeros_cw = jnp.zeros((DEPTH, C_CONV, C_WIDTH), F32)
    res = _adamw(small_parts, _pack(w_small + [zeros_cw]), _pack(m_small + [zeros_cw]), _pack(v_small + [zeros_cw]),
                 name="adamw_small", tr=8)
    unpacked = [_unpack(r, small_shapes) for r in res]
    for i, n in enumerate(SMALL_NAMES):
        out[n] = [u[i] for u in unpacked]

    k = conv_w.shape[2]
    g_cw = lax.dynamic_slice_in_dim(unpacked[0][-1], me * k, k, axis=2)
    zero_parts = jnp.zeros((N_DEV - 1,) + (8, LANES), F32)
    res = _adamw(jnp.concatenate([_pack([g_cw])[None], zero_parts]), _pack([conv_w]), _pack([m_conv_w]),
                 _pack([v_conv_w]), name="adamw_conv_w", tr=8)
    out['conv_w'] = [_unpack(r, [conv_w.shape])[0] for r in res]

    outputs = [loss, grad_x[None]]
    for i in range(4):
        outputs += [out[n][i] for n in WEIGHT_NAMES]
    return tuple(outputs)
```

```python
import functools
import math

import jax
import jax.numpy as jnp
from jax import lax
from jax.experimental import pallas as pl
from jax.experimental.pallas import tpu as pltpu

F32 = jnp.float32
BF16 = jnp.bfloat16

N_DEV = 8
DEPTH = 4
HEAD_DIM = 64
LANES = 128
QBLK = 128
A_WIDTH = 256
A_KV_WIDTH = 128
B_WIDTH = 384
C_WIDTH = 384
C_BLOCKS = 6
C_CONV = 4
C_EXP = 8.0
MIX_WIDTH = A_WIDTH + B_WIDTH + C_WIDTH
IN_COLS = A_WIDTH + 2 * A_KV_WIDTH + 3 * B_WIDTH + 2 * C_WIDTH
A_MAX_DIST = 127
B_BRANCHES = ((128, 1), (512, 4), (2048, 16))
ROPE_THETA = 10000.0
EPS = 1e-6
SCALE = HEAD_DIM ** -0.5

ADAM_LR = 0.001
ADAM_B1 = 0.9
ADAM_B2 = 0.999
ADAM_EPS = 1e-08
ADAM_WD = 0.01
ADAM_STEP = 10

ATTN_CHUNK = 1024
ATTN_UNROLL = 2
VMEM_LIMIT = 56 * 1024 * 1024

NT_DIMS = (((1,), (1,)), ((), ()))
TN_DIMS = (((0,), (0,)), ((), ()))
NN_DIMS = (((1,), (0,)), ((), ()))

WEIGHT_NAMES = ['norm_ffn1', 'ffn1_gate', 'ffn1_up', 'ffn1_down', 'norm_mix', 'w_in', 'attn_sinks', 'conv_w',
                'conv_b', 'rg_w_r', 'rg_b_r', 'rg_w_i', 'rg_b_i', 'rg_lambda', 'w_out', 'norm_ffn2', 'ffn2_gate',
                'ffn2_up', 'ffn2_down', 'norm_final']
BIG_NAMES = ['ffn1_gate', 'ffn1_up', 'ffn1_down', 'w_in', 'w_out', 'ffn2_gate', 'ffn2_up', 'ffn2_down']
SMALL_NAMES = ['norm_ffn1', 'norm_mix', 'norm_ffn2', 'norm_final', 'attn_sinks', 'conv_b', 'rg_w_r', 'rg_b_r',
               'rg_w_i', 'rg_b_i', 'rg_lambda']


def _params(sem, vmem=VMEM_LIMIT):
    return pltpu.CompilerParams(dimension_semantics=sem, vmem_limit_bytes=vmem)


def _dot(a, b, dims=NN_DIMS):
    return lax.dot_general(a, b, dims, preferred_element_type=F32)


def _sigmoid(x):
    return 1.0 / (1.0 + jnp.exp(-x))


def _mm(a, b, mode, *, name, tm=512, tn=512, tk=512, out_dtype=F32, alpha=1.0, res=None):
    if mode == 'nn':
        (M, K), N = a.shape, b.shape[1]
    elif mode == 'nt':
        (M, K), N = a.shape, b.shape[0]
    else:
        (K, M), N = a.shape, b.shape[1]
    tm, tn, tk = min(tm, M), min(tn, N), min(tk, K)
    ni, nj, nk = M // tm, N // tn, K // tk
    assert ni * tm == M and nj * tn == N and nk * tk == K, (name, a.shape, b.shape, tm, tn, tk)
    if mode == 'tn':
        a_spec = pl.BlockSpec((tk, tm), lambda j, i, k: (k, i))
    else:
        a_spec = pl.BlockSpec((tm, tk), lambda j, i, k: (i, k))
    if mode == 'nt':
        b_spec = pl.BlockSpec((tn, tk), lambda j, i, k: (j, k))
    else:
        b_spec = pl.BlockSpec((tk, tn), lambda j, i, k: (k, j))
    dims = {'nn': NN_DIMS, 'nt': NT_DIMS, 'tn': TN_DIMS}[mode]
    o_spec = pl.BlockSpec((tm, tn), lambda j, i, k: (i, j))
    has_res = res is not None

    def body(*refs):
        if has_res:
            a_ref, b_ref, r_ref, o_ref = refs[:4]
        else:
            a_ref, b_ref, o_ref = refs[:3]
        part = _dot(a_ref[...].astype(BF16), b_ref[...].astype(BF16), dims)

        def finish(acc):
            out = acc * alpha if alpha != 1.0 else acc
            if has_res:
                out = r_ref[...] + out
            o_ref[...] = out.astype(out_dtype)

        if nk == 1:
            finish(part)
        else:
            acc_ref = refs[-1]
            k = pl.program_id(2)

            @pl.when(k == 0)
            def _():
                acc_ref[...] = part

            @pl.when(k > 0)
            def _():
                acc_ref[...] += part

            @pl.when(k == nk - 1)
            def _():
                finish(acc_ref[...])

    in_specs = [a_spec, b_spec] + ([o_spec] if has_res else [])
    operands = [a, b] + ([res] if has_res else [])
    return pl.pallas_call(
        body, name=name, grid=(nj, ni, nk), in_specs=in_specs, out_specs=o_spec,
        out_shape=jax.ShapeDtypeStruct((M, N), out_dtype),
        scratch_shapes=[pltpu.VMEM((tm, tn), F32)] if nk > 1 else [],
        compiler_params=_params(("parallel", "parallel", "arbitrary")),
    )(*operands)


def _rms_fwd(x, g, *, name, tm=512):
    T, D = x.shape
    tm = min(tm, T)

    def body(x_ref, g_ref, o_ref):
        xv = x_ref[...]
        rstd = lax.rsqrt(jnp.mean(xv * xv, axis=-1, keepdims=True) + EPS)
        o_ref[...] = (xv * rstd * g_ref[...]).astype(BF16)

    return pl.pallas_call(
        body, name=name, grid=(T // tm,),
        in_specs=[pl.BlockSpec((tm, D), lambda i: (i, 0)), pl.BlockSpec((1, D), lambda i: (0, 0))],
        out_specs=pl.BlockSpec((tm, D), lambda i: (i, 0)),
        out_shape=jax.ShapeDtypeStruct((T, D), BF16),
        compiler_params=_params(("parallel",)),
    )(x, g)


def _rms_bwd_math(xv, g, dn):
    rstd = lax.rsqrt(jnp.mean(xv * xv, axis=-1, keepdims=True) + EPS)
    xhat = xv * rstd
    dxhat = dn * g
    dx = rstd * (dxhat - xhat * jnp.mean(dxhat * xhat, axis=-1, keepdims=True))
    return dx, jnp.sum(dn * xhat, axis=0, keepdims=True)


def _rms_bwd(x, g, dn, dres, *, name, tm=512):
    T, D = x.shape
    tm = min(tm, T)

    def body(x_ref, g_ref, dn_ref, dres_ref, dx_ref, dg_ref):
        dx, dg = _rms_bwd_math(x_ref[...], g_ref[...], dn_ref[...])
        dx_ref[...] = dres_ref[...] + dx

        @pl.when(pl.program_id(0) == 0)
        def _():
            dg_ref[...] = jnp.zeros_like(dg_ref)

        dg_ref[...] += dg

    row = pl.BlockSpec((tm, D), lambda i: (i, 0))
    vec = pl.BlockSpec((1, D), lambda i: (0, 0))
    return pl.pallas_call(
        body, name=name, grid=(T // tm,),
        in_specs=[row, vec, row, row], out_specs=[row, vec],
        out_shape=[jax.ShapeDtypeStruct((T, D), F32), jax.ShapeDtypeStruct((1, D), F32)],
        compiler_params=_params(("arbitrary",)),
    )(x, g, dn, dres)


def _loss_head(x, g, target, *, name, tm=512):
    T, D = x.shape
    tm = min(tm, T)

    def body(x_ref, g_ref, t_ref, loss_ref, dx_ref, dg_ref):
        xv = x_ref[...]
        g = g_ref[...]
        rstd = lax.rsqrt(jnp.mean(xv * xv, axis=-1, keepdims=True) + EPS)
        y = xv * rstd * g
        err = y - t_ref[...]
        part = 0.5 * jnp.sum(jnp.mean(err * err, axis=-1, keepdims=True), axis=0, keepdims=True)
        dx, dg = _rms_bwd_math(xv, g, err * (1.0 / D))
        dx_ref[...] = dx

        @pl.when(pl.program_id(0) == 0)
        def _():
            dg_ref[...] = jnp.zeros_like(dg_ref)
            loss_ref[...] = jnp.zeros_like(loss_ref)

        dg_ref[...] += dg
        loss_ref[...] += jnp.broadcast_to(part, loss_ref.shape)

    row = pl.BlockSpec((tm, D), lambda i: (i, 0))
    vec = pl.BlockSpec((1, D), lambda i: (0, 0))
    lspec = pl.BlockSpec((1, LANES), lambda i: (0, 0))
    return pl.pallas_call(
        body, name=name, grid=(T // tm,),
        in_specs=[row, vec, row], out_specs=[lspec, row, vec],
        out_shape=[jax.ShapeDtypeStruct((1, LANES), F32), jax.ShapeDtypeStruct((T, D), F32),
                   jax.ShapeDtypeStruct((1, D), F32)],
        compiler_params=_params(("arbitrary",)),
    )(x, g, target)


def _resident(shape):
    return pl.BlockSpec(shape, lambda i: (0,) * len(shape), pipeline_mode=pl.Buffered(1))


def _ffn_chunk(F):
    for c in (1408, 1024, 512, 256, 128):
        if F % c == 0:
            return c
    return F


def _ffn_fwd(x, g, wg, wu, wd, *, name, tm=256):
    T, D = x.shape
    F = wg.shape[1]
    tm = min(tm, T)
    fc = _ffn_chunk(F)

    def body(x_ref, g_ref, wg_ref, wu_ref, wd_ref, o_ref, a_ref, u_ref):
        xv = x_ref[...]
        rstd = lax.rsqrt(jnp.mean(xv * xv, axis=-1, keepdims=True) + EPS)
        n = (xv * rstd * g_ref[...]).astype(BF16)
        acc = jnp.zeros((tm, D), F32)
        for c in range(F // fc):
            sl = slice(c * fc, (c + 1) * fc)
            a = _dot(n, wg_ref[:, sl])
            u = _dot(n, wu_ref[:, sl])
            a_ref[:, sl] = a.astype(BF16)
            u_ref[:, sl] = u.astype(BF16)
            act = (a * _sigmoid(a) * u).astype(BF16)
            acc = acc + _dot(act, wd_ref[sl, :])
        o_ref[...] = xv + 0.5 * acc

    row = pl.BlockSpec((tm, D), lambda i: (i, 0))
    hid = pl.BlockSpec((tm, F), lambda i: (i, 0))
    return pl.pallas_call(
        body, name=name, grid=(T // tm,),
        in_specs=[row, pl.BlockSpec((1, D), lambda i: (0, 0)),
                  _resident((D, F)), _resident((D, F)), _resident((F, D))],
        out_specs=[row, hid, hid],
        out_shape=[jax.ShapeDtypeStruct((T, D), F32), jax.ShapeDtypeStruct((T, F), BF16),
                   jax.ShapeDtypeStruct((T, F), BF16)],
        compiler_params=_params(("parallel",)),
    )(x, g, wg, wu, wd)


def _ffn_bwd(x, g, dy, a, u, wg, wu, wd, *, name, tm=256):
    T, D = x.shape
    F = wg.shape[1]
    tm = min(tm, T)
    fc = _ffn_chunk(F)

    def body(x_ref, g_ref, dy_ref, a_ref, u_ref, wg_ref, wu_ref, wd_ref,
             dx_ref, dg_ref, n_ref, act_ref, da_ref, du_ref):
        xv = x_ref[...]
        g = g_ref[...]
        rstd = lax.rsqrt(jnp.mean(xv * xv, axis=-1, keepdims=True) + EPS)
        n_ref[...] = (xv * rstd * g).astype(BF16)
        dy = dy_ref[...]
        dyh = (0.5 * dy).astype(BF16)
        dn = jnp.zeros((tm, D), F32)
        for c in range(F // fc):
            sl = slice(c * fc, (c + 1) * fc)
            av = a_ref[:, sl].astype(F32)
            uv = u_ref[:, sl].astype(F32)
            dact = _dot(dyh, wd_ref[sl, :], NT_DIMS)
            s = _sigmoid(av)
            silu = av * s
            act_ref[:, sl] = (silu * uv).astype(BF16)
            da = (dact * uv * (s * (1.0 + av * (1.0 - s)))).astype(BF16)
            du = (dact * silu).astype(BF16)
            da_ref[:, sl] = da
            du_ref[:, sl] = du
            dn = dn + _dot(da, wg_ref[:, sl], NT_DIMS) + _dot(du, wu_ref[:, sl], NT_DIMS)
        dx, dg = _rms_bwd_math(xv, g, dn)
        dx_ref[...] = dy + dx

        @pl.when(pl.program_id(0) == 0)
        def _():
            dg_ref[...] = jnp.zeros_like(dg_ref)

        dg_ref[...] += dg

    row = pl.BlockSpec((tm, D), lambda i: (i, 0))
    hid = pl.BlockSpec((tm, F), lambda i: (i, 0))
    vec = pl.BlockSpec((1, D), lambda i: (0, 0))
    return pl.pallas_call(
        body, name=name, grid=(T // tm,),
        in_specs=[row, vec, row, hid, hid,
                  _resident((D, F)), _resident((D, F)), _resident((F, D))],
        out_specs=[row, vec, row, hid, hid, hid],
        out_shape=[jax.ShapeDtypeStruct((T, D), F32), jax.ShapeDtypeStruct((1, D), F32),
                   jax.ShapeDtypeStruct((T, D), BF16), jax.ShapeDtypeStruct((T, F), BF16),
                   jax.ShapeDtypeStruct((T, F), BF16), jax.ShapeDtypeStruct((T, F), BF16)],
        compiler_params=_params(("arbitrary",)),
    )(x, g, dy, a, u, wg, wu, wd)


def _lane_iota(shape):
    return lax.broadcasted_iota(jnp.int32, shape, 1)


def _rope_partner(x):
    first_half = (_lane_iota(x.shape) & (HEAD_DIM - 1)) < HEAD_DIM // 2
    return jnp.where(first_half, pltpu.roll(x, LANES - HEAD_DIM // 2, 1), pltpu.roll(x, HEAD_DIM // 2, 1))


def _swap_heads(x):
    return pltpu.roll(x, HEAD_DIM, 1)


def _split_rope(proj, cos_t, sin_t, *, name, tm=256):
    T = proj.shape[0]
    tm = min(tm, T)

    def body(p_ref, c_ref, s_ref, qa_ref, ka_ref, va_ref, qb_ref, kb_ref, vb_ref, xc_ref, gc_ref):
        cos = c_ref[...]
        sin = s_ref[...]

        def rope(x):
            return x * cos + _rope_partner(x) * sin

        lo = _lane_iota((tm, LANES)) < HEAD_DIM
        col = 0
        for j in range(A_WIDTH // LANES):
            qa_ref[:, j * LANES:(j + 1) * LANES] = (rope(p_ref[:, col:col + LANES]) * SCALE).astype(BF16)
            col += LANES
        kr = rope(p_ref[:, col:col + LANES])
        col += LANES
        vr = p_ref[:, col:col + LANES]
        col += LANES
        for src, dst in ((kr, ka_ref), (vr, va_ref)):
            sw = _swap_heads(src)
            dst[:, 0:LANES] = jnp.where(lo, src, sw).astype(BF16)
            dst[:, LANES:2 * LANES] = jnp.where(lo, sw, src).astype(BF16)
        for dst, roped, scale in ((qb_ref, True, SCALE), (kb_ref, True, 1.0), (vb_ref, False, 1.0)):
            for j in range(B_WIDTH // LANES):
                v = p_ref[:, col:col + LANES]
                if roped:
                    v = rope(v) * scale
                dst[:, j * LANES:(j + 1) * LANES] = v.astype(BF16)
                col += LANES
        xc_ref[...] = p_ref[:, col:col + C_WIDTH]
        gc_ref[...] = p_ref[:, col + C_WIDTH:col + 2 * C_WIDTH]

    def row(w):
        return pl.BlockSpec((tm, w), lambda i: (i, 0))

    widths = [A_WIDTH, A_WIDTH, A_WIDTH, B_WIDTH, B_WIDTH, B_WIDTH, C_WIDTH, C_WIDTH]
    dtypes = [BF16] * 6 + [F32] * 2
    return pl.pallas_call(
        body, name=name, grid=(T // tm,),
        in_specs=[row(IN_COLS), row(LANES), row(LANES)],
        out_specs=[row(w) for w in widths],
        out_shape=[jax.ShapeDtypeStruct((T, w), d) for w, d in zip(widths, dtypes)],
        compiler_params=_params(("parallel",)),
    )(proj, cos_t, sin_t)


def _merge_dproj(dqa, dka2, dva2, dqb, dkb, dvb, dxc, dgc, cos_t, sin_t, *, name, tm=256):
    T = dqa.shape[0]
    tm = min(tm, T)
    nb = len(dqb)

    def body(*refs):
        dqa_ref, dka_ref, dva_ref = refs[:3]
        dqb_refs = refs[3:3 + nb]
        dkb_refs = refs[3 + nb:3 + 2 * nb]
        dvb_refs = refs[3 + 2 * nb:3 + 3 * nb]
        dxc_ref, dgc_ref, c_ref, s_ref, o_ref = refs[3 + 3 * nb:]
        cos = c_ref[...]
        sin = s_ref[...]

        def rope_t(dy):
            return dy * cos - _rope_partner(dy) * sin

        lo = _lane_iota((tm, LANES)) < HEAD_DIM
        col = 0
        for j in range(A_WIDTH // LANES):
            o_ref[:, col:col + LANES] = (rope_t(dqa_ref[:, j * LANES:(j + 1) * LANES]) * SCALE).astype(BF16)
            col += LANES
        for src, roped in ((dka_ref, True), (dva_ref, False)):
            b0 = src[:, 0:LANES]
            b1 = src[:, LANES:2 * LANES]
            v = jnp.where(lo, b0 + _swap_heads(b0), b1 + _swap_heads(b1))
            if roped:
                v = rope_t(v)
            o_ref[:, col:col + LANES] = v.astype(BF16)
            col += LANES
        for group, roped, scale in ((dqb_refs, True, SCALE), (dkb_refs, True, 1.0), (dvb_refs, False, 1.0)):
            for j in range(B_WIDTH // LANES):
                sl = slice(j * LANES, (j + 1) * LANES)
                v = group[0][:, sl]
                for r in group[1:]:
                    v = v + r[:, sl]
                if roped:
                    v = rope_t(v) * scale
                o_ref[:, col:col + LANES] = v.astype(BF16)
                col += LANES
        o_ref[:, col:col + C_WIDTH] = dxc_ref[...].astype(BF16)
        o_ref[:, col + C_WIDTH:col + 2 * C_WIDTH] = dgc_ref[...].astype(BF16)

    def row(w):
        return pl.BlockSpec((tm, w), lambda i: (i, 0))

    ins = [dqa, dka2, dva2, *dqb, *dkb, *dvb, dxc, dgc, cos_t, sin_t]
    return pl.pallas_call(
        body, name=name, grid=(T // tm,),
        in_specs=[row(v.shape[1]) for v in ins],
        out_specs=row(IN_COLS),
        out_shape=jax.ShapeDtypeStruct((T, IN_COLS), BF16),
        compiler_params=_params(("parallel",)),
    )(*ins)


def _band_masks(max_dist):
    row = lax.broadcasted_iota(jnp.int32, (QBLK, 2 * QBLK), 0)
    key = lax.broadcasted_iota(jnp.int32, (QBLK, 2 * QBLK), 1)
    dist = row + QBLK - key
    wide = jnp.logical_and(dist >= 0, dist <= max_dist)
    return wide, wide[:, :QBLK], key >= QBLK


def _head_masks(rows=QBLK):
    lo = _lane_iota((rows, LANES)) < HEAD_DIM
    return lo, jnp.logical_not(lo)


def _keep(hm, x):
    return x * jnp.where(hm, 1.0, 0.0).astype(x.dtype)


def _head_col(x, hm):
    return jnp.max(jnp.where(hm, x, -jnp.inf), axis=1, keepdims=True)


def _attn_specs(R, C):
    chunk = min(ATTN_CHUNK, R)
    nb = chunk // QBLK
    nch = R // chunk
    main = pl.BlockSpec((chunk, LANES), lambda j, c: (c, j))
    prev = pl.BlockSpec((QBLK, LANES), lambda j, c: (jnp.maximum(c * nb - 1, 0), j))
    nxt = pl.BlockSpec((QBLK, LANES), lambda j, c: (jnp.minimum((c + 1) * nb, R // QBLK - 1), j))
    return chunk, nb, nch, main, prev, nxt


def _attn_fwd(q, k, v, max_dist, *, name):
    R, C = q.shape
    chunk, nb, nch, main, prev, _ = _attn_specs(R, C)

    def body(q_ref, k_ref, v_ref, kp_ref, vp_ref, o_ref, lse_ref):
        c = pl.program_id(1)
        wide_mask, _, own_block = _band_masks(max_dist)
        heads = _head_masks()

        def block(q_blk, kk, vv, mask):
            o_h, lse_h = [], []
            for hm in heads:
                s = jnp.where(mask, _dot(_keep(hm, q_blk), kk, NT_DIMS), -jnp.inf)
                m = jnp.max(jnp.maximum(s[:, :QBLK], s[:, QBLK:]), axis=1, keepdims=True)
                p = jnp.exp(s - m)
                l = jnp.sum(p[:, :QBLK] + p[:, QBLK:], axis=1, keepdims=True)
                o_h.append(_dot(p.astype(BF16), vv) / l)
                lse_h.append(jnp.broadcast_to(m + jnp.log(l), (QBLK, LANES)))
            return jnp.where(heads[0], o_h[0], o_h[1]), jnp.where(heads[0], lse_h[0], lse_h[1])

        first = pl.ds(0, QBLK)
        o0, l0 = block(q_ref[first, :], jnp.concatenate([kp_ref[...], k_ref[first, :]], axis=0),
                       jnp.concatenate([vp_ref[...], v_ref[first, :]], axis=0),
                       jnp.logical_and(wide_mask, jnp.logical_or(own_block, c > 0)))
        o_ref[first, :] = o0
        lse_ref[first, :] = l0

        def loop(qb, carry):
            cur = pl.ds(pl.multiple_of(qb * QBLK, QBLK), QBLK)
            both = pl.ds(pl.multiple_of((qb - 1) * QBLK, QBLK), 2 * QBLK)
            o, l = block(q_ref[cur, :], k_ref[both, :], v_ref[both, :], wide_mask)
            o_ref[cur, :] = o
            lse_ref[cur, :] = l
            return carry

        if nb > 1:
            lax.fori_loop(1, nb, loop, 0, unroll=ATTN_UNROLL)

    return pl.pallas_call(
        body, name=name, grid=(C // LANES, nch),
        in_specs=[main, main, main, prev, prev], out_specs=[main, main],
        out_shape=[jax.ShapeDtypeStruct((R, C), F32), jax.ShapeDtypeStruct((R, C), F32)],
        compiler_params=_params(("parallel", "parallel")),
    )(q, k, v, k, v)


def _attn_bwd(q, k, v, do, lse, delta, max_dist, *, name):
    R, C = q.shape
    chunk, nb, nch, main, prev, nxt = _attn_specs(R, C)

    def body(q_ref, k_ref, v_ref, do_ref, lse_ref, dl_ref, kp_ref, vp_ref, qn_ref, don_ref, lsen_ref, dln_ref,
             dq_ref, dk_ref, dv_ref):
        c = pl.program_id(1)
        wide_mask, prev_mask, own_block = _band_masks(max_dist)
        heads = _head_masks()

        def pair(q_blk, do_blk, lse_blk, dl_blk, kk, vv, mask, want_dq=True):
            dq = jnp.zeros((QBLK, LANES), F32)
            dkk = jnp.zeros(kk.shape, F32)
            dvv = jnp.zeros(kk.shape, F32)
            for hm, khm in zip(heads, _head_masks(kk.shape[0])):
                qh = _keep(hm, q_blk)
                doh = _keep(hm, do_blk)
                p = jnp.where(mask, jnp.exp(_dot(qh, kk, NT_DIMS) - _head_col(lse_blk, hm)), 0.0)
                ds = (p * (_dot(doh, vv, NT_DIMS) - _head_col(dl_blk, hm))).astype(BF16)
                if want_dq:
                    dq = dq + _dot(ds, _keep(khm, kk))
                dkk = dkk + _dot(ds, qh, TN_DIMS)
                dvv = dvv + _dot(p.astype(BF16), doh, TN_DIMS)
            return dq, dkk, dvv

        dk_ref[...] = jnp.zeros_like(dk_ref)
        dv_ref[...] = jnp.zeros_like(dv_ref)

        first = pl.ds(0, QBLK)
        dq0, dkk0, dvv0 = pair(q_ref[first, :], do_ref[first, :], lse_ref[first, :], dl_ref[first, :],
                               jnp.concatenate([kp_ref[...], k_ref[first, :]], axis=0),
                               jnp.concatenate([vp_ref[...], v_ref[first, :]], axis=0),
                               jnp.logical_and(wide_mask, jnp.logical_or(own_block, c > 0)))
        dq_ref[first, :] = dq0
        dk_ref[first, :] += dkk0[QBLK:, :]
        dv_ref[first, :] += dvv0[QBLK:, :]

        def loop(qb, carry):
            cur = pl.ds(pl.multiple_of(qb * QBLK, QBLK), QBLK)
            both = pl.ds(pl.multiple_of((qb - 1) * QBLK, QBLK), 2 * QBLK)
            dq, dkk, dvv = pair(q_ref[cur, :], do_ref[cur, :], lse_ref[cur, :], dl_ref[cur, :],
                                k_ref[both, :], v_ref[both, :], wide_mask)
            dq_ref[cur, :] = dq
            dk_ref[both, :] += dkk
            dv_ref[both, :] += dvv
            return carry

        if nb > 1:
            lax.fori_loop(1, nb, loop, 0)

        last = pl.ds((nb - 1) * QBLK, QBLK)
        _, dk_n, dv_n = pair(qn_ref[...], don_ref[...], lsen_ref[...], dln_ref[...], k_ref[last, :], v_ref[last, :],
                             jnp.logical_and(prev_mask, c < nch - 1), want_dq=False)
        dk_ref[last, :] += dk_n
        dv_ref[last, :] += dv_n

    return pl.pallas_call(
        body, name=name, grid=(C // LANES, nch),
        in_specs=[main] * 6 + [prev, prev] + [nxt] * 4, out_specs=[main, main, main],
        out_shape=[jax.ShapeDtypeStruct((R, C), F32)] * 3,
        compiler_params=_params(("parallel", "parallel")),
    )(q, k, v, do, lse, delta, k, v, q, do, lse, delta)


def _head_sum(x):
    r = lax.broadcasted_iota(jnp.int32, (LANES, LANES), 0) // HEAD_DIM
    c = lax.broadcasted_iota(jnp.int32, (LANES, LANES), 1) // HEAD_DIM
    ones = jnp.where(r == c, 1.0, 0.0).astype(BF16)
    outs = []
    for j in range(x.shape[1] // LANES):
        rem = x[:, j * LANES:(j + 1) * LANES]
        acc = jnp.zeros(rem.shape, F32)
        for _ in range(3):
            part = rem.astype(BF16)
            acc = acc + _dot(part, ones)
            rem = rem - part.astype(F32)
        outs.append(acc)
    return outs[0] if len(outs) == 1 else jnp.concatenate(outs, axis=1)


def _branch_weights(lses):
    m = functools.reduce(jnp.maximum, lses)
    es = [jnp.exp(l - m) for l in lses]
    den = functools.reduce(lambda a, b: a + b, es)
    return [e / den for e in es]


def _combine_fwd(oa, lsea, sink, obs, lsebs, oc, *, name, tm=256):
    T = oa.shape[0]
    tm = min(tm, T)
    nb = len(obs)

    def body(*refs):
        oa_ref, lsea_ref, sink_ref = refs[:3]
        ob_refs = refs[3:3 + nb]
        lse_refs = refs[3 + nb:3 + 2 * nb]
        oc_ref, out_ref = refs[3 + 2 * nb:]
        out_ref[:, 0:A_WIDTH] = (oa_ref[...] * _sigmoid(lsea_ref[...] - sink_ref[...])).astype(BF16)
        ws = _branch_weights([r[...] for r in lse_refs])
        ob = ob_refs[0][...] * ws[0]
        for r, w in zip(ob_refs[1:], ws[1:]):
            ob = ob + r[...] * w
        out_ref[:, A_WIDTH:A_WIDTH + B_WIDTH] = ob.astype(BF16)
        out_ref[:, A_WIDTH + B_WIDTH:MIX_WIDTH] = oc_ref[...].astype(BF16)

    def row(w):
        return pl.BlockSpec((tm, w), lambda i: (i, 0))

    ins = [oa, lsea, sink, *obs, *lsebs, oc]
    in_specs = [row(A_WIDTH), row(A_WIDTH), pl.BlockSpec((1, A_WIDTH), lambda i: (0, 0))]
    in_specs += [row(B_WIDTH)] * (2 * nb) + [row(C_WIDTH)]
    return pl.pallas_call(
        body, name=name, grid=(T // tm,), in_specs=in_specs, out_specs=row(MIX_WIDTH),
        out_shape=jax.ShapeDtypeStruct((T, MIX_WIDTH), BF16),
        compiler_params=_params(("parallel",)),
    )(*ins)


def _combine_bwd(dmix, oa, lsea, sink, obs, lsebs, *, name, tm=256):
    T = oa.shape[0]
    tm = min(tm, T)
    nb = len(obs)

    def body(*refs):
        dmix_ref, oa_ref, lsea_ref, sink_ref = refs[:4]
        ob_refs = refs[4:4 + nb]
        lse_refs = refs[4 + nb:4 + 2 * nb]
        outs = refs[4 + 2 * nb:]
        doa_ref, dla_ref = outs[:2]
        dob_refs = outs[2:2 + nb]
        dlb_refs = outs[2 + nb:2 + 2 * nb]
        doc_ref, dsink_ref = outs[2 + 2 * nb:]

        d_a = dmix_ref[:, 0:A_WIDTH]
        d_b = dmix_ref[:, A_WIDTH:A_WIDTH + B_WIDTH]
        doc_ref[...] = dmix_ref[:, A_WIDTH + B_WIDTH:MIX_WIDTH]

        gate = _sigmoid(lsea_ref[...] - sink_ref[...])
        doa_ref[...] = (d_a * gate).astype(BF16)
        dgate = _head_sum(d_a * oa_ref[...])
        dlse = dgate * gate * (1.0 - gate)
        dla_ref[...] = dgate * gate - dlse

        @pl.when(pl.program_id(0) == 0)
        def _():
            dsink_ref[...] = jnp.zeros_like(dsink_ref)

        dsink_ref[...] -= jnp.sum(dlse, axis=0, keepdims=True)

        ws = _branch_weights([r[...] for r in lse_refs])
        dws = [_head_sum(d_b * r[...]) for r in ob_refs]
        sw = ws[0] * dws[0]
        for w, dw in zip(ws[1:], dws[1:]):
            sw = sw + w * dw
        for w, do_ref, dl_ref in zip(ws, dob_refs, dlb_refs):
            do_ref[...] = (w * d_b).astype(BF16)
            dl_ref[...] = w * sw

    def row(w):
        return pl.BlockSpec((tm, w), lambda i: (i, 0))

    vec = pl.BlockSpec((1, A_WIDTH), lambda i: (0, 0))
    ins = [dmix, oa, lsea, sink, *obs, *lsebs]
    in_specs = [row(MIX_WIDTH), row(A_WIDTH), row(A_WIDTH), vec] + [row(B_WIDTH)] * (2 * nb)
    out_specs = [row(A_WIDTH), row(A_WIDTH)] + [row(B_WIDTH)] * (2 * nb) + [row(C_WIDTH), vec]
    out_shape = [jax.ShapeDtypeStruct((T, A_WIDTH), BF16), jax.ShapeDtypeStruct((T, A_WIDTH), F32)]
    out_shape += [jax.ShapeDtypeStruct((T, B_WIDTH), BF16)] * nb + [jax.ShapeDtypeStruct((T, B_WIDTH), F32)] * nb
    out_shape += [jax.ShapeDtypeStruct((T, C_WIDTH), F32), jax.ShapeDtypeStruct((1, A_WIDTH), F32)]
    res = pl.pallas_call(
        body, name=name, grid=(T // tm,), in_specs=in_specs, out_specs=out_specs, out_shape=out_shape,
        compiler_params=_params(("arbitrary",)),
    )(*ins)
    return res[0], res[1], list(res[2:2 + nb]), list(res[2 + nb:2 + 2 * nb]), res[2 + 2 * nb], res[3 + 2 * nb]


HIST = 8


def _softplus_neg(lam):
    e = jnp.exp(-jnp.abs(lam))
    log1p = jnp.where(e < 0.01, e * (1.0 - e * (0.5 - e * (1.0 / 3.0))), jnp.log(1.0 + e))
    return jnp.maximum(-lam, 0.0) + log1p


def _neg_expm1(x):
    series = -x * (1.0 + x * (0.5 + x * (1.0 / 6.0 + x * (1.0 / 24.0 + x * (1.0 / 120.0)))))
    return jnp.where(x > -0.1, series, 1.0 - jnp.exp(x))


def _gelu_parts(x):
    k = math.sqrt(2.0 / math.pi)
    t = jnp.tanh(k * (x + 0.044715 * (x * x * x)))
    cdf = 0.5 * (1.0 + t)
    return x * cdf, cdf + 0.5 * x * (1.0 - t * t) * (k * (1.0 + 3.0 * 0.044715 * (x * x)))


def _rglru_gates(y, pos_ref, wr_ref, br_ref, wi_ref, bi_ref, lam_ref):
    yb = y.astype(BF16)
    r = _sigmoid(_dot(yb, wr_ref[...]) + br_ref[...])
    ig = _sigmoid(_dot(yb, wi_ref[...]) + bi_ref[...])
    sp = _softplus_neg(lam_ref[...])
    log_a = -C_EXP * r * sp
    reset = pos_ref[...] == 0
    a = jnp.where(reset, 0.0, jnp.exp(log_a))
    mult = jnp.where(reset, 1.0, jnp.sqrt(_neg_expm1(2.0 * log_a)))
    return yb, r, ig, sp, reset, a, mult


def _conv_fwd(xs_ref, cw_ref, cb_ref, tm):
    y = cb_ref[...] + cw_ref[0:1, :] * xs_ref[HIST:HIST + tm, :]
    for j in range(1, C_CONV):
        y = y + cw_ref[j:j + 1, :] * xs_ref[HIST - j:HIST - j + tm, :]
    return y


def _rglru_fwd(xc, gc, pos, cw, cb, wr, br, wi, bi, lam, *, name, tm=512):
    T, W = xc.shape
    tm = min(tm, T)

    def body(xc_ref, gc_ref, pos_ref, cw_ref, cb_ref, wr_ref, br_ref, wi_ref, bi_ref, lam_ref,
             out_ref, hs_ref, xs, a_s, b_s, h_s):
        @pl.when(pl.program_id(0) == 0)
        def _():
            xs[0:HIST, :] = jnp.zeros((HIST, W), F32)
            h_s[...] = jnp.zeros_like(h_s)

        xv = xc_ref[...]
        xs[HIST:HIST + tm, :] = xv
        y = _conv_fwd(xs, cw_ref, cb_ref, tm)
        xs[0:HIST, :] = xv[tm - HIST:tm, :]
        _, _, ig, _, _, a, mult = _rglru_gates(y, pos_ref, wr_ref, br_ref, wi_ref, bi_ref, lam_ref)
        a_s[...] = a
        b_s[...] = mult * (ig * y)

        def step(t, h):
            row = pl.ds(t, 1)
            h = a_s[row, :] * h + b_s[row, :]
            b_s[row, :] = h
            return h

        h_s[...] = lax.fori_loop(0, tm, step, h_s[...], unroll=8)
        hs = b_s[...]
        hs_ref[...] = hs
        out_ref[...] = hs * _gelu_parts(gc_ref[...])[0]

    row = pl.BlockSpec((tm, W), lambda i: (i, 0))
    full = lambda shape: pl.BlockSpec(shape, lambda i: (0,) * len(shape))
    return pl.pallas_call(
        body, name=name, grid=(T // tm,),
        in_specs=[row, row, pl.BlockSpec((tm, 1), lambda i: (i, 0)), full((C_CONV, W)), full((1, W)),
                  full((W, W)), full((1, W)), full((W, W)), full((1, W)), full((1, W))],
        out_specs=[row, row],
        out_shape=[jax.ShapeDtypeStruct((T, W), F32)] * 2,
        scratch_shapes=[pltpu.VMEM((tm + HIST, W), F32), pltpu.VMEM((tm, W), F32), pltpu.VMEM((tm, W), F32),
                        pltpu.VMEM((1, W), F32)],
        compiler_params=_params(("arbitrary",)),
    )(xc, gc, pos, cw, cb, wr, br, wi, bi, lam)


def _rglru_bwd(xc, gc, pos, hs, dout, cw, cb, wr, br, wi, bi, lam, *, name, tm=512):
    T, W = xc.shape
    tm = min(tm, T)
    nt = T // tm
    hb = tm // HIST

    def body(xc_ref, gc_ref, pos_ref, hs_ref, dout_ref, xch_ref, hsh_ref,
             cw_ref, cb_ref, wr_ref, br_ref, wi_ref, bi_ref, lam_ref,
             dxc_ref, dgc_ref, dcw_ref, dcb_ref, dwr_ref, dbr_ref, dwi_ref, dbi_ref, dlam_ref,
             xs, hsx, dys, a_s, d_s, carry_s):
        i = pl.program_id(0)

        @pl.when(i == 0)
        def _():
            for r in (dcw_ref, dcb_ref, dwr_ref, dbr_ref, dwi_ref, dbi_ref, dlam_ref, carry_s):
                r[...] = jnp.zeros_like(r)
            dys[tm:tm + HIST, :] = jnp.zeros((HIST, W), F32)

        has_prev = i < nt - 1
        xs[0:HIST, :] = jnp.where(has_prev, xch_ref[...], 0.0)
        hsx[0:HIST, :] = jnp.where(has_prev, hsh_ref[...], 0.0)
        xs[HIST:HIST + tm, :] = xc_ref[...]
        hs = hs_ref[...]
        hsx[HIST:HIST + tm, :] = hs
        y = _conv_fwd(xs, cw_ref, cb_ref, tm)
        yb, r, ig, sp, reset, a, mult = _rglru_gates(y, pos_ref, wr_ref, br_ref, wi_ref, bi_ref, lam_ref)

        gelu, dgelu = _gelu_parts(gc_ref[...])
        dout = dout_ref[...]
        dgc_ref[...] = dout * hs * dgelu
        a_s[...] = a
        d_s[...] = dout * gelu

        def step(k, carry):
            row = pl.ds(tm - 1 - k, 1)
            dh = d_s[row, :] + carry
            d_s[row, :] = dh
            return a_s[row, :] * dh

        carry_s[...] = lax.fori_loop(0, tm, step, carry_s[...], unroll=8)
        dh = d_s[...]
        hprev = hsx[HIST - 1:HIST - 1 + tm, :]
        igy = ig * y
        dmult = dh * igy
        digy = dh * mult
        dlog_a = jnp.where(reset, 0.0, dh * hprev * a - dmult * a * a / mult)
        dlam_ref[...] += jnp.sum(dlog_a * (C_EXP * r) * _sigmoid(-lam_ref[...]), axis=0, keepdims=True)
        dz_r = dlog_a * (-C_EXP * sp) * r * (1.0 - r)
        dz_i = digy * y * ig * (1.0 - ig)
        dzr_b = dz_r.astype(BF16)
        dzi_b = dz_i.astype(BF16)
        dy = digy * ig + _dot(dzr_b, wr_ref[...], NT_DIMS) + _dot(dzi_b, wi_ref[...], NT_DIMS)
        dwr_ref[...] += _dot(yb, dzr_b, TN_DIMS)
        dwi_ref[...] += _dot(yb, dzi_b, TN_DIMS)
        dbr_ref[...] += jnp.sum(dz_r, axis=0, keepdims=True)
        dbi_ref[...] += jnp.sum(dz_i, axis=0, keepdims=True)

        dys[0:tm, :] = dy
        dxc = cw_ref[0:1, :] * dy
        for j in range(1, C_CONV):
            dxc = dxc + cw_ref[j:j + 1, :] * dys[j:j + tm, :]
        dxc_ref[...] = dxc
        dys[tm:tm + HIST, :] = dy[0:HIST, :]
        dcb_ref[...] += jnp.sum(dy, axis=0, keepdims=True)
        for j in range(C_CONV):
            dcw_ref[j:j + 1, :] += jnp.sum(dy * xs[HIST - j:HIST - j + tm, :], axis=0, keepdims=True)

    row = pl.BlockSpec((tm, W), lambda i: (nt - 1 - i, 0))
    halo = pl.BlockSpec((HIST, W), lambda i: (jnp.maximum((nt - 1 - i) * hb - 1, 0), 0))
    full = lambda shape: pl.BlockSpec(shape, lambda i: (0,) * len(shape))
    out_specs = [row, row, full((C_CONV, W)), full((1, W)), full((W, W)), full((1, W)), full((W, W)), full((1, W)),
                 full((1, W))]
    out_shape = [jax.ShapeDtypeStruct((T, W), F32)] * 2
    out_shape += [jax.ShapeDtypeStruct(s, F32) for s in ((C_CONV, W), (1, W), (W, W), (1, W), (W, W), (1, W), (1, W))]
    return pl.pallas_call(
        body, name=name, grid=(nt,),
        in_specs=[row, row, pl.BlockSpec((tm, 1), lambda i: (nt - 1 - i, 0)), row, row, halo, halo,
                  full((C_CONV, W)), full((1, W)), full((W, W)), full((1, W)), full((W, W)), full((1, W)),
                  full((1, W))],
        out_specs=out_specs, out_shape=out_shape,
        scratch_shapes=[pltpu.VMEM((tm + HIST, W), F32), pltpu.VMEM((tm + HIST, W), F32),
                        pltpu.VMEM((tm + HIST, W), F32), pltpu.VMEM((tm, W), F32), pltpu.VMEM((tm, W), F32),
                        pltpu.VMEM((1, W), F32)],
        compiler_params=_params(("arbitrary",)),
    )(xc, gc, pos, hs, dout, xc, hs, cw, cb, wr, br, wi, bi, lam)


def _adam_math(w, g, m, v):
    m = ADAM_B1 * m + (1.0 - ADAM_B1) * g
    v = ADAM_B2 * v + (1.0 - ADAM_B2) * (g * g)
    m_hat = m / (1.0 - ADAM_B1 ** ADAM_STEP)
    v_hat = v / (1.0 - ADAM_B2 ** ADAM_STEP)
    delta = -ADAM_LR * (m_hat / (jnp.sqrt(v_hat) + ADAM_EPS) + ADAM_WD * w)
    return delta, m, v


def _pick_rows(R, cap=512, mult=16):
    for d in range(min(cap, R), 0, -1):
        if R % d == 0 and d % mult == 0:
            return d
    return R


def _adamw(parts, w, m, v, *, name, tr=None, part=0, prev=None):
    R, C = w.shape
    r = parts.shape[1]
    tr = _pick_rows(r) if tr is None else tr
    assert r % tr == 0 and R % r == 0, (name, R, r, tr)
    nt = r // tr

    def body(p_ref, w_ref, m_ref, v_ref, *rest):
        g_ref, d_ref, nm_ref, nv_ref = rest[-4:]
        g = p_ref[0].astype(F32)
        for d in range(1, N_DEV):
            g = g + p_ref[d].astype(F32)
        delta, nm, nv = _adam_math(w_ref[...], g, m_ref[...], v_ref[...])
        g_ref[...] = g
        d_ref[...] = delta
        nm_ref[...] = nm
        nv_ref[...] = nv

    row = pl.BlockSpec((tr, C), lambda i: (part * nt + i, 0))
    in_specs = [pl.BlockSpec((N_DEV, tr, C), lambda i: (0, i, 0)), row, row, row]
    operands = [parts, w, m, v]
    aliases = {}
    if prev is not None:
        in_specs += [pl.BlockSpec(memory_space=pl.ANY)] * 4
        operands += list(prev)
        aliases = {4 + i: i for i in range(4)}
    return pl.pallas_call(
        body, name=name, grid=(nt,), in_specs=in_specs,
        out_specs=[row] * 4, out_shape=[jax.ShapeDtypeStruct((R, C), F32)] * 4,
        input_output_aliases=aliases,
        compiler_params=_params(("parallel",)),
    )(*operands)


def _exchange(srcs, gather, *, name):
    n = len(srcs)
    out_shape = [jax.ShapeDtypeStruct((N_DEV,) + s.shape if gather else s.shape, s.dtype) for s in srcs]

    def body(*refs):
        ins, outs = refs[:n], refs[n:2 * n]
        send_sems, recv_sems, local_sems = refs[2 * n:]
        x, y, c = lax.axis_index("x"), lax.axis_index("y"), lax.axis_index("c")
        me = 4 * x + 2 * y + c
        local_copies, sends, arrivals = [], [], []
        for a in range(n):
            mine = ins[a] if gather else ins[a].at[me]
            local = pltpu.make_async_copy(mine, outs[a].at[me], local_sems.at[a])
            local.start()
            local_copies.append(local)
            for k in range(1, N_DEV):
                px, py, pc = x ^ ((k >> 2) & 1), y ^ ((k >> 1) & 1), c ^ (k & 1)
                peer = 4 * px + 2 * py + pc
                send = pltpu.make_async_remote_copy(
                    src_ref=ins[a] if gather else ins[a].at[peer], dst_ref=outs[a].at[me],
                    send_sem=send_sems.at[a * N_DEV + k], recv_sem=recv_sems.at[a * N_DEV + k],
                    device_id=(px, py, pc), device_id_type=pl.DeviceIdType.MESH)
                send.start()
                sends.append(send)
                arrivals.append(pltpu.make_async_remote_copy(
                    src_ref=mine, dst_ref=outs[a].at[peer],
                    send_sem=send_sems.at[a * N_DEV + k], recv_sem=recv_sems.at[a * N_DEV + k],
                    device_id=(px, py, pc), device_id_type=pl.DeviceIdType.MESH))
        for cp in sends:
            cp.wait_send()
        for cp in arrivals:
            cp.wait_recv()
        for cp in local_copies:
            cp.wait()

    return pl.pallas_call(
        body, name=name,
        in_specs=[pl.BlockSpec(memory_space=pl.ANY)] * n, out_specs=[pl.BlockSpec(memory_space=pl.ANY)] * n,
        out_shape=out_shape,
        scratch_shapes=[pltpu.SemaphoreType.DMA((n * N_DEV,)), pltpu.SemaphoreType.DMA((n * N_DEV,)),
                        pltpu.SemaphoreType.DMA((n,))],
    )(*srcs)


_HBM = pl.BlockSpec(memory_space=pltpu.HBM)
_SEM = pl.BlockSpec(memory_space=pltpu.SEMAPHORE)
_EFFECT = pltpu.SideEffectType.DATAFLOW_SIDE_EFFECTING


def _peers():
    x, y, c = lax.axis_index("x"), lax.axis_index("y"), lax.axis_index("c")
    out = []
    for k in range(1, N_DEV):
        px, py, pc = x ^ ((k >> 2) & 1), y ^ ((k >> 1) & 1), c ^ (k & 1)
        out.append((k, (px, py, pc), 4 * px + 2 * py + pc))
    return 4 * x + 2 * y + c, out


def _split_copies(src_refs, land_refs, send_sems, recv_sems, gather):
    me, peers = _peers()
    out = []
    for a, (src_ref, land_ref) in enumerate(zip(src_refs, land_refs)):
        for k, dev, blk in peers:
            common = dict(send_sem=send_sems.at[a * N_DEV + k], recv_sem=recv_sems.at[a * N_DEV + k], device_id=dev,
                          device_id_type=pl.DeviceIdType.MESH)
            src = src_ref if gather else src_ref.at[blk]
            out.append((pltpu.make_async_remote_copy(src_ref=src, dst_ref=land_ref.at[me], **common),
                        pltpu.make_async_remote_copy(src_ref=src, dst_ref=land_ref.at[blk], **common)))
    return out


def _exchange_start(srcs, gather, *, name):
    n = len(srcs)
    lands = [lax.empty((N_DEV,) + (s.shape if gather else s.shape[1:]), s.dtype) for s in srcs]

    def body(*refs):
        src_refs, land_refs = refs[:n], refs[n:2 * n]
        send_sems, recv_sems = refs[2 * n:2 * n + 2]
        token = refs[-1]
        for outgoing, _ in _split_copies(src_refs, land_refs, send_sems, recv_sems, gather):
            outgoing.start()
        token[...] = jnp.zeros_like(token)

    res = pl.pallas_call(
        body, name=name,
        out_shape=(pltpu.SemaphoreType.DMA((n * N_DEV,)), pltpu.SemaphoreType.DMA((n * N_DEV,)),
                   *[pltpu.HBM(a.shape, a.dtype) for a in srcs + lands], jax.ShapeDtypeStruct((8, LANES), F32)),
        in_specs=(_HBM,) * (2 * n),
        out_specs=(_SEM, _SEM) + (_HBM,) * (2 * n) + (pl.BlockSpec(memory_space=pltpu.VMEM),),
        input_output_aliases={i: i + 2 for i in range(2 * n)},
        compiler_params=pltpu.CompilerParams(has_side_effects=_EFFECT),
    )(*[pltpu.with_memory_space_constraint(a, pltpu.HBM) for a in srcs + lands])
    return res[0], res[1], list(res[2:2 + n]), list(res[2 + n:2 + 2 * n]), res[-1]


def _exchange_wait(started, after, gather, *, name):
    send_sems, recv_sems, srcs, lands, _ = started
    n = len(srcs)

    def body(*refs):
        src_refs, land_refs = refs[:n], refs[n:2 * n]
        send_sems, recv_sems = refs[2 * n:2 * n + 2]
        for outgoing, incoming in _split_copies(src_refs, land_refs, send_sems, recv_sems, gather):
            outgoing.wait_send()
            incoming.wait_recv()

    res = pl.pallas_call(
        body, name=name,
        out_shape=tuple(pltpu.HBM(a.shape, a.dtype) for a in srcs + lands),
        in_specs=(_HBM,) * (2 * n) + (_SEM, _SEM, pl.BlockSpec(memory_space=pl.ANY)), out_specs=(_HBM,) * (2 * n),
        input_output_aliases={i: i for i in range(2 * n)},
        compiler_params=pltpu.CompilerParams(has_side_effects=_EFFECT),
    )(*srcs, *lands, send_sems, recv_sems, after)
    return list(res[:n]), list(res[n:])


def _cols_to_blocks(g, *, name, tr=128):
    R, C = g.shape
    w = C // N_DEV
    tr = min(tr, R)

    def body(g_ref, o_ref):
        for p in range(N_DEV):
            o_ref[p] = g_ref[:, p * w:(p + 1) * w].astype(BF16)

    return pl.pallas_call(
        body, name=name, grid=(R // tr,),
        in_specs=[pl.BlockSpec((tr, C), lambda i: (i, 0))],
        out_specs=pl.BlockSpec((N_DEV, tr, w), lambda i: (0, i, 0)),
        out_shape=jax.ShapeDtypeStruct((N_DEV, R, w), BF16),
        compiler_params=_params(("parallel",)),
    )(g)


def _blocks_to_cols(b, *, name, tr=128):
    _, R, w = b.shape
    tr = min(tr, R)

    def body(b_ref, o_ref):
        o_ref[...] = jnp.concatenate([b_ref[p].astype(F32) for p in range(N_DEV)], axis=1).astype(o_ref.dtype)

    return pl.pallas_call(
        body, name=name, grid=(R // tr,),
        in_specs=[pl.BlockSpec((N_DEV, tr, w), lambda i: (0, i, 0))],
        out_specs=pl.BlockSpec((tr, N_DEV * w), lambda i: (i, 0)),
        out_shape=jax.ShapeDtypeStruct((R, N_DEV * w), b.dtype),
        compiler_params=_params(("parallel",)),
    )(b)


def _pack_rows(arrays, *, name, pick=None):
    B, _, w = arrays[0].shape
    rows = [a.shape[1] for a in arrays]
    first = 0
    if pick is not None:
        B, first = 1, pick

    def body(*refs):
        o_ref = refs[-1]
        r = 0
        for a_ref, n in zip(refs[:-1], rows):
            o_ref[0, r:r + n, :] = a_ref[0].astype(BF16)
            r += n

    return pl.pallas_call(
        body, name=name, grid=(B,),
        in_specs=[pl.BlockSpec((1, n, w), lambda b: (first + b, 0, 0)) for n in rows],
        out_specs=pl.BlockSpec((1, sum(rows), w), lambda b: (b, 0, 0)),
        out_shape=jax.ShapeDtypeStruct((B, sum(rows), w), BF16),
        compiler_params=_params(("parallel",)),
    )(*arrays)


def _unpack_rows(land, src, rows, *, name):
    _, R, w = land.shape
    src_spec = (pl.BlockSpec((1, R, w), lambda p: (p, 0, 0)) if src.ndim == 3
                else pl.BlockSpec((R, w), lambda p: (0, 0)))

    def body(land_ref, src_ref, *o_refs):
        me = 4 * lax.axis_index("x") + 2 * lax.axis_index("y") + lax.axis_index("c")
        mine = pl.program_id(0) == me
        r = 0
        for o_ref, n in zip(o_refs, rows):
            rows_i = slice(r, r + n)

            @pl.when(mine)
            def _(o_ref=o_ref, rows_i=rows_i):
                o_ref[0] = src_ref[0, rows_i, :] if src.ndim == 3 else src_ref[rows_i, :]

            @pl.when(jnp.logical_not(mine))
            def _(o_ref=o_ref, rows_i=rows_i):
                o_ref[0] = land_ref[0, rows_i, :]

            r += n

    return pl.pallas_call(
        body, name=name, grid=(N_DEV,),
        in_specs=[pl.BlockSpec((1, R, w), lambda p: (p, 0, 0)), src_spec],
        out_specs=[pl.BlockSpec((1, n, w), lambda p: (p, 0, 0)) for n in rows],
        out_shape=[jax.ShapeDtypeStruct((N_DEV, n, w), land.dtype) for n in rows],
        compiler_params=_params(("parallel",)),
    )(land, src)


def _to_blocks(w, axis):
    shape = w.shape
    k = shape[axis] // N_DEV
    w = w.reshape(shape[:axis] + (N_DEV, k) + shape[axis + 1:])
    return jnp.moveaxis(w, axis, 0)


def _from_blocks(wb, axis):
    w = jnp.moveaxis(wb, 0, axis)
    shape = w.shape
    return w.reshape(shape[:axis] + (shape[axis] * shape[axis + 1],) + shape[axis + 2:])


def _block_diag(w):
    n, k, _ = w.shape
    eye = jnp.eye(n, dtype=w.dtype)
    return (eye[:, None, :, None] * w[:, :, None, :]).reshape(n * k, n * k)


def _diag_blocks(wd):
    k = HEAD_DIM
    return jnp.stack([wd[h * k:(h + 1) * k, h * k:(h + 1) * k] for h in range(C_BLOCKS)])


def _pack(arrays):
    rows = []
    for a in arrays:
        flat = a.reshape(-1).astype(F32)
        pad = (-flat.shape[0]) % LANES
        rows.append(jnp.pad(flat, (0, pad)).reshape(-1, LANES))
    out = jnp.concatenate(rows, axis=0)
    return jnp.pad(out, ((0, (-out.shape[0]) % 8), (0, 0)))


def _unpack(packed, shapes):
    outs, r = [], 0
    for s in shapes:
        size = math.prod(s)
        nrows = -(-size // LANES)
        outs.append(packed[r:r + nrows].reshape(-1)[:size].reshape(s))
        r += nrows
    return outs


def _rope_tables(positions):
    inv = 1.0 / (ROPE_THETA ** (jnp.arange(0, HEAD_DIM, 2, dtype=F32) / HEAD_DIM))
    ang = positions.astype(F32)[:, None] * inv
    cos, sin = jnp.cos(ang), jnp.sin(ang)
    return jnp.tile(cos, (1, 4)), jnp.tile(jnp.concatenate([-sin, sin], axis=1), (1, 2))


def _dilate(t, d):
    return t.reshape(t.shape[0] // d, d * t.shape[1])


def _layer_fwd(l, x, pos, cos_t, sin_t, W):
    tag = f"l{l}"
    saved = {'x0': x}
    x1, a1, u1 = _ffn_fwd(x, W['norm_ffn1'][l], W['ffn1_gate'][l], W['ffn1_up'][l], W['ffn1_down'][l],
                          name=f"ffn1_fwd_{tag}")
    h = _rms_fwd(x1, W['norm_mix'][l], name=f"mixnorm_fwd_{tag}")
    proj = _mm(h, W['w_in'][l], 'nn', name=f"proj_{tag}", tm=512, tn=IN_COLS, tk=h.shape[1])
    qa, ka2, va2, qb, kb, vb, xc, gc = _split_rope(proj, cos_t, sin_t, name=f"split_{tag}")
    oa, lsea = _attn_fwd(qa, ka2, va2, A_MAX_DIST, name=f"attn_a_fwd_{tag}")
    obs, lsebs = [], []
    for bi, (window, d) in enumerate(B_BRANCHES):
        o, lse = _attn_fwd(_dilate(qb, d), _dilate(kb, d), _dilate(vb, d), window // d,
                           name=f"attn_b{bi}_fwd_{tag}")
        obs.append(o.reshape(qb.shape))
        lsebs.append(lse.reshape(qb.shape))
    oc, hs = _rglru_fwd(xc, gc, pos, W['conv_w'][l], W['conv_b'][l], W['rg_w_r'][l], W['rg_b_r'][l],
                        W['rg_w_i'][l], W['rg_b_i'][l], W['rg_lambda'][l], name=f"rglru_fwd_{tag}")
    mix = _combine_fwd(oa, lsea, W['sinks'][l], obs, lsebs, oc, name=f"combine_fwd_{tag}")
    x2 = _mm(mix, W['w_out'][l], 'nn', name=f"outproj_{tag}", tm=512, tn=x.shape[1], tk=MIX_WIDTH, res=x1)
    x3, a2, u2 = _ffn_fwd(x2, W['norm_ffn2'][l], W['ffn2_gate'][l], W['ffn2_up'][l], W['ffn2_down'][l],
                          name=f"ffn2_fwd_{tag}")
    saved.update(a1=a1, u1=u1, x1=x1, h=h, qa=qa, ka2=ka2, va2=va2, qb=qb, kb=kb, vb=vb, xc=xc, gc=gc, oa=oa,
                 lsea=lsea, obs=obs, lsebs=lsebs, hs=hs, mix=mix, x2=x2, a2=a2, u2=u2)
    return x3, saved


def _ffn_grads(tag, which, x, g, dy, a, u, wg, wu, wd):
    T, D = x.shape
    F = wg.shape[1]
    dx, dg, n, act, da, du = _ffn_bwd(x, g, dy, a, u, wg, wu, wd, name=f"{which}_bwd_{tag}")
    fc = _ffn_chunk(F)
    d_gate = _mm(n, da, 'tn', name=f"{which}_dgate_{tag}", tm=D, tn=fc, tk=2048)
    d_up = _mm(n, du, 'tn', name=f"{which}_dup_{tag}", tm=D, tn=fc, tk=2048)
    d_down = _mm(act, dy, 'tn', name=f"{which}_ddown_{tag}", tm=fc, tn=D, tk=1024, alpha=0.5)
    return dx, dg, d_gate, d_up, d_down


def _layer_bwd(l, dx3, pos, cos_t, sin_t, W, S):
    tag = f"l{l}"
    G = {}
    dx2, G['norm_ffn2'], G['ffn2_gate'], G['ffn2_up'], G['ffn2_down'] = _ffn_grads(
        tag, 'ffn2', S['x2'], W['norm_ffn2'][l], dx3, S['a2'], S['u2'], W['ffn2_gate'][l], W['ffn2_up'][l],
        W['ffn2_down'][l])
    D = dx2.shape[1]
    dmix = _mm(dx2, W['w_out'][l], 'nt', name=f"outproj_dx_{tag}", tm=512, tn=MIX_WIDTH, tk=D)
    G['w_out'] = _mm(S['mix'], dx2, 'tn', name=f"outproj_dw_{tag}", tm=MIX_WIDTH, tn=D, tk=2048)
    doa, dla, dobs, dlbs, doc, dsink = _combine_bwd(dmix, S['oa'], S['lsea'], W['sinks'][l], S['obs'], S['lsebs'],
                                                    name=f"combine_bwd_{tag}")
    G['attn_sinks'] = dsink.reshape(A_WIDTH // HEAD_DIM, HEAD_DIM)[:, 0]
    dqa, dka2, dva2 = _attn_bwd(S['qa'], S['ka2'], S['va2'], doa, S['lsea'], dla, A_MAX_DIST,
                                name=f"attn_a_bwd_{tag}")
    dqb, dkb, dvb = [], [], []
    shape = S['qb'].shape
    for bi, (window, d) in enumerate(B_BRANCHES):
        dq, dk, dv = _attn_bwd(_dilate(S['qb'], d), _dilate(S['kb'], d), _dilate(S['vb'], d), _dilate(dobs[bi], d),
                               _dilate(S['lsebs'][bi], d), _dilate(dlbs[bi], d), window // d,
                               name=f"attn_b{bi}_bwd_{tag}")
        dqb.append(dq.reshape(shape))
        dkb.append(dk.reshape(shape))
        dvb.append(dv.reshape(shape))
    (dxc, dgc, G['conv_w'], G['conv_b'], dwr, G['rg_b_r'], dwi, G['rg_b_i'], G['rg_lambda']) = _rglru_bwd(
        S['xc'], S['gc'], pos, S['hs'], doc, W['conv_w'][l], W['conv_b'][l], W['rg_w_r'][l], W['rg_b_r'][l],
        W['rg_w_i'][l], W['rg_b_i'][l], W['rg_lambda'][l], name=f"rglru_bwd_{tag}")
    G['rg_w_r'] = _diag_blocks(dwr)
    G['rg_w_i'] = _diag_blocks(dwi)
    dproj = _merge_dproj(dqa, dka2, dva2, dqb, dkb, dvb, dxc, dgc, cos_t, sin_t, name=f"merge_{tag}")
    dh = _mm(dproj, W['w_in'][l], 'nt', name=f"proj_dx_{tag}", tm=512, tn=D, tk=IN_COLS)
    G['w_in'] = _mm(S['h'], dproj, 'tn', name=f"proj_dw_{tag}", tm=D, tn=IN_COLS, tk=1024)
    dx1, G['norm_mix'] = _rms_bwd(S['x1'], W['norm_mix'][l], dh, dx2, name=f"mixnorm_bwd_{tag}")
    dx0, G['norm_ffn1'], G['ffn1_gate'], G['ffn1_up'], G['ffn1_down'] = _ffn_grads(
        tag, 'ffn1', S['x0'], W['norm_ffn1'][l], dx1, S['a1'], S['u1'], W['ffn1_gate'][l], W['ffn1_up'][l],
        W['ffn1_down'][l])
    return dx0, G


def _device_step(x, positions, loss_target, W, before_layer=None, after_layer_bwd=None):
    T = x.shape[0]
    pos = positions.reshape(T, 1)
    cos_t, sin_t = _rope_tables(positions)
    saved = []
    for l in range(DEPTH):
        if before_layer is not None:
            before_layer(l, x)
        x, S = _layer_fwd(l, x, pos, cos_t, sin_t, W)
        saved.append(S)
    loss, dx, dg_final = _loss_head(x, W['norm_final'], loss_target, name="loss_head")
    grads = [None] * DEPTH
    for l in reversed(range(DEPTH)):
        dx, grads[l] = _layer_bwd(l, dx, pos, cos_t, sin_t, W, saved[l])
        if after_layer_bwd is not None:
            after_layer_bwd(l, grads[l])
    return loss, dx, grads, dg_final


SHARD_AXIS = {'ffn1_gate': 2, 'ffn1_up': 2, 'ffn1_down': 1, 'w_in': 2, 'w_out': 1, 'ffn2_gate': 2, 'ffn2_up': 2,
              'ffn2_down': 1, 'conv_w': 2}


def kernel(x, positions, norm_ffn1, ffn1_gate, ffn1_up, ffn1_down, norm_mix, w_in, attn_sinks, conv_w, conv_b, rg_w_r, rg_b_r, rg_w_i, rg_b_i, rg_lambda, w_out, norm_ffn2, ffn2_gate, ffn2_up, ffn2_down, norm_final, loss_target, m_norm_ffn1, m_ffn1_gate, m_ffn1_up, m_ffn1_down, m_norm_mix, m_w_in, m_attn_sinks, m_conv_w, m_conv_b, m_rg_w_r, m_rg_b_r, m_rg_w_i, m_rg_b_i, m_rg_lambda, m_w_out, m_norm_ffn2, m_ffn2_gate, m_ffn2_up, m_ffn2_down, m_norm_final, v_norm_ffn1, v_ffn1_gate, v_ffn1_up, v_ffn1_down, v_norm_mix, v_w_in, v_attn_sinks, v_conv_w, v_conv_b, v_rg_w_r, v_rg_b_r, v_rg_w_i, v_rg_b_i, v_rg_lambda, v_w_out, v_norm_ffn2, v_ffn2_gate, v_ffn2_up, v_ffn2_down, v_norm_final):
    given = dict(locals())
    me = 4 * lax.axis_index("x") + 2 * lax.axis_index("y") + lax.axis_index("c")

    classes = {}
    for n in BIG_NAMES:
        classes.setdefault(given[n].shape[2], []).append(n)
    classes = list(classes.values())

    class_rows = [[given[n].shape[1] for n in names] for names in classes]

    def pack(get, tag, pick=None):
        return [_pack_rows([get(n) for n in names], name=f"pack{ci}_{tag}", pick=pick)
                for ci, names in enumerate(classes)]

    def unpack(lands, srcs, tag):
        out = {}
        for ci, names in enumerate(classes):
            arrays = _unpack_rows(lands[ci], srcs[ci], class_rows[ci], name=f"unpack{ci}_{tag}")
            out.update(zip(names, arrays))
        return out

    gathers = [_exchange_start([p[0] for p in pack(lambda n: given[n], f"w_l{l}", pick=l)], True,
                               name=f"gather_start_l{l}") for l in range(DEPTH)]
    started = functools.reduce(lambda a, b: a + b, [g[4][0, 0] for g in gathers])
    conv_full = _exchange([conv_w], True, name="gather_conv_w")[0]

    W = {n: [None] * DEPTH for n in BIG_NAMES}

    def before_layer(l, x_in):
        srcs, lands = _exchange_wait(gathers[l], x_in, True, name=f"gather_wait_l{l}")
        blocks = unpack(lands, srcs, f"w_l{l}")
        for n in BIG_NAMES:
            if SHARD_AXIS[n] == 2:
                W[n][l] = _blocks_to_cols(blocks[n], name=f"cols_{n}_l{l}")
            else:
                W[n][l] = blocks[n].reshape(-1, blocks[n].shape[2])

    scatters = [None] * DEPTH

    def after_layer_bwd(l, G):
        def blocks_of(n):
            if SHARD_AXIS[n] == 2:
                return _cols_to_blocks(G[n], name=f"blocks_{n}_l{l}")
            return G[n].reshape(N_DEV, -1, G[n].shape[1])

        scatters[l] = _exchange_start(pack(blocks_of, f"g_l{l}"), False, name=f"scatter_start_l{l}")
        if l > 0:
            W['norm_ffn2'][l - 1] = W['norm_ffn2'][l - 1] + scatters[l][4][0, 0]

    W['conv_w'] = [_from_blocks(conv_full[:, l], 1) for l in range(DEPTH)]
    for n in ('norm_ffn1', 'norm_mix', 'norm_ffn2', 'conv_b', 'rg_lambda'):
        W[n] = [given[n][l][None, :] for l in range(DEPTH)]
    W['norm_final'] = norm_final[None, :]
    W['sinks'] = [jnp.repeat(attn_sinks[l], HEAD_DIM)[None, :] for l in range(DEPTH)]
    for n in ('rg_w_r', 'rg_w_i'):
        W[n] = [_block_diag(given[n][l]).astype(BF16) for l in range(DEPTH)]
    for n in ('rg_b_r', 'rg_b_i'):
        W[n] = [given[n][l].reshape(1, C_WIDTH) for l in range(DEPTH)]

    W['norm_ffn1'][0] = W['norm_ffn1'][0] + started

    loss_part, grad_x, grads, dg_final = _device_step(x[0], positions[0], loss_target[0], W, before_layer,
                                                      after_layer_bwd)
    loss = lax.psum(loss_part[0, 0], ("x", "y", "c"))

    layer_parts = []
    for l in range(DEPTH):
        srcs, lands = _exchange_wait(scatters[l], grad_x, False, name=f"scatter_wait_l{l}")
        layer_parts.append(unpack(lands, srcs, f"g_l{l}"))

    small_shapes = [given[n].shape for n in SMALL_NAMES] + [(DEPTH, C_CONV, C_WIDTH)]
    small_grads = []
    for n in SMALL_NAMES:
        if n == 'norm_final':
            small_grads.append(dg_final.reshape(-1))
        else:
            small_grads.append(jnp.stack([grads[l][n].reshape(given[n].shape[1:]) for l in range(DEPTH)]))
    small_grads.append(jnp.stack([grads[l]['conv_w'] for l in range(DEPTH)]))
    small_parts = _exchange([_pack(small_grads)], True, name="gather_small_grads")[0]

    out = {}
    for n in BIG_NAMES:
        shape = given[n].shape
        two_d = (shape[0] * shape[1], shape[2])
        res = None
        for l in range(DEPTH):
            res = _adamw(layer_parts[l][n], given[n].reshape(two_d), given['m_' + n].reshape(two_d),
                         given['v_' + n].reshape(two_d), name=f"adamw_{n}_l{l}", part=l, prev=res)
        out[n] = [r.reshape(shape) for r in res]

    w_small = [given[n] for n in SMALL_NAMES]
    m_small = [given['m_' + n] for n in SMALL_NAMES]
    v_small = [given['v_' + n] for n in SMALL_NAMES]
    zeros_cw = jnp.zeros((DEPTH, C_CONV, C_WIDTH), F32)
    res = _adamw(small_parts, _pack(w_small + [zeros_cw]), _pack(m_small + [zeros_cw]), _pack(v_small + [zeros_cw]),
                 name="adamw_small", tr=8)
    unpacked = [_unpack(r, small_shapes) for r in res]
    for i, n in enumerate(SMALL_NAMES):
        out[n] = [u[i] for u in unpacked]

    k = conv_w.shape[2]
    g_cw = lax.dynamic_slice_in_dim(unpacked[0][-1], me * k, k, axis=2)
    zero_parts = jnp.zeros((N_DEV - 1,) + (8, LANES), F32)
    res = _adamw(jnp.concatenate([_pack([g_cw])[None], zero_parts]), _pack([conv_w]), _pack([m_conv_w]),
                 _pack([v_conv_w]), name="adamw_conv_w", tr=8)
    out['conv_w'] = [_unpack(r, [conv_w.shape])[0] for r in res]

    outputs = [loss, grad_x[None]]
    for i in range(4):
        outputs += [out[n][i] for n in WEIGHT_NAMES]
    return tuple(outputs)
```

```python
import functools
import math

import jax
import jax.numpy as jnp
from jax import lax
from jax.experimental import pallas as pl
from jax.experimental.pallas import tpu as pltpu

F32 = jnp.float32
BF16 = jnp.bfloat16

N_DEV = 8
DEPTH = 4
HEAD_DIM = 64
LANES = 128
QBLK = 128
A_WIDTH = 256
A_KV_WIDTH = 128
B_WIDTH = 384
C_WIDTH = 384
C_BLOCKS = 6
C_CONV = 4
C_EXP = 8.0
MIX_WIDTH = A_WIDTH + B_WIDTH + C_WIDTH
IN_COLS = A_WIDTH + 2 * A_KV_WIDTH + 3 * B_WIDTH + 2 * C_WIDTH
A_MAX_DIST = 127
B_BRANCHES = ((128, 1), (512, 4), (2048, 16))
ROPE_THETA = 10000.0
EPS = 1e-6
SCALE = HEAD_DIM ** -0.5

ADAM_LR = 0.001
ADAM_B1 = 0.9
ADAM_B2 = 0.999
ADAM_EPS = 1e-08
ADAM_WD = 0.01
ADAM_STEP = 10

ATTN_CHUNK = 1024
ATTN_FWD_UNROLL = 4
ATTN_BWD_UNROLL = 2
VMEM_LIMIT = 56 * 1024 * 1024

NT_DIMS = (((1,), (1,)), ((), ()))
TN_DIMS = (((0,), (0,)), ((), ()))
NN_DIMS = (((1,), (0,)), ((), ()))

WEIGHT_NAMES = ['norm_ffn1', 'ffn1_gate', 'ffn1_up', 'ffn1_down', 'norm_mix', 'w_in', 'attn_sinks', 'conv_w',
                'conv_b', 'rg_w_r', 'rg_b_r', 'rg_w_i', 'rg_b_i', 'rg_lambda', 'w_out', 'norm_ffn2', 'ffn2_gate',
                'ffn2_up', 'ffn2_down', 'norm_final']
BIG_NAMES = ['ffn1_gate', 'ffn1_up', 'ffn1_down', 'w_in', 'w_out', 'ffn2_gate', 'ffn2_up', 'ffn2_down']
SCATTER_STAGES = (['ffn2_gate', 'ffn2_up', 'ffn2_down', 'w_out', 'w_in'], ['ffn1_gate', 'ffn1_up', 'ffn1_down'])
SMALL_NAMES = ['norm_ffn1', 'norm_mix', 'norm_ffn2', 'norm_final', 'attn_sinks', 'conv_b', 'rg_w_r', 'rg_b_r',
               'rg_w_i', 'rg_b_i', 'rg_lambda']


def _params(sem, vmem=VMEM_LIMIT):
    return pltpu.CompilerParams(dimension_semantics=sem, vmem_limit_bytes=vmem)


def _dot(a, b, dims=NN_DIMS):
    return lax.dot_general(a, b, dims, preferred_element_type=F32)


def _sigmoid(x):
    return 1.0 / (1.0 + jnp.exp(-x))


def _mm(a, b, mode, *, name, tm=512, tn=512, tk=512, out_dtype=F32, alpha=1.0, res=None):
    if mode == 'nn':
        (M, K), N = a.shape, b.shape[1]
    elif mode == 'nt':
        (M, K), N = a.shape, b.shape[0]
    else:
        (K, M), N = a.shape, b.shape[1]
    tm, tn, tk = min(tm, M), min(tn, N), min(tk, K)
    ni, nj, nk = M // tm, N // tn, K // tk
    assert ni * tm == M and nj * tn == N and nk * tk == K, (name, a.shape, b.shape, tm, tn, tk)
    if mode == 'tn':
        a_spec = pl.BlockSpec((tk, tm), lambda j, i, k: (k, i))
    else:
        a_spec = pl.BlockSpec((tm, tk), lambda j, i, k: (i, k))
    if mode == 'nt':
        b_spec = pl.BlockSpec((tn, tk), lambda j, i, k: (j, k))
    else:
        b_spec = pl.BlockSpec((tk, tn), lambda j, i, k: (k, j))
    dims = {'nn': NN_DIMS, 'nt': NT_DIMS, 'tn': TN_DIMS}[mode]
    o_spec = pl.BlockSpec((tm, tn), lambda j, i, k: (i, j))
    has_res = res is not None

    def body(*refs):
        if has_res:
            a_ref, b_ref, r_ref, o_ref = refs[:4]
        else:
            a_ref, b_ref, o_ref = refs[:3]
        part = _dot(a_ref[...].astype(BF16), b_ref[...].astype(BF16), dims)

        def finish(acc):
            out = acc * alpha if alpha != 1.0 else acc
            if has_res:
                out = r_ref[...] + out
            o_ref[...] = out.astype(out_dtype)

        if nk == 1:
            finish(part)
        else:
            acc_ref = refs[-1]
            k = pl.program_id(2)

            @pl.when(k == 0)
            def _():
                acc_ref[...] = part

            @pl.when(k > 0)
            def _():
                acc_ref[...] += part

            @pl.when(k == nk - 1)
            def _():
                finish(acc_ref[...])

    in_specs = [a_spec, b_spec] + ([o_spec] if has_res else [])
    operands = [a, b] + ([res] if has_res else [])
    return pl.pallas_call(
        body, name=name, grid=(nj, ni, nk), in_specs=in_specs, out_specs=o_spec,
        out_shape=jax.ShapeDtypeStruct((M, N), out_dtype),
        scratch_shapes=[pltpu.VMEM((tm, tn), F32)] if nk > 1 else [],
        compiler_params=_params(("parallel", "parallel", "arbitrary")),
    )(*operands)


def _rms_fwd(x, g, *, name, tm=512):
    T, D = x.shape
    tm = min(tm, T)

    def body(x_ref, g_ref, o_ref):
        xv = x_ref[...]
        rstd = lax.rsqrt(jnp.mean(xv * xv, axis=-1, keepdims=True) + EPS)
        o_ref[...] = (xv * rstd * g_ref[...]).astype(BF16)

    return pl.pallas_call(
        body, name=name, grid=(T // tm,),
        in_specs=[pl.BlockSpec((tm, D), lambda i: (i, 0)), pl.BlockSpec((1, D), lambda i: (0, 0))],
        out_specs=pl.BlockSpec((tm, D), lambda i: (i, 0)),
        out_shape=jax.ShapeDtypeStruct((T, D), BF16),
        compiler_params=_params(("parallel",)),
    )(x, g)


def _rms_bwd_math(xv, g, dn):
    rstd = lax.rsqrt(jnp.mean(xv * xv, axis=-1, keepdims=True) + EPS)
    xhat = xv * rstd
    dxhat = dn * g
    dx = rstd * (dxhat - xhat * jnp.mean(dxhat * xhat, axis=-1, keepdims=True))
    return dx, jnp.sum(dn * xhat, axis=0, keepdims=True)


def _rms_bwd(x, g, dn, dres, *, name, tm=512):
    T, D = x.shape
    tm = min(tm, T)

    def body(x_ref, g_ref, dn_ref, dres_ref, dx_ref, dg_ref):
        dx, dg = _rms_bwd_math(x_ref[...], g_ref[...], dn_ref[...])
        dx_ref[...] = dres_ref[...] + dx

        @pl.when(pl.program_id(0) == 0)
        def _():
            dg_ref[...] = jnp.zeros_like(dg_ref)

        dg_ref[...] += dg

    row = pl.BlockSpec((tm, D), lambda i: (i, 0))
    vec = pl.BlockSpec((1, D), lambda i: (0, 0))
    return pl.pallas_call(
        body, name=name, grid=(T // tm,),
        in_specs=[row, vec, row, row], out_specs=[row, vec],
        out_shape=[jax.ShapeDtypeStruct((T, D), F32), jax.ShapeDtypeStruct((1, D), F32)],
        compiler_params=_params(("arbitrary",)),
    )(x, g, dn, dres)


def _loss_head(x, g, target, *, name, tm=512):
    T, D = x.shape
    tm = min(tm, T)

    def body(x_ref, g_ref, t_ref, loss_ref, dx_ref, dg_ref):
        xv = x_ref[...]
        g = g_ref[...]
        rstd = lax.rsqrt(jnp.mean(xv * xv, axis=-1, keepdims=True) + EPS)
        y = xv * rstd * g
        err = y - t_ref[...]
        part = 0.5 * jnp.sum(jnp.mean(err * err, axis=-1, keepdims=True), axis=0, keepdims=True)
        dx, dg = _rms_bwd_math(xv, g, err * (1.0 / D))
        dx_ref[...] = dx

        @pl.when(pl.program_id(0) == 0)
        def _():
            dg_ref[...] = jnp.zeros_like(dg_ref)
            loss_ref[...] = jnp.zeros_like(loss_ref)

        dg_ref[...] += dg
        loss_ref[...] += jnp.broadcast_to(part, loss_ref.shape)

    row = pl.BlockSpec((tm, D), lambda i: (i, 0))
    vec = pl.BlockSpec((1, D), lambda i: (0, 0))
    lspec = pl.BlockSpec((1, LANES), lambda i: (0, 0))
    return pl.pallas_call(
        body, name=name, grid=(T // tm,),
        in_specs=[row, vec, row], out_specs=[lspec, row, vec],
        out_shape=[jax.ShapeDtypeStruct((1, LANES), F32), jax.ShapeDtypeStruct((T, D), F32),
                   jax.ShapeDtypeStruct((1, D), F32)],
        compiler_params=_params(("arbitrary",)),
    )(x, g, target)


def _resident(shape):
    return pl.BlockSpec(shape, lambda i: (0,) * len(shape), pipeline_mode=pl.Buffered(1))


def _ffn_chunk(F):
    for c in (1408, 1024, 512, 256, 128):
        if F % c == 0:
            return c
    return F


def _ffn_fwd(x, g, wg, wu, wd, *, name, tm=256):
    T, D = x.shape
    F = wg.shape[1]
    tm = min(tm, T)
    fc = _ffn_chunk(F)

    def body(x_ref, g_ref, wg_ref, wu_ref, wd_ref, o_ref, a_ref, u_ref):
        xv = x_ref[...]
        rstd = lax.rsqrt(jnp.mean(xv * xv, axis=-1, keepdims=True) + EPS)
        n = (xv * rstd * g_ref[...]).astype(BF16)
        acc = jnp.zeros((tm, D), F32)
        for c in range(F // fc):
            sl = slice(c * fc, (c + 1) * fc)
            a = _dot(n, wg_ref[:, sl])
            u = _dot(n, wu_ref[:, sl])
            a_ref[:, sl] = a.astype(BF16)
            u_ref[:, sl] = u.astype(BF16)
            act = (a * _sigmoid(a) * u).astype(BF16)
            acc = acc + _dot(act, wd_ref[sl, :])
        o_ref[...] = xv + 0.5 * acc

    row = pl.BlockSpec((tm, D), lambda i: (i, 0))
    hid = pl.BlockSpec((tm, F), lambda i: (i, 0))
    return pl.pallas_call(
        body, name=name, grid=(T // tm,),
        in_specs=[row, pl.BlockSpec((1, D), lambda i: (0, 0)),
                  _resident((D, F)), _resident((D, F)), _resident((F, D))],
        out_specs=[row, hid, hid],
        out_shape=[jax.ShapeDtypeStruct((T, D), F32), jax.ShapeDtypeStruct((T, F), BF16),
                   jax.ShapeDtypeStruct((T, F), BF16)],
        compiler_params=_params(("parallel",)),
    )(x, g, wg, wu, wd)


def _ffn_bwd(x, g, dy, a, u, wg, wu, wd, *, name, tm=256):
    T, D = x.shape
    F = wg.shape[1]
    tm = min(tm, T)
    fc = _ffn_chunk(F)

    def body(x_ref, g_ref, dy_ref, a_ref, u_ref, wg_ref, wu_ref, wd_ref,
             dx_ref, dg_ref, n_ref, act_ref, da_ref, du_ref):
        xv = x_ref[...]
        g = g_ref[...]
        rstd = lax.rsqrt(jnp.mean(xv * xv, axis=-1, keepdims=True) + EPS)
        n_ref[...] = (xv * rstd * g).astype(BF16)
        dy = dy_ref[...]
        dyh = (0.5 * dy).astype(BF16)
        dn = jnp.zeros((tm, D), F32)
        for c in range(F // fc):
            sl = slice(c * fc, (c + 1) * fc)
            av = a_ref[:, sl].astype(F32)
            uv = u_ref[:, sl].astype(F32)
            dact = _dot(dyh, wd_ref[sl, :], NT_DIMS)
            s = _sigmoid(av)
            silu = av * s
            act_ref[:, sl] = (silu * uv).astype(BF16)
            da = (dact * uv * (s * (1.0 + av * (1.0 - s)))).astype(BF16)
            du = (dact * silu).astype(BF16)
            da_ref[:, sl] = da
            du_ref[:, sl] = du
            dn = dn + _dot(da, wg_ref[:, sl], NT_DIMS) + _dot(du, wu_ref[:, sl], NT_DIMS)
        dx, dg = _rms_bwd_math(xv, g, dn)
        dx_ref[...] = dy + dx

        @pl.when(pl.program_id(0) == 0)
        def _():
            dg_ref[...] = jnp.zeros_like(dg_ref)

        dg_ref[...] += dg

    row = pl.BlockSpec((tm, D), lambda i: (i, 0))
    hid = pl.BlockSpec((tm, F), lambda i: (i, 0))
    vec = pl.BlockSpec((1, D), lambda i: (0, 0))
    return pl.pallas_call(
        body, name=name, grid=(T // tm,),
        in_specs=[row, vec, row, hid, hid,
                  _resident((D, F)), _resident((D, F)), _resident((F, D))],
        out_specs=[row, vec, row, hid, hid, hid],
        out_shape=[jax.ShapeDtypeStruct((T, D), F32), jax.ShapeDtypeStruct((1, D), F32),
                   jax.ShapeDtypeStruct((T, D), BF16), jax.ShapeDtypeStruct((T, F), BF16),
                   jax.ShapeDtypeStruct((T, F), BF16), jax.ShapeDtypeStruct((T, F), BF16)],
        compiler_params=_params(("arbitrary",)),
    )(x, g, dy, a, u, wg, wu, wd)


def _lane_iota(shape):
    return lax.broadcasted_iota(jnp.int32, shape, 1)


def _rope_partner(x):
    first_half = (_lane_iota(x.shape) & (HEAD_DIM - 1)) < HEAD_DIM // 2
    return jnp.where(first_half, pltpu.roll(x, LANES - HEAD_DIM // 2, 1), pltpu.roll(x, HEAD_DIM // 2, 1))


def _swap_heads(x):
    return pltpu.roll(x, HEAD_DIM, 1)


def _split_rope(proj, cos_t, sin_t, *, name, tm=256):
    T = proj.shape[0]
    tm = min(tm, T)

    def body(p_ref, c_ref, s_ref, qa_ref, ka_ref, va_ref, qb_ref, kb_ref, vb_ref, xc_ref, gc_ref):
        cos = c_ref[...]
        sin = s_ref[...]

        def rope(x):
            return x * cos + _rope_partner(x) * sin

        lo = _lane_iota((tm, LANES)) < HEAD_DIM
        col = 0
        for j in range(A_WIDTH // LANES):
            qa_ref[:, j * LANES:(j + 1) * LANES] = (rope(p_ref[:, col:col + LANES]) * SCALE).astype(BF16)
            col += LANES
        kr = rope(p_ref[:, col:col + LANES])
        col += LANES
        vr = p_ref[:, col:col + LANES]
        col += LANES
        for src, dst in ((kr, ka_ref), (vr, va_ref)):
            sw = _swap_heads(src)
            dst[:, 0:LANES] = jnp.where(lo, src, sw).astype(BF16)
            dst[:, LANES:2 * LANES] = jnp.where(lo, sw, src).astype(BF16)
        for dst, roped, scale in ((qb_ref, True, SCALE), (kb_ref, True, 1.0), (vb_ref, False, 1.0)):
            for j in range(B_WIDTH // LANES):
                v = p_ref[:, col:col + LANES]
                if roped:
                    v = rope(v) * scale
                dst[:, j * LANES:(j + 1) * LANES] = v.astype(BF16)
                col += LANES
        xc_ref[...] = p_ref[:, col:col + C_WIDTH]
        gc_ref[...] = p_ref[:, col + C_WIDTH:col + 2 * C_WIDTH]

    def row(w):
        return pl.BlockSpec((tm, w), lambda i: (i, 0))

    widths = [A_WIDTH, A_WIDTH, A_WIDTH, B_WIDTH, B_WIDTH, B_WIDTH, C_WIDTH, C_WIDTH]
    dtypes = [BF16] * 6 + [F32] * 2
    return pl.pallas_call(
        body, name=name, grid=(T // tm,),
        in_specs=[row(IN_COLS), row(LANES), row(LANES)],
        out_specs=[row(w) for w in widths],
        out_shape=[jax.ShapeDtypeStruct((T, w), d) for w, d in zip(widths, dtypes)],
        compiler_params=_params(("parallel",)),
    )(proj, cos_t, sin_t)


def _merge_dproj(dqa, dka2, dva2, dqb, dkb, dvb, dxc, dgc, cos_t, sin_t, *, name, tm=256):
    T = dqa.shape[0]
    tm = min(tm, T)
    nb = len(dqb)

    def body(*refs):
        dqa_ref, dka_ref, dva_ref = refs[:3]
        dqb_refs = refs[3:3 + nb]
        dkb_refs = refs[3 + nb:3 + 2 * nb]
        dvb_refs = refs[3 + 2 * nb:3 + 3 * nb]
        dxc_ref, dgc_ref, c_ref, s_ref, o_ref = refs[3 + 3 * nb:]
        cos = c_ref[...]
        sin = s_ref[...]

        def rope_t(dy):
            return dy * cos - _rope_partner(dy) * sin

        lo = _lane_iota((tm, LANES)) < HEAD_DIM
        col = 0
        for j in range(A_WIDTH // LANES):
            o_ref[:, col:col + LANES] = (rope_t(dqa_ref[:, j * LANES:(j + 1) * LANES]) * SCALE).astype(BF16)
            col += LANES
        for src, roped in ((dka_ref, True), (dva_ref, False)):
            b0 = src[:, 0:LANES]
            b1 = src[:, LANES:2 * LANES]
            v = jnp.where(lo, b0 + _swap_heads(b0), b1 + _swap_heads(b1))
            if roped:
                v = rope_t(v)
            o_ref[:, col:col + LANES] = v.astype(BF16)
            col += LANES
        for group, roped, scale in ((dqb_refs, True, SCALE), (dkb_refs, True, 1.0), (dvb_refs, False, 1.0)):
            for j in range(B_WIDTH // LANES):
                sl = slice(j * LANES, (j + 1) * LANES)
                v = group[0][:, sl]
                for r in group[1:]:
                    v = v + r[:, sl]
                if roped:
                    v = rope_t(v) * scale
                o_ref[:, col:col + LANES] = v.astype(BF16)
                col += LANES
        o_ref[:, col:col + C_WIDTH] = dxc_ref[...].astype(BF16)
        o_ref[:, col + C_WIDTH:col + 2 * C_WIDTH] = dgc_ref[...].astype(BF16)

    def row(w):
        return pl.BlockSpec((tm, w), lambda i: (i, 0))

    ins = [dqa, dka2, dva2, *dqb, *dkb, *dvb, dxc, dgc, cos_t, sin_t]
    return pl.pallas_call(
        body, name=name, grid=(T // tm,),
        in_specs=[row(v.shape[1]) for v in ins],
        out_specs=row(IN_COLS),
        out_shape=jax.ShapeDtypeStruct((T, IN_COLS), BF16),
        compiler_params=_params(("parallel",)),
    )(*ins)


def _band_masks(max_dist):
    row = lax.broadcasted_iota(jnp.int32, (QBLK, 2 * QBLK), 0)
    key = lax.broadcasted_iota(jnp.int32, (QBLK, 2 * QBLK), 1)
    dist = row + QBLK - key
    wide = jnp.logical_and(dist >= 0, dist <= max_dist)
    return wide, wide[:, :QBLK], key >= QBLK


def _head_masks(rows=QBLK):
    lo = _lane_iota((rows, LANES)) < HEAD_DIM
    return lo, jnp.logical_not(lo)


def _keep(hm, x):
    return x * jnp.where(hm, 1.0, 0.0).astype(x.dtype)


def _head_col(x, hm):
    return jnp.max(jnp.where(hm, x, -jnp.inf), axis=1, keepdims=True)


def _attn_specs(R, C):
    chunk = min(ATTN_CHUNK, R)
    nb = chunk // QBLK
    nch = R // chunk
    main = pl.BlockSpec((chunk, LANES), lambda j, c: (c, j))
    prev = pl.BlockSpec((QBLK, LANES), lambda j, c: (jnp.maximum(c * nb - 1, 0), j))
    nxt = pl.BlockSpec((QBLK, LANES), lambda j, c: (jnp.minimum((c + 1) * nb, R // QBLK - 1), j))
    return chunk, nb, nch, main, prev, nxt


def _attn_fwd(q, k, v, max_dist, *, name):
    R, C = q.shape
    chunk, nb, nch, main, prev, _ = _attn_specs(R, C)

    def body(q_ref, k_ref, v_ref, kp_ref, vp_ref, o_ref, lse_ref):
        c = pl.program_id(1)
        wide_mask, _, own_block = _band_masks(max_dist)
        heads = _head_masks()

        def block(q_blk, kk, vv, mask):
            o_h, lse_h = [], []
            for hm in heads:
                s = jnp.where(mask, _dot(_keep(hm, q_blk), kk, NT_DIMS), -jnp.inf)
                m = jnp.max(jnp.maximum(s[:, :QBLK], s[:, QBLK:]), axis=1, keepdims=True)
                p = jnp.exp(s - m)
                l = jnp.sum(p[:, :QBLK] + p[:, QBLK:], axis=1, keepdims=True)
                o_h.append(_dot(p.astype(BF16), vv) / l)
                lse_h.append(jnp.broadcast_to(m + jnp.log(l), (QBLK, LANES)))
            return jnp.where(heads[0], o_h[0], o_h[1]), jnp.where(heads[0], lse_h[0], lse_h[1])

        first = pl.ds(0, QBLK)
        o0, l0 = block(q_ref[first, :], jnp.concatenate([kp_ref[...], k_ref[first, :]], axis=0),
                       jnp.concatenate([vp_ref[...], v_ref[first, :]], axis=0),
                       jnp.logical_and(wide_mask, jnp.logical_or(own_block, c > 0)))
        o_ref[first, :] = o0
        lse_ref[first, :] = l0

        def loop(qb, carry):
            cur = pl.ds(pl.multiple_of(qb * QBLK, QBLK), QBLK)
            both = pl.ds(pl.multiple_of((qb - 1) * QBLK, QBLK), 2 * QBLK)
            o, l = block(q_ref[cur, :], k_ref[both, :], v_ref[both, :], wide_mask)
            o_ref[cur, :] = o
            lse_ref[cur, :] = l
            return carry

        if nb > 1:
            lax.fori_loop(1, nb, loop, 0, unroll=ATTN_FWD_UNROLL)

    return pl.pallas_call(
        body, name=name, grid=(C // LANES, nch),
        in_specs=[main, main, main, prev, prev], out_specs=[main, main],
        out_shape=[jax.ShapeDtypeStruct((R, C), F32), jax.ShapeDtypeStruct((R, C), F32)],
        compiler_params=_params(("parallel", "parallel")),
    )(q, k, v, k, v)


def _attn_bwd(q, k, v, do, lse, delta, max_dist, *, name):
    R, C = q.shape
    chunk, nb, nch, main, prev, nxt = _attn_specs(R, C)

    def body(q_ref, k_ref, v_ref, do_ref, lse_ref, dl_ref, kp_ref, vp_ref, qn_ref, don_ref, lsen_ref, dln_ref,
             dq_ref, dk_ref, dv_ref):
        c = pl.program_id(1)
        wide_mask, prev_mask, own_block = _band_masks(max_dist)
        heads = _head_masks()

        def pair(q_blk, do_blk, lse_blk, dl_blk, kk, vv, mask, want_dq=True):
            dq = jnp.zeros((QBLK, LANES), F32)
            dkk = jnp.zeros(kk.shape, F32)
            dvv = jnp.zeros(kk.shape, F32)
            for hm, khm in zip(heads, _head_masks(kk.shape[0])):
                qh = _keep(hm, q_blk)
                doh = _keep(hm, do_blk)
                p = jnp.where(mask, jnp.exp(_dot(qh, kk, NT_DIMS) - _head_col(lse_blk, hm)), 0.0)
                ds = (p * (_dot(doh, vv, NT_DIMS) - _head_col(dl_blk, hm))).astype(BF16)
                if want_dq:
                    dq = dq + _dot(ds, _keep(khm, kk))
                dkk = dkk + _dot(ds, qh, TN_DIMS)
                dvv = dvv + _dot(p.astype(BF16), doh, TN_DIMS)
            return dq, dkk, dvv

        dk_ref[...] = jnp.zeros_like(dk_ref)
        dv_ref[...] = jnp.zeros_like(dv_ref)

        first = pl.ds(0, QBLK)
        dq0, dkk0, dvv0 = pair(q_ref[first, :], do_ref[first, :], lse_ref[first, :], dl_ref[first, :],
                               jnp.concatenate([kp_ref[...], k_ref[first, :]], axis=0),
                               jnp.concatenate([vp_ref[...], v_ref[first, :]], axis=0),
                               jnp.logical_and(wide_mask, jnp.logical_or(own_block, c > 0)))
        dq_ref[first, :] = dq0
        dk_ref[first, :] += dkk0[QBLK:, :]
        dv_ref[first, :] += dvv0[QBLK:, :]

        def loop(qb, carry):
            cur = pl.ds(pl.multiple_of(qb * QBLK, QBLK), QBLK)
            both = pl.ds(pl.multiple_of((qb - 1) * QBLK, QBLK), 2 * QBLK)
            dq, dkk, dvv = pair(q_ref[cur, :], do_ref[cur, :], lse_ref[cur, :], dl_ref[cur, :],
                                k_ref[both, :], v_ref[both, :], wide_mask)
            dq_ref[cur, :] = dq
            dk_ref[both, :] += dkk
            dv_ref[both, :] += dvv
            return carry

        if nb > 1:
            lax.fori_loop(1, nb, loop, 0, unroll=ATTN_BWD_UNROLL)

        last = pl.ds((nb - 1) * QBLK, QBLK)
        _, dk_n, dv_n = pair(qn_ref[...], don_ref[...], lsen_ref[...], dln_ref[...], k_ref[last, :], v_ref[last, :],
                             jnp.logical_and(prev_mask, c < nch - 1), want_dq=False)
        dk_ref[last, :] += dk_n
        dv_ref[last, :] += dv_n

    return pl.pallas_call(
        body, name=name, grid=(C // LANES, nch),
        in_specs=[main] * 6 + [prev, prev] + [nxt] * 4, out_specs=[main, main, main],
        out_shape=[jax.ShapeDtypeStruct((R, C), F32)] * 3,
        compiler_params=_params(("parallel", "parallel")),
    )(q, k, v, do, lse, delta, k, v, q, do, lse, delta)


def _head_sum(x):
    r = lax.broadcasted_iota(jnp.int32, (LANES, LANES), 0) // HEAD_DIM
    c = lax.broadcasted_iota(jnp.int32, (LANES, LANES), 1) // HEAD_DIM
    ones = jnp.where(r == c, 1.0, 0.0).astype(BF16)
    outs = []
    for j in range(x.shape[1] // LANES):
        rem = x[:, j * LANES:(j + 1) * LANES]
        acc = jnp.zeros(rem.shape, F32)
        for _ in range(3):
            part = rem.astype(BF16)
            acc = acc + _dot(part, ones)
            rem = rem - part.astype(F32)
        outs.append(acc)
    return outs[0] if len(outs) == 1 else jnp.concatenate(outs, axis=1)


def _branch_weights(lses):
    m = functools.reduce(jnp.maximum, lses)
    es = [jnp.exp(l - m) for l in lses]
    den = functools.reduce(lambda a, b: a + b, es)
    return [e / den for e in es]


def _combine_fwd(oa, lsea, sink, obs, lsebs, oc, *, name, tm=256):
    T = oa.shape[0]
    tm = min(tm, T)
    nb = len(obs)

    def body(*refs):
        oa_ref, lsea_ref, sink_ref = refs[:3]
        ob_refs = refs[3:3 + nb]
        lse_refs = refs[3 + nb:3 + 2 * nb]
        oc_ref, out_ref = refs[3 + 2 * nb:]
        out_ref[:, 0:A_WIDTH] = (oa_ref[...] * _sigmoid(lsea_ref[...] - sink_ref[...])).astype(BF16)
        ws = _branch_weights([r[...] for r in lse_refs])
        ob = ob_refs[0][...] * ws[0]
        for r, w in zip(ob_refs[1:], ws[1:]):
            ob = ob + r[...] * w
        out_ref[:, A_WIDTH:A_WIDTH + B_WIDTH] = ob.astype(BF16)
        out_ref[:, A_WIDTH + B_WIDTH:MIX_WIDTH] = oc_ref[...].astype(BF16)

    def row(w):
        return pl.BlockSpec((tm, w), lambda i: (i, 0))

    ins = [oa, lsea, sink, *obs, *lsebs, oc]
    in_specs = [row(A_WIDTH), row(A_WIDTH), pl.BlockSpec((1, A_WIDTH), lambda i: (0, 0))]
    in_specs += [row(B_WIDTH)] * (2 * nb) + [row(C_WIDTH)]
    return pl.pallas_call(
        body, name=name, grid=(T // tm,), in_specs=in_specs, out_specs=row(MIX_WIDTH),
        out_shape=jax.ShapeDtypeStruct((T, MIX_WIDTH), BF16),
        compiler_params=_params(("parallel",)),
    )(*ins)


def _combine_bwd(dmix, oa, lsea, sink, obs, lsebs, *, name, tm=256):
    T = oa.shape[0]
    tm = min(tm, T)
    nb = len(obs)

    def body(*refs):
        dmix_ref, oa_ref, lsea_ref, sink_ref = refs[:4]
        ob_refs = refs[4:4 + nb]
        lse_refs = refs[4 + nb:4 + 2 * nb]
        outs = refs[4 + 2 * nb:]
        doa_ref, dla_ref = outs[:2]
        dob_refs = outs[2:2 + nb]
        dlb_refs = outs[2 + nb:2 + 2 * nb]
        doc_ref, dsink_ref = outs[2 + 2 * nb:]

        d_a = dmix_ref[:, 0:A_WIDTH]
        d_b = dmix_ref[:, A_WIDTH:A_WIDTH + B_WIDTH]
        doc_ref[...] = dmix_ref[:, A_WIDTH + B_WIDTH:MIX_WIDTH]

        gate = _sigmoid(lsea_ref[...] - sink_ref[...])
        doa_ref[...] = (d_a * gate).astype(BF16)
        dgate = _head_sum(d_a * oa_ref[...])
        dlse = dgate * gate * (1.0 - gate)
        dla_ref[...] = dgate * gate - dlse

        @pl.when(pl.program_id(0) == 0)
        def _():
            dsink_ref[...] = jnp.zeros_like(dsink_ref)

        dsink_ref[...] -= jnp.sum(dlse, axis=0, keepdims=True)

        ws = _branch_weights([r[...] for r in lse_refs])
        dws = [_head_sum(d_b * r[...]) for r in ob_refs]
        sw = ws[0] * dws[0]
        for w, dw in zip(ws[1:], dws[1:]):
            sw = sw + w * dw
        for w, do_ref, dl_ref in zip(ws, dob_refs, dlb_refs):
            do_ref[...] = (w * d_b).astype(BF16)
            dl_ref[...] = w * sw

    def row(w):
        return pl.BlockSpec((tm, w), lambda i: (i, 0))

    vec = pl.BlockSpec((1, A_WIDTH), lambda i: (0, 0))
    ins = [dmix, oa, lsea, sink, *obs, *lsebs]
    in_specs = [row(MIX_WIDTH), row(A_WIDTH), row(A_WIDTH), vec] + [row(B_WIDTH)] * (2 * nb)
    out_specs = [row(A_WIDTH), row(A_WIDTH)] + [row(B_WIDTH)] * (2 * nb) + [row(C_WIDTH), vec]
    out_shape = [jax.ShapeDtypeStruct((T, A_WIDTH), BF16), jax.ShapeDtypeStruct((T, A_WIDTH), F32)]
    out_shape += [jax.ShapeDtypeStruct((T, B_WIDTH), BF16)] * nb + [jax.ShapeDtypeStruct((T, B_WIDTH), F32)] * nb
    out_shape += [jax.ShapeDtypeStruct((T, C_WIDTH), F32), jax.ShapeDtypeStruct((1, A_WIDTH), F32)]
    res = pl.pallas_call(
        body, name=name, grid=(T // tm,), in_specs=in_specs, out_specs=out_specs, out_shape=out_shape,
        compiler_params=_params(("arbitrary",)),
    )(*ins)
    return res[0], res[1], list(res[2:2 + nb]), list(res[2 + nb:2 + 2 * nb]), res[2 + 2 * nb], res[3 + 2 * nb]


HIST = 8


def _softplus_neg(lam):
    e = jnp.exp(-jnp.abs(lam))
    log1p = jnp.where(e < 0.01, e * (1.0 - e * (0.5 - e * (1.0 / 3.0))), jnp.log(1.0 + e))
    return jnp.maximum(-lam, 0.0) + log1p


def _neg_expm1(x):
    series = -x * (1.0 + x * (0.5 + x * (1.0 / 6.0 + x * (1.0 / 24.0 + x * (1.0 / 120.0)))))
    return jnp.where(x > -0.1, series, 1.0 - jnp.exp(x))


def _gelu_parts(x):
    k = math.sqrt(2.0 / math.pi)
    t = jnp.tanh(k * (x + 0.044715 * (x * x * x)))
    cdf = 0.5 * (1.0 + t)
    return x * cdf, cdf + 0.5 * x * (1.0 - t * t) * (k * (1.0 + 3.0 * 0.044715 * (x * x)))


def _rglru_gates(y, pos_ref, wr_ref, br_ref, wi_ref, bi_ref, lam_ref):
    yb = y.astype(BF16)
    r = _sigmoid(_dot(yb, wr_ref[...]) + br_ref[...])
    ig = _sigmoid(_dot(yb, wi_ref[...]) + bi_ref[...])
    sp = _softplus_neg(lam_ref[...])
    log_a = -C_EXP * r * sp
    reset = pos_ref[...] == 0
    a = jnp.where(reset, 0.0, jnp.exp(log_a))
    mult = jnp.where(reset, 1.0, jnp.sqrt(_neg_expm1(2.0 * log_a)))
    return yb, r, ig, sp, reset, a, mult


def _conv_fwd(xs_ref, cw_ref, cb_ref, tm):
    y = cb_ref[...] + cw_ref[0:1, :] * xs_ref[HIST:HIST + tm, :]
    for j in range(1, C_CONV):
        y = y + cw_ref[j:j + 1, :] * xs_ref[HIST - j:HIST - j + tm, :]
    return y


def _rglru_fwd(xc, gc, pos, cw, cb, wr, br, wi, bi, lam, *, name, tm=512):
    T, W = xc.shape
    tm = min(tm, T)

    def body(xc_ref, gc_ref, pos_ref, cw_ref, cb_ref, wr_ref, br_ref, wi_ref, bi_ref, lam_ref,
             out_ref, hs_ref, xs, a_s, b_s, h_s):
        @pl.when(pl.program_id(0) == 0)
        def _():
            xs[0:HIST, :] = jnp.zeros((HIST, W), F32)
            h_s[...] = jnp.zeros_like(h_s)

        xv = xc_ref[...]
        xs[HIST:HIST + tm, :] = xv
        y = _conv_fwd(xs, cw_ref, cb_ref, tm)
        xs[0:HIST, :] = xv[tm - HIST:tm, :]
        _, _, ig, _, _, a, mult = _rglru_gates(y, pos_ref, wr_ref, br_ref, wi_ref, bi_ref, lam_ref)
        a_s[...] = a
        b_s[...] = mult * (ig * y)

        def step(t, h):
            row = pl.ds(t, 1)
            h = a_s[row, :] * h + b_s[row, :]
            b_s[row, :] = h
            return h

        h_s[...] = lax.fori_loop(0, tm, step, h_s[...], unroll=8)
        hs = b_s[...]
        hs_ref[...] = hs
        out_ref[...] = hs * _gelu_parts(gc_ref[...])[0]

    row = pl.BlockSpec((tm, W), lambda i: (i, 0))
    full = lambda shape: pl.BlockSpec(shape, lambda i: (0,) * len(shape))
    return pl.pallas_call(
        body, name=name, grid=(T // tm,),
        in_specs=[row, row, pl.BlockSpec((tm, 1), lambda i: (i, 0)), full((C_CONV, W)), full((1, W)),
                  full((W, W)), full((1, W)), full((W, W)), full((1, W)), full((1, W))],
        out_specs=[row, row],
        out_shape=[jax.ShapeDtypeStruct((T, W), F32)] * 2,
        scratch_shapes=[pltpu.VMEM((tm + HIST, W), F32), pltpu.VMEM((tm, W), F32), pltpu.VMEM((tm, W), F32),
                        pltpu.VMEM((1, W), F32)],
        compiler_params=_params(("arbitrary",)),
    )(xc, gc, pos, cw, cb, wr, br, wi, bi, lam)


def _rglru_bwd(xc, gc, pos, hs, dout, cw, cb, wr, br, wi, bi, lam, *, name, tm=512):
    T, W = xc.shape
    tm = min(tm, T)
    nt = T // tm
    hb = tm // HIST

    def body(xc_ref, gc_ref, pos_ref, hs_ref, dout_ref, xch_ref, hsh_ref,
             cw_ref, cb_ref, wr_ref, br_ref, wi_ref, bi_ref, lam_ref,
             dxc_ref, dgc_ref, dcw_ref, dcb_ref, dwr_ref, dbr_ref, dwi_ref, dbi_ref, dlam_ref,
             xs, hsx, dys, a_s, d_s, carry_s):
        i = pl.program_id(0)

        @pl.when(i == 0)
        def _():
            for r in (dcw_ref, dcb_ref, dwr_ref, dbr_ref, dwi_ref, dbi_ref, dlam_ref, carry_s):
                r[...] = jnp.zeros_like(r)
            dys[tm:tm + HIST, :] = jnp.zeros((HIST, W), F32)

        has_prev = i < nt - 1
        xs[0:HIST, :] = jnp.where(has_prev, xch_ref[...], 0.0)
        hsx[0:HIST, :] = jnp.where(has_prev, hsh_ref[...], 0.0)
        xs[HIST:HIST + tm, :] = xc_ref[...]
        hs = hs_ref[...]
        hsx[HIST:HIST + tm, :] = hs
        y = _conv_fwd(xs, cw_ref, cb_ref, tm)
        yb, r, ig, sp, reset, a, mult = _rglru_gates(y, pos_ref, wr_ref, br_ref, wi_ref, bi_ref, lam_ref)

        gelu, dgelu = _gelu_parts(gc_ref[...])
        dout = dout_ref[...]
        dgc_ref[...] = dout * hs * dgelu
        a_s[...] = a
        d_s[...] = dout * gelu

        def step(k, carry):
            row = pl.ds(tm - 1 - k, 1)
            dh = d_s[row, :] + carry
            d_s[row, :] = dh
            return a_s[row, :] * dh

        carry_s[...] = lax.fori_loop(0, tm, step, carry_s[...], unroll=8)
        dh = d_s[...]
        hprev = hsx[HIST - 1:HIST - 1 + tm, :]
        igy = ig * y
        dmult = dh * igy
        digy = dh * mult
        dlog_a = jnp.where(reset, 0.0, dh * hprev * a - dmult * a * a / mult)
        dlam_ref[...] += jnp.sum(dlog_a * (C_EXP * r) * _sigmoid(-lam_ref[...]), axis=0, keepdims=True)
        dz_r = dlog_a * (-C_EXP * sp) * r * (1.0 - r)
        dz_i = digy * y * ig * (1.0 - ig)
        dzr_b = dz_r.astype(BF16)
        dzi_b = dz_i.astype(BF16)
        dy = digy * ig + _dot(dzr_b, wr_ref[...], NT_DIMS) + _dot(dzi_b, wi_ref[...], NT_DIMS)
        dwr_ref[...] += _dot(yb, dzr_b, TN_DIMS)
        dwi_ref[...] += _dot(yb, dzi_b, TN_DIMS)
        dbr_ref[...] += jnp.sum(dz_r, axis=0, keepdims=True)
        dbi_ref[...] += jnp.sum(dz_i, axis=0, keepdims=True)

        dys[0:tm, :] = dy
        dxc = cw_ref[0:1, :] * dy
        for j in range(1, C_CONV):
            dxc = dxc + cw_ref[j:j + 1, :] * dys[j:j + tm, :]
        dxc_ref[...] = dxc
        dys[tm:tm + HIST, :] = dy[0:HIST, :]
        dcb_ref[...] += jnp.sum(dy, axis=0, keepdims=True)
        for j in range(C_CONV):
            dcw_ref[j:j + 1, :] += jnp.sum(dy * xs[HIST - j:HIST - j + tm, :], axis=0, keepdims=True)

    row = pl.BlockSpec((tm, W), lambda i: (nt - 1 - i, 0))
    halo = pl.BlockSpec((HIST, W), lambda i: (jnp.maximum((nt - 1 - i) * hb - 1, 0), 0))
    full = lambda shape: pl.BlockSpec(shape, lambda i: (0,) * len(shape))
    out_specs = [row, row, full((C_CONV, W)), full((1, W)), full((W, W)), full((1, W)), full((W, W)), full((1, W)),
                 full((1, W))]
    out_shape = [jax.ShapeDtypeStruct((T, W), F32)] * 2
    out_shape += [jax.ShapeDtypeStruct(s, F32) for s in ((C_CONV, W), (1, W), (W, W), (1, W), (W, W), (1, W), (1, W))]
    return pl.pallas_call(
        body, name=name, grid=(nt,),
        in_specs=[row, row, pl.BlockSpec((tm, 1), lambda i: (nt - 1 - i, 0)), row, row, halo, halo,
                  full((C_CONV, W)), full((1, W)), full((W, W)), full((1, W)), full((W, W)), full((1, W)),
                  full((1, W))],
        out_specs=out_specs, out_shape=out_shape,
        scratch_shapes=[pltpu.VMEM((tm + HIST, W), F32), pltpu.VMEM((tm + HIST, W), F32),
                        pltpu.VMEM((tm + HIST, W), F32), pltpu.VMEM((tm, W), F32), pltpu.VMEM((tm, W), F32),
                        pltpu.VMEM((1, W), F32)],
        compiler_params=_params(("arbitrary",)),
    )(xc, gc, pos, hs, dout, xc, hs, cw, cb, wr, br, wi, bi, lam)


def _adam_math(w, g, m, v):
    m = ADAM_B1 * m + (1.0 - ADAM_B1) * g
    v = ADAM_B2 * v + (1.0 - ADAM_B2) * (g * g)
    m_hat = m / (1.0 - ADAM_B1 ** ADAM_STEP)
    v_hat = v / (1.0 - ADAM_B2 ** ADAM_STEP)
    delta = -ADAM_LR * (m_hat / (jnp.sqrt(v_hat) + ADAM_EPS) + ADAM_WD * w)
    return delta, m, v


def _pick_rows(R, cap=512, mult=16):
    for d in range(min(cap, R), 0, -1):
        if R % d == 0 and d % mult == 0:
            return d
    return R


def _adamw(parts, w, m, v, *, name, tr=None, part=0, prev=None):
    R, C = w.shape
    r = parts.shape[1]
    tr = _pick_rows(r) if tr is None else tr
    assert r % tr == 0 and R % r == 0, (name, R, r, tr)
    nt = r // tr

    def body(p_ref, w_ref, m_ref, v_ref, *rest):
        g_ref, d_ref, nm_ref, nv_ref = rest[-4:]
        g = p_ref[0].astype(F32)
        for d in range(1, N_DEV):
            g = g + p_ref[d].astype(F32)
        delta, nm, nv = _adam_math(w_ref[...], g, m_ref[...], v_ref[...])
        g_ref[...] = g
        d_ref[...] = delta
        nm_ref[...] = nm
        nv_ref[...] = nv

    row = pl.BlockSpec((tr, C), lambda i: (part * nt + i, 0))
    in_specs = [pl.BlockSpec((N_DEV, tr, C), lambda i: (0, i, 0)), row, row, row]
    operands = [parts, w, m, v]
    aliases = {}
    if prev is not None:
        in_specs += [pl.BlockSpec(memory_space=pl.ANY)] * 4
        operands += list(prev)
        aliases = {4 + i: i for i in range(4)}
    return pl.pallas_call(
        body, name=name, grid=(nt,), in_specs=in_specs,
        out_specs=[row] * 4, out_shape=[jax.ShapeDtypeStruct((R, C), F32)] * 4,
        input_output_aliases=aliases,
        compiler_params=_params(("parallel",)),
    )(*operands)


def _exchange(srcs, gather, *, name):
    n = len(srcs)
    out_shape = [jax.ShapeDtypeStruct((N_DEV,) + s.shape if gather else s.shape, s.dtype) for s in srcs]

    def body(*refs):
        ins, outs = refs[:n], refs[n:2 * n]
        send_sems, recv_sems, local_sems = refs[2 * n:]
        x, y, c = lax.axis_index("x"), lax.axis_index("y"), lax.axis_index("c")
        me = 4 * x + 2 * y + c
        local_copies, sends, arrivals = [], [], []
        for a in range(n):
            mine = ins[a] if gather else ins[a].at[me]
            local = pltpu.make_async_copy(mine, outs[a].at[me], local_sems.at[a])
            local.start()
            local_copies.append(local)
            for k in range(1, N_DEV):
                px, py, pc = x ^ ((k >> 2) & 1), y ^ ((k >> 1) & 1), c ^ (k & 1)
                peer = 4 * px + 2 * py + pc
                send = pltpu.make_async_remote_copy(
                    src_ref=ins[a] if gather else ins[a].at[peer], dst_ref=outs[a].at[me],
                    send_sem=send_sems.at[a * N_DEV + k], recv_sem=recv_sems.at[a * N_DEV + k],
                    device_id=(px, py, pc), device_id_type=pl.DeviceIdType.MESH)
                send.start()
                sends.append(send)
                arrivals.append(pltpu.make_async_remote_copy(
                    src_ref=mine, dst_ref=outs[a].at[peer],
                    send_sem=send_sems.at[a * N_DEV + k], recv_sem=recv_sems.at[a * N_DEV + k],
                    device_id=(px, py, pc), device_id_type=pl.DeviceIdType.MESH))
        for cp in sends:
            cp.wait_send()
        for cp in arrivals:
            cp.wait_recv()
        for cp in local_copies:
            cp.wait()

    return pl.pallas_call(
        body, name=name,
        in_specs=[pl.BlockSpec(memory_space=pl.ANY)] * n, out_specs=[pl.BlockSpec(memory_space=pl.ANY)] * n,
        out_shape=out_shape,
        scratch_shapes=[pltpu.SemaphoreType.DMA((n * N_DEV,)), pltpu.SemaphoreType.DMA((n * N_DEV,)),
                        pltpu.SemaphoreType.DMA((n,))],
    )(*srcs)


_HBM = pl.BlockSpec(memory_space=pltpu.HBM)
_SEM = pl.BlockSpec(memory_space=pltpu.SEMAPHORE)
_EFFECT = pltpu.SideEffectType.DATAFLOW_SIDE_EFFECTING


def _peers():
    x, y, c = lax.axis_index("x"), lax.axis_index("y"), lax.axis_index("c")
    out = []
    for k in range(1, N_DEV):
        px, py, pc = x ^ ((k >> 2) & 1), y ^ ((k >> 1) & 1), c ^ (k & 1)
        out.append((k, (px, py, pc), 4 * px + 2 * py + pc))
    return 4 * x + 2 * y + c, out


def _split_copies(src_refs, land_refs, send_sems, recv_sems, gather):
    me, peers = _peers()
    out = []
    for a, (src_ref, land_ref) in enumerate(zip(src_refs, land_refs)):
        for k, dev, blk in peers:
            common = dict(send_sem=send_sems.at[a * N_DEV + k], recv_sem=recv_sems.at[a * N_DEV + k], device_id=dev,
                          device_id_type=pl.DeviceIdType.MESH)
            src = src_ref if gather else src_ref.at[blk]
            out.append((pltpu.make_async_remote_copy(src_ref=src, dst_ref=land_ref.at[me], **common),
                        pltpu.make_async_remote_copy(src_ref=src, dst_ref=land_ref.at[blk], **common)))
    return out


def _exchange_start(srcs, gather, *, name, after=None):
    n = len(srcs)
    lands = [lax.empty((N_DEV,) + (s.shape if gather else s.shape[1:]), s.dtype) for s in srcs]

    def body(*refs):
        src_refs, land_refs = refs[:n], refs[n:2 * n]
        send_sems, recv_sems = refs[-2 * n - 3:-2 * n - 1]
        token = refs[-1]
        for outgoing, _ in _split_copies(src_refs, land_refs, send_sems, recv_sems, gather):
            outgoing.start()
        token[...] = jnp.zeros_like(token)

    res = pl.pallas_call(
        body, name=name,
        out_shape=(pltpu.SemaphoreType.DMA((n * N_DEV,)), pltpu.SemaphoreType.DMA((n * N_DEV,)),
                   *[pltpu.HBM(a.shape, a.dtype) for a in srcs + lands], jax.ShapeDtypeStruct((8, LANES), F32)),
        in_specs=(_HBM,) * (2 * n) + ((pl.BlockSpec(memory_space=pl.ANY),) if after is not None else ()),
        out_specs=(_SEM, _SEM) + (_HBM,) * (2 * n) + (pl.BlockSpec(memory_space=pltpu.VMEM),),
        input_output_aliases={i: i + 2 for i in range(2 * n)},
        compiler_params=pltpu.CompilerParams(has_side_effects=_EFFECT),
    )(*[pltpu.with_memory_space_constraint(a, pltpu.HBM) for a in srcs + lands],
      *([after] if after is not None else []))
    return res[0], res[1], list(res[2:2 + n]), list(res[2 + n:2 + 2 * n]), res[-1]


def _exchange_wait(started, after, gather, *, name):
    send_sems, recv_sems, srcs, lands, _ = started
    n = len(srcs)

    def body(*refs):
        src_refs, land_refs = refs[:n], refs[n:2 * n]
        send_sems, recv_sems = refs[2 * n:2 * n + 2]
        for outgoing, incoming in _split_copies(src_refs, land_refs, send_sems, recv_sems, gather):
            outgoing.wait_send()
            incoming.wait_recv()

    res = pl.pallas_call(
        body, name=name,
        out_shape=tuple(pltpu.HBM(a.shape, a.dtype) for a in srcs + lands),
        in_specs=(_HBM,) * (2 * n) + (_SEM, _SEM, pl.BlockSpec(memory_space=pl.ANY)), out_specs=(_HBM,) * (2 * n),
        input_output_aliases={i: i for i in range(2 * n)},
        compiler_params=pltpu.CompilerParams(has_side_effects=_EFFECT),
    )(*srcs, *lands, send_sems, recv_sems, after)
    return list(res[:n]), list(res[n:])


def _cols_to_blocks(g, *, name, tr=128):
    R, C = g.shape
    w = C // N_DEV
    tr = min(tr, R)

    def body(g_ref, o_ref):
        for p in range(N_DEV):
            o_ref[p] = g_ref[:, p * w:(p + 1) * w].astype(BF16)

    return pl.pallas_call(
        body, name=name, grid=(R // tr,),
        in_specs=[pl.BlockSpec((tr, C), lambda i: (i, 0))],
        out_specs=pl.BlockSpec((N_DEV, tr, w), lambda i: (0, i, 0)),
        out_shape=jax.ShapeDtypeStruct((N_DEV, R, w), BF16),
        compiler_params=_params(("parallel",)),
    )(g)


def _blocks_to_cols(b, *, name, tr=128):
    _, R, w = b.shape
    tr = min(tr, R)

    def body(b_ref, o_ref):
        o_ref[...] = jnp.concatenate([b_ref[p].astype(F32) for p in range(N_DEV)], axis=1).astype(o_ref.dtype)

    return pl.pallas_call(
        body, name=name, grid=(R // tr,),
        in_specs=[pl.BlockSpec((N_DEV, tr, w), lambda i: (0, i, 0))],
        out_specs=pl.BlockSpec((tr, N_DEV * w), lambda i: (i, 0)),
        out_shape=jax.ShapeDtypeStruct((R, N_DEV * w), b.dtype),
        compiler_params=_params(("parallel",)),
    )(b)


def _pack_rows(arrays, *, name, pick=None):
    B, _, w = arrays[0].shape
    rows = [a.shape[1] for a in arrays]
    first = 0
    if pick is not None:
        B, first = 1, pick

    def body(*refs):
        o_ref = refs[-1]
        r = 0
        for a_ref, n in zip(refs[:-1], rows):
            o_ref[0, r:r + n, :] = a_ref[0].astype(BF16)
            r += n

    return pl.pallas_call(
        body, name=name, grid=(B,),
        in_specs=[pl.BlockSpec((1, n, w), lambda b: (first + b, 0, 0)) for n in rows],
        out_specs=pl.BlockSpec((1, sum(rows), w), lambda b: (b, 0, 0)),
        out_shape=jax.ShapeDtypeStruct((B, sum(rows), w), BF16),
        compiler_params=_params(("parallel",)),
    )(*arrays)


def _unpack_rows(land, src, rows, *, name):
    _, R, w = land.shape
    src_spec = (pl.BlockSpec((1, R, w), lambda p: (p, 0, 0)) if src.ndim == 3
                else pl.BlockSpec((R, w), lambda p: (0, 0)))

    def body(land_ref, src_ref, *o_refs):
        me = 4 * lax.axis_index("x") + 2 * lax.axis_index("y") + lax.axis_index("c")
        mine = pl.program_id(0) == me
        r = 0
        for o_ref, n in zip(o_refs, rows):
            rows_i = slice(r, r + n)

            @pl.when(mine)
            def _(o_ref=o_ref, rows_i=rows_i):
                o_ref[0] = src_ref[0, rows_i, :] if src.ndim == 3 else src_ref[rows_i, :]

            @pl.when(jnp.logical_not(mine))
            def _(o_ref=o_ref, rows_i=rows_i):
                o_ref[0] = land_ref[0, rows_i, :]

            r += n

    return pl.pallas_call(
        body, name=name, grid=(N_DEV,),
        in_specs=[pl.BlockSpec((1, R, w), lambda p: (p, 0, 0)), src_spec],
        out_specs=[pl.BlockSpec((1, n, w), lambda p: (p, 0, 0)) for n in rows],
        out_shape=[jax.ShapeDtypeStruct((N_DEV, n, w), land.dtype) for n in rows],
        compiler_params=_params(("parallel",)),
    )(land, src)


def _to_blocks(w, axis):
    shape = w.shape
    k = shape[axis] // N_DEV
    w = w.reshape(shape[:axis] + (N_DEV, k) + shape[axis + 1:])
    return jnp.moveaxis(w, axis, 0)


def _from_blocks(wb, axis):
    w = jnp.moveaxis(wb, 0, axis)
    shape = w.shape
    return w.reshape(shape[:axis] + (shape[axis] * shape[axis + 1],) + shape[axis + 2:])


def _block_diag(w):
    n, k, _ = w.shape
    eye = jnp.eye(n, dtype=w.dtype)
    return (eye[:, None, :, None] * w[:, :, None, :]).reshape(n * k, n * k)


def _diag_blocks(wd):
    k = HEAD_DIM
    return jnp.stack([wd[h * k:(h + 1) * k, h * k:(h + 1) * k] for h in range(C_BLOCKS)])


def _pack(arrays):
    rows = []
    for a in arrays:
        flat = a.reshape(-1).astype(F32)
        pad = (-flat.shape[0]) % LANES
        rows.append(jnp.pad(flat, (0, pad)).reshape(-1, LANES))
    out = jnp.concatenate(rows, axis=0)
    return jnp.pad(out, ((0, (-out.shape[0]) % 8), (0, 0)))


def _unpack(packed, shapes):
    outs, r = [], 0
    for s in shapes:
        size = math.prod(s)
        nrows = -(-size // LANES)
        outs.append(packed[r:r + nrows].reshape(-1)[:size].reshape(s))
        r += nrows
    return outs


def _rope_tables(positions):
    inv = 1.0 / (ROPE_THETA ** (jnp.arange(0, HEAD_DIM, 2, dtype=F32) / HEAD_DIM))
    ang = positions.astype(F32)[:, None] * inv
    cos, sin = jnp.cos(ang), jnp.sin(ang)
    return jnp.tile(cos, (1, 4)), jnp.tile(jnp.concatenate([-sin, sin], axis=1), (1, 2))


def _dilate(t, d):
    return t.reshape(t.shape[0] // d, d * t.shape[1])


def _layer_fwd(l, x, pos, cos_t, sin_t, W):
    tag = f"l{l}"
    saved = {'x0': x}
    x1, a1, u1 = _ffn_fwd(x, W['norm_ffn1'][l], W['ffn1_gate'][l], W['ffn1_up'][l], W['ffn1_down'][l],
                          name=f"ffn1_fwd_{tag}")
    h = _rms_fwd(x1, W['norm_mix'][l], name=f"mixnorm_fwd_{tag}")
    proj = _mm(h, W['w_in'][l], 'nn', name=f"proj_{tag}", tm=512, tn=IN_COLS, tk=h.shape[1])
    qa, ka2, va2, qb, kb, vb, xc, gc = _split_rope(proj, cos_t, sin_t, name=f"split_{tag}")
    oa, lsea = _attn_fwd(qa, ka2, va2, A_MAX_DIST, name=f"attn_a_fwd_{tag}")
    obs, lsebs = [], []
    for bi, (window, d) in enumerate(B_BRANCHES):
        o, lse = _attn_fwd(_dilate(qb, d), _dilate(kb, d), _dilate(vb, d), window // d,
                           name=f"attn_b{bi}_fwd_{tag}")
        obs.append(o.reshape(qb.shape))
        lsebs.append(lse.reshape(qb.shape))
    oc, hs = _rglru_fwd(xc, gc, pos, W['conv_w'][l], W['conv_b'][l], W['rg_w_r'][l], W['rg_b_r'][l],
                        W['rg_w_i'][l], W['rg_b_i'][l], W['rg_lambda'][l], name=f"rglru_fwd_{tag}")
    mix = _combine_fwd(oa, lsea, W['sinks'][l], obs, lsebs, oc, name=f"combine_fwd_{tag}")
    x2 = _mm(mix, W['w_out'][l], 'nn', name=f"outproj_{tag}", tm=512, tn=x.shape[1], tk=MIX_WIDTH, res=x1)
    x3, a2, u2 = _ffn_fwd(x2, W['norm_ffn2'][l], W['ffn2_gate'][l], W['ffn2_up'][l], W['ffn2_down'][l],
                          name=f"ffn2_fwd_{tag}")
    saved.update(a1=a1, u1=u1, x1=x1, h=h, qa=qa, ka2=ka2, va2=va2, qb=qb, kb=kb, vb=vb, xc=xc, gc=gc, oa=oa,
                 lsea=lsea, obs=obs, lsebs=lsebs, hs=hs, mix=mix, x2=x2, a2=a2, u2=u2)
    return x3, saved


def _ffn_grads(tag, which, x, g, dy, a, u, wg, wu, wd):
    T, D = x.shape
    F = wg.shape[1]
    dx, dg, n, act, da, du = _ffn_bwd(x, g, dy, a, u, wg, wu, wd, name=f"{which}_bwd_{tag}")
    fc = _ffn_chunk(F)
    d_gate = _mm(n, da, 'tn', name=f"{which}_dgate_{tag}", tm=D, tn=fc, tk=2048)
    d_up = _mm(n, du, 'tn', name=f"{which}_dup_{tag}", tm=D, tn=fc, tk=2048)
    d_down = _mm(act, dy, 'tn', name=f"{which}_ddown_{tag}", tm=fc, tn=D, tk=1024, alpha=0.5)
    return dx, dg, d_gate, d_up, d_down


def _layer_bwd(l, dx3, pos, cos_t, sin_t, W, S, on_grads=None):
    tag = f"l{l}"
    G = {}
    dx2, G['norm_ffn2'], G['ffn2_gate'], G['ffn2_up'], G['ffn2_down'] = _ffn_grads(
        tag, 'ffn2', S['x2'], W['norm_ffn2'][l], dx3, S['a2'], S['u2'], W['ffn2_gate'][l], W['ffn2_up'][l],
        W['ffn2_down'][l])
    D = dx2.shape[1]
    dmix = _mm(dx2, W['w_out'][l], 'nt', name=f"outproj_dx_{tag}", tm=512, tn=MIX_WIDTH, tk=D)
    G['w_out'] = _mm(S['mix'], dx2, 'tn', name=f"outproj_dw_{tag}", tm=MIX_WIDTH, tn=D, tk=2048)
    doa, dla, dobs, dlbs, doc, dsink = _combine_bwd(dmix, S['oa'], S['lsea'], W['sinks'][l], S['obs'], S['lsebs'],
                                                    name=f"combine_bwd_{tag}")
    G['attn_sinks'] = dsink.reshape(A_WIDTH // HEAD_DIM, HEAD_DIM)[:, 0]
    dqa, dka2, dva2 = _attn_bwd(S['qa'], S['ka2'], S['va2'], doa, S['lsea'], dla, A_MAX_DIST,
                                name=f"attn_a_bwd_{tag}")
    dqb, dkb, dvb = [], [], []
    shape = S['qb'].shape
    for bi, (window, d) in enumerate(B_BRANCHES):
        dq, dk, dv = _attn_bwd(_dilate(S['qb'], d), _dilate(S['kb'], d), _dilate(S['vb'], d), _dilate(dobs[bi], d),
                               _dilate(S['lsebs'][bi], d), _dilate(dlbs[bi], d), window // d,
                               name=f"attn_b{bi}_bwd_{tag}")
        dqb.append(dq.reshape(shape))
        dkb.append(dk.reshape(shape))
        dvb.append(dv.reshape(shape))
    (dxc, dgc, G['conv_w'], G['conv_b'], dwr, G['rg_b_r'], dwi, G['rg_b_i'], G['rg_lambda']) = _rglru_bwd(
        S['xc'], S['gc'], pos, S['hs'], doc, W['conv_w'][l], W['conv_b'][l], W['rg_w_r'][l], W['rg_b_r'][l],
        W['rg_w_i'][l], W['rg_b_i'][l], W['rg_lambda'][l], name=f"rglru_bwd_{tag}")
    G['rg_w_r'] = _diag_blocks(dwr)
    G['rg_w_i'] = _diag_blocks(dwi)
    dproj = _merge_dproj(dqa, dka2, dva2, dqb, dkb, dvb, dxc, dgc, cos_t, sin_t, name=f"merge_{tag}")
    dh = _mm(dproj, W['w_in'][l], 'nt', name=f"proj_dx_{tag}", tm=512, tn=D, tk=IN_COLS)
    G['w_in'] = _mm(S['h'], dproj, 'tn', name=f"proj_dw_{tag}", tm=D, tn=IN_COLS, tk=1024)
    g_mix = W['norm_mix'][l]
    if on_grads is not None:
        g_mix = g_mix + on_grads(l, 0, G)
    dx1, G['norm_mix'] = _rms_bwd(S['x1'], g_mix, dh, dx2, name=f"mixnorm_bwd_{tag}")
    dx0, G['norm_ffn1'], G['ffn1_gate'], G['ffn1_up'], G['ffn1_down'] = _ffn_grads(
        tag, 'ffn1', S['x0'], W['norm_ffn1'][l], dx1, S['a1'], S['u1'], W['ffn1_gate'][l], W['ffn1_up'][l],
        W['ffn1_down'][l])
    if on_grads is not None:
        on_grads(l, 1, G)
    return dx0, G


def _device_step(x, positions, loss_target, W, before_layer=None, on_grads=None):
    T = x.shape[0]
    pos = positions.reshape(T, 1)
    cos_t, sin_t = _rope_tables(positions)
    saved = []
    for l in range(DEPTH):
        if before_layer is not None:
            before_layer(l, x)
        x, S = _layer_fwd(l, x, pos, cos_t, sin_t, W)
        saved.append(S)
    loss, dx, dg_final = _loss_head(x, W['norm_final'], loss_target, name="loss_head")
    grads = [None] * DEPTH
    for l in reversed(range(DEPTH)):
        dx, grads[l] = _layer_bwd(l, dx, pos, cos_t, sin_t, W, saved[l], on_grads)
    return loss, dx, grads, dg_final


SHARD_AXIS = {'ffn1_gate': 2, 'ffn1_up': 2, 'ffn1_down': 1, 'w_in': 2, 'w_out': 1, 'ffn2_gate': 2, 'ffn2_up': 2,
              'ffn2_down': 1, 'conv_w': 2}


def kernel(x, positions, norm_ffn1, ffn1_gate, ffn1_up, ffn1_down, norm_mix, w_in, attn_sinks, conv_w, conv_b, rg_w_r, rg_b_r, rg_w_i, rg_b_i, rg_lambda, w_out, norm_ffn2, ffn2_gate, ffn2_up, ffn2_down, norm_final, loss_target, m_norm_ffn1, m_ffn1_gate, m_ffn1_up, m_ffn1_down, m_norm_mix, m_w_in, m_attn_sinks, m_conv_w, m_conv_b, m_rg_w_r, m_rg_b_r, m_rg_w_i, m_rg_b_i, m_rg_lambda, m_w_out, m_norm_ffn2, m_ffn2_gate, m_ffn2_up, m_ffn2_down, m_norm_final, v_norm_ffn1, v_ffn1_gate, v_ffn1_up, v_ffn1_down, v_norm_mix, v_w_in, v_attn_sinks, v_conv_w, v_conv_b, v_rg_w_r, v_rg_b_r, v_rg_w_i, v_rg_b_i, v_rg_lambda, v_w_out, v_norm_ffn2, v_ffn2_gate, v_ffn2_up, v_ffn2_down, v_norm_final):
    given = dict(locals())
    me = 4 * lax.axis_index("x") + 2 * lax.axis_index("y") + lax.axis_index("c")

    def by_width(names):
        classes = {}
        for n in names:
            classes.setdefault(given[n].shape[2], []).append(n)
        return list(classes.values())

    def pack(names, get, tag, pick=None):
        return [_pack_rows([get(n) for n in cls], name=f"pack{ci}_{tag}", pick=pick)
                for ci, cls in enumerate(by_width(names))]

    def unpack(names, lands, srcs, tag):
        out = {}
        for ci, cls in enumerate(by_width(names)):
            arrays = _unpack_rows(lands[ci], srcs[ci], [given[n].shape[1] for n in cls], name=f"unpack{ci}_{tag}")
            out.update(zip(cls, arrays))
        return out

    def gather_start(l, after=None):
        return _exchange_start([p[0] for p in pack(BIG_NAMES, lambda n: given[n], f"w_l{l}", pick=l)], True,
                               name=f"gather_start_l{l}", after=after)

    def gather_wait(l, started, after):
        srcs, lands = _exchange_wait(started, after, True, name=f"gather_wait_l{l}")
        blocks = unpack(BIG_NAMES, lands, srcs, f"w_l{l}")
        for n in BIG_NAMES:
            if SHARD_AXIS[n] == 2:
                W[n][l] = _blocks_to_cols(blocks[n], name=f"cols_{n}_l{l}")
            else:
                W[n][l] = blocks[n].reshape(-1, blocks[n].shape[2])
        return lands[0]

    W = {n: [None] * DEPTH for n in BIG_NAMES}
    first = gather_start(0)
    conv_full = _exchange([conv_w], True, name="gather_conv_w")[0]
    landed = gather_wait(0, first, conv_full)
    gathers = [None] + [gather_start(l, after=landed) for l in range(1, DEPTH)]
    started = functools.reduce(lambda a, b: a + b, [g[4][0, 0] for g in gathers[1:]])

    def before_layer(l, x_in):
        if l > 0:
            gather_wait(l, gathers[l], x_in)

    scatters = {}

    def on_grads(l, stage, G):
        def blocks_of(n):
            if SHARD_AXIS[n] == 2:
                return _cols_to_blocks(G[n], name=f"blocks_{n}_l{l}")
            return G[n].reshape(N_DEV, -1, G[n].shape[1])

        scatters[l, stage] = _exchange_start(pack(SCATTER_STAGES[stage], blocks_of, f"g{stage}_l{l}"), False,
                                             name=f"scatter_start{stage}_l{l}")
        token = scatters[l, stage][4][0, 0]
        if stage == 1 and l > 0:
            W['norm_ffn2'][l - 1] = W['norm_ffn2'][l - 1] + token
        return token

    W['conv_w'] = [_from_blocks(conv_full[:, l], 1) for l in range(DEPTH)]
    for n in ('norm_ffn1', 'norm_mix', 'norm_ffn2', 'conv_b', 'rg_lambda'):
        W[n] = [given[n][l][None, :] for l in range(DEPTH)]
    W['norm_final'] = norm_final[None, :]
    W['sinks'] = [jnp.repeat(attn_sinks[l], HEAD_DIM)[None, :] for l in range(DEPTH)]
    for n in ('rg_w_r', 'rg_w_i'):
        W[n] = [_block_diag(given[n][l]).astype(BF16) for l in range(DEPTH)]
    for n in ('rg_b_r', 'rg_b_i'):
        W[n] = [given[n][l].reshape(1, C_WIDTH) for l in range(DEPTH)]

    W['norm_ffn1'][0] = W['norm_ffn1'][0] + started

    loss_part, grad_x, grads, dg_final = _device_step(x[0], positions[0], loss_target[0], W, before_layer, on_grads)
    loss = lax.psum(loss_part[0, 0], ("x", "y", "c"))

    small_shapes = [given[n].shape for n in SMALL_NAMES] + [(DEPTH, C_CONV, C_WIDTH)]
    small_grads = []
    for n in SMALL_NAMES:
        if n == 'norm_final':
            small_grads.append(dg_final.reshape(-1))
        else:
            small_grads.append(jnp.stack([grads[l][n].reshape(given[n].shape[1:]) for l in range(DEPTH)]))
    small_grads.append(jnp.stack([grads[l]['conv_w'] for l in range(DEPTH)]))
    small_parts = _exchange([_pack(small_grads)], True, name="gather_small_grads")[0]

    out = {}
    for stage in (0, 1):
        parts = {}
        for l in reversed(range(DEPTH)):
            last = stage == 1 and l == 0
            srcs, lands = _exchange_wait(scatters[l, stage], out['w_in'][1] if last else grad_x, False,
                                         name=f"scatter_wait{stage}_l{l}")
            parts[l] = unpack(SCATTER_STAGES[stage], lands, srcs, f"g{stage}_l{l}")
        for n in SCATTER_STAGES[stage]:
            shape = given[n].shape
            two_d = (shape[0] * shape[1], shape[2])
            res = None
            for l in reversed(range(DEPTH)):
                res = _adamw(parts[l][n], given[n].reshape(two_d), given['m_' + n].reshape(two_d),
                             given['v_' + n].reshape(two_d), name=f"adamw_{n}_l{l}", part=l, prev=res)
            out[n] = [r.reshape(shape) for r in res]

    w_small = [given[n] for n in SMALL_NAMES]
    m_small = [given['m_' + n] for n in SMALL_NAMES]
    v_small = [given['v_' + n] for n in SMALL_NAMES]
    zeros_cw = jnp.zeros((DEPTH, C_CONV, C_WIDTH), F32)
    res = _adamw(small_parts, _pack(w_small + [zeros_cw]), _pack(m_small + [zeros_cw]), _pack(v_small + [zeros_cw]),
                 name="adamw_small", tr=8)
    unpacked = [_unpack(r, small_shapes) for r in res]
    for i, n in enumerate(SMALL_NAMES):
        out[n] = [u[i] for u in unpacked]

    k = conv_w.shape[2]
    g_cw = lax.dynamic_slice_in_dim(unpacked[0][-1], me * k, k, axis=2)
    zero_parts = jnp.zeros((N_DEV - 1,) + (8, LANES), F32)
    res = _adamw(jnp.concatenate([_pack([g_cw])[None], zero_parts]), _pack([conv_w]), _pack([m_conv_w]),
                 _pack([v_conv_w]), name="adamw_conv_w", tr=8)
    out['conv_w'] = [_unpack(r, [conv_w.shape])[0] for r in res]

    outputs = [loss, grad_x[None]]
    for i in range(4):
        outputs += [out[n][i] for n in WEIGHT_NAMES]
    return tuple(outputs)
```

```python
import functools
import math

import jax
import jax.numpy as jnp
from jax import lax
from jax.experimental import pallas as pl
from jax.experimental.pallas import tpu as pltpu

F32 = jnp.float32
BF16 = jnp.bfloat16

N_DEV = 8
DEPTH = 4
HEAD_DIM = 64
LANES = 128
QBLK = 128
A_WIDTH = 256
A_KV_WIDTH = 128
B_WIDTH = 384
C_WIDTH = 384
C_BLOCKS = 6
C_CONV = 4
C_EXP = 8.0
MIX_WIDTH = A_WIDTH + B_WIDTH + C_WIDTH
IN_COLS = A_WIDTH + 2 * A_KV_WIDTH + 3 * B_WIDTH + 2 * C_WIDTH
A_MAX_DIST = 127
B_BRANCHES = ((128, 1), (512, 4), (2048, 16))
ROPE_THETA = 10000.0
EPS = 1e-6
SCALE = HEAD_DIM ** -0.5

ADAM_LR = 0.001
ADAM_B1 = 0.9
ADAM_B2 = 0.999
ADAM_EPS = 1e-08
ADAM_WD = 0.01
ADAM_STEP = 10

ATTN_CHUNK = 1024
ATTN_FWD_UNROLL = True
ATTN_BWD_UNROLL = True
VMEM_LIMIT = 56 * 1024 * 1024

NT_DIMS = (((1,), (1,)), ((), ()))
TN_DIMS = (((0,), (0,)), ((), ()))
NN_DIMS = (((1,), (0,)), ((), ()))

WEIGHT_NAMES = ['norm_ffn1', 'ffn1_gate', 'ffn1_up', 'ffn1_down', 'norm_mix', 'w_in', 'attn_sinks', 'conv_w',
                'conv_b', 'rg_w_r', 'rg_b_r', 'rg_w_i', 'rg_b_i', 'rg_lambda', 'w_out', 'norm_ffn2', 'ffn2_gate',
                'ffn2_up', 'ffn2_down', 'norm_final']
BIG_NAMES = ['ffn1_gate', 'ffn1_up', 'ffn1_down', 'w_in', 'w_out', 'ffn2_gate', 'ffn2_up', 'ffn2_down']
SCATTER_STAGES = (['ffn2_gate', 'ffn2_up', 'ffn2_down', 'w_out', 'w_in'], ['ffn1_gate', 'ffn1_up', 'ffn1_down'])
SMALL_NAMES = ['norm_ffn1', 'norm_mix', 'norm_ffn2', 'norm_final', 'attn_sinks', 'conv_b', 'rg_w_r', 'rg_b_r',
               'rg_w_i', 'rg_b_i', 'rg_lambda']


def _params(sem, vmem=VMEM_LIMIT):
    return pltpu.CompilerParams(dimension_semantics=sem, vmem_limit_bytes=vmem)


def _dot(a, b, dims=NN_DIMS):
    return lax.dot_general(a, b, dims, preferred_element_type=F32)


def _sigmoid(x):
    return 1.0 / (1.0 + jnp.exp(-x))


def _mm(a, b, mode, *, name, tm=512, tn=512, tk=512, out_dtype=F32, alpha=1.0, res=None):
    if mode == 'nn':
        (M, K), N = a.shape, b.shape[1]
    elif mode == 'nt':
        (M, K), N = a.shape, b.shape[0]
    else:
        (K, M), N = a.shape, b.shape[1]
    tm, tn, tk = min(tm, M), min(tn, N), min(tk, K)
    ni, nj, nk = M // tm, N // tn, K // tk
    assert ni * tm == M and nj * tn == N and nk * tk == K, (name, a.shape, b.shape, tm, tn, tk)
    if mode == 'tn':
        a_spec = pl.BlockSpec((tk, tm), lambda j, i, k: (k, i))
    else:
        a_spec = pl.BlockSpec((tm, tk), lambda j, i, k: (i, k))
    if mode == 'nt':
        b_spec = pl.BlockSpec((tn, tk), lambda j, i, k: (j, k))
    else:
        b_spec = pl.BlockSpec((tk, tn), lambda j, i, k: (k, j))
    dims = {'nn': NN_DIMS, 'nt': NT_DIMS, 'tn': TN_DIMS}[mode]
    o_spec = pl.BlockSpec((tm, tn), lambda j, i, k: (i, j))
    has_res = res is not None

    def body(*refs):
        if has_res:
            a_ref, b_ref, r_ref, o_ref = refs[:4]
        else:
            a_ref, b_ref, o_ref = refs[:3]
        part = _dot(a_ref[...].astype(BF16), b_ref[...].astype(BF16), dims)

        def finish(acc):
            out = acc * alpha if alpha != 1.0 else acc
            if has_res:
                out = r_ref[...] + out
            o_ref[...] = out.astype(out_dtype)

        if nk == 1:
            finish(part)
        else:
            acc_ref = refs[-1]
            k = pl.program_id(2)

            @pl.when(k == 0)
            def _():
                acc_ref[...] = part

            @pl.when(k > 0)
            def _():
                acc_ref[...] += part

            @pl.when(k == nk - 1)
            def _():
                finish(acc_ref[...])

    in_specs = [a_spec, b_spec] + ([o_spec] if has_res else [])
    operands = [a, b] + ([res] if has_res else [])
    return pl.pallas_call(
        body, name=name, grid=(nj, ni, nk), in_specs=in_specs, out_specs=o_spec,
        out_shape=jax.ShapeDtypeStruct((M, N), out_dtype),
        scratch_shapes=[pltpu.VMEM((tm, tn), F32)] if nk > 1 else [],
        compiler_params=_params(("parallel", "parallel", "arbitrary")),
    )(*operands)


def _rms_fwd(x, g, *, name, tm=512):
    T, D = x.shape
    tm = min(tm, T)

    def body(x_ref, g_ref, o_ref):
        xv = x_ref[...]
        rstd = lax.rsqrt(jnp.mean(xv * xv, axis=-1, keepdims=True) + EPS)
        o_ref[...] = (xv * rstd * g_ref[...]).astype(BF16)

    return pl.pallas_call(
        body, name=name, grid=(T // tm,),
        in_specs=[pl.BlockSpec((tm, D), lambda i: (i, 0)), pl.BlockSpec((1, D), lambda i: (0, 0))],
        out_specs=pl.BlockSpec((tm, D), lambda i: (i, 0)),
        out_shape=jax.ShapeDtypeStruct((T, D), BF16),
        compiler_params=_params(("parallel",)),
    )(x, g)


def _rms_bwd_math(xv, g, dn):
    rstd = lax.rsqrt(jnp.mean(xv * xv, axis=-1, keepdims=True) + EPS)
    xhat = xv * rstd
    dxhat = dn * g
    dx = rstd * (dxhat - xhat * jnp.mean(dxhat * xhat, axis=-1, keepdims=True))
    return dx, jnp.sum(dn * xhat, axis=0, keepdims=True)


def _rms_bwd(x, g, dn, dres, *, name, tm=512):
    T, D = x.shape
    tm = min(tm, T)

    def body(x_ref, g_ref, dn_ref, dres_ref, dx_ref, dg_ref):
        dx, dg = _rms_bwd_math(x_ref[...], g_ref[...], dn_ref[...])
        dx_ref[...] = dres_ref[...] + dx

        @pl.when(pl.program_id(0) == 0)
        def _():
            dg_ref[...] = jnp.zeros_like(dg_ref)

        dg_ref[...] += dg

    row = pl.BlockSpec((tm, D), lambda i: (i, 0))
    vec = pl.BlockSpec((1, D), lambda i: (0, 0))
    return pl.pallas_call(
        body, name=name, grid=(T // tm,),
        in_specs=[row, vec, row, row], out_specs=[row, vec],
        out_shape=[jax.ShapeDtypeStruct((T, D), F32), jax.ShapeDtypeStruct((1, D), F32)],
        compiler_params=_params(("arbitrary",)),
    )(x, g, dn, dres)


def _loss_head(x, g, target, *, name, tm=512):
    T, D = x.shape
    tm = min(tm, T)

    def body(x_ref, g_ref, t_ref, loss_ref, dx_ref, dg_ref):
        xv = x_ref[...]
        g = g_ref[...]
        rstd = lax.rsqrt(jnp.mean(xv * xv, axis=-1, keepdims=True) + EPS)
        y = xv * rstd * g
        err = y - t_ref[...]
        part = 0.5 * jnp.sum(jnp.mean(err * err, axis=-1, keepdims=True), axis=0, keepdims=True)
        dx, dg = _rms_bwd_math(xv, g, err * (1.0 / D))
        dx_ref[...] = dx

        @pl.when(pl.program_id(0) == 0)
        def _():
            dg_ref[...] = jnp.zeros_like(dg_ref)
            loss_ref[...] = jnp.zeros_like(loss_ref)

        dg_ref[...] += dg
        loss_ref[...] += jnp.broadcast_to(part, loss_ref.shape)

    row = pl.BlockSpec((tm, D), lambda i: (i, 0))
    vec = pl.BlockSpec((1, D), lambda i: (0, 0))
    lspec = pl.BlockSpec((1, LANES), lambda i: (0, 0))
    return pl.pallas_call(
        body, name=name, grid=(T // tm,),
        in_specs=[row, vec, row], out_specs=[lspec, row, vec],
        out_shape=[jax.ShapeDtypeStruct((1, LANES), F32), jax.ShapeDtypeStruct((T, D), F32),
                   jax.ShapeDtypeStruct((1, D), F32)],
        compiler_params=_params(("arbitrary",)),
    )(x, g, target)


def _resident(shape):
    return pl.BlockSpec(shape, lambda i: (0,) * len(shape), pipeline_mode=pl.Buffered(1))


def _ffn_chunk(F):
    for c in (1408, 1024, 512, 256, 128):
        if F % c == 0:
            return c
    return F


def _ffn_fwd(x, g, wg, wu, wd, *, name, tm=256):
    T, D = x.shape
    F = wg.shape[1]
    tm = min(tm, T)
    fc = _ffn_chunk(F)

    def body(x_ref, g_ref, wg_ref, wu_ref, wd_ref, o_ref, a_ref, u_ref):
        xv = x_ref[...]
        rstd = lax.rsqrt(jnp.mean(xv * xv, axis=-1, keepdims=True) + EPS)
        n = (xv * rstd * g_ref[...]).astype(BF16)
        acc = jnp.zeros((tm, D), F32)
        for c in range(F // fc):
            sl = slice(c * fc, (c + 1) * fc)
            a = _dot(n, wg_ref[:, sl])
            u = _dot(n, wu_ref[:, sl])
            a_ref[:, sl] = a.astype(BF16)
            u_ref[:, sl] = u.astype(BF16)
            act = (a * _sigmoid(a) * u).astype(BF16)
            acc = acc + _dot(act, wd_ref[sl, :])
        o_ref[...] = xv + 0.5 * acc

    row = pl.BlockSpec((tm, D), lambda i: (i, 0))
    hid = pl.BlockSpec((tm, F), lambda i: (i, 0))
    return pl.pallas_call(
        body, name=name, grid=(T // tm,),
        in_specs=[row, pl.BlockSpec((1, D), lambda i: (0, 0)),
                  _resident((D, F)), _resident((D, F)), _resident((F, D))],
        out_specs=[row, hid, hid],
        out_shape=[jax.ShapeDtypeStruct((T, D), F32), jax.ShapeDtypeStruct((T, F), BF16),
                   jax.ShapeDtypeStruct((T, F), BF16)],
        compiler_params=_params(("parallel",)),
    )(x, g, wg, wu, wd)


def _ffn_bwd(x, g, dy, a, u, wg, wu, wd, *, name, tm=256):
    T, D = x.shape
    F = wg.shape[1]
    tm = min(tm, T)
    fc = _ffn_chunk(F)

    def body(x_ref, g_ref, dy_ref, a_ref, u_ref, wg_ref, wu_ref, wd_ref,
             dx_ref, dg_ref, n_ref, act_ref, da_ref, du_ref):
        xv = x_ref[...]
        g = g_ref[...]
        rstd = lax.rsqrt(jnp.mean(xv * xv, axis=-1, keepdims=True) + EPS)
        n_ref[...] = (xv * rstd * g).astype(BF16)
        dy = dy_ref[...]
        dyh = (0.5 * dy).astype(BF16)
        dn = jnp.zeros((tm, D), F32)
        for c in range(F // fc):
            sl = slice(c * fc, (c + 1) * fc)
            av = a_ref[:, sl].astype(F32)
            uv = u_ref[:, sl].astype(F32)
            dact = _dot(dyh, wd_ref[sl, :], NT_DIMS)
            s = _sigmoid(av)
            silu = av * s
            act_ref[:, sl] = (silu * uv).astype(BF16)
            da = (dact * uv * (s * (1.0 + av * (1.0 - s)))).astype(BF16)
            du = (dact * silu).astype(BF16)
            da_ref[:, sl] = da
            du_ref[:, sl] = du
            dn = dn + _dot(da, wg_ref[:, sl], NT_DIMS) + _dot(du, wu_ref[:, sl], NT_DIMS)
        dx, dg = _rms_bwd_math(xv, g, dn)
        dx_ref[...] = dy + dx

        @pl.when(pl.program_id(0) == 0)
        def _():
            dg_ref[...] = jnp.zeros_like(dg_ref)

        dg_ref[...] += dg

    row = pl.BlockSpec((tm, D), lambda i: (i, 0))
    hid = pl.BlockSpec((tm, F), lambda i: (i, 0))
    vec = pl.BlockSpec((1, D), lambda i: (0, 0))
    return pl.pallas_call(
        body, name=name, grid=(T // tm,),
        in_specs=[row, vec, row, hid, hid,
                  _resident((D, F)), _resident((D, F)), _resident((F, D))],
        out_specs=[row, vec, row, hid, hid, hid],
        out_shape=[jax.ShapeDtypeStruct((T, D), F32), jax.ShapeDtypeStruct((1, D), F32),
                   jax.ShapeDtypeStruct((T, D), BF16), jax.ShapeDtypeStruct((T, F), BF16),
                   jax.ShapeDtypeStruct((T, F), BF16), jax.ShapeDtypeStruct((T, F), BF16)],
        compiler_params=_params(("arbitrary",)),
    )(x, g, dy, a, u, wg, wu, wd)


def _lane_iota(shape):
    return lax.broadcasted_iota(jnp.int32, shape, 1)


def _rope_partner(x):
    first_half = (_lane_iota(x.shape) & (HEAD_DIM - 1)) < HEAD_DIM // 2
    return jnp.where(first_half, pltpu.roll(x, LANES - HEAD_DIM // 2, 1), pltpu.roll(x, HEAD_DIM // 2, 1))


def _swap_heads(x):
    return pltpu.roll(x, HEAD_DIM, 1)


def _split_rope(proj, cos_t, sin_t, *, name, tm=256):
    T = proj.shape[0]
    tm = min(tm, T)

    def body(p_ref, c_ref, s_ref, qa_ref, ka_ref, va_ref, qb_ref, kb_ref, vb_ref, xc_ref, gc_ref):
        cos = c_ref[...]
        sin = s_ref[...]

        def rope(x):
            return x * cos + _rope_partner(x) * sin

        lo = _lane_iota((tm, LANES)) < HEAD_DIM
        col = 0
        for j in range(A_WIDTH // LANES):
            qa_ref[:, j * LANES:(j + 1) * LANES] = (rope(p_ref[:, col:col + LANES]) * SCALE).astype(BF16)
            col += LANES
        kr = rope(p_ref[:, col:col + LANES])
        col += LANES
        vr = p_ref[:, col:col + LANES]
        col += LANES
        for src, dst in ((kr, ka_ref), (vr, va_ref)):
            sw = _swap_heads(src)
            dst[:, 0:LANES] = jnp.where(lo, src, sw).astype(BF16)
            dst[:, LANES:2 * LANES] = jnp.where(lo, sw, src).astype(BF16)
        for dst, roped, scale in ((qb_ref, True, SCALE), (kb_ref, True, 1.0), (vb_ref, False, 1.0)):
            for j in range(B_WIDTH // LANES):
                v = p_ref[:, col:col + LANES]
                if roped:
                    v = rope(v) * scale
                dst[:, j * LANES:(j + 1) * LANES] = v.astype(BF16)
                col += LANES
        xc_ref[...] = p_ref[:, col:col + C_WIDTH]
        gc_ref[...] = p_ref[:, col + C_WIDTH:col + 2 * C_WIDTH]

    def row(w):
        return pl.BlockSpec((tm, w), lambda i: (i, 0))

    widths = [A_WIDTH, A_WIDTH, A_WIDTH, B_WIDTH, B_WIDTH, B_WIDTH, C_WIDTH, C_WIDTH]
    dtypes = [BF16] * 6 + [F32] * 2
    return pl.pallas_call(
        body, name=name, grid=(T // tm,),
        in_specs=[row(IN_COLS), row(LANES), row(LANES)],
        out_specs=[row(w) for w in widths],
        out_shape=[jax.ShapeDtypeStruct((T, w), d) for w, d in zip(widths, dtypes)],
        compiler_params=_params(("parallel",)),
    )(proj, cos_t, sin_t)


def _merge_dproj(dqa, dka2, dva2, dqb, dkb, dvb, dxc, dgc, cos_t, sin_t, *, name, tm=256):
    T = dqa.shape[0]
    tm = min(tm, T)
    nb = len(dqb)

    def body(*refs):
        dqa_ref, dka_ref, dva_ref = refs[:3]
        dqb_refs = refs[3:3 + nb]
        dkb_refs = refs[3 + nb:3 + 2 * nb]
        dvb_refs = refs[3 + 2 * nb:3 + 3 * nb]
        dxc_ref, dgc_ref, c_ref, s_ref, o_ref = refs[3 + 3 * nb:]
        cos = c_ref[...]
        sin = s_ref[...]

        def rope_t(dy):
            return dy * cos - _rope_partner(dy) * sin

        lo = _lane_iota((tm, LANES)) < HEAD_DIM
        col = 0
        for j in range(A_WIDTH // LANES):
            o_ref[:, col:col + LANES] = (rope_t(dqa_ref[:, j * LANES:(j + 1) * LANES]) * SCALE).astype(BF16)
            col += LANES
        for src, roped in ((dka_ref, True), (dva_ref, False)):
            b0 = src[:, 0:LANES]
            b1 = src[:, LANES:2 * LANES]
            v = jnp.where(lo, b0 + _swap_heads(b0), b1 + _swap_heads(b1))
            if roped:
                v = rope_t(v)
            o_ref[:, col:col + LANES] = v.astype(BF16)
            col += LANES
        for group, roped, scale in ((dqb_refs, True, SCALE), (dkb_refs, True, 1.0), (dvb_refs, False, 1.0)):
            for j in range(B_WIDTH // LANES):
                sl = slice(j * LANES, (j + 1) * LANES)
                v = group[0][:, sl]
                for r in group[1:]:
                    v = v + r[:, sl]
                if roped:
                    v = rope_t(v) * scale
                o_ref[:, col:col + LANES] = v.astype(BF16)
                col += LANES
        o_ref[:, col:col + C_WIDTH] = dxc_ref[...].astype(BF16)
        o_ref[:, col + C_WIDTH:col + 2 * C_WIDTH] = dgc_ref[...].astype(BF16)

    def row(w):
        return pl.BlockSpec((tm, w), lambda i: (i, 0))

    ins = [dqa, dka2, dva2, *dqb, *dkb, *dvb, dxc, dgc, cos_t, sin_t]
    return pl.pallas_call(
        body, name=name, grid=(T // tm,),
        in_specs=[row(v.shape[1]) for v in ins],
        out_specs=row(IN_COLS),
        out_shape=jax.ShapeDtypeStruct((T, IN_COLS), BF16),
        compiler_params=_params(("parallel",)),
    )(*ins)


def _band_masks(max_dist):
    row = lax.broadcasted_iota(jnp.int32, (QBLK, 2 * QBLK), 0)
    key = lax.broadcasted_iota(jnp.int32, (QBLK, 2 * QBLK), 1)
    dist = row + QBLK - key
    wide = jnp.logical_and(dist >= 0, dist <= max_dist)
    return wide, wide[:, :QBLK], key >= QBLK


def _head_masks(rows=QBLK):
    lo = _lane_iota((rows, LANES)) < HEAD_DIM
    return lo, jnp.logical_not(lo)


def _keep(hm, x):
    return x * jnp.where(hm, 1.0, 0.0).astype(x.dtype)


def _head_col(x, hm):
    return jnp.max(jnp.where(hm, x, -jnp.inf), axis=1, keepdims=True)


def _attn_specs(R, C):
    chunk = min(ATTN_CHUNK, R)
    nb = chunk // QBLK
    nch = R // chunk
    main = pl.BlockSpec((chunk, LANES), lambda j, c: (c, j))
    prev = pl.BlockSpec((QBLK, LANES), lambda j, c: (jnp.maximum(c * nb - 1, 0), j))
    nxt = pl.BlockSpec((QBLK, LANES), lambda j, c: (jnp.minimum((c + 1) * nb, R // QBLK - 1), j))
    return chunk, nb, nch, main, prev, nxt


def _attn_fwd(q, k, v, max_dist, *, name):
    R, C = q.shape
    chunk, nb, nch, main, prev, _ = _attn_specs(R, C)

    def body(q_ref, k_ref, v_ref, kp_ref, vp_ref, o_ref, lse_ref):
        c = pl.program_id(1)
        wide_mask, _, own_block = _band_masks(max_dist)
        heads = _head_masks()

        def block(q_blk, kk, vv, mask):
            o_h, lse_h = [], []
            for hm in heads:
                s = jnp.where(mask, _dot(_keep(hm, q_blk), kk, NT_DIMS), -jnp.inf)
                m = jnp.max(jnp.maximum(s[:, :QBLK], s[:, QBLK:]), axis=1, keepdims=True)
                p = jnp.exp(s - m)
                l = jnp.sum(p[:, :QBLK] + p[:, QBLK:], axis=1, keepdims=True)
                o_h.append(_dot(p.astype(BF16), vv) / l)
                lse_h.append(jnp.broadcast_to(m + jnp.log(l), (QBLK, LANES)))
            return jnp.where(heads[0], o_h[0], o_h[1]), jnp.where(heads[0], lse_h[0], lse_h[1])

        first = pl.ds(0, QBLK)
        o0, l0 = block(q_ref[first, :], jnp.concatenate([kp_ref[...], k_ref[first, :]], axis=0),
                       jnp.concatenate([vp_ref[...], v_ref[first, :]], axis=0),
                       jnp.logical_and(wide_mask, jnp.logical_or(own_block, c > 0)))
        o_ref[first, :] = o0
        lse_ref[first, :] = l0

        def loop(qb, carry):
            cur = pl.ds(pl.multiple_of(qb * QBLK, QBLK), QBLK)
            both = pl.ds(pl.multiple_of((qb - 1) * QBLK, QBLK), 2 * QBLK)
            o, l = block(q_ref[cur, :], k_ref[both, :], v_ref[both, :], wide_mask)
            o_ref[cur, :] = o
            lse_ref[cur, :] = l
            return carry

        if nb > 1:
            lax.fori_loop(1, nb, loop, 0, unroll=ATTN_FWD_UNROLL)

    return pl.pallas_call(
        body, name=name, grid=(C // LANES, nch),
        in_specs=[main, main, main, prev, prev], out_specs=[main, main],
        out_shape=[jax.ShapeDtypeStruct((R, C), F32), jax.ShapeDtypeStruct((R, C), F32)],
        compiler_params=_params(("parallel", "parallel")),
    )(q, k, v, k, v)


def _attn_bwd(q, k, v, do, lse, delta, max_dist, *, name):
    R, C = q.shape
    chunk, nb, nch, main, prev, nxt = _attn_specs(R, C)

    def body(q_ref, k_ref, v_ref, do_ref, lse_ref, dl_ref, kp_ref, vp_ref, qn_ref, don_ref, lsen_ref, dln_ref,
             dq_ref, dk_ref, dv_ref):
        c = pl.program_id(1)
        wide_mask, prev_mask, own_block = _band_masks(max_dist)
        heads = _head_masks()

        def pair(q_blk, do_blk, lse_blk, dl_blk, kk, vv, mask, want_dq=True):
            dq = jnp.zeros((QBLK, LANES), F32)
            dkk = jnp.zeros(kk.shape, F32)
            dvv = jnp.zeros(kk.shape, F32)
            for hm, khm in zip(heads, _head_masks(kk.shape[0])):
                qh = _keep(hm, q_blk)
                doh = _keep(hm, do_blk)
                p = jnp.where(mask, jnp.exp(_dot(qh, kk, NT_DIMS) - _head_col(lse_blk, hm)), 0.0)
                ds = (p * (_dot(doh, vv, NT_DIMS) - _head_col(dl_blk, hm))).astype(BF16)
                if want_dq:
                    dq = dq + _dot(ds, _keep(khm, kk))
                dkk = dkk + _dot(ds, qh, TN_DIMS)
                dvv = dvv + _dot(p.astype(BF16), doh, TN_DIMS)
            return dq, dkk, dvv

        dk_ref[...] = jnp.zeros_like(dk_ref)
        dv_ref[...] = jnp.zeros_like(dv_ref)

        first = pl.ds(0, QBLK)
        dq0, dkk0, dvv0 = pair(q_ref[first, :], do_ref[first, :], lse_ref[first, :], dl_ref[first, :],
                               jnp.concatenate([kp_ref[...], k_ref[first, :]], axis=0),
                               jnp.concatenate([vp_ref[...], v_ref[first, :]], axis=0),
                               jnp.logical_and(wide_mask, jnp.logical_or(own_block, c > 0)))
        dq_ref[first, :] = dq0
        dk_ref[first, :] += dkk0[QBLK:, :]
        dv_ref[first, :] += dvv0[QBLK:, :]

        def loop(qb, carry):
            cur = pl.ds(pl.multiple_of(qb * QBLK, QBLK), QBLK)
            both = pl.ds(pl.multiple_of((qb - 1) * QBLK, QBLK), 2 * QBLK)
            dq, dkk, dvv = pair(q_ref[cur, :], do_ref[cur, :], lse_ref[cur, :], dl_ref[cur, :],
                                k_ref[both, :], v_ref[both, :], wide_mask)
            dq_ref[cur, :] = dq
            dk_ref[both, :] += dkk
            dv_ref[both, :] += dvv
            return carry

        if nb > 1:
            lax.fori_loop(1, nb, loop, 0, unroll=ATTN_BWD_UNROLL)

        last = pl.ds((nb - 1) * QBLK, QBLK)
        _, dk_n, dv_n = pair(qn_ref[...], don_ref[...], lsen_ref[...], dln_ref[...], k_ref[last, :], v_ref[last, :],
                             jnp.logical_and(prev_mask, c < nch - 1), want_dq=False)
        dk_ref[last, :] += dk_n
        dv_ref[last, :] += dv_n

    return pl.pallas_call(
        body, name=name, grid=(C // LANES, nch),
        in_specs=[main] * 6 + [prev, prev] + [nxt] * 4, out_specs=[main, main, main],
        out_shape=[jax.ShapeDtypeStruct((R, C), F32)] * 3,
        compiler_params=_params(("parallel", "parallel")),
    )(q, k, v, do, lse, delta, k, v, q, do, lse, delta)


def _head_sum(x):
    r = lax.broadcasted_iota(jnp.int32, (LANES, LANES), 0) // HEAD_DIM
    c = lax.broadcasted_iota(jnp.int32, (LANES, LANES), 1) // HEAD_DIM
    ones = jnp.where(r == c, 1.0, 0.0).astype(BF16)
    outs = []
    for j in range(x.shape[1] // LANES):
        rem = x[:, j * LANES:(j + 1) * LANES]
        acc = jnp.zeros(rem.shape, F32)
        for _ in range(3):
            part = rem.astype(BF16)
            acc = acc + _dot(part, ones)
            rem = rem - part.astype(F32)
        outs.append(acc)
    return outs[0] if len(outs) == 1 else jnp.concatenate(outs, axis=1)


def _branch_weights(lses):
    m = functools.reduce(jnp.maximum, lses)
    es = [jnp.exp(l - m) for l in lses]
    den = functools.reduce(lambda a, b: a + b, es)
    return [e / den for e in es]


def _combine_fwd(oa, lsea, sink, obs, lsebs, oc, *, name, tm=256):
    T = oa.shape[0]
    tm = min(tm, T)
    nb = len(obs)

    def body(*refs):
        oa_ref, lsea_ref, sink_ref = refs[:3]
        ob_refs = refs[3:3 + nb]
        lse_refs = refs[3 + nb:3 + 2 * nb]
        oc_ref, out_ref = refs[3 + 2 * nb:]
        out_ref[:, 0:A_WIDTH] = (oa_ref[...] * _sigmoid(lsea_ref[...] - sink_ref[...])).astype(BF16)
        ws = _branch_weights([r[...] for r in lse_refs])
        ob = ob_refs[0][...] * ws[0]
        for r, w in zip(ob_refs[1:], ws[1:]):
            ob = ob + r[...] * w
        out_ref[:, A_WIDTH:A_WIDTH + B_WIDTH] = ob.astype(BF16)
        out_ref[:, A_WIDTH + B_WIDTH:MIX_WIDTH] = oc_ref[...].astype(BF16)

    def row(w):
        return pl.BlockSpec((tm, w), lambda i: (i, 0))

    ins = [oa, lsea, sink, *obs, *lsebs, oc]
    in_specs = [row(A_WIDTH), row(A_WIDTH), pl.BlockSpec((1, A_WIDTH), lambda i: (0, 0))]
    in_specs += [row(B_WIDTH)] * (2 * nb) + [row(C_WIDTH)]
    return pl.pallas_call(
        body, name=name, grid=(T // tm,), in_specs=in_specs, out_specs=row(MIX_WIDTH),
        out_shape=jax.ShapeDtypeStruct((T, MIX_WIDTH), BF16),
        compiler_params=_params(("parallel",)),
    )(*ins)


def _combine_bwd(dmix, oa, lsea, sink, obs, lsebs, *, name, tm=256):
    T = oa.shape[0]
    tm = min(tm, T)
    nb = len(obs)

    def body(*refs):
        dmix_ref, oa_ref, lsea_ref, sink_ref = refs[:4]
        ob_refs = refs[4:4 + nb]
        lse_refs = refs[4 + nb:4 + 2 * nb]
        outs = refs[4 + 2 * nb:]
        doa_ref, dla_ref = outs[:2]
        dob_refs = outs[2:2 + nb]
        dlb_refs = outs[2 + nb:2 + 2 * nb]
        doc_ref, dsink_ref = outs[2 + 2 * nb:]

        d_a = dmix_ref[:, 0:A_WIDTH]
        d_b = dmix_ref[:, A_WIDTH:A_WIDTH + B_WIDTH]
        doc_ref[...] = dmix_ref[:, A_WIDTH + B_WIDTH:MIX_WIDTH]

        gate = _sigmoid(lsea_ref[...] - sink_ref[...])
        doa_ref[...] = (d_a * gate).astype(BF16)
        dgate = _head_sum(d_a * oa_ref[...])
        dlse = dgate * gate * (1.0 - gate)
        dla_ref[...] = dgate * gate - dlse

        @pl.when(pl.program_id(0) == 0)
        def _():
            dsink_ref[...] = jnp.zeros_like(dsink_ref)

        dsink_ref[...] -= jnp.sum(dlse, axis=0, keepdims=True)

        ws = _branch_weights([r[...] for r in lse_refs])
        dws = [_head_sum(d_b * r[...]) for r in ob_refs]
        sw = ws[0] * dws[0]
        for w, dw in zip(ws[1:], dws[1:]):
            sw = sw + w * dw
        for w, do_ref, dl_ref in zip(ws, dob_refs, dlb_refs):
            do_ref[...] = (w * d_b).astype(BF16)
            dl_ref[...] = w * sw

    def row(w):
        return pl.BlockSpec((tm, w), lambda i: (i, 0))

    vec = pl.BlockSpec((1, A_WIDTH), lambda i: (0, 0))
    ins = [dmix, oa, lsea, sink, *obs, *lsebs]
    in_specs = [row(MIX_WIDTH), row(A_WIDTH), row(A_WIDTH), vec] + [row(B_WIDTH)] * (2 * nb)
    out_specs = [row(A_WIDTH), row(A_WIDTH)] + [row(B_WIDTH)] * (2 * nb) + [row(C_WIDTH), vec]
    out_shape = [jax.ShapeDtypeStruct((T, A_WIDTH), BF16), jax.ShapeDtypeStruct((T, A_WIDTH), F32)]
    out_shape += [jax.ShapeDtypeStruct((T, B_WIDTH), BF16)] * nb + [jax.ShapeDtypeStruct((T, B_WIDTH), F32)] * nb
    out_shape += [jax.ShapeDtypeStruct((T, C_WIDTH), F32), jax.ShapeDtypeStruct((1, A_WIDTH), F32)]
    res = pl.pallas_call(
        body, name=name, grid=(T // tm,), in_specs=in_specs, out_specs=out_specs, out_shape=out_shape,
        compiler_params=_params(("arbitrary",)),
    )(*ins)
    return res[0], res[1], list(res[2:2 + nb]), list(res[2 + nb:2 + 2 * nb]), res[2 + 2 * nb], res[3 + 2 * nb]


HIST = 8


def _softplus_neg(lam):
    e = jnp.exp(-jnp.abs(lam))
    log1p = jnp.where(e < 0.01, e * (1.0 - e * (0.5 - e * (1.0 / 3.0))), jnp.log(1.0 + e))
    return jnp.maximum(-lam, 0.0) + log1p


def _neg_expm1(x):
    series = -x * (1.0 + x * (0.5 + x * (1.0 / 6.0 + x * (1.0 / 24.0 + x * (1.0 / 120.0)))))
    return jnp.where(x > -0.1, series, 1.0 - jnp.exp(x))


def _gelu_parts(x):
    k = math.sqrt(2.0 / math.pi)
    t = jnp.tanh(k * (x + 0.044715 * (x * x * x)))
    cdf = 0.5 * (1.0 + t)
    return x * cdf, cdf + 0.5 * x * (1.0 - t * t) * (k * (1.0 + 3.0 * 0.044715 * (x * x)))


def _rglru_gates(y, pos_ref, wr_ref, br_ref, wi_ref, bi_ref, lam_ref):
    yb = y.astype(BF16)
    r = _sigmoid(_dot(yb, wr_ref[...]) + br_ref[...])
    ig = _sigmoid(_dot(yb, wi_ref[...]) + bi_ref[...])
    sp = _softplus_neg(lam_ref[...])
    log_a = -C_EXP * r * sp
    reset = pos_ref[...] == 0
    a = jnp.where(reset, 0.0, jnp.exp(log_a))
    mult = jnp.where(reset, 1.0, jnp.sqrt(_neg_expm1(2.0 * log_a)))
    return yb, r, ig, sp, reset, a, mult


def _conv_fwd(xs_ref, cw_ref, cb_ref, tm):
    y = cb_ref[...] + cw_ref[0:1, :] * xs_ref[HIST:HIST + tm, :]
    for j in range(1, C_CONV):
        y = y + cw_ref[j:j + 1, :] * xs_ref[HIST - j:HIST - j + tm, :]
    return y


def _rglru_fwd(xc, gc, pos, cw, cb, wr, br, wi, bi, lam, *, name, tm=512):
    T, W = xc.shape
    tm = min(tm, T)

    def body(xc_ref, gc_ref, pos_ref, cw_ref, cb_ref, wr_ref, br_ref, wi_ref, bi_ref, lam_ref,
             out_ref, hs_ref, xs, a_s, b_s, h_s):
        @pl.when(pl.program_id(0) == 0)
        def _():
            xs[0:HIST, :] = jnp.zeros((HIST, W), F32)
            h_s[...] = jnp.zeros_like(h_s)

        xv = xc_ref[...]
        xs[HIST:HIST + tm, :] = xv
        y = _conv_fwd(xs, cw_ref, cb_ref, tm)
        xs[0:HIST, :] = xv[tm - HIST:tm, :]
        _, _, ig, _, _, a, mult = _rglru_gates(y, pos_ref, wr_ref, br_ref, wi_ref, bi_ref, lam_ref)
        a_s[...] = a
        b_s[...] = mult * (ig * y)

        def step(t, h):
            row = pl.ds(t, 1)
            h = a_s[row, :] * h + b_s[row, :]
            b_s[row, :] = h
            return h

        h_s[...] = lax.fori_loop(0, tm, step, h_s[...], unroll=8)
        hs = b_s[...]
        hs_ref[...] = hs
        out_ref[...] = hs * _gelu_parts(gc_ref[...])[0]

    row = pl.BlockSpec((tm, W), lambda i: (i, 0))
    full = lambda shape: pl.BlockSpec(shape, lambda i: (0,) * len(shape))
    return pl.pallas_call(
        body, name=name, grid=(T // tm,),
        in_specs=[row, row, pl.BlockSpec((tm, 1), lambda i: (i, 0)), full((C_CONV, W)), full((1, W)),
                  full((W, W)), full((1, W)), full((W, W)), full((1, W)), full((1, W))],
        out_specs=[row, row],
        out_shape=[jax.ShapeDtypeStruct((T, W), F32)] * 2,
        scratch_shapes=[pltpu.VMEM((tm + HIST, W), F32), pltpu.VMEM((tm, W), F32), pltpu.VMEM((tm, W), F32),
                        pltpu.VMEM((1, W), F32)],
        compiler_params=_params(("arbitrary",)),
    )(xc, gc, pos, cw, cb, wr, br, wi, bi, lam)


def _rglru_bwd(xc, gc, pos, hs, dout, cw, cb, wr, br, wi, bi, lam, *, name, tm=512):
    T, W = xc.shape
    tm = min(tm, T)
    nt = T // tm
    hb = tm // HIST

    def body(xc_ref, gc_ref, pos_ref, hs_ref, dout_ref, xch_ref, hsh_ref,
             cw_ref, cb_ref, wr_ref, br_ref, wi_ref, bi_ref, lam_ref,
             dxc_ref, dgc_ref, dcw_ref, dcb_ref, dwr_ref, dbr_ref, dwi_ref, dbi_ref, dlam_ref,
             xs, hsx, dys, a_s, d_s, carry_s):
        i = pl.program_id(0)

        @pl.when(i == 0)
        def _():
            for r in (dcw_ref, dcb_ref, dwr_ref, dbr_ref, dwi_ref, dbi_ref, dlam_ref, carry_s):
                r[...] = jnp.zeros_like(r)
            dys[tm:tm + HIST, :] = jnp.zeros((HIST, W), F32)

        has_prev = i < nt - 1
        xs[0:HIST, :] = jnp.where(has_prev, xch_ref[...], 0.0)
        hsx[0:HIST, :] = jnp.where(has_prev, hsh_ref[...], 0.0)
        xs[HIST:HIST + tm, :] = xc_ref[...]
        hs = hs_ref[...]
        hsx[HIST:HIST + tm, :] = hs
        y = _conv_fwd(xs, cw_ref, cb_ref, tm)
        yb, r, ig, sp, reset, a, mult = _rglru_gates(y, pos_ref, wr_ref, br_ref, wi_ref, bi_ref, lam_ref)

        gelu, dgelu = _gelu_parts(gc_ref[...])
        dout = dout_ref[...]
        dgc_ref[...] = dout * hs * dgelu
        a_s[...] = a
        d_s[...] = dout * gelu

        def step(k, carry):
            row = pl.ds(tm - 1 - k, 1)
            dh = d_s[row, :] + carry
            d_s[row, :] = dh
            return a_s[row, :] * dh

        carry_s[...] = lax.fori_loop(0, tm, step, carry_s[...], unroll=8)
        dh = d_s[...]
        hprev = hsx[HIST - 1:HIST - 1 + tm, :]
        igy = ig * y
        dmult = dh * igy
        digy = dh * mult
        dlog_a = jnp.where(reset, 0.0, dh * hprev * a - dmult * a * a / mult)
        dlam_ref[...] += jnp.sum(dlog_a * (C_EXP * r) * _sigmoid(-lam_ref[...]), axis=0, keepdims=True)
        dz_r = dlog_a * (-C_EXP * sp) * r * (1.0 - r)
        dz_i = digy * y * ig * (1.0 - ig)
        dzr_b = dz_r.astype(BF16)
        dzi_b = dz_i.astype(BF16)
        dy = digy * ig + _dot(dzr_b, wr_ref[...], NT_DIMS) + _dot(dzi_b, wi_ref[...], NT_DIMS)
        dwr_ref[...] += _dot(yb, dzr_b, TN_DIMS)
        dwi_ref[...] += _dot(yb, dzi_b, TN_DIMS)
        dbr_ref[...] += jnp.sum(dz_r, axis=0, keepdims=True)
        dbi_ref[...] += jnp.sum(dz_i, axis=0, keepdims=True)

        dys[0:tm, :] = dy
        dxc = cw_ref[0:1, :] * dy
        for j in range(1, C_CONV):
            dxc = dxc + cw_ref[j:j + 1, :] * dys[j:j + tm, :]
        dxc_ref[...] = dxc
        dys[tm:tm + HIST, :] = dy[0:HIST, :]
        dcb_ref[...] += jnp.sum(dy, axis=0, keepdims=True)
        for j in range(C_CONV):
            dcw_ref[j:j + 1, :] += jnp.sum(dy * xs[HIST - j:HIST - j + tm, :], axis=0, keepdims=True)

    row = pl.BlockSpec((tm, W), lambda i: (nt - 1 - i, 0))
    halo = pl.BlockSpec((HIST, W), lambda i: (jnp.maximum((nt - 1 - i) * hb - 1, 0), 0))
    full = lambda shape: pl.BlockSpec(shape, lambda i: (0,) * len(shape))
    out_specs = [row, row, full((C_CONV, W)), full((1, W)), full((W, W)), full((1, W)), full((W, W)), full((1, W)),
                 full((1, W))]
    out_shape = [jax.ShapeDtypeStruct((T, W), F32)] * 2
    out_shape += [jax.ShapeDtypeStruct(s, F32) for s in ((C_CONV, W), (1, W), (W, W), (1, W), (W, W), (1, W), (1, W))]
    return pl.pallas_call(
        body, name=name, grid=(nt,),
        in_specs=[row, row, pl.BlockSpec((tm, 1), lambda i: (nt - 1 - i, 0)), row, row, halo, halo,
                  full((C_CONV, W)), full((1, W)), full((W, W)), full((1, W)), full((W, W)), full((1, W)),
                  full((1, W))],
        out_specs=out_specs, out_shape=out_shape,
        scratch_shapes=[pltpu.VMEM((tm + HIST, W), F32), pltpu.VMEM((tm + HIST, W), F32),
                        pltpu.VMEM((tm + HIST, W), F32), pltpu.VMEM((tm, W), F32), pltpu.VMEM((tm, W), F32),
                        pltpu.VMEM((1, W), F32)],
        compiler_params=_params(("arbitrary",)),
    )(xc, gc, pos, hs, dout, xc, hs, cw, cb, wr, br, wi, bi, lam)


def _adam_math(w, g, m, v):
    m = ADAM_B1 * m + (1.0 - ADAM_B1) * g
    v = ADAM_B2 * v + (1.0 - ADAM_B2) * (g * g)
    m_hat = m / (1.0 - ADAM_B1 ** ADAM_STEP)
    v_hat = v / (1.0 - ADAM_B2 ** ADAM_STEP)
    delta = -ADAM_LR * (m_hat / (jnp.sqrt(v_hat) + ADAM_EPS) + ADAM_WD * w)
    return delta, m, v


def _pick_rows(R, cap=512, mult=16):
    for d in range(min(cap, R), 0, -1):
        if R % d == 0 and d % mult == 0:
            return d
    return R


def _adamw(parts, w, m, v, *, name, tr=None, part=0, prev=None):
    R, C = w.shape
    r = parts.shape[1]
    tr = _pick_rows(r) if tr is None else tr
    assert r % tr == 0 and R % r == 0, (name, R, r, tr)
    nt = r // tr

    def body(p_ref, w_ref, m_ref, v_ref, *rest):
        g_ref, d_ref, nm_ref, nv_ref = rest[-4:]
        g = p_ref[0].astype(F32)
        for d in range(1, N_DEV):
            g = g + p_ref[d].astype(F32)
        delta, nm, nv = _adam_math(w_ref[...], g, m_ref[...], v_ref[...])
        g_ref[...] = g
        d_ref[...] = delta
        nm_ref[...] = nm
        nv_ref[...] = nv

    row = pl.BlockSpec((tr, C), lambda i: (part * nt + i, 0))
    in_specs = [pl.BlockSpec((N_DEV, tr, C), lambda i: (0, i, 0)), row, row, row]
    operands = [parts, w, m, v]
    aliases = {}
    if prev is not None:
        in_specs += [pl.BlockSpec(memory_space=pl.ANY)] * 4
        operands += list(prev)
        aliases = {4 + i: i for i in range(4)}
    return pl.pallas_call(
        body, name=name, grid=(nt,), in_specs=in_specs,
        out_specs=[row] * 4, out_shape=[jax.ShapeDtypeStruct((R, C), F32)] * 4,
        input_output_aliases=aliases,
        compiler_params=_params(("parallel",)),
    )(*operands)


def _exchange(srcs, gather, *, name):
    n = len(srcs)
    out_shape = [jax.ShapeDtypeStruct((N_DEV,) + s.shape if gather else s.shape, s.dtype) for s in srcs]

    def body(*refs):
        ins, outs = refs[:n], refs[n:2 * n]
        send_sems, recv_sems, local_sems = refs[2 * n:]
        x, y, c = lax.axis_index("x"), lax.axis_index("y"), lax.axis_index("c")
        me = 4 * x + 2 * y + c
        local_copies, sends, arrivals = [], [], []
        for a in range(n):
            mine = ins[a] if gather else ins[a].at[me]
            local = pltpu.make_async_copy(mine, outs[a].at[me], local_sems.at[a])
            local.start()
            local_copies.append(local)
            for k in range(1, N_DEV):
                px, py, pc = x ^ ((k >> 2) & 1), y ^ ((k >> 1) & 1), c ^ (k & 1)
                peer = 4 * px + 2 * py + pc
                send = pltpu.make_async_remote_copy(
                    src_ref=ins[a] if gather else ins[a].at[peer], dst_ref=outs[a].at[me],
                    send_sem=send_sems.at[a * N_DEV + k], recv_sem=recv_sems.at[a * N_DEV + k],
                    device_id=(px, py, pc), device_id_type=pl.DeviceIdType.MESH)
                send.start()
                sends.append(send)
                arrivals.append(pltpu.make_async_remote_copy(
                    src_ref=mine, dst_ref=outs[a].at[peer],
                    send_sem=send_sems.at[a * N_DEV + k], recv_sem=recv_sems.at[a * N_DEV + k],
                    device_id=(px, py, pc), device_id_type=pl.DeviceIdType.MESH))
        for cp in sends:
            cp.wait_send()
        for cp in arrivals:
            cp.wait_recv()
        for cp in local_copies:
            cp.wait()

    return pl.pallas_call(
        body, name=name,
        in_specs=[pl.BlockSpec(memory_space=pl.ANY)] * n, out_specs=[pl.BlockSpec(memory_space=pl.ANY)] * n,
        out_shape=out_shape,
        scratch_shapes=[pltpu.SemaphoreType.DMA((n * N_DEV,)), pltpu.SemaphoreType.DMA((n * N_DEV,)),
                        pltpu.SemaphoreType.DMA((n,))],
    )(*srcs)


_HBM = pl.BlockSpec(memory_space=pltpu.HBM)
_SEM = pl.BlockSpec(memory_space=pltpu.SEMAPHORE)
_EFFECT = pltpu.SideEffectType.DATAFLOW_SIDE_EFFECTING


def _peers():
    x, y, c = lax.axis_index("x"), lax.axis_index("y"), lax.axis_index("c")
    out = []
    for k in range(1, N_DEV):
        px, py, pc = x ^ ((k >> 2) & 1), y ^ ((k >> 1) & 1), c ^ (k & 1)
        out.append((k, (px, py, pc), 4 * px + 2 * py + pc))
    return 4 * x + 2 * y + c, out


def _split_copies(src_refs, land_refs, send_sems, recv_sems, gather):
    me, peers = _peers()
    out = []
    for a, (src_ref, land_ref) in enumerate(zip(src_refs, land_refs)):
        for k, dev, blk in peers:
            common = dict(send_sem=send_sems.at[a * N_DEV + k], recv_sem=recv_sems.at[a * N_DEV + k], device_id=dev,
                          device_id_type=pl.DeviceIdType.MESH)
            src = src_ref if gather else src_ref.at[blk]
            out.append((pltpu.make_async_remote_copy(src_ref=src, dst_ref=land_ref.at[me], **common),
                        pltpu.make_async_remote_copy(src_ref=src, dst_ref=land_ref.at[blk], **common)))
    return out


def _exchange_start(srcs, gather, *, name, after=None):
    n = len(srcs)
    lands = [lax.empty((N_DEV,) + (s.shape if gather else s.shape[1:]), s.dtype) for s in srcs]

    def body(*refs):
        src_refs, land_refs = refs[:n], refs[n:2 * n]
        send_sems, recv_sems = refs[-2 * n - 3:-2 * n - 1]
        token = refs[-1]
        for outgoing, _ in _split_copies(src_refs, land_refs, send_sems, recv_sems, gather):
            outgoing.start()
        token[...] = jnp.zeros_like(token)

    res = pl.pallas_call(
        body, name=name,
        out_shape=(pltpu.SemaphoreType.DMA((n * N_DEV,)), pltpu.SemaphoreType.DMA((n * N_DEV,)),
                   *[pltpu.HBM(a.shape, a.dtype) for a in srcs + lands], jax.ShapeDtypeStruct((8, LANES), F32)),
        in_specs=(_HBM,) * (2 * n) + ((pl.BlockSpec(memory_space=pl.ANY),) if after is not None else ()),
        out_specs=(_SEM, _SEM) + (_HBM,) * (2 * n) + (pl.BlockSpec(memory_space=pltpu.VMEM),),
        input_output_aliases={i: i + 2 for i in range(2 * n)},
        compiler_params=pltpu.CompilerParams(has_side_effects=_EFFECT),
    )(*[pltpu.with_memory_space_constraint(a, pltpu.HBM) for a in srcs + lands],
      *([after] if after is not None else []))
    return res[0], res[1], list(res[2:2 + n]), list(res[2 + n:2 + 2 * n]), res[-1]


def _exchange_wait(started, after, gather, *, name):
    send_sems, recv_sems, srcs, lands, _ = started
    n = len(srcs)

    def body(*refs):
        src_refs, land_refs = refs[:n], refs[n:2 * n]
        send_sems, recv_sems = refs[2 * n:2 * n + 2]
        for outgoing, incoming in _split_copies(src_refs, land_refs, send_sems, recv_sems, gather):
            outgoing.wait_send()
            incoming.wait_recv()

    res = pl.pallas_call(
        body, name=name,
        out_shape=tuple(pltpu.HBM(a.shape, a.dtype) for a in srcs + lands),
        in_specs=(_HBM,) * (2 * n) + (_SEM, _SEM, pl.BlockSpec(memory_space=pl.ANY)), out_specs=(_HBM,) * (2 * n),
        input_output_aliases={i: i for i in range(2 * n)},
        compiler_params=pltpu.CompilerParams(has_side_effects=_EFFECT),
    )(*srcs, *lands, send_sems, recv_sems, after)
    return list(res[:n]), list(res[n:])


def _cols_to_blocks(g, *, name, tr=128):
    R, C = g.shape
    w = C // N_DEV
    tr = min(tr, R)

    def body(g_ref, o_ref):
        for p in range(N_DEV):
            o_ref[p] = g_ref[:, p * w:(p + 1) * w].astype(BF16)

    return pl.pallas_call(
        body, name=name, grid=(R // tr,),
        in_specs=[pl.BlockSpec((tr, C), lambda i: (i, 0))],
        out_specs=pl.BlockSpec((N_DEV, tr, w), lambda i: (0, i, 0)),
        out_shape=jax.ShapeDtypeStruct((N_DEV, R, w), BF16),
        compiler_params=_params(("parallel",)),
    )(g)


def _blocks_to_cols(b, *, name, tr=128):
    _, R, w = b.shape
    tr = min(tr, R)

    def body(b_ref, o_ref):
        o_ref[...] = jnp.concatenate([b_ref[p].astype(F32) for p in range(N_DEV)], axis=1).astype(o_ref.dtype)

    return pl.pallas_call(
        body, name=name, grid=(R // tr,),
        in_specs=[pl.BlockSpec((N_DEV, tr, w), lambda i: (0, i, 0))],
        out_specs=pl.BlockSpec((tr, N_DEV * w), lambda i: (i, 0)),
        out_shape=jax.ShapeDtypeStruct((R, N_DEV * w), b.dtype),
        compiler_params=_params(("parallel",)),
    )(b)


def _pack_rows(arrays, *, name, pick=None):
    B, _, w = arrays[0].shape
    rows = [a.shape[1] for a in arrays]
    first = 0
    if pick is not None:
        B, first = 1, pick

    def body(*refs):
        o_ref = refs[-1]
        r = 0
        for a_ref, n in zip(refs[:-1], rows):
            o_ref[0, r:r + n, :] = a_ref[0].astype(BF16)
            r += n

    return pl.pallas_call(
        body, name=name, grid=(B,),
        in_specs=[pl.BlockSpec((1, n, w), lambda b: (first + b, 0, 0)) for n in rows],
        out_specs=pl.BlockSpec((1, sum(rows), w), lambda b: (b, 0, 0)),
        out_shape=jax.ShapeDtypeStruct((B, sum(rows), w), BF16),
        compiler_params=_params(("parallel",)),
    )(*arrays)


def _unpack_rows(land, src, rows, *, name):
    _, R, w = land.shape
    src_spec = (pl.BlockSpec((1, R, w), lambda p: (p, 0, 0)) if src.ndim == 3
                else pl.BlockSpec((R, w), lambda p: (0, 0)))

    def body(land_ref, src_ref, *o_refs):
        me = 4 * lax.axis_index("x") + 2 * lax.axis_index("y") + lax.axis_index("c")
        mine = pl.program_id(0) == me
        r = 0
        for o_ref, n in zip(o_refs, rows):
            rows_i = slice(r, r + n)

            @pl.when(mine)
            def _(o_ref=o_ref, rows_i=rows_i):
                o_ref[0] = src_ref[0, rows_i, :] if src.ndim == 3 else src_ref[rows_i, :]

            @pl.when(jnp.logical_not(mine))
            def _(o_ref=o_ref, rows_i=rows_i):
                o_ref[0] = land_ref[0, rows_i, :]

            r += n

    return pl.pallas_call(
        body, name=name, grid=(N_DEV,),
        in_specs=[pl.BlockSpec((1, R, w), lambda p: (p, 0, 0)), src_spec],
        out_specs=[pl.BlockSpec((1, n, w), lambda p: (p, 0, 0)) for n in rows],
        out_shape=[jax.ShapeDtypeStruct((N_DEV, n, w), land.dtype) for n in rows],
        compiler_params=_params(("parallel",)),
    )(land, src)


def _to_blocks(w, axis):
    shape = w.shape
    k = shape[axis] // N_DEV
    w = w.reshape(shape[:axis] + (N_DEV, k) + shape[axis + 1:])
    return jnp.moveaxis(w, axis, 0)


def _from_blocks(wb, axis):
    w = jnp.moveaxis(wb, 0, axis)
    shape = w.shape
    return w.reshape(shape[:axis] + (shape[axis] * shape[axis + 1],) + shape[axis + 2:])


def _block_diag(w):
    n, k, _ = w.shape
    eye = jnp.eye(n, dtype=w.dtype)
    return (eye[:, None, :, None] * w[:, :, None, :]).reshape(n * k, n * k)


def _diag_blocks(wd):
    k = HEAD_DIM
    return jnp.stack([wd[h * k:(h + 1) * k, h * k:(h + 1) * k] for h in range(C_BLOCKS)])


def _pack(arrays):
    rows = []
    for a in arrays:
        flat = a.reshape(-1).astype(F32)
        pad = (-flat.shape[0]) % LANES
        rows.append(jnp.pad(flat, (0, pad)).reshape(-1, LANES))
    out = jnp.concatenate(rows, axis=0)
    return jnp.pad(out, ((0, (-out.shape[0]) % 8), (0, 0)))


def _unpack(packed, shapes):
    outs, r = [], 0
    for s in shapes:
        size = math.prod(s)
        nrows = -(-size // LANES)
        outs.append(packed[r:r + nrows].reshape(-1)[:size].reshape(s))
        r += nrows
    return outs


def _rope_tables(positions):
    inv = 1.0 / (ROPE_THETA ** (jnp.arange(0, HEAD_DIM, 2, dtype=F32) / HEAD_DIM))
    ang = positions.astype(F32)[:, None] * inv
    cos, sin = jnp.cos(ang), jnp.sin(ang)
    return jnp.tile(cos, (1, 4)), jnp.tile(jnp.concatenate([-sin, sin], axis=1), (1, 2))


def _dilate(t, d):
    return t.reshape(t.shape[0] // d, d * t.shape[1])


def _layer_fwd(l, x, pos, cos_t, sin_t, W, before_mixer=None):
    tag = f"l{l}"
    saved = {'x0': x}
    x1, a1, u1 = _ffn_fwd(x, W['norm_ffn1'][l], W['ffn1_gate'][l], W['ffn1_up'][l], W['ffn1_down'][l],
                          name=f"ffn1_fwd_{tag}")
    if before_mixer is not None:
        before_mixer(l, x1)
    h = _rms_fwd(x1, W['norm_mix'][l], name=f"mixnorm_fwd_{tag}")
    proj = _mm(h, W['w_in'][l], 'nn', name=f"proj_{tag}", tm=512, tn=IN_COLS, tk=h.shape[1])
    qa, ka2, va2, qb, kb, vb, xc, gc = _split_rope(proj, cos_t, sin_t, name=f"split_{tag}")
    oa, lsea = _attn_fwd(qa, ka2, va2, A_MAX_DIST, name=f"attn_a_fwd_{tag}")
    obs, lsebs = [], []
    for bi, (window, d) in enumerate(B_BRANCHES):
        o, lse = _attn_fwd(_dilate(qb, d), _dilate(kb, d), _dilate(vb, d), window // d,
                           name=f"attn_b{bi}_fwd_{tag}")
        obs.append(o.reshape(qb.shape))
        lsebs.append(lse.reshape(qb.shape))
    oc, hs = _rglru_fwd(xc, gc, pos, W['conv_w'][l], W['conv_b'][l], W['rg_w_r'][l], W['rg_b_r'][l],
                        W['rg_w_i'][l], W['rg_b_i'][l], W['rg_lambda'][l], name=f"rglru_fwd_{tag}")
    mix = _combine_fwd(oa, lsea, W['sinks'][l], obs, lsebs, oc, name=f"combine_fwd_{tag}")
    x2 = _mm(mix, W['w_out'][l], 'nn', name=f"outproj_{tag}", tm=512, tn=x.shape[1], tk=MIX_WIDTH, res=x1)
    x3, a2, u2 = _ffn_fwd(x2, W['norm_ffn2'][l], W['ffn2_gate'][l], W['ffn2_up'][l], W['ffn2_down'][l],
                          name=f"ffn2_fwd_{tag}")
    saved.update(a1=a1, u1=u1, x1=x1, h=h, qa=qa, ka2=ka2, va2=va2, qb=qb, kb=kb, vb=vb, xc=xc, gc=gc, oa=oa,
                 lsea=lsea, obs=obs, lsebs=lsebs, hs=hs, mix=mix, x2=x2, a2=a2, u2=u2)
    return x3, saved


def _ffn_grads(tag, which, x, g, dy, a, u, wg, wu, wd):
    T, D = x.shape
    F = wg.shape[1]
    dx, dg, n, act, da, du = _ffn_bwd(x, g, dy, a, u, wg, wu, wd, name=f"{which}_bwd_{tag}")
    fc = _ffn_chunk(F)
    d_gate = _mm(n, da, 'tn', name=f"{which}_dgate_{tag}", tm=D, tn=fc, tk=2048)
    d_up = _mm(n, du, 'tn', name=f"{which}_dup_{tag}", tm=D, tn=fc, tk=2048)
    d_down = _mm(act, dy, 'tn', name=f"{which}_ddown_{tag}", tm=fc, tn=D, tk=1024, alpha=0.5)
    return dx, dg, d_gate, d_up, d_down


def _layer_bwd(l, dx3, pos, cos_t, sin_t, W, S, on_grads=None):
    tag = f"l{l}"
    G = {}
    dx2, G['norm_ffn2'], G['ffn2_gate'], G['ffn2_up'], G['ffn2_down'] = _ffn_grads(
        tag, 'ffn2', S['x2'], W['norm_ffn2'][l], dx3, S['a2'], S['u2'], W['ffn2_gate'][l], W['ffn2_up'][l],
        W['ffn2_down'][l])
    D = dx2.shape[1]
    dmix = _mm(dx2, W['w_out'][l], 'nt', name=f"outproj_dx_{tag}", tm=512, tn=MIX_WIDTH, tk=D)
    G['w_out'] = _mm(S['mix'], dx2, 'tn', name=f"outproj_dw_{tag}", tm=MIX_WIDTH, tn=D, tk=2048)
    doa, dla, dobs, dlbs, doc, dsink = _combine_bwd(dmix, S['oa'], S['lsea'], W['sinks'][l], S['obs'], S['lsebs'],
                                                    name=f"combine_bwd_{tag}")
    G['attn_sinks'] = dsink.reshape(A_WIDTH // HEAD_DIM, HEAD_DIM)[:, 0]
    dqa, dka2, dva2 = _attn_bwd(S['qa'], S['ka2'], S['va2'], doa, S['lsea'], dla, A_MAX_DIST,
                                name=f"attn_a_bwd_{tag}")
    dqb, dkb, dvb = [], [], []
    shape = S['qb'].shape
    for bi, (window, d) in enumerate(B_BRANCHES):
        dq, dk, dv = _attn_bwd(_dilate(S['qb'], d), _dilate(S['kb'], d), _dilate(S['vb'], d), _dilate(dobs[bi], d),
                               _dilate(S['lsebs'][bi], d), _dilate(dlbs[bi], d), window // d,
                               name=f"attn_b{bi}_bwd_{tag}")
        dqb.append(dq.reshape(shape))
        dkb.append(dk.reshape(shape))
        dvb.append(dv.reshape(shape))
    (dxc, dgc, G['conv_w'], G['conv_b'], dwr, G['rg_b_r'], dwi, G['rg_b_i'], G['rg_lambda']) = _rglru_bwd(
        S['xc'], S['gc'], pos, S['hs'], doc, W['conv_w'][l], W['conv_b'][l], W['rg_w_r'][l], W['rg_b_r'][l],
        W['rg_w_i'][l], W['rg_b_i'][l], W['rg_lambda'][l], name=f"rglru_bwd_{tag}")
    G['rg_w_r'] = _diag_blocks(dwr)
    G['rg_w_i'] = _diag_blocks(dwi)
    dproj = _merge_dproj(dqa, dka2, dva2, dqb, dkb, dvb, dxc, dgc, cos_t, sin_t, name=f"merge_{tag}")
    dh = _mm(dproj, W['w_in'][l], 'nt', name=f"proj_dx_{tag}", tm=512, tn=D, tk=IN_COLS)
    G['w_in'] = _mm(S['h'], dproj, 'tn', name=f"proj_dw_{tag}", tm=D, tn=IN_COLS, tk=1024)
    g_mix = W['norm_mix'][l]
    if on_grads is not None:
        g_mix = g_mix + on_grads(l, 0, G)
    dx1, G['norm_mix'] = _rms_bwd(S['x1'], g_mix, dh, dx2, name=f"mixnorm_bwd_{tag}")
    dx0, G['norm_ffn1'], G['ffn1_gate'], G['ffn1_up'], G['ffn1_down'] = _ffn_grads(
        tag, 'ffn1', S['x0'], W['norm_ffn1'][l], dx1, S['a1'], S['u1'], W['ffn1_gate'][l], W['ffn1_up'][l],
        W['ffn1_down'][l])
    if on_grads is not None:
        on_grads(l, 1, G)
    return dx0, G


def _device_step(x, positions, loss_target, W, before_layer=None, on_grads=None, before_mixer=None):
    T = x.shape[0]
    pos = positions.reshape(T, 1)
    cos_t, sin_t = _rope_tables(positions)
    saved = []
    for l in range(DEPTH):
        if before_layer is not None:
            before_layer(l, x)
        x, S = _layer_fwd(l, x, pos, cos_t, sin_t, W, before_mixer)
        saved.append(S)
    loss, dx, dg_final = _loss_head(x, W['norm_final'], loss_target, name="loss_head")
    grads = [None] * DEPTH
    for l in reversed(range(DEPTH)):
        dx, grads[l] = _layer_bwd(l, dx, pos, cos_t, sin_t, W, saved[l], on_grads)
    return loss, dx, grads, dg_final


SHARD_AXIS = {'ffn1_gate': 2, 'ffn1_up': 2, 'ffn1_down': 1, 'w_in': 2, 'w_out': 1, 'ffn2_gate': 2, 'ffn2_up': 2,
              'ffn2_down': 1, 'conv_w': 2}


def kernel(x, positions, norm_ffn1, ffn1_gate, ffn1_up, ffn1_down, norm_mix, w_in, attn_sinks, conv_w, conv_b, rg_w_r, rg_b_r, rg_w_i, rg_b_i, rg_lambda, w_out, norm_ffn2, ffn2_gate, ffn2_up, ffn2_down, norm_final, loss_target, m_norm_ffn1, m_ffn1_gate, m_ffn1_up, m_ffn1_down, m_norm_mix, m_w_in, m_attn_sinks, m_conv_w, m_conv_b, m_rg_w_r, m_rg_b_r, m_rg_w_i, m_rg_b_i, m_rg_lambda, m_w_out, m_norm_ffn2, m_ffn2_gate, m_ffn2_up, m_ffn2_down, m_norm_final, v_norm_ffn1, v_ffn1_gate, v_ffn1_up, v_ffn1_down, v_norm_mix, v_w_in, v_attn_sinks, v_conv_w, v_conv_b, v_rg_w_r, v_rg_b_r, v_rg_w_i, v_rg_b_i, v_rg_lambda, v_w_out, v_norm_ffn2, v_ffn2_gate, v_ffn2_up, v_ffn2_down, v_norm_final):
    given = dict(locals())
    me = 4 * lax.axis_index("x") + 2 * lax.axis_index("y") + lax.axis_index("c")

    def by_width(names):
        classes = {}
        for n in names:
            classes.setdefault(given[n].shape[2], []).append(n)
        return list(classes.values())

    def pack(names, get, tag, pick=None):
        return [_pack_rows([get(n) for n in cls], name=f"pack{ci}_{tag}", pick=pick)
                for ci, cls in enumerate(by_width(names))]

    def unpack(names, lands, srcs, tag):
        out = {}
        for ci, cls in enumerate(by_width(names)):
            arrays = _unpack_rows(lands[ci], srcs[ci], [given[n].shape[1] for n in cls], name=f"unpack{ci}_{tag}")
            out.update(zip(cls, arrays))
        return out

    def gather_start(l, names, tag, after=None):
        return _exchange_start([p[0] for p in pack(names, lambda n: given[n], f"w{tag}_l{l}", pick=l)], True,
                               name=f"gather_start{tag}_l{l}", after=after)

    def gather_wait(l, names, tag, started, after):
        srcs, lands = _exchange_wait(started, after, True, name=f"gather_wait{tag}_l{l}")
        blocks = unpack(names, lands, srcs, f"w{tag}_l{l}")
        for n in names:
            if SHARD_AXIS[n] == 2:
                W[n][l] = _blocks_to_cols(blocks[n], name=f"cols_{n}_l{l}")
            else:
                W[n][l] = blocks[n].reshape(-1, blocks[n].shape[2])
        return lands[0]

    W = {n: [None] * DEPTH for n in BIG_NAMES}
    ffn1_names, later_names = SCATTER_STAGES[1], SCATTER_STAGES[0]
    first = gather_start(0, ffn1_names, "a")
    conv_full = _exchange([conv_w], True, name="gather_conv_w")[0]
    landed = gather_wait(0, ffn1_names, "a", first, conv_full)
    second = gather_start(0, later_names, "b", after=landed)
    started = second[4][0, 0]
    gathers = [None] * DEPTH

    def before_layer(l, x_in):
        if l > 0:
            gather_wait(l, BIG_NAMES, "", gathers[l], x_in)

    def before_mixer(l, x1):
        if l == 0:
            landed = gather_wait(0, later_names, "b", second, x1)
            token = 0.0
            for k in range(1, DEPTH):
                gathers[k] = gather_start(k, BIG_NAMES, "", after=landed)
                token = token + gathers[k][4][0, 0]
            W['norm_mix'][0] = W['norm_mix'][0] + token

    scatters = {}

    def on_grads(l, stage, G):
        def blocks_of(n):
            if SHARD_AXIS[n] == 2:
                return _cols_to_blocks(G[n], name=f"blocks_{n}_l{l}")
            return G[n].reshape(N_DEV, -1, G[n].shape[1])

        scatters[l, stage] = _exchange_start(pack(SCATTER_STAGES[stage], blocks_of, f"g{stage}_l{l}"), False,
                                             name=f"scatter_start{stage}_l{l}")
        token = scatters[l, stage][4][0, 0]
        if stage == 1 and l > 0:
            W['norm_ffn2'][l - 1] = W['norm_ffn2'][l - 1] + token
        return token

    W['conv_w'] = [_from_blocks(conv_full[:, l], 1) for l in range(DEPTH)]
    for n in ('norm_ffn1', 'norm_mix', 'norm_ffn2', 'conv_b', 'rg_lambda'):
        W[n] = [given[n][l][None, :] for l in range(DEPTH)]
    W['norm_final'] = norm_final[None, :]
    W['sinks'] = [jnp.repeat(attn_sinks[l], HEAD_DIM)[None, :] for l in range(DEPTH)]
    for n in ('rg_w_r', 'rg_w_i'):
        W[n] = [_block_diag(given[n][l]).astype(BF16) for l in range(DEPTH)]
    for n in ('rg_b_r', 'rg_b_i'):
        W[n] = [given[n][l].reshape(1, C_WIDTH) for l in range(DEPTH)]

    W['norm_ffn1'][0] = W['norm_ffn1'][0] + started

    loss_part, grad_x, grads, dg_final = _device_step(x[0], positions[0], loss_target[0], W, before_layer, on_grads,
                                                      before_mixer)
    loss = lax.psum(loss_part[0, 0], ("x", "y", "c"))

    small_shapes = [given[n].shape for n in SMALL_NAMES] + [(DEPTH, C_CONV, C_WIDTH)]
    small_grads = []
    for n in SMALL_NAMES:
        if n == 'norm_final':
            small_grads.append(dg_final.reshape(-1))
        else:
            small_grads.append(jnp.stack([grads[l][n].reshape(given[n].shape[1:]) for l in range(DEPTH)]))
    small_grads.append(jnp.stack([grads[l]['conv_w'] for l in range(DEPTH)]))
    small_parts = _exchange([_pack(small_grads)], True, name="gather_small_grads")[0]

    out = {}
    for stage in (0, 1):
        parts = {}
        for l in reversed(range(DEPTH)):
            last = stage == 1 and l == 0
            srcs, lands = _exchange_wait(scatters[l, stage], out['w_in'][1] if last else grad_x, False,
                                         name=f"scatter_wait{stage}_l{l}")
            parts[l] = unpack(SCATTER_STAGES[stage], lands, srcs, f"g{stage}_l{l}")
        for n in SCATTER_STAGES[stage]:
            shape = given[n].shape
            two_d = (shape[0] * shape[1], shape[2])
            res = None
            for l in reversed(range(DEPTH)):
                res = _adamw(parts[l][n], given[n].reshape(two_d), given['m_' + n].reshape(two_d),
                             given['v_' + n].reshape(two_d), name=f"adamw_{n}_l{l}", part=l, prev=res)
            out[n] = [r.reshape(shape) for r in res]

    w_small = [given[n] for n in SMALL_NAMES]
    m_small = [given['m_' + n] for n in SMALL_NAMES]
    v_small = [given['v_' + n] for n in SMALL_NAMES]
    zeros_cw = jnp.zeros((DEPTH, C_CONV, C_WIDTH), F32)
    res = _adamw(small_parts, _pack(w_small + [zeros_cw]), _pack(m_small + [zeros_cw]), _pack(v_small + [zeros_cw]),
                 name="adamw_small", tr=8)
    unpacked = [_unpack(r, small_shapes) for r in res]
    for i, n in enumerate(SMALL_NAMES):
        out[n] = [u[i] for u in unpacked]

    k = conv_w.shape[2]
    g_cw = lax.dynamic_slice_in_dim(unpacked[0][-1], me * k, k, axis=2)
    zero_parts = jnp.zeros((N_DEV - 1,) + (8, LANES), F32)
    res = _adamw(jnp.concatenate([_pack([g_cw])[None], zero_parts]), _pack([conv_w]), _pack([m_conv_w]),
                 _pack([v_conv_w]), name="adamw_conv_w", tr=8)
    out['conv_w'] = [_unpack(r, [conv_w.shape])[0] for r in res]

    outputs = [loss, grad_x[None]]
    for i in range(4):
        outputs += [out[n][i] for n in WEIGHT_NAMES]
    return tuple(outputs)
```

```python
import functools
import math

import jax
import jax.numpy as jnp
from jax import lax
from jax.experimental import pallas as pl
from jax.experimental.pallas import tpu as pltpu

F32 = jnp.float32
BF16 = jnp.bfloat16

N_DEV = 8
DEPTH = 4
HEAD_DIM = 64
LANES = 128
QBLK = 128
A_WIDTH = 256
A_KV_WIDTH = 128
B_WIDTH = 384
C_WIDTH = 384
C_BLOCKS = 6
C_CONV = 4
C_EXP = 8.0
MIX_WIDTH = A_WIDTH + B_WIDTH + C_WIDTH
IN_COLS = A_WIDTH + 2 * A_KV_WIDTH + 3 * B_WIDTH + 2 * C_WIDTH
A_MAX_DIST = 127
B_BRANCHES = ((128, 1), (512, 4), (2048, 16))
ROPE_THETA = 10000.0
EPS = 1e-6
SCALE = HEAD_DIM ** -0.5

ADAM_LR = 0.001
ADAM_B1 = 0.9
ADAM_B2 = 0.999
ADAM_EPS = 1e-08
ADAM_WD = 0.01
ADAM_STEP = 10

ATTN_CHUNK = 1024
ATTN_FWD_UNROLL = True
ATTN_BWD_UNROLL = True
VMEM_LIMIT = 56 * 1024 * 1024

NT_DIMS = (((1,), (1,)), ((), ()))
TN_DIMS = (((0,), (0,)), ((), ()))
NN_DIMS = (((1,), (0,)), ((), ()))

WEIGHT_NAMES = ['norm_ffn1', 'ffn1_gate', 'ffn1_up', 'ffn1_down', 'norm_mix', 'w_in', 'attn_sinks', 'conv_w',
                'conv_b', 'rg_w_r', 'rg_b_r', 'rg_w_i', 'rg_b_i', 'rg_lambda', 'w_out', 'norm_ffn2', 'ffn2_gate',
                'ffn2_up', 'ffn2_down', 'norm_final']
BIG_NAMES = ['ffn1_gate', 'ffn1_up', 'ffn1_down', 'w_in', 'w_out', 'ffn2_gate', 'ffn2_up', 'ffn2_down']
SCATTER_STAGES = (['ffn2_gate', 'ffn2_up', 'ffn2_down', 'w_out', 'w_in'], ['ffn1_gate', 'ffn1_up', 'ffn1_down'])
SMALL_NAMES = ['norm_ffn1', 'norm_mix', 'norm_ffn2', 'norm_final', 'attn_sinks', 'conv_b', 'rg_w_r', 'rg_b_r',
               'rg_w_i', 'rg_b_i', 'rg_lambda']


def _params(sem, vmem=VMEM_LIMIT):
    return pltpu.CompilerParams(dimension_semantics=sem, vmem_limit_bytes=vmem)


def _dot(a, b, dims=NN_DIMS):
    return lax.dot_general(a, b, dims, preferred_element_type=F32)


def _sigmoid(x):
    return 1.0 / (1.0 + jnp.exp(-x))


def _mm(a, b, mode, *, name, tm=512, tn=512, tk=512, out_dtype=F32, alpha=1.0, res=None):
    if mode == 'nn':
        (M, K), N = a.shape, b.shape[1]
    elif mode == 'nt':
        (M, K), N = a.shape, b.shape[0]
    else:
        (K, M), N = a.shape, b.shape[1]
    tm, tn, tk = min(tm, M), min(tn, N), min(tk, K)
    ni, nj, nk = M // tm, N // tn, K // tk
    assert ni * tm == M and nj * tn == N and nk * tk == K, (name, a.shape, b.shape, tm, tn, tk)
    if mode == 'tn':
        a_spec = pl.BlockSpec((tk, tm), lambda j, i, k: (k, i))
    else:
        a_spec = pl.BlockSpec((tm, tk), lambda j, i, k: (i, k))
    if mode == 'nt':
        b_spec = pl.BlockSpec((tn, tk), lambda j, i, k: (j, k))
    else:
        b_spec = pl.BlockSpec((tk, tn), lambda j, i, k: (k, j))
    dims = {'nn': NN_DIMS, 'nt': NT_DIMS, 'tn': TN_DIMS}[mode]
    o_spec = pl.BlockSpec((tm, tn), lambda j, i, k: (i, j))
    has_res = res is not None

    def body(*refs):
        if has_res:
            a_ref, b_ref, r_ref, o_ref = refs[:4]
        else:
            a_ref, b_ref, o_ref = refs[:3]
        part = _dot(a_ref[...].astype(BF16), b_ref[...].astype(BF16), dims)

        def finish(acc):
            out = acc * alpha if alpha != 1.0 else acc
            if has_res:
                out = r_ref[...] + out
            o_ref[...] = out.astype(out_dtype)

        if nk == 1:
            finish(part)
        else:
            acc_ref = refs[-1]
            k = pl.program_id(2)

            @pl.when(k == 0)
            def _():
                acc_ref[...] = part

            @pl.when(k > 0)
            def _():
                acc_ref[...] += part

            @pl.when(k == nk - 1)
            def _():
                finish(acc_ref[...])

    in_specs = [a_spec, b_spec] + ([o_spec] if has_res else [])
    operands = [a, b] + ([res] if has_res else [])
    return pl.pallas_call(
        body, name=name, grid=(nj, ni, nk), in_specs=in_specs, out_specs=o_spec,
        out_shape=jax.ShapeDtypeStruct((M, N), out_dtype),
        scratch_shapes=[pltpu.VMEM((tm, tn), F32)] if nk > 1 else [],
        compiler_params=_params(("parallel", "parallel", "arbitrary")),
    )(*operands)


def _rms_fwd(x, g, *, name, tm=512):
    T, D = x.shape
    tm = min(tm, T)

    def body(x_ref, g_ref, o_ref):
        xv = x_ref[...]
        rstd = lax.rsqrt(jnp.mean(xv * xv, axis=-1, keepdims=True) + EPS)
        o_ref[...] = (xv * rstd * g_ref[...]).astype(BF16)

    return pl.pallas_call(
        body, name=name, grid=(T // tm,),
        in_specs=[pl.BlockSpec((tm, D), lambda i: (i, 0)), pl.BlockSpec((1, D), lambda i: (0, 0))],
        out_specs=pl.BlockSpec((tm, D), lambda i: (i, 0)),
        out_shape=jax.ShapeDtypeStruct((T, D), BF16),
        compiler_params=_params(("parallel",)),
    )(x, g)


def _rms_bwd_math(xv, g, dn):
    rstd = lax.rsqrt(jnp.mean(xv * xv, axis=-1, keepdims=True) + EPS)
    xhat = xv * rstd
    dxhat = dn * g
    dx = rstd * (dxhat - xhat * jnp.mean(dxhat * xhat, axis=-1, keepdims=True))
    return dx, jnp.sum(dn * xhat, axis=0, keepdims=True)


def _rms_bwd(x, g, dn, dres, *, name, tm=512):
    T, D = x.shape
    tm = min(tm, T)

    def body(x_ref, g_ref, dn_ref, dres_ref, dx_ref, dg_ref):
        dx, dg = _rms_bwd_math(x_ref[...], g_ref[...], dn_ref[...])
        dx_ref[...] = dres_ref[...] + dx

        @pl.when(pl.program_id(0) == 0)
        def _():
            dg_ref[...] = jnp.zeros_like(dg_ref)

        dg_ref[...] += dg

    row = pl.BlockSpec((tm, D), lambda i: (i, 0))
    vec = pl.BlockSpec((1, D), lambda i: (0, 0))
    return pl.pallas_call(
        body, name=name, grid=(T // tm,),
        in_specs=[row, vec, row, row], out_specs=[row, vec],
        out_shape=[jax.ShapeDtypeStruct((T, D), F32), jax.ShapeDtypeStruct((1, D), F32)],
        compiler_params=_params(("arbitrary",)),
    )(x, g, dn, dres)


def _loss_head(x, g, target, *, name, tm=512):
    T, D = x.shape
    tm = min(tm, T)

    def body(x_ref, g_ref, t_ref, loss_ref, dx_ref, dg_ref):
        xv = x_ref[...]
        g = g_ref[...]
        rstd = lax.rsqrt(jnp.mean(xv * xv, axis=-1, keepdims=True) + EPS)
        y = xv * rstd * g
        err = y - t_ref[...]
        part = 0.5 * jnp.sum(jnp.mean(err * err, axis=-1, keepdims=True), axis=0, keepdims=True)
        dx, dg = _rms_bwd_math(xv, g, err * (1.0 / D))
        dx_ref[...] = dx

        @pl.when(pl.program_id(0) == 0)
        def _():
            dg_ref[...] = jnp.zeros_like(dg_ref)
            loss_ref[...] = jnp.zeros_like(loss_ref)

        dg_ref[...] += dg
        loss_ref[...] += jnp.broadcast_to(part, loss_ref.shape)

    row = pl.BlockSpec((tm, D), lambda i: (i, 0))
    vec = pl.BlockSpec((1, D), lambda i: (0, 0))
    lspec = pl.BlockSpec((1, LANES), lambda i: (0, 0))
    return pl.pallas_call(
        body, name=name, grid=(T // tm,),
        in_specs=[row, vec, row], out_specs=[lspec, row, vec],
        out_shape=[jax.ShapeDtypeStruct((1, LANES), F32), jax.ShapeDtypeStruct((T, D), F32),
                   jax.ShapeDtypeStruct((1, D), F32)],
        compiler_params=_params(("arbitrary",)),
    )(x, g, target)


def _resident(shape):
    return pl.BlockSpec(shape, lambda i: (0,) * len(shape), pipeline_mode=pl.Buffered(1))


def _ffn_chunk(F):
    for c in (1408, 1024, 512, 256, 128):
        if F % c == 0:
            return c
    return F


def _ffn_fwd(x, g, wg, wu, wd, *, name, tm=256):
    T, D = x.shape
    F = wg.shape[1]
    tm = min(tm, T)
    fc = _ffn_chunk(F)

    def body(x_ref, g_ref, wg_ref, wu_ref, wd_ref, o_ref, a_ref, u_ref):
        xv = x_ref[...]
        rstd = lax.rsqrt(jnp.mean(xv * xv, axis=-1, keepdims=True) + EPS)
        n = (xv * rstd * g_ref[...]).astype(BF16)
        acc = jnp.zeros((tm, D), F32)
        for c in range(F // fc):
            sl = slice(c * fc, (c + 1) * fc)
            a = _dot(n, wg_ref[:, sl])
            u = _dot(n, wu_ref[:, sl])
            a_ref[:, sl] = a.astype(BF16)
            u_ref[:, sl] = u.astype(BF16)
            act = (a * _sigmoid(a) * u).astype(BF16)
            acc = acc + _dot(act, wd_ref[sl, :])
        o_ref[...] = xv + 0.5 * acc

    row = pl.BlockSpec((tm, D), lambda i: (i, 0))
    hid = pl.BlockSpec((tm, F), lambda i: (i, 0))
    return pl.pallas_call(
        body, name=name, grid=(T // tm,),
        in_specs=[row, pl.BlockSpec((1, D), lambda i: (0, 0)),
                  _resident((D, F)), _resident((D, F)), _resident((F, D))],
        out_specs=[row, hid, hid],
        out_shape=[jax.ShapeDtypeStruct((T, D), F32), jax.ShapeDtypeStruct((T, F), BF16),
                   jax.ShapeDtypeStruct((T, F), BF16)],
        compiler_params=_params(("parallel",)),
    )(x, g, wg, wu, wd)


def _ffn_bwd(x, g, dy, a, u, wg, wu, wd, *, name, tm=256):
    T, D = x.shape
    F = wg.shape[1]
    tm = min(tm, T)
    fc = _ffn_chunk(F)

    def body(x_ref, g_ref, dy_ref, a_ref, u_ref, wg_ref, wu_ref, wd_ref,
             dx_ref, dg_ref, n_ref, act_ref, da_ref, du_ref):
        xv = x_ref[...]
        g = g_ref[...]
        rstd = lax.rsqrt(jnp.mean(xv * xv, axis=-1, keepdims=True) + EPS)
        n_ref[...] = (xv * rstd * g).astype(BF16)
        dy = dy_ref[...]
        dyh = (0.5 * dy).astype(BF16)
        dn = jnp.zeros((tm, D), F32)
        for c in range(F // fc):
            sl = slice(c * fc, (c + 1) * fc)
            av = a_ref[:, sl].astype(F32)
            uv = u_ref[:, sl].astype(F32)
            dact = _dot(dyh, wd_ref[sl, :], NT_DIMS)
            s = _sigmoid(av)
            silu = av * s
            act_ref[:, sl] = (silu * uv).astype(BF16)
            da = (dact * uv * (s * (1.0 + av * (1.0 - s)))).astype(BF16)
            du = (dact * silu).astype(BF16)
            da_ref[:, sl] = da
            du_ref[:, sl] = du
            dn = dn + _dot(da, wg_ref[:, sl], NT_DIMS) + _dot(du, wu_ref[:, sl], NT_DIMS)
        dx, dg = _rms_bwd_math(xv, g, dn)
        dx_ref[...] = dy + dx

        @pl.when(pl.program_id(0) == 0)
        def _():
            dg_ref[...] = jnp.zeros_like(dg_ref)

        dg_ref[...] += dg

    row = pl.BlockSpec((tm, D), lambda i: (i, 0))
    hid = pl.BlockSpec((tm, F), lambda i: (i, 0))
    vec = pl.BlockSpec((1, D), lambda i: (0, 0))
    return pl.pallas_call(
        body, name=name, grid=(T // tm,),
        in_specs=[row, vec, row, hid, hid,
                  _resident((D, F)), _resident((D, F)), _resident((F, D))],
        out_specs=[row, vec, row, hid, hid, hid],
        out_shape=[jax.ShapeDtypeStruct((T, D), F32), jax.ShapeDtypeStruct((1, D), F32),
                   jax.ShapeDtypeStruct((T, D), BF16), jax.ShapeDtypeStruct((T, F), BF16),
                   jax.ShapeDtypeStruct((T, F), BF16), jax.ShapeDtypeStruct((T, F), BF16)],
        compiler_params=_params(("arbitrary",)),
    )(x, g, dy, a, u, wg, wu, wd)


def _lane_iota(shape):
    return lax.broadcasted_iota(jnp.int32, shape, 1)


def _rope_partner(x):
    first_half = (_lane_iota(x.shape) & (HEAD_DIM - 1)) < HEAD_DIM // 2
    return jnp.where(first_half, pltpu.roll(x, LANES - HEAD_DIM // 2, 1), pltpu.roll(x, HEAD_DIM // 2, 1))


def _swap_heads(x):
    return pltpu.roll(x, HEAD_DIM, 1)


def _undilate(blk_ref, d, stage):
    if d == 1:
        return blk_ref[...]
    n, width = blk_ref.shape
    W = width // d
    for r in range(d):
        for g in range(W // LANES):
            stage.at[g][pl.ds(r, n, stride=d), :] = blk_ref[:, r * W + g * LANES:r * W + (g + 1) * LANES]
    return jnp.concatenate([stage.at[g][...] for g in range(W // LANES)], axis=1)


def _dilate_into(out_ref, value, d, stage):
    if d == 1:
        out_ref[...] = value.astype(out_ref.dtype)
        return
    n = value.shape[0] // d
    W = value.shape[1]
    for g in range(W // LANES):
        stage.at[g][...] = value[:, g * LANES:(g + 1) * LANES]
    for r in range(d):
        for g in range(W // LANES):
            out_ref[:, r * W + g * LANES:r * W + (g + 1) * LANES] = (
                stage.at[g][pl.ds(r, n, stride=d), :].astype(out_ref.dtype))


def _dilated_spec(tm, d, W):
    return pl.BlockSpec((tm // d, d * W), lambda i: (i, 0))


def _stage(tm, W):
    return pltpu.VMEM((W // LANES, tm, LANES), F32)


DILATIONS = tuple(d for _, d in B_BRANCHES)


def _split_rope(proj, cos_t, sin_t, *, name, tm=256):
    T = proj.shape[0]
    tm = min(tm, T)
    nd = len(DILATIONS)

    def body(p_ref, c_ref, s_ref, qa_ref, ka_ref, va_ref, *rest):
        b_refs = rest[:3 * nd]
        xc_ref, gc_ref, stage = rest[3 * nd:]
        cos = c_ref[...]
        sin = s_ref[...]

        def rope(x):
            return x * cos + _rope_partner(x) * sin

        lo = _lane_iota((tm, LANES)) < HEAD_DIM
        col = 0
        for j in range(A_WIDTH // LANES):
            qa_ref[:, j * LANES:(j + 1) * LANES] = (rope(p_ref[:, col:col + LANES]) * SCALE).astype(BF16)
            col += LANES
        kr = rope(p_ref[:, col:col + LANES])
        col += LANES
        vr = p_ref[:, col:col + LANES]
        col += LANES
        for src, dst in ((kr, ka_ref), (vr, va_ref)):
            sw = _swap_heads(src)
            dst[:, 0:LANES] = jnp.where(lo, src, sw).astype(BF16)
            dst[:, LANES:2 * LANES] = jnp.where(lo, sw, src).astype(BF16)
        for which, (roped, scale) in enumerate(((True, SCALE), (True, 1.0), (False, 1.0))):
            parts = []
            for j in range(B_WIDTH // LANES):
                v = p_ref[:, col:col + LANES]
                parts.append(rope(v) * scale if roped else v)
                col += LANES
            value = jnp.concatenate(parts, axis=1)
            for di, d in enumerate(DILATIONS):
                _dilate_into(b_refs[which * nd + di], value, d, stage)
        xc_ref[...] = p_ref[:, col:col + C_WIDTH]
        gc_ref[...] = p_ref[:, col + C_WIDTH:col + 2 * C_WIDTH]

    def row(w):
        return pl.BlockSpec((tm, w), lambda i: (i, 0))

    out_specs = [row(A_WIDTH)] * 3 + [_dilated_spec(tm, d, B_WIDTH) for _ in range(3) for d in DILATIONS]
    out_specs += [row(C_WIDTH)] * 2
    out_shape = [jax.ShapeDtypeStruct((T, A_WIDTH), BF16)] * 3
    out_shape += [jax.ShapeDtypeStruct((T // d, d * B_WIDTH), BF16) for _ in range(3) for d in DILATIONS]
    out_shape += [jax.ShapeDtypeStruct((T, C_WIDTH), F32)] * 2
    res = pl.pallas_call(
        body, name=name, grid=(T // tm,),
        in_specs=[row(IN_COLS), row(LANES), row(LANES)], out_specs=out_specs, out_shape=out_shape,
        scratch_shapes=[_stage(tm, B_WIDTH)],
        compiler_params=_params(("parallel",)),
    )(proj, cos_t, sin_t)
    qa, ka2, va2 = res[:3]
    qb, kb, vb = (list(res[3 + i * nd:3 + (i + 1) * nd]) for i in range(3))
    return qa, ka2, va2, qb, kb, vb, res[-2], res[-1]


def _merge_dproj(dqa, dka2, dva2, dqb, dkb, dvb, dxc, dgc, cos_t, sin_t, *, name, tm=256):
    T = dqa.shape[0]
    tm = min(tm, T)
    nb = len(dqb)

    def body(*refs):
        dqa_ref, dka_ref, dva_ref = refs[:3]
        dqb_refs = refs[3:3 + nb]
        dkb_refs = refs[3 + nb:3 + 2 * nb]
        dvb_refs = refs[3 + 2 * nb:3 + 3 * nb]
        dxc_ref, dgc_ref, c_ref, s_ref, o_ref, stage = refs[3 + 3 * nb:]
        cos = c_ref[...]
        sin = s_ref[...]

        def rope_t(dy):
            return dy * cos - _rope_partner(dy) * sin

        lo = _lane_iota((tm, LANES)) < HEAD_DIM
        col = 0
        for j in range(A_WIDTH // LANES):
            o_ref[:, col:col + LANES] = (rope_t(dqa_ref[:, j * LANES:(j + 1) * LANES]) * SCALE).astype(BF16)
            col += LANES
        for src, roped in ((dka_ref, True), (dva_ref, False)):
            b0 = src[:, 0:LANES]
            b1 = src[:, LANES:2 * LANES]
            v = jnp.where(lo, b0 + _swap_heads(b0), b1 + _swap_heads(b1))
            if roped:
                v = rope_t(v)
            o_ref[:, col:col + LANES] = v.astype(BF16)
            col += LANES
        for group, roped, scale in ((dqb_refs, True, SCALE), (dkb_refs, True, 1.0), (dvb_refs, False, 1.0)):
            total = _undilate(group[0], DILATIONS[0], stage)
            for r, d in zip(group[1:], DILATIONS[1:]):
                total = total + _undilate(r, d, stage)
            for j in range(B_WIDTH // LANES):
                v = total[:, j * LANES:(j + 1) * LANES]
                if roped:
                    v = rope_t(v) * scale
                o_ref[:, col:col + LANES] = v.astype(BF16)
                col += LANES
        o_ref[:, col:col + C_WIDTH] = dxc_ref[...].astype(BF16)
        o_ref[:, col + C_WIDTH:col + 2 * C_WIDTH] = dgc_ref[...].astype(BF16)

    def row(w):
        return pl.BlockSpec((tm, w), lambda i: (i, 0))

    ins = [dqa, dka2, dva2, *dqb, *dkb, *dvb, dxc, dgc, cos_t, sin_t]
    in_specs = [row(A_WIDTH)] * 3 + [_dilated_spec(tm, d, B_WIDTH) for _ in range(3) for d in DILATIONS]
    in_specs += [row(C_WIDTH)] * 2 + [row(LANES)] * 2
    return pl.pallas_call(
        body, name=name, grid=(T // tm,), in_specs=in_specs,
        out_specs=row(IN_COLS),
        out_shape=jax.ShapeDtypeStruct((T, IN_COLS), BF16),
        scratch_shapes=[_stage(tm, B_WIDTH)],
        compiler_params=_params(("parallel",)),
    )(*ins)


def _band_masks(max_dist):
    row = lax.broadcasted_iota(jnp.int32, (QBLK, 2 * QBLK), 0)
    key = lax.broadcasted_iota(jnp.int32, (QBLK, 2 * QBLK), 1)
    dist = row + QBLK - key
    wide = jnp.logical_and(dist >= 0, dist <= max_dist)
    return wide, wide[:, :QBLK], key >= QBLK


def _head_masks(rows=QBLK):
    lo = _lane_iota((rows, LANES)) < HEAD_DIM
    return lo, jnp.logical_not(lo)


def _keep(hm, x):
    return x * jnp.where(hm, 1.0, 0.0).astype(x.dtype)


def _head_col(x, hm):
    return jnp.max(jnp.where(hm, x, -jnp.inf), axis=1, keepdims=True)


def _attn_specs(R, C):
    chunk = min(ATTN_CHUNK, R)
    nb = chunk // QBLK
    nch = R // chunk
    main = pl.BlockSpec((chunk, LANES), lambda j, c: (c, j))
    prev = pl.BlockSpec((QBLK, LANES), lambda j, c: (jnp.maximum(c * nb - 1, 0), j))
    nxt = pl.BlockSpec((QBLK, LANES), lambda j, c: (jnp.minimum((c + 1) * nb, R // QBLK - 1), j))
    return chunk, nb, nch, main, prev, nxt


def _attn_fwd(q, k, v, max_dist, *, name):
    R, C = q.shape
    chunk, nb, nch, main, prev, _ = _attn_specs(R, C)

    def body(q_ref, k_ref, v_ref, kp_ref, vp_ref, o_ref, lse_ref):
        c = pl.program_id(1)
        wide_mask, _, own_block = _band_masks(max_dist)
        heads = _head_masks()

        def block(q_blk, kk, vv, mask):
            o_h, lse_h = [], []
            for hm in heads:
                s = jnp.where(mask, _dot(_keep(hm, q_blk), kk, NT_DIMS), -jnp.inf)
                m = jnp.max(jnp.maximum(s[:, :QBLK], s[:, QBLK:]), axis=1, keepdims=True)
                p = jnp.exp(s - m)
                l = jnp.sum(p[:, :QBLK] + p[:, QBLK:], axis=1, keepdims=True)
                o_h.append(_dot(p.astype(BF16), vv) / l)
                lse_h.append(jnp.broadcast_to(m + jnp.log(l), (QBLK, LANES)))
            return jnp.where(heads[0], o_h[0], o_h[1]), jnp.where(heads[0], lse_h[0], lse_h[1])

        first = pl.ds(0, QBLK)
        o0, l0 = block(q_ref[first, :], jnp.concatenate([kp_ref[...], k_ref[first, :]], axis=0),
                       jnp.concatenate([vp_ref[...], v_ref[first, :]], axis=0),
                       jnp.logical_and(wide_mask, jnp.logical_or(own_block, c > 0)))
        o_ref[first, :] = o0
        lse_ref[first, :] = l0

        def loop(qb, carry):
            cur = pl.ds(pl.multiple_of(qb * QBLK, QBLK), QBLK)
            both = pl.ds(pl.multiple_of((qb - 1) * QBLK, QBLK), 2 * QBLK)
            o, l = block(q_ref[cur, :], k_ref[both, :], v_ref[both, :], wide_mask)
            o_ref[cur, :] = o
            lse_ref[cur, :] = l
            return carry

        if nb > 1:
            lax.fori_loop(1, nb, loop, 0, unroll=ATTN_FWD_UNROLL)

    return pl.pallas_call(
        body, name=name, grid=(C // LANES, nch),
        in_specs=[main, main, main, prev, prev], out_specs=[main, main],
        out_shape=[jax.ShapeDtypeStruct((R, C), F32), jax.ShapeDtypeStruct((R, C), F32)],
        compiler_params=_params(("parallel", "parallel")),
    )(q, k, v, k, v)


def _attn_bwd(q, k, v, do, lse, delta, max_dist, *, name):
    R, C = q.shape
    chunk, nb, nch, main, prev, nxt = _attn_specs(R, C)

    def body(q_ref, k_ref, v_ref, do_ref, lse_ref, dl_ref, kp_ref, vp_ref, qn_ref, don_ref, lsen_ref, dln_ref,
             dq_ref, dk_ref, dv_ref):
        c = pl.program_id(1)
        wide_mask, prev_mask, own_block = _band_masks(max_dist)
        heads = _head_masks()

        def pair(q_blk, do_blk, lse_blk, dl_blk, kk, vv, mask, want_dq=True):
            dq = jnp.zeros((QBLK, LANES), F32)
            dkk = jnp.zeros(kk.shape, F32)
            dvv = jnp.zeros(kk.shape, F32)
            for hm, khm in zip(heads, _head_masks(kk.shape[0])):
                qh = _keep(hm, q_blk)
                doh = _keep(hm, do_blk)
                p = jnp.where(mask, jnp.exp(_dot(qh, kk, NT_DIMS) - _head_col(lse_blk, hm)), 0.0)
                ds = (p * (_dot(doh, vv, NT_DIMS) - _head_col(dl_blk, hm))).astype(BF16)
                if want_dq:
                    dq = dq + _dot(ds, _keep(khm, kk))
                dkk = dkk + _dot(ds, qh, TN_DIMS)
                dvv = dvv + _dot(p.astype(BF16), doh, TN_DIMS)
            return dq, dkk, dvv

        dk_ref[...] = jnp.zeros_like(dk_ref)
        dv_ref[...] = jnp.zeros_like(dv_ref)

        first = pl.ds(0, QBLK)
        dq0, dkk0, dvv0 = pair(q_ref[first, :], do_ref[first, :], lse_ref[first, :], dl_ref[first, :],
                               jnp.concatenate([kp_ref[...], k_ref[first, :]], axis=0),
                               jnp.concatenate([vp_ref[...], v_ref[first, :]], axis=0),
                               jnp.logical_and(wide_mask, jnp.logical_or(own_block, c > 0)))
        dq_ref[first, :] = dq0
        dk_ref[first, :] += dkk0[QBLK:, :]
        dv_ref[first, :] += dvv0[QBLK:, :]

        def loop(qb, carry):
            cur = pl.ds(pl.multiple_of(qb * QBLK, QBLK), QBLK)
            both = pl.ds(pl.multiple_of((qb - 1) * QBLK, QBLK), 2 * QBLK)
            dq, dkk, dvv = pair(q_ref[cur, :], do_ref[cur, :], lse_ref[cur, :], dl_ref[cur, :],
                                k_ref[both, :], v_ref[both, :], wide_mask)
            dq_ref[cur, :] = dq
            dk_ref[both, :] += dkk
            dv_ref[both, :] += dvv
            return carry

        if nb > 1:
            lax.fori_loop(1, nb, loop, 0, unroll=ATTN_BWD_UNROLL)

        last = pl.ds((nb - 1) * QBLK, QBLK)
        _, dk_n, dv_n = pair(qn_ref[...], don_ref[...], lsen_ref[...], dln_ref[...], k_ref[last, :], v_ref[last, :],
                             jnp.logical_and(prev_mask, c < nch - 1), want_dq=False)
        dk_ref[last, :] += dk_n
        dv_ref[last, :] += dv_n

    return pl.pallas_call(
        body, name=name, grid=(C // LANES, nch),
        in_specs=[main] * 6 + [prev, prev] + [nxt] * 4, out_specs=[main, main, main],
        out_shape=[jax.ShapeDtypeStruct((R, C), F32)] * 3,
        compiler_params=_params(("parallel", "parallel")),
    )(q, k, v, do, lse, delta, k, v, q, do, lse, delta)


def _head_sum(x):
    r = lax.broadcasted_iota(jnp.int32, (LANES, LANES), 0) // HEAD_DIM
    c = lax.broadcasted_iota(jnp.int32, (LANES, LANES), 1) // HEAD_DIM
    ones = jnp.where(r == c, 1.0, 0.0).astype(BF16)
    outs = []
    for j in range(x.shape[1] // LANES):
        rem = x[:, j * LANES:(j + 1) * LANES]
        acc = jnp.zeros(rem.shape, F32)
        for _ in range(3):
            part = rem.astype(BF16)
            acc = acc + _dot(part, ones)
            rem = rem - part.astype(F32)
        outs.append(acc)
    return outs[0] if len(outs) == 1 else jnp.concatenate(outs, axis=1)


def _branch_weights(lses):
    m = functools.reduce(jnp.maximum, lses)
    es = [jnp.exp(l - m) for l in lses]
    den = functools.reduce(lambda a, b: a + b, es)
    return [e / den for e in es]


def _combine_fwd(oa, lsea, sink, obs, lsebs, oc, *, name, tm=256):
    T = oa.shape[0]
    tm = min(tm, T)
    nb = len(obs)

    def body(*refs):
        oa_ref, lsea_ref, sink_ref = refs[:3]
        ob_refs = refs[3:3 + nb]
        lse_refs = refs[3 + nb:3 + 2 * nb]
        oc_ref, out_ref, stage = refs[3 + 2 * nb:]
        out_ref[:, 0:A_WIDTH] = (oa_ref[...] * _sigmoid(lsea_ref[...] - sink_ref[...])).astype(BF16)
        ws = _branch_weights([_undilate(r, d, stage) for r, d in zip(lse_refs, DILATIONS)])
        ob = _undilate(ob_refs[0], DILATIONS[0], stage) * ws[0]
        for r, d, w in zip(ob_refs[1:], DILATIONS[1:], ws[1:]):
            ob = ob + _undilate(r, d, stage) * w
        out_ref[:, A_WIDTH:A_WIDTH + B_WIDTH] = ob.astype(BF16)
        out_ref[:, A_WIDTH + B_WIDTH:MIX_WIDTH] = oc_ref[...].astype(BF16)

    def row(w):
        return pl.BlockSpec((tm, w), lambda i: (i, 0))

    ins = [oa, lsea, sink, *obs, *lsebs, oc]
    in_specs = [row(A_WIDTH), row(A_WIDTH), pl.BlockSpec((1, A_WIDTH), lambda i: (0, 0))]
    in_specs += [_dilated_spec(tm, d, B_WIDTH) for _ in range(2) for d in DILATIONS] + [row(C_WIDTH)]
    return pl.pallas_call(
        body, name=name, grid=(T // tm,), in_specs=in_specs, out_specs=row(MIX_WIDTH),
        out_shape=jax.ShapeDtypeStruct((T, MIX_WIDTH), BF16),
        scratch_shapes=[_stage(tm, B_WIDTH)],
        compiler_params=_params(("parallel",)),
    )(*ins)


def _combine_bwd(dmix, oa, lsea, sink, obs, lsebs, *, name, tm=256):
    T = oa.shape[0]
    tm = min(tm, T)
    nb = len(obs)

    def body(*refs):
        dmix_ref, oa_ref, lsea_ref, sink_ref = refs[:4]
        ob_refs = refs[4:4 + nb]
        lse_refs = refs[4 + nb:4 + 2 * nb]
        outs = refs[4 + 2 * nb:-1]
        stage = refs[-1]
        doa_ref, dla_ref = outs[:2]
        dob_refs = outs[2:2 + nb]
        dlb_refs = outs[2 + nb:2 + 2 * nb]
        doc_ref, dsink_ref = outs[2 + 2 * nb:]

        d_a = dmix_ref[:, 0:A_WIDTH]
        d_b = dmix_ref[:, A_WIDTH:A_WIDTH + B_WIDTH]
        doc_ref[...] = dmix_ref[:, A_WIDTH + B_WIDTH:MIX_WIDTH]

        gate = _sigmoid(lsea_ref[...] - sink_ref[...])
        doa_ref[...] = (d_a * gate).astype(BF16)
        dgate = _head_sum(d_a * oa_ref[...])
        dlse = dgate * gate * (1.0 - gate)
        dla_ref[...] = dgate * gate - dlse

        @pl.when(pl.program_id(0) == 0)
        def _():
            dsink_ref[...] = jnp.zeros_like(dsink_ref)

        dsink_ref[...] -= jnp.sum(dlse, axis=0, keepdims=True)

        ws = _branch_weights([_undilate(r, d, stage) for r, d in zip(lse_refs, DILATIONS)])
        dws = [_head_sum(d_b * _undilate(r, d, stage)) for r, d in zip(ob_refs, DILATIONS)]
        sw = ws[0] * dws[0]
        for w, dw in zip(ws[1:], dws[1:]):
            sw = sw + w * dw
        for w, d, do_ref, dl_ref in zip(ws, DILATIONS, dob_refs, dlb_refs):
            _dilate_into(do_ref, w * d_b, d, stage)
            _dilate_into(dl_ref, w * sw, d, stage)

    def row(w):
        return pl.BlockSpec((tm, w), lambda i: (i, 0))

    vec = pl.BlockSpec((1, A_WIDTH), lambda i: (0, 0))
    dil = [_dilated_spec(tm, d, B_WIDTH) for _ in range(2) for d in DILATIONS]
    ins = [dmix, oa, lsea, sink, *obs, *lsebs]
    in_specs = [row(MIX_WIDTH), row(A_WIDTH), row(A_WIDTH), vec] + dil
    out_specs = [row(A_WIDTH), row(A_WIDTH)] + dil + [row(C_WIDTH), vec]
    out_shape = [jax.ShapeDtypeStruct((T, A_WIDTH), BF16), jax.ShapeDtypeStruct((T, A_WIDTH), F32)]
    out_shape += [jax.ShapeDtypeStruct((T // d, d * B_WIDTH), BF16) for d in DILATIONS]
    out_shape += [jax.ShapeDtypeStruct((T // d, d * B_WIDTH), F32) for d in DILATIONS]
    out_shape += [jax.ShapeDtypeStruct((T, C_WIDTH), F32), jax.ShapeDtypeStruct((1, A_WIDTH), F32)]
    res = pl.pallas_call(
        body, name=name, grid=(T // tm,), in_specs=in_specs, out_specs=out_specs, out_shape=out_shape,
        scratch_shapes=[_stage(tm, B_WIDTH)],
        compiler_params=_params(("arbitrary",)),
    )(*ins)
    return res[0], res[1], list(res[2:2 + nb]), list(res[2 + nb:2 + 2 * nb]), res[2 + 2 * nb], res[3 + 2 * nb]


HIST = 8


def _softplus_neg(lam):
    e = jnp.exp(-jnp.abs(lam))
    log1p = jnp.where(e < 0.01, e * (1.0 - e * (0.5 - e * (1.0 / 3.0))), jnp.log(1.0 + e))
    return jnp.maximum(-lam, 0.0) + log1p


def _neg_expm1(x):
    series = -x * (1.0 + x * (0.5 + x * (1.0 / 6.0 + x * (1.0 / 24.0 + x * (1.0 / 120.0)))))
    return jnp.where(x > -0.1, series, 1.0 - jnp.exp(x))


def _gelu_parts(x):
    k = math.sqrt(2.0 / math.pi)
    t = jnp.tanh(k * (x + 0.044715 * (x * x * x)))
    cdf = 0.5 * (1.0 + t)
    return x * cdf, cdf + 0.5 * x * (1.0 - t * t) * (k * (1.0 + 3.0 * 0.044715 * (x * x)))


def _rglru_gates(y, pos_ref, wr_ref, br_ref, wi_ref, bi_ref, lam_ref):
    yb = y.astype(BF16)
    r = _sigmoid(_dot(yb, wr_ref[...]) + br_ref[...])
    ig = _sigmoid(_dot(yb, wi_ref[...]) + bi_ref[...])
    sp = _softplus_neg(lam_ref[...])
    log_a = -C_EXP * r * sp
    reset = pos_ref[...] == 0
    a = jnp.where(reset, 0.0, jnp.exp(log_a))
    mult = jnp.where(reset, 1.0, jnp.sqrt(_neg_expm1(2.0 * log_a)))
    return yb, r, ig, sp, reset, a, mult


def _conv_fwd(xs_ref, cw_ref, cb_ref, tm):
    y = cb_ref[...] + cw_ref[0:1, :] * xs_ref[HIST:HIST + tm, :]
    for j in range(1, C_CONV):
        y = y + cw_ref[j:j + 1, :] * xs_ref[HIST - j:HIST - j + tm, :]
    return y


SCAN_GROUP = 8


def _blocked_scan(c, d, c_s, d_s, grp_a, grp_h, carry, reverse):
    tm, W = d.shape
    groups = tm // SCAN_GROUP
    order = range(SCAN_GROUP - 1, -1, -1) if reverse else range(SCAN_GROUP)
    outs, lasts = [], []
    for k in range(W // LANES):
        lanes = slice(k * LANES, (k + 1) * LANES)
        ck, dk, ga, gh = c_s.at[k], d_s.at[k], grp_a.at[k], grp_h.at[k]
        ck[...] = c[:, lanes]
        dk[...] = d[:, lanes]
        prod = state = None
        for j in order:
            rows = pl.ds(j, groups, stride=SCAN_GROUP)
            cj, dj = ck[rows, :], dk[rows, :]
            if prod is None:
                prod, state = cj, dj
            else:
                state = cj * state + dj
                prod = cj * prod
            ck[rows, :] = prod
            dk[rows, :] = state
        ga[...] = prod
        gh[...] = state

        def step(i, h, ga=ga, gh=gh):
            row = pl.ds(groups - 1 - i if reverse else i, 1)
            a, t = ga[row, :], gh[row, :]
            ga[row, :] = h
            return a * h + t

        lasts.append(lax.fori_loop(0, groups, step, carry[:, lanes], unroll=8))
        entering = ga[...]
        for j in range(SCAN_GROUP):
            rows = pl.ds(j, groups, stride=SCAN_GROUP)
            dk[rows, :] = dk[rows, :] + ck[rows, :] * entering
        outs.append(dk[...])
    return jnp.concatenate(outs, axis=1), jnp.concatenate(lasts, axis=1)


def _rglru_fwd(xc, gc, pos, cw, cb, wr, br, wi, bi, lam, *, name, tm=512):
    T, W = xc.shape
    tm = min(tm, T)

    def body(xc_ref, gc_ref, pos_ref, cw_ref, cb_ref, wr_ref, br_ref, wi_ref, bi_ref, lam_ref,
             out_ref, hs_ref, xs, a_s, b_s, h_s, grp_a, grp_h):
        @pl.when(pl.program_id(0) == 0)
        def _():
            xs[0:HIST, :] = jnp.zeros((HIST, W), F32)
            h_s[...] = jnp.zeros_like(h_s)

        xv = xc_ref[...]
        xs[HIST:HIST + tm, :] = xv
        y = _conv_fwd(xs, cw_ref, cb_ref, tm)
        xs[0:HIST, :] = xv[tm - HIST:tm, :]
        _, _, ig, _, _, a, mult = _rglru_gates(y, pos_ref, wr_ref, br_ref, wi_ref, bi_ref, lam_ref)
        hs, h_s[...] = _blocked_scan(a, mult * (ig * y), a_s, b_s, grp_a, grp_h, h_s[...], reverse=False)
        hs_ref[...] = hs
        out_ref[...] = hs * _gelu_parts(gc_ref[...])[0]

    row = pl.BlockSpec((tm, W), lambda i: (i, 0))
    full = lambda shape: pl.BlockSpec(shape, lambda i: (0,) * len(shape))
    return pl.pallas_call(
        body, name=name, grid=(T // tm,),
        in_specs=[row, row, pl.BlockSpec((tm, 1), lambda i: (i, 0)), full((C_CONV, W)), full((1, W)),
                  full((W, W)), full((1, W)), full((W, W)), full((1, W)), full((1, W))],
        out_specs=[row, row],
        out_shape=[jax.ShapeDtypeStruct((T, W), F32)] * 2,
        scratch_shapes=[pltpu.VMEM((tm + HIST, W), F32), pltpu.VMEM((W // LANES, tm, LANES), F32),
                        pltpu.VMEM((W // LANES, tm, LANES), F32), pltpu.VMEM((1, W), F32),
                        pltpu.VMEM((W // LANES, tm // SCAN_GROUP, LANES), F32),
                        pltpu.VMEM((W // LANES, tm // SCAN_GROUP, LANES), F32)],
        compiler_params=_params(("arbitrary",)),
    )(xc, gc, pos, cw, cb, wr, br, wi, bi, lam)


def _rglru_bwd(xc, gc, pos, hs, dout, cw, cb, wr, br, wi, bi, lam, *, name, tm=512):
    T, W = xc.shape
    tm = min(tm, T)
    nt = T // tm
    hb = tm // HIST

    def body(xc_ref, gc_ref, pos_ref, hs_ref, dout_ref, xch_ref, hsh_ref,
             cw_ref, cb_ref, wr_ref, br_ref, wi_ref, bi_ref, lam_ref,
             dxc_ref, dgc_ref, dcw_ref, dcb_ref, dwr_ref, dbr_ref, dwi_ref, dbi_ref, dlam_ref,
             xs, hsx, dys, asx, a_s, d_s, carry_s, grp_a, grp_h):
        i = pl.program_id(0)

        @pl.when(i == 0)
        def _():
            for r in (dcw_ref, dcb_ref, dwr_ref, dbr_ref, dwi_ref, dbi_ref, dlam_ref, carry_s):
                r[...] = jnp.zeros_like(r)
            dys[tm:tm + HIST, :] = jnp.zeros((HIST, W), F32)
            asx[tm:tm + HIST, :] = jnp.zeros((HIST, W), F32)

        has_prev = i < nt - 1
        xs[0:HIST, :] = jnp.where(has_prev, xch_ref[...], 0.0)
        hsx[0:HIST, :] = jnp.where(has_prev, hsh_ref[...], 0.0)
        xs[HIST:HIST + tm, :] = xc_ref[...]
        hs = hs_ref[...]
        hsx[HIST:HIST + tm, :] = hs
        y = _conv_fwd(xs, cw_ref, cb_ref, tm)
        yb, r, ig, sp, reset, a, mult = _rglru_gates(y, pos_ref, wr_ref, br_ref, wi_ref, bi_ref, lam_ref)

        gelu, dgelu = _gelu_parts(gc_ref[...])
        dout = dout_ref[...]
        dgc_ref[...] = dout * hs * dgelu
        asx[0:tm, :] = a
        a_up = asx[1:1 + tm, :]
        asx[tm:tm + HIST, :] = a[0:HIST, :]
        dh, carry_s[...] = _blocked_scan(a_up, dout * gelu, a_s, d_s, grp_a, grp_h, carry_s[...], reverse=True)
        hprev = hsx[HIST - 1:HIST - 1 + tm, :]
        igy = ig * y
        dmult = dh * igy
        digy = dh * mult
        dlog_a = jnp.where(reset, 0.0, dh * hprev * a - dmult * a * a / mult)
        dlam_ref[...] += jnp.sum(dlog_a * (C_EXP * r) * _sigmoid(-lam_ref[...]), axis=0, keepdims=True)
        dz_r = dlog_a * (-C_EXP * sp) * r * (1.0 - r)
        dz_i = digy * y * ig * (1.0 - ig)
        dzr_b = dz_r.astype(BF16)
        dzi_b = dz_i.astype(BF16)
        dy = digy * ig + _dot(dzr_b, wr_ref[...], NT_DIMS) + _dot(dzi_b, wi_ref[...], NT_DIMS)
        dwr_ref[...] += _dot(yb, dzr_b, TN_DIMS)
        dwi_ref[...] += _dot(yb, dzi_b, TN_DIMS)
        dbr_ref[...] += jnp.sum(dz_r, axis=0, keepdims=True)
        dbi_ref[...] += jnp.sum(dz_i, axis=0, keepdims=True)

        dys[0:tm, :] = dy
        dxc = cw_ref[0:1, :] * dy
        for j in range(1, C_CONV):
            dxc = dxc + cw_ref[j:j + 1, :] * dys[j:j + tm, :]
        dxc_ref[...] = dxc
        dys[tm:tm + HIST, :] = dy[0:HIST, :]
        dcb_ref[...] += jnp.sum(dy, axis=0, keepdims=True)
        for j in range(C_CONV):
            dcw_ref[j:j + 1, :] += jnp.sum(dy * xs[HIST - j:HIST - j + tm, :], axis=0, keepdims=True)

    row = pl.BlockSpec((tm, W), lambda i: (nt - 1 - i, 0))
    halo = pl.BlockSpec((HIST, W), lambda i: (jnp.maximum((nt - 1 - i) * hb - 1, 0), 0))
    full = lambda shape: pl.BlockSpec(shape, lambda i: (0,) * len(shape))
    out_specs = [row, row, full((C_CONV, W)), full((1, W)), full((W, W)), full((1, W)), full((W, W)), full((1, W)),
                 full((1, W))]
    out_shape = [jax.ShapeDtypeStruct((T, W), F32)] * 2
    out_shape += [jax.ShapeDtypeStruct(s, F32) for s in ((C_CONV, W), (1, W), (W, W), (1, W), (W, W), (1, W), (1, W))]
    return pl.pallas_call(
        body, name=name, grid=(nt,),
        in_specs=[row, row, pl.BlockSpec((tm, 1), lambda i: (nt - 1 - i, 0)), row, row, halo, halo,
                  full((C_CONV, W)), full((1, W)), full((W, W)), full((1, W)), full((W, W)), full((1, W)),
                  full((1, W))],
        out_specs=out_specs, out_shape=out_shape,
        scratch_shapes=[pltpu.VMEM((tm + HIST, W), F32), pltpu.VMEM((tm + HIST, W), F32),
                        pltpu.VMEM((tm + HIST, W), F32), pltpu.VMEM((tm + HIST, W), F32),
                        pltpu.VMEM((W // LANES, tm, LANES), F32), pltpu.VMEM((W // LANES, tm, LANES), F32),
                        pltpu.VMEM((1, W), F32), pltpu.VMEM((W // LANES, tm // SCAN_GROUP, LANES), F32),
                        pltpu.VMEM((W // LANES, tm // SCAN_GROUP, LANES), F32)],
        compiler_params=_params(("arbitrary",)),
    )(xc, gc, pos, hs, dout, xc, hs, cw, cb, wr, br, wi, bi, lam)


def _adam_math(w, g, m, v):
    m = ADAM_B1 * m + (1.0 - ADAM_B1) * g
    v = ADAM_B2 * v + (1.0 - ADAM_B2) * (g * g)
    m_hat = m / (1.0 - ADAM_B1 ** ADAM_STEP)
    v_hat = v / (1.0 - ADAM_B2 ** ADAM_STEP)
    delta = -ADAM_LR * (m_hat / (jnp.sqrt(v_hat) + ADAM_EPS) + ADAM_WD * w)
    return delta, m, v


def _pick_rows(R, cap=512, mult=16):
    for d in range(min(cap, R), 0, -1):
        if R % d == 0 and d % mult == 0:
            return d
    return R


def _adamw(parts, w, m, v, *, name, tr=None, part=0, prev=None):
    R, C = w.shape
    r = parts.shape[1]
    tr = _pick_rows(r) if tr is None else tr
    assert r % tr == 0 and R % r == 0, (name, R, r, tr)
    nt = r // tr

    def body(p_ref, w_ref, m_ref, v_ref, *rest):
        g_ref, d_ref, nm_ref, nv_ref = rest[-4:]
        g = p_ref[0].astype(F32)
        for d in range(1, N_DEV):
            g = g + p_ref[d].astype(F32)
        delta, nm, nv = _adam_math(w_ref[...], g, m_ref[...], v_ref[...])
        g_ref[...] = g
        d_ref[...] = delta
        nm_ref[...] = nm
        nv_ref[...] = nv

    row = pl.BlockSpec((tr, C), lambda i: (part * nt + i, 0))
    in_specs = [pl.BlockSpec((N_DEV, tr, C), lambda i: (0, i, 0)), row, row, row]
    operands = [parts, w, m, v]
    aliases = {}
    if prev is not None:
        in_specs += [pl.BlockSpec(memory_space=pl.ANY)] * 4
        operands += list(prev)
        aliases = {4 + i: i for i in range(4)}
    return pl.pallas_call(
        body, name=name, grid=(nt,), in_specs=in_specs,
        out_specs=[row] * 4, out_shape=[jax.ShapeDtypeStruct((R, C), F32)] * 4,
        input_output_aliases=aliases,
        compiler_params=_params(("parallel",)),
    )(*operands)


def _exchange(srcs, gather, *, name):
    n = len(srcs)
    out_shape = [jax.ShapeDtypeStruct((N_DEV,) + s.shape if gather else s.shape, s.dtype) for s in srcs]

    def body(*refs):
        ins, outs = refs[:n], refs[n:2 * n]
        send_sems, recv_sems, local_sems = refs[2 * n:]
        x, y, c = lax.axis_index("x"), lax.axis_index("y"), lax.axis_index("c")
        me = 4 * x + 2 * y + c
        local_copies, sends, arrivals = [], [], []
        for a in range(n):
            mine = ins[a] if gather else ins[a].at[me]
            local = pltpu.make_async_copy(mine, outs[a].at[me], local_sems.at[a])
            local.start()
            local_copies.append(local)
            for k in range(1, N_DEV):
                px, py, pc = x ^ ((k >> 2) & 1), y ^ ((k >> 1) & 1), c ^ (k & 1)
                peer = 4 * px + 2 * py + pc
                send = pltpu.make_async_remote_copy(
                    src_ref=ins[a] if gather else ins[a].at[peer], dst_ref=outs[a].at[me],
                    send_sem=send_sems.at[a * N_DEV + k], recv_sem=recv_sems.at[a * N_DEV + k],
                    device_id=(px, py, pc), device_id_type=pl.DeviceIdType.MESH)
                send.start()
                sends.append(send)
                arrivals.append(pltpu.make_async_remote_copy(
                    src_ref=mine, dst_ref=outs[a].at[peer],
                    send_sem=send_sems.at[a * N_DEV + k], recv_sem=recv_sems.at[a * N_DEV + k],
                    device_id=(px, py, pc), device_id_type=pl.DeviceIdType.MESH))
        for cp in sends:
            cp.wait_send()
        for cp in arrivals:
            cp.wait_recv()
        for cp in local_copies:
            cp.wait()

    return pl.pallas_call(
        body, name=name,
        in_specs=[pl.BlockSpec(memory_space=pl.ANY)] * n, out_specs=[pl.BlockSpec(memory_space=pl.ANY)] * n,
        out_shape=out_shape,
        scratch_shapes=[pltpu.SemaphoreType.DMA((n * N_DEV,)), pltpu.SemaphoreType.DMA((n * N_DEV,)),
                        pltpu.SemaphoreType.DMA((n,))],
    )(*srcs)


_HBM = pl.BlockSpec(memory_space=pltpu.HBM)
_SEM = pl.BlockSpec(memory_space=pltpu.SEMAPHORE)
_EFFECT = pltpu.SideEffectType.DATAFLOW_SIDE_EFFECTING


def _peers():
    x, y, c = lax.axis_index("x"), lax.axis_index("y"), lax.axis_index("c")
    out = []
    for k in range(1, N_DEV):
        px, py, pc = x ^ ((k >> 2) & 1), y ^ ((k >> 1) & 1), c ^ (k & 1)
        out.append((k, (px, py, pc), 4 * px + 2 * py + pc))
    return 4 * x + 2 * y + c, out


def _split_copies(src_refs, land_refs, send_sems, recv_sems, gather):
    me, peers = _peers()
    out = []
    for a, (src_ref, land_ref) in enumerate(zip(src_refs, land_refs)):
        for k, dev, blk in peers:
            common = dict(send_sem=send_sems.at[a * N_DEV + k], recv_sem=recv_sems.at[a * N_DEV + k], device_id=dev,
                          device_id_type=pl.DeviceIdType.MESH)
            src = src_ref if gather else src_ref.at[blk]
            out.append((pltpu.make_async_remote_copy(src_ref=src, dst_ref=land_ref.at[me], **common),
                        pltpu.make_async_remote_copy(src_ref=src, dst_ref=land_ref.at[blk], **common)))
    return out


def _exchange_start(srcs, gather, *, name, after=None):
    n = len(srcs)
    lands = [lax.empty((N_DEV,) + (s.shape if gather else s.shape[1:]), s.dtype) for s in srcs]

    def body(*refs):
        src_refs, land_refs = refs[:n], refs[n:2 * n]
        send_sems, recv_sems = refs[-2 * n - 3:-2 * n - 1]
        token = refs[-1]
        for outgoing, _ in _split_copies(src_refs, land_refs, send_sems, recv_sems, gather):
            outgoing.start()
        token[...] = jnp.zeros_like(token)

    res = pl.pallas_call(
        body, name=name,
        out_shape=(pltpu.SemaphoreType.DMA((n * N_DEV,)), pltpu.SemaphoreType.DMA((n * N_DEV,)),
                   *[pltpu.HBM(a.shape, a.dtype) for a in srcs + lands], jax.ShapeDtypeStruct((8, LANES), F32)),
        in_specs=(_HBM,) * (2 * n) + ((pl.BlockSpec(memory_space=pl.ANY),) if after is not None else ()),
        out_specs=(_SEM, _SEM) + (_HBM,) * (2 * n) + (pl.BlockSpec(memory_space=pltpu.VMEM),),
        input_output_aliases={i: i + 2 for i in range(2 * n)},
        compiler_params=pltpu.CompilerParams(has_side_effects=_EFFECT),
    )(*[pltpu.with_memory_space_constraint(a, pltpu.HBM) for a in srcs + lands],
      *([after] if after is not None else []))
    return res[0], res[1], list(res[2:2 + n]), list(res[2 + n:2 + 2 * n]), res[-1]


def _exchange_wait(started, after, gather, *, name):
    send_sems, recv_sems, srcs, lands, _ = started
    n = len(srcs)

    def body(*refs):
        src_refs, land_refs = refs[:n], refs[n:2 * n]
        send_sems, recv_sems = refs[2 * n:2 * n + 2]
        for outgoing, incoming in _split_copies(src_refs, land_refs, send_sems, recv_sems, gather):
            outgoing.wait_send()
            incoming.wait_recv()

    res = pl.pallas_call(
        body, name=name,
        out_shape=tuple(pltpu.HBM(a.shape, a.dtype) for a in srcs + lands),
        in_specs=(_HBM,) * (2 * n) + (_SEM, _SEM, pl.BlockSpec(memory_space=pl.ANY)), out_specs=(_HBM,) * (2 * n),
        input_output_aliases={i: i for i in range(2 * n)},
        compiler_params=pltpu.CompilerParams(has_side_effects=_EFFECT),
    )(*srcs, *lands, send_sems, recv_sems, after)
    return list(res[:n]), list(res[n:])


def _cols_to_blocks(g, *, name, tr=128):
    R, C = g.shape
    w = C // N_DEV
    tr = min(tr, R)

    def body(g_ref, o_ref):
        for p in range(N_DEV):
            o_ref[p] = g_ref[:, p * w:(p + 1) * w].astype(BF16)

    return pl.pallas_call(
        body, name=name, grid=(R // tr,),
        in_specs=[pl.BlockSpec((tr, C), lambda i: (i, 0))],
        out_specs=pl.BlockSpec((N_DEV, tr, w), lambda i: (0, i, 0)),
        out_shape=jax.ShapeDtypeStruct((N_DEV, R, w), BF16),
        compiler_params=_params(("parallel",)),
    )(g)


def _blocks_to_cols(b, *, name, tr=128):
    _, R, w = b.shape
    tr = min(tr, R)

    def body(b_ref, o_ref):
        o_ref[...] = jnp.concatenate([b_ref[p].astype(F32) for p in range(N_DEV)], axis=1).astype(o_ref.dtype)

    return pl.pallas_call(
        body, name=name, grid=(R // tr,),
        in_specs=[pl.BlockSpec((N_DEV, tr, w), lambda i: (0, i, 0))],
        out_specs=pl.BlockSpec((tr, N_DEV * w), lambda i: (i, 0)),
        out_shape=jax.ShapeDtypeStruct((R, N_DEV * w), b.dtype),
        compiler_params=_params(("parallel",)),
    )(b)


def _pack_rows(arrays, *, name, pick=None):
    B, _, w = arrays[0].shape
    rows = [a.shape[1] for a in arrays]
    first = 0
    if pick is not None:
        B, first = 1, pick

    def body(*refs):
        o_ref = refs[-1]
        r = 0
        for a_ref, n in zip(refs[:-1], rows):
            o_ref[0, r:r + n, :] = a_ref[0].astype(BF16)
            r += n

    return pl.pallas_call(
        body, name=name, grid=(B,),
        in_specs=[pl.BlockSpec((1, n, w), lambda b: (first + b, 0, 0)) for n in rows],
        out_specs=pl.BlockSpec((1, sum(rows), w), lambda b: (b, 0, 0)),
        out_shape=jax.ShapeDtypeStruct((B, sum(rows), w), BF16),
        compiler_params=_params(("parallel",)),
    )(*arrays)


def _unpack_rows(land, src, rows, *, name):
    _, R, w = land.shape
    src_spec = (pl.BlockSpec((1, R, w), lambda p: (p, 0, 0)) if src.ndim == 3
                else pl.BlockSpec((R, w), lambda p: (0, 0)))

    def body(land_ref, src_ref, *o_refs):
        me = 4 * lax.axis_index("x") + 2 * lax.axis_index("y") + lax.axis_index("c")
        mine = pl.program_id(0) == me
        r = 0
        for o_ref, n in zip(o_refs, rows):
            rows_i = slice(r, r + n)

            @pl.when(mine)
            def _(o_ref=o_ref, rows_i=rows_i):
                o_ref[0] = src_ref[0, rows_i, :] if src.ndim == 3 else src_ref[rows_i, :]

            @pl.when(jnp.logical_not(mine))
            def _(o_ref=o_ref, rows_i=rows_i):
                o_ref[0] = land_ref[0, rows_i, :]

            r += n

    return pl.pallas_call(
        body, name=name, grid=(N_DEV,),
        in_specs=[pl.BlockSpec((1, R, w), lambda p: (p, 0, 0)), src_spec],
        out_specs=[pl.BlockSpec((1, n, w), lambda p: (p, 0, 0)) for n in rows],
        out_shape=[jax.ShapeDtypeStruct((N_DEV, n, w), land.dtype) for n in rows],
        compiler_params=_params(("parallel",)),
    )(land, src)


def _to_blocks(w, axis):
    shape = w.shape
    k = shape[axis] // N_DEV
    w = w.reshape(shape[:axis] + (N_DEV, k) + shape[axis + 1:])
    return jnp.moveaxis(w, axis, 0)


def _from_blocks(wb, axis):
    w = jnp.moveaxis(wb, 0, axis)
    shape = w.shape
    return w.reshape(shape[:axis] + (shape[axis] * shape[axis + 1],) + shape[axis + 2:])


def _block_diag(w):
    n, k, _ = w.shape
    eye = jnp.eye(n, dtype=w.dtype)
    return (eye[:, None, :, None] * w[:, :, None, :]).reshape(n * k, n * k)


def _diag_blocks(wd):
    k = HEAD_DIM
    return jnp.stack([wd[h * k:(h + 1) * k, h * k:(h + 1) * k] for h in range(C_BLOCKS)])


def _pack(arrays):
    rows = []
    for a in arrays:
        flat = a.reshape(-1).astype(F32)
        pad = (-flat.shape[0]) % LANES
        rows.append(jnp.pad(flat, (0, pad)).reshape(-1, LANES))
    out = jnp.concatenate(rows, axis=0)
    return jnp.pad(out, ((0, (-out.shape[0]) % 8), (0, 0)))


def _unpack(packed, shapes):
    outs, r = [], 0
    for s in shapes:
        size = math.prod(s)
        nrows = -(-size // LANES)
        outs.append(packed[r:r + nrows].reshape(-1)[:size].reshape(s))
        r += nrows
    return outs


def _rope_tables(positions):
    inv = 1.0 / (ROPE_THETA ** (jnp.arange(0, HEAD_DIM, 2, dtype=F32) / HEAD_DIM))
    ang = positions.astype(F32)[:, None] * inv
    cos, sin = jnp.cos(ang), jnp.sin(ang)
    return jnp.tile(cos, (1, 4)), jnp.tile(jnp.concatenate([-sin, sin], axis=1), (1, 2))


def _layer_fwd(l, x, pos, cos_t, sin_t, W, before_mixer=None):
    tag = f"l{l}"
    saved = {'x0': x}
    x1, a1, u1 = _ffn_fwd(x, W['norm_ffn1'][l], W['ffn1_gate'][l], W['ffn1_up'][l], W['ffn1_down'][l],
                          name=f"ffn1_fwd_{tag}")
    if before_mixer is not None:
        before_mixer(l, x1)
    h = _rms_fwd(x1, W['norm_mix'][l], name=f"mixnorm_fwd_{tag}")
    proj = _mm(h, W['w_in'][l], 'nn', name=f"proj_{tag}", tm=512, tn=IN_COLS, tk=h.shape[1])
    qa, ka2, va2, qb, kb, vb, xc, gc = _split_rope(proj, cos_t, sin_t, name=f"split_{tag}")
    oa, lsea = _attn_fwd(qa, ka2, va2, A_MAX_DIST, name=f"attn_a_fwd_{tag}")
    obs, lsebs = [], []
    for bi, (window, d) in enumerate(B_BRANCHES):
        o, lse = _attn_fwd(qb[bi], kb[bi], vb[bi], window // d, name=f"attn_b{bi}_fwd_{tag}")
        obs.append(o)
        lsebs.append(lse)
    oc, hs = _rglru_fwd(xc, gc, pos, W['conv_w'][l], W['conv_b'][l], W['rg_w_r'][l], W['rg_b_r'][l],
                        W['rg_w_i'][l], W['rg_b_i'][l], W['rg_lambda'][l], name=f"rglru_fwd_{tag}")
    mix = _combine_fwd(oa, lsea, W['sinks'][l], obs, lsebs, oc, name=f"combine_fwd_{tag}")
    x2 = _mm(mix, W['w_out'][l], 'nn', name=f"outproj_{tag}", tm=512, tn=x.shape[1], tk=MIX_WIDTH, res=x1)
    x3, a2, u2 = _ffn_fwd(x2, W['norm_ffn2'][l], W['ffn2_gate'][l], W['ffn2_up'][l], W['ffn2_down'][l],
                          name=f"ffn2_fwd_{tag}")
    saved.update(a1=a1, u1=u1, x1=x1, h=h, qa=qa, ka2=ka2, va2=va2, qb=qb, kb=kb, vb=vb, xc=xc, gc=gc, oa=oa,
                 lsea=lsea, obs=obs, lsebs=lsebs, hs=hs, mix=mix, x2=x2, a2=a2, u2=u2)
    return x3, saved


def _ffn_grads(tag, which, x, g, dy, a, u, wg, wu, wd):
    T, D = x.shape
    F = wg.shape[1]
    dx, dg, n, act, da, du = _ffn_bwd(x, g, dy, a, u, wg, wu, wd, name=f"{which}_bwd_{tag}")
    fc = _ffn_chunk(F)
    d_gate = _mm(n, da, 'tn', name=f"{which}_dgate_{tag}", tm=D, tn=fc, tk=2048)
    d_up = _mm(n, du, 'tn', name=f"{which}_dup_{tag}", tm=D, tn=fc, tk=2048)
    d_down = _mm(act, dy, 'tn', name=f"{which}_ddown_{tag}", tm=fc, tn=D, tk=1024, alpha=0.5)
    return dx, dg, d_gate, d_up, d_down


def _layer_bwd(l, dx3, pos, cos_t, sin_t, W, S, on_grads=None):
    tag = f"l{l}"
    G = {}
    dx2, G['norm_ffn2'], G['ffn2_gate'], G['ffn2_up'], G['ffn2_down'] = _ffn_grads(
        tag, 'ffn2', S['x2'], W['norm_ffn2'][l], dx3, S['a2'], S['u2'], W['ffn2_gate'][l], W['ffn2_up'][l],
        W['ffn2_down'][l])
    D = dx2.shape[1]
    dmix = _mm(dx2, W['w_out'][l], 'nt', name=f"outproj_dx_{tag}", tm=512, tn=MIX_WIDTH, tk=D)
    G['w_out'] = _mm(S['mix'], dx2, 'tn', name=f"outproj_dw_{tag}", tm=MIX_WIDTH, tn=D, tk=2048)
    doa, dla, dobs, dlbs, doc, dsink = _combine_bwd(dmix, S['oa'], S['lsea'], W['sinks'][l], S['obs'], S['lsebs'],
                                                    name=f"combine_bwd_{tag}")
    G['attn_sinks'] = dsink.reshape(A_WIDTH // HEAD_DIM, HEAD_DIM)[:, 0]
    dqa, dka2, dva2 = _attn_bwd(S['qa'], S['ka2'], S['va2'], doa, S['lsea'], dla, A_MAX_DIST,
                                name=f"attn_a_bwd_{tag}")
    dqb, dkb, dvb = [], [], []
    for bi, (window, d) in enumerate(B_BRANCHES):
        dq, dk, dv = _attn_bwd(S['qb'][bi], S['kb'][bi], S['vb'][bi], dobs[bi], S['lsebs'][bi], dlbs[bi], window // d,
                               name=f"attn_b{bi}_bwd_{tag}")
        dqb.append(dq)
        dkb.append(dk)
        dvb.append(dv)
    (dxc, dgc, G['conv_w'], G['conv_b'], dwr, G['rg_b_r'], dwi, G['rg_b_i'], G['rg_lambda']) = _rglru_bwd(
        S['xc'], S['gc'], pos, S['hs'], doc, W['conv_w'][l], W['conv_b'][l], W['rg_w_r'][l], W['rg_b_r'][l],
        W['rg_w_i'][l], W['rg_b_i'][l], W['rg_lambda'][l], name=f"rglru_bwd_{tag}")
    G['rg_w_r'] = _diag_blocks(dwr)
    G['rg_w_i'] = _diag_blocks(dwi)
    dproj = _merge_dproj(dqa, dka2, dva2, dqb, dkb, dvb, dxc, dgc, cos_t, sin_t, name=f"merge_{tag}")
    dh = _mm(dproj, W['w_in'][l], 'nt', name=f"proj_dx_{tag}", tm=512, tn=D, tk=IN_COLS)
    G['w_in'] = _mm(S['h'], dproj, 'tn', name=f"proj_dw_{tag}", tm=D, tn=IN_COLS, tk=1024)
    g_mix = W['norm_mix'][l]
    if on_grads is not None:
        g_mix = g_mix + on_grads(l, 0, G)
    dx1, G['norm_mix'] = _rms_bwd(S['x1'], g_mix, dh, dx2, name=f"mixnorm_bwd_{tag}")
    dx0, G['norm_ffn1'], G['ffn1_gate'], G['ffn1_up'], G['ffn1_down'] = _ffn_grads(
        tag, 'ffn1', S['x0'], W['norm_ffn1'][l], dx1, S['a1'], S['u1'], W['ffn1_gate'][l], W['ffn1_up'][l],
        W['ffn1_down'][l])
    if on_grads is not None:
        on_grads(l, 1, G)
    return dx0, G


def _device_step(x, positions, loss_target, W, before_layer=None, on_grads=None, before_mixer=None):
    T = x.shape[0]
    pos = positions.reshape(T, 1)
    cos_t, sin_t = _rope_tables(positions)
    saved = []
    for l in range(DEPTH):
        if before_layer is not None:
            before_layer(l, x)
        x, S = _layer_fwd(l, x, pos, cos_t, sin_t, W, before_mixer)
        saved.append(S)
    loss, dx, dg_final = _loss_head(x, W['norm_final'], loss_target, name="loss_head")
    grads = [None] * DEPTH
    for l in reversed(range(DEPTH)):
        dx, grads[l] = _layer_bwd(l, dx, pos, cos_t, sin_t, W, saved[l], on_grads)
    return loss, dx, grads, dg_final


SHARD_AXIS = {'ffn1_gate': 2, 'ffn1_up': 2, 'ffn1_down': 1, 'w_in': 2, 'w_out': 1, 'ffn2_gate': 2, 'ffn2_up': 2,
              'ffn2_down': 1, 'conv_w': 2}


def kernel(x, positions, norm_ffn1, ffn1_gate, ffn1_up, ffn1_down, norm_mix, w_in, attn_sinks, conv_w, conv_b, rg_w_r, rg_b_r, rg_w_i, rg_b_i, rg_lambda, w_out, norm_ffn2, ffn2_gate, ffn2_up, ffn2_down, norm_final, loss_target, m_norm_ffn1, m_ffn1_gate, m_ffn1_up, m_ffn1_down, m_norm_mix, m_w_in, m_attn_sinks, m_conv_w, m_conv_b, m_rg_w_r, m_rg_b_r, m_rg_w_i, m_rg_b_i, m_rg_lambda, m_w_out, m_norm_ffn2, m_ffn2_gate, m_ffn2_up, m_ffn2_down, m_norm_final, v_norm_ffn1, v_ffn1_gate, v_ffn1_up, v_ffn1_down, v_norm_mix, v_w_in, v_attn_sinks, v_conv_w, v_conv_b, v_rg_w_r, v_rg_b_r, v_rg_w_i, v_rg_b_i, v_rg_lambda, v_w_out, v_norm_ffn2, v_ffn2_gate, v_ffn2_up, v_ffn2_down, v_norm_final):
    given = dict(locals())
    me = 4 * lax.axis_index("x") + 2 * lax.axis_index("y") + lax.axis_index("c")

    def by_width(names):
        classes = {}
        for n in names:
            classes.setdefault(given[n].shape[2], []).append(n)
        return list(classes.values())

    def pack(names, get, tag, pick=None):
        return [_pack_rows([get(n) for n in cls], name=f"pack{ci}_{tag}", pick=pick)
                for ci, cls in enumerate(by_width(names))]

    def unpack(names, lands, srcs, tag):
        out = {}
        for ci, cls in enumerate(by_width(names)):
            arrays = _unpack_rows(lands[ci], srcs[ci], [given[n].shape[1] for n in cls], name=f"unpack{ci}_{tag}")
            out.update(zip(cls, arrays))
        return out

    def gather_start(l, names, tag, after=None):
        return _exchange_start([p[0] for p in pack(names, lambda n: given[n], f"w{tag}_l{l}", pick=l)], True,
                               name=f"gather_start{tag}_l{l}", after=after)

    def gather_wait(l, names, tag, started, after):
        srcs, lands = _exchange_wait(started, after, True, name=f"gather_wait{tag}_l{l}")
        blocks = unpack(names, lands, srcs, f"w{tag}_l{l}")
        for n in names:
            if SHARD_AXIS[n] == 2:
                W[n][l] = _blocks_to_cols(blocks[n], name=f"cols_{n}_l{l}")
            else:
                W[n][l] = blocks[n].reshape(-1, blocks[n].shape[2])
        return lands[0]

    W = {n: [None] * DEPTH for n in BIG_NAMES}
    ffn1_names, later_names = SCATTER_STAGES[1], SCATTER_STAGES[0]
    first = gather_start(0, ffn1_names, "a")
    conv_full = _exchange([conv_w], True, name="gather_conv_w")[0]
    landed = gather_wait(0, ffn1_names, "a", first, conv_full)
    second = gather_start(0, later_names, "b", after=landed)
    started = second[4][0, 0]
    gathers = [None] * DEPTH

    def before_layer(l, x_in):
        if l > 0:
            gather_wait(l, BIG_NAMES, "", gathers[l], x_in)

    def before_mixer(l, x1):
        if l == 0:
            landed = gather_wait(0, later_names, "b", second, x1)
            token = 0.0
            for k in range(1, DEPTH):
                gathers[k] = gather_start(k, BIG_NAMES, "", after=landed)
                token = token + gathers[k][4][0, 0]
            W['norm_mix'][0] = W['norm_mix'][0] + token

    scatters = {}

    def on_grads(l, stage, G):
        def blocks_of(n):
            if SHARD_AXIS[n] == 2:
                return _cols_to_blocks(G[n], name=f"blocks_{n}_l{l}")
            return G[n].reshape(N_DEV, -1, G[n].shape[1])

        scatters[l, stage] = _exchange_start(pack(SCATTER_STAGES[stage], blocks_of, f"g{stage}_l{l}"), False,
                                             name=f"scatter_start{stage}_l{l}")
        token = scatters[l, stage][4][0, 0]
        if stage == 1 and l > 0:
            W['norm_ffn2'][l - 1] = W['norm_ffn2'][l - 1] + token
        return token

    W['conv_w'] = [_from_blocks(conv_full[:, l], 1) for l in range(DEPTH)]
    for n in ('norm_ffn1', 'norm_mix', 'norm_ffn2', 'conv_b', 'rg_lambda'):
        W[n] = [given[n][l][None, :] for l in range(DEPTH)]
    W['norm_final'] = norm_final[None, :]
    W['sinks'] = [jnp.repeat(attn_sinks[l], HEAD_DIM)[None, :] for l in range(DEPTH)]
    for n in ('rg_w_r', 'rg_w_i'):
        W[n] = [_block_diag(given[n][l]).astype(BF16) for l in range(DEPTH)]
    for n in ('rg_b_r', 'rg_b_i'):
        W[n] = [given[n][l].reshape(1, C_WIDTH) for l in range(DEPTH)]

    W['norm_ffn1'][0] = W['norm_ffn1'][0] + started

    loss_part, grad_x, grads, dg_final = _device_step(x[0], positions[0], loss_target[0], W, before_layer, on_grads,
                                                      before_mixer)
    loss = lax.psum(loss_part[0, 0], ("x", "y", "c"))

    small_shapes = [given[n].shape for n in SMALL_NAMES] + [(DEPTH, C_CONV, C_WIDTH)]
    small_grads = []
    for n in SMALL_NAMES:
        if n == 'norm_final':
            small_grads.append(dg_final.reshape(-1))
        else:
            small_grads.append(jnp.stack([grads[l][n].reshape(given[n].shape[1:]) for l in range(DEPTH)]))
    small_grads.append(jnp.stack([grads[l]['conv_w'] for l in range(DEPTH)]))
    small_parts = _exchange([_pack(small_grads)], True, name="gather_small_grads")[0]

    out = {}
    for stage in (0, 1):
        parts = {}
        for l in reversed(range(DEPTH)):
            last = stage == 1 and l == 0
            srcs, lands = _exchange_wait(scatters[l, stage], out['w_in'][1] if last else grad_x, False,
                                         name=f"scatter_wait{stage}_l{l}")
            parts[l] = unpack(SCATTER_STAGES[stage], lands, srcs, f"g{stage}_l{l}")
        for n in SCATTER_STAGES[stage]:
            shape = given[n].shape
            two_d = (shape[0] * shape[1], shape[2])
            res = None
            for l in reversed(range(DEPTH)):
                res = _adamw(parts[l][n], given[n].reshape(two_d), given['m_' + n].reshape(two_d),
                             given['v_' + n].reshape(two_d), name=f"adamw_{n}_l{l}", part=l, prev=res)
            out[n] = [r.reshape(shape) for r in res]

    w_small = [given[n] for n in SMALL_NAMES]
    m_small = [given['m_' + n] for n in SMALL_NAMES]
    v_small = [given['v_' + n] for n in SMALL_NAMES]
    zeros_cw = jnp.zeros((DEPTH, C_CONV, C_WIDTH), F32)
    res = _adamw(small_parts, _pack(w_small + [zeros_cw]), _pack(m_small + [zeros_cw]), _pack(v_small + [zeros_cw]),
                 name="adamw_small", tr=8)
    unpacked = [_unpack(r, small_shapes) for r in res]
    for i, n in enumerate(SMALL_NAMES):
        out[n] = [u[i] for u in unpacked]

    k = conv_w.shape[2]
    g_cw = lax.dynamic_slice_in_dim(unpacked[0][-1], me * k, k, axis=2)
    zero_parts = jnp.zeros((N_DEV - 1,) + (8, LANES), F32)
    res = _adamw(jnp.concatenate([_pack([g_cw])[None], zero_parts]), _pack([conv_w]), _pack([m_conv_w]),
                 _pack([v_conv_w]), name="adamw_conv_w", tr=8)
    out['conv_w'] = [_unpack(r, [conv_w.shape])[0] for r in res]

    outputs = [loss, grad_x[None]]
    for i in range(4):
        outputs += [out[n][i] for n in WEIGHT_NAMES]
    return tuple(outputs)
```

```python
import functools
import math

import jax
import jax.numpy as jnp
from jax import lax
from jax.experimental import pallas as pl
from jax.experimental.pallas import tpu as pltpu

F32 = jnp.float32
BF16 = jnp.bfloat16

N_DEV = 8
DEPTH = 4
HEAD_DIM = 64
LANES = 128
QBLK = 128
A_WIDTH = 256
A_KV_WIDTH = 128
B_WIDTH = 384
C_WIDTH = 384
C_BLOCKS = 6
C_CONV = 4
C_EXP = 8.0
MIX_WIDTH = A_WIDTH + B_WIDTH + C_WIDTH
IN_COLS = A_WIDTH + 2 * A_KV_WIDTH + 3 * B_WIDTH + 2 * C_WIDTH
A_MAX_DIST = 127
B_BRANCHES = ((128, 1), (512, 4), (2048, 16))
ROPE_THETA = 10000.0
EPS = 1e-6
SCALE = HEAD_DIM ** -0.5

ADAM_LR = 0.001
ADAM_B1 = 0.9
ADAM_B2 = 0.999
ADAM_EPS = 1e-08
ADAM_WD = 0.01
ADAM_STEP = 10

ATTN_CHUNK = 1024
ATTN_FWD_UNROLL = True
ATTN_BWD_UNROLL = True
VMEM_LIMIT = 56 * 1024 * 1024

NT_DIMS = (((1,), (1,)), ((), ()))
TN_DIMS = (((0,), (0,)), ((), ()))
NN_DIMS = (((1,), (0,)), ((), ()))

WEIGHT_NAMES = ['norm_ffn1', 'ffn1_gate', 'ffn1_up', 'ffn1_down', 'norm_mix', 'w_in', 'attn_sinks', 'conv_w',
                'conv_b', 'rg_w_r', 'rg_b_r', 'rg_w_i', 'rg_b_i', 'rg_lambda', 'w_out', 'norm_ffn2', 'ffn2_gate',
                'ffn2_up', 'ffn2_down', 'norm_final']
BIG_NAMES = ['ffn1_gate', 'ffn1_up', 'ffn1_down', 'w_in', 'w_out', 'ffn2_gate', 'ffn2_up', 'ffn2_down']
SCATTER_STAGES = (['ffn2_gate', 'ffn2_up', 'ffn2_down', 'w_out', 'w_in'], ['ffn1_gate', 'ffn1_up', 'ffn1_down'])
SMALL_NAMES = ['norm_ffn1', 'norm_mix', 'norm_ffn2', 'norm_final', 'attn_sinks', 'conv_b', 'rg_w_r', 'rg_b_r',
               'rg_w_i', 'rg_b_i', 'rg_lambda']


def _params(sem, vmem=VMEM_LIMIT):
    return pltpu.CompilerParams(dimension_semantics=sem, vmem_limit_bytes=vmem)


def _dot(a, b, dims=NN_DIMS):
    return lax.dot_general(a, b, dims, preferred_element_type=F32)


def _sigmoid(x):
    return 1.0 / (1.0 + jnp.exp(-x))


def _mm(a, b, mode, *, name, tm=512, tn=512, tk=512, out_dtype=F32, alpha=1.0, res=None):
    if mode == 'nn':
        (M, K), N = a.shape, b.shape[1]
    elif mode == 'nt':
        (M, K), N = a.shape, b.shape[0]
    else:
        (K, M), N = a.shape, b.shape[1]
    tm, tn, tk = min(tm, M), min(tn, N), min(tk, K)
    ni, nj, nk = M // tm, N // tn, K // tk
    assert ni * tm == M and nj * tn == N and nk * tk == K, (name, a.shape, b.shape, tm, tn, tk)
    if mode == 'tn':
        a_spec = pl.BlockSpec((tk, tm), lambda j, i, k: (k, i))
    else:
        a_spec = pl.BlockSpec((tm, tk), lambda j, i, k: (i, k))
    if mode == 'nt':
        b_spec = pl.BlockSpec((tn, tk), lambda j, i, k: (j, k))
    else:
        b_spec = pl.BlockSpec((tk, tn), lambda j, i, k: (k, j))
    dims = {'nn': NN_DIMS, 'nt': NT_DIMS, 'tn': TN_DIMS}[mode]
    o_spec = pl.BlockSpec((tm, tn), lambda j, i, k: (i, j))
    has_res = res is not None

    def body(*refs):
        if has_res:
            a_ref, b_ref, r_ref, o_ref = refs[:4]
        else:
            a_ref, b_ref, o_ref = refs[:3]
        part = _dot(a_ref[...].astype(BF16), b_ref[...].astype(BF16), dims)

        def finish(acc):
            out = acc * alpha if alpha != 1.0 else acc
            if has_res:
                out = r_ref[...] + out
            o_ref[...] = out.astype(out_dtype)

        if nk == 1:
            finish(part)
        else:
            acc_ref = refs[-1]
            k = pl.program_id(2)

            @pl.when(k == 0)
            def _():
                acc_ref[...] = part

            @pl.when(k > 0)
            def _():
                acc_ref[...] += part

            @pl.when(k == nk - 1)
            def _():
                finish(acc_ref[...])

    in_specs = [a_spec, b_spec] + ([o_spec] if has_res else [])
    operands = [a, b] + ([res] if has_res else [])
    return pl.pallas_call(
        body, name=name, grid=(nj, ni, nk), in_specs=in_specs, out_specs=o_spec,
        out_shape=jax.ShapeDtypeStruct((M, N), out_dtype),
        scratch_shapes=[pltpu.VMEM((tm, tn), F32)] if nk > 1 else [],
        compiler_params=_params(("parallel", "parallel", "arbitrary")),
    )(*operands)


def _rms_fwd(x, g, *, name, tm=512):
    T, D = x.shape
    tm = min(tm, T)

    def body(x_ref, g_ref, o_ref):
        xv = x_ref[...]
        rstd = lax.rsqrt(jnp.mean(xv * xv, axis=-1, keepdims=True) + EPS)
        o_ref[...] = (xv * rstd * g_ref[...]).astype(BF16)

    return pl.pallas_call(
        body, name=name, grid=(T // tm,),
        in_specs=[pl.BlockSpec((tm, D), lambda i: (i, 0)), pl.BlockSpec((1, D), lambda i: (0, 0))],
        out_specs=pl.BlockSpec((tm, D), lambda i: (i, 0)),
        out_shape=jax.ShapeDtypeStruct((T, D), BF16),
        compiler_params=_params(("parallel",)),
    )(x, g)


def _rms_bwd_math(xv, g, dn):
    rstd = lax.rsqrt(jnp.mean(xv * xv, axis=-1, keepdims=True) + EPS)
    xhat = xv * rstd
    dxhat = dn * g
    dx = rstd * (dxhat - xhat * jnp.mean(dxhat * xhat, axis=-1, keepdims=True))
    return dx, jnp.sum(dn * xhat, axis=0, keepdims=True)


def _rms_bwd(x, g, dn, dres, *, name, tm=512):
    T, D = x.shape
    tm = min(tm, T)

    def body(x_ref, g_ref, dn_ref, dres_ref, dx_ref, dg_ref):
        dx, dg = _rms_bwd_math(x_ref[...], g_ref[...], dn_ref[...])
        dx_ref[...] = dres_ref[...] + dx

        @pl.when(pl.program_id(0) == 0)
        def _():
            dg_ref[...] = jnp.zeros_like(dg_ref)

        dg_ref[...] += dg

    row = pl.BlockSpec((tm, D), lambda i: (i, 0))
    vec = pl.BlockSpec((1, D), lambda i: (0, 0))
    return pl.pallas_call(
        body, name=name, grid=(T // tm,),
        in_specs=[row, vec, row, row], out_specs=[row, vec],
        out_shape=[jax.ShapeDtypeStruct((T, D), F32), jax.ShapeDtypeStruct((1, D), F32)],
        compiler_params=_params(("arbitrary",)),
    )(x, g, dn, dres)


def _loss_head(x, g, target, *, name, tm=512):
    T, D = x.shape
    tm = min(tm, T)

    def body(x_ref, g_ref, t_ref, loss_ref, dx_ref, dg_ref):
        xv = x_ref[...]
        g = g_ref[...]
        rstd = lax.rsqrt(jnp.mean(xv * xv, axis=-1, keepdims=True) + EPS)
        y = xv * rstd * g
        err = y - t_ref[...]
        part = 0.5 * jnp.sum(jnp.mean(err * err, axis=-1, keepdims=True), axis=0, keepdims=True)
        dx, dg = _rms_bwd_math(xv, g, err * (1.0 / D))
        dx_ref[...] = dx

        @pl.when(pl.program_id(0) == 0)
        def _():
            dg_ref[...] = jnp.zeros_like(dg_ref)
            loss_ref[...] = jnp.zeros_like(loss_ref)

        dg_ref[...] += dg
        loss_ref[...] += jnp.broadcast_to(part, loss_ref.shape)

    row = pl.BlockSpec((tm, D), lambda i: (i, 0))
    vec = pl.BlockSpec((1, D), lambda i: (0, 0))
    lspec = pl.BlockSpec((1, LANES), lambda i: (0, 0))
    return pl.pallas_call(
        body, name=name, grid=(T // tm,),
        in_specs=[row, vec, row], out_specs=[lspec, row, vec],
        out_shape=[jax.ShapeDtypeStruct((1, LANES), F32), jax.ShapeDtypeStruct((T, D), F32),
                   jax.ShapeDtypeStruct((1, D), F32)],
        compiler_params=_params(("arbitrary",)),
    )(x, g, target)


def _resident(shape):
    return pl.BlockSpec(shape, lambda i: (0,) * len(shape), pipeline_mode=pl.Buffered(1))


def _ffn_chunk(F):
    for c in (1408, 1024, 512, 256, 128):
        if F % c == 0:
            return c
    return F


def _ffn_fwd(x, g, wg, wu, wd, *, name, tm=256):
    T, D = x.shape
    F = wg.shape[1]
    tm = min(tm, T)
    fc = _ffn_chunk(F)

    def body(x_ref, g_ref, wg_ref, wu_ref, wd_ref, o_ref, a_ref, u_ref):
        xv = x_ref[...]
        rstd = lax.rsqrt(jnp.mean(xv * xv, axis=-1, keepdims=True) + EPS)
        n = (xv * rstd * g_ref[...]).astype(BF16)
        acc = jnp.zeros((tm, D), F32)
        for c in range(F // fc):
            sl = slice(c * fc, (c + 1) * fc)
            a = _dot(n, wg_ref[:, sl])
            u = _dot(n, wu_ref[:, sl])
            a_ref[:, sl] = a.astype(BF16)
            u_ref[:, sl] = u.astype(BF16)
            act = (a * _sigmoid(a) * u).astype(BF16)
            acc = acc + _dot(act, wd_ref[sl, :])
        o_ref[...] = xv + 0.5 * acc

    row = pl.BlockSpec((tm, D), lambda i: (i, 0))
    hid = pl.BlockSpec((tm, F), lambda i: (i, 0))
    return pl.pallas_call(
        body, name=name, grid=(T // tm,),
        in_specs=[row, pl.BlockSpec((1, D), lambda i: (0, 0)),
                  _resident((D, F)), _resident((D, F)), _resident((F, D))],
        out_specs=[row, hid, hid],
        out_shape=[jax.ShapeDtypeStruct((T, D), F32), jax.ShapeDtypeStruct((T, F), BF16),
                   jax.ShapeDtypeStruct((T, F), BF16)],
        compiler_params=_params(("parallel",)),
    )(x, g, wg, wu, wd)


def _ffn_bwd(x, g, dy, a, u, wg, wu, wd, *, name, tm=256):
    T, D = x.shape
    F = wg.shape[1]
    tm = min(tm, T)
    fc = _ffn_chunk(F)

    def body(x_ref, g_ref, dy_ref, a_ref, u_ref, wg_ref, wu_ref, wd_ref,
             dx_ref, dg_ref, n_ref, act_ref, da_ref, du_ref):
        xv = x_ref[...]
        g = g_ref[...]
        rstd = lax.rsqrt(jnp.mean(xv * xv, axis=-1, keepdims=True) + EPS)
        n_ref[...] = (xv * rstd * g).astype(BF16)
        dy = dy_ref[...]
        dyh = (0.5 * dy).astype(BF16)
        dn = jnp.zeros((tm, D), F32)
        for c in range(F // fc):
            sl = slice(c * fc, (c + 1) * fc)
            av = a_ref[:, sl].astype(F32)
            uv = u_ref[:, sl].astype(F32)
            dact = _dot(dyh, wd_ref[sl, :], NT_DIMS)
            s = _sigmoid(av)
            silu = av * s
            act_ref[:, sl] = (silu * uv).astype(BF16)
            da = (dact * uv * (s * (1.0 + av * (1.0 - s)))).astype(BF16)
            du = (dact * silu).astype(BF16)
            da_ref[:, sl] = da
            du_ref[:, sl] = du
            dn = dn + _dot(da, wg_ref[:, sl], NT_DIMS) + _dot(du, wu_ref[:, sl], NT_DIMS)
        dx, dg = _rms_bwd_math(xv, g, dn)
        dx_ref[...] = dy + dx

        @pl.when(pl.program_id(0) == 0)
        def _():
            dg_ref[...] = jnp.zeros_like(dg_ref)

        dg_ref[...] += dg

    row = pl.BlockSpec((tm, D), lambda i: (i, 0))
    hid = pl.BlockSpec((tm, F), lambda i: (i, 0))
    vec = pl.BlockSpec((1, D), lambda i: (0, 0))
    return pl.pallas_call(
        body, name=name, grid=(T // tm,),
        in_specs=[row, vec, row, hid, hid,
                  _resident((D, F)), _resident((D, F)), _resident((F, D))],
        out_specs=[row, vec, row, hid, hid, hid],
        out_shape=[jax.ShapeDtypeStruct((T, D), F32), jax.ShapeDtypeStruct((1, D), F32),
                   jax.ShapeDtypeStruct((T, D), BF16), jax.ShapeDtypeStruct((T, F), BF16),
                   jax.ShapeDtypeStruct((T, F), BF16), jax.ShapeDtypeStruct((T, F), BF16)],
        compiler_params=_params(("arbitrary",)),
    )(x, g, dy, a, u, wg, wu, wd)


def _lane_iota(shape):
    return lax.broadcasted_iota(jnp.int32, shape, 1)


def _rope_partner(x):
    first_half = (_lane_iota(x.shape) & (HEAD_DIM - 1)) < HEAD_DIM // 2
    return jnp.where(first_half, pltpu.roll(x, LANES - HEAD_DIM // 2, 1), pltpu.roll(x, HEAD_DIM // 2, 1))


def _swap_heads(x):
    return pltpu.roll(x, HEAD_DIM, 1)


def _undilate(blk_ref, d, stage):
    if d == 1:
        return blk_ref[...]
    n, width = blk_ref.shape
    W = width // d
    for r in range(d):
        for g in range(W // LANES):
            stage.at[g][pl.ds(r, n, stride=d), :] = blk_ref[:, r * W + g * LANES:r * W + (g + 1) * LANES]
    return jnp.concatenate([stage.at[g][...] for g in range(W // LANES)], axis=1)


def _dilate_into(out_ref, value, d, stage):
    if d == 1:
        out_ref[...] = value.astype(out_ref.dtype)
        return
    n = value.shape[0] // d
    W = value.shape[1]
    for g in range(W // LANES):
        stage.at[g][...] = value[:, g * LANES:(g + 1) * LANES]
    for r in range(d):
        for g in range(W // LANES):
            out_ref[:, r * W + g * LANES:r * W + (g + 1) * LANES] = (
                stage.at[g][pl.ds(r, n, stride=d), :].astype(out_ref.dtype))


def _dilated_spec(tm, d, W):
    return pl.BlockSpec((tm // d, d * W), lambda i: (i, 0))


def _stage(tm, W):
    return pltpu.VMEM((W // LANES, tm, LANES), F32)


DILATIONS = tuple(d for _, d in B_BRANCHES)


def _split_rope(proj, cos_t, sin_t, *, name, tm=256):
    T = proj.shape[0]
    tm = min(tm, T)
    nd = len(DILATIONS)

    def body(p_ref, c_ref, s_ref, qa_ref, ka_ref, va_ref, *rest):
        b_refs = rest[:3 * nd]
        xc_ref, gc_ref, stage = rest[3 * nd:]
        cos = c_ref[...]
        sin = s_ref[...]

        def rope(x):
            return x * cos + _rope_partner(x) * sin

        lo = _lane_iota((tm, LANES)) < HEAD_DIM
        col = 0
        for j in range(A_WIDTH // LANES):
            qa_ref[:, j * LANES:(j + 1) * LANES] = (rope(p_ref[:, col:col + LANES]) * SCALE).astype(BF16)
            col += LANES
        kr = rope(p_ref[:, col:col + LANES])
        col += LANES
        vr = p_ref[:, col:col + LANES]
        col += LANES
        for src, dst in ((kr, ka_ref), (vr, va_ref)):
            sw = _swap_heads(src)
            dst[:, 0:LANES] = jnp.where(lo, src, sw).astype(BF16)
            dst[:, LANES:2 * LANES] = jnp.where(lo, sw, src).astype(BF16)
        for which, (roped, scale) in enumerate(((True, SCALE), (True, 1.0), (False, 1.0))):
            parts = []
            for j in range(B_WIDTH // LANES):
                v = p_ref[:, col:col + LANES]
                parts.append(rope(v) * scale if roped else v)
                col += LANES
            value = jnp.concatenate(parts, axis=1)
            for di, d in enumerate(DILATIONS):
                _dilate_into(b_refs[which * nd + di], value, d, stage)
        xc_ref[...] = p_ref[:, col:col + C_WIDTH]
        gc_ref[...] = p_ref[:, col + C_WIDTH:col + 2 * C_WIDTH]

    def row(w):
        return pl.BlockSpec((tm, w), lambda i: (i, 0))

    out_specs = [row(A_WIDTH)] * 3 + [_dilated_spec(tm, d, B_WIDTH) for _ in range(3) for d in DILATIONS]
    out_specs += [row(C_WIDTH)] * 2
    out_shape = [jax.ShapeDtypeStruct((T, A_WIDTH), BF16)] * 3
    out_shape += [jax.ShapeDtypeStruct((T // d, d * B_WIDTH), BF16) for _ in range(3) for d in DILATIONS]
    out_shape += [jax.ShapeDtypeStruct((T, C_WIDTH), F32)] * 2
    res = pl.pallas_call(
        body, name=name, grid=(T // tm,),
        in_specs=[row(IN_COLS), row(LANES), row(LANES)], out_specs=out_specs, out_shape=out_shape,
        scratch_shapes=[_stage(tm, B_WIDTH)],
        compiler_params=_params(("parallel",)),
    )(proj, cos_t, sin_t)
    qa, ka2, va2 = res[:3]
    qb, kb, vb = (list(res[3 + i * nd:3 + (i + 1) * nd]) for i in range(3))
    return qa, ka2, va2, qb, kb, vb, res[-2], res[-1]


def _merge_dproj(dqa, dka2, dva2, dqb, dkb, dvb, dxc, dgc, cos_t, sin_t, *, name, tm=256):
    T = dqa.shape[0]
    tm = min(tm, T)
    nb = len(dqb)

    def body(*refs):
        dqa_ref, dka_ref, dva_ref = refs[:3]
        dqb_refs = refs[3:3 + nb]
        dkb_refs = refs[3 + nb:3 + 2 * nb]
        dvb_refs = refs[3 + 2 * nb:3 + 3 * nb]
        dxc_ref, dgc_ref, c_ref, s_ref, o_ref, stage = refs[3 + 3 * nb:]
        cos = c_ref[...]
        sin = s_ref[...]

        def rope_t(dy):
            return dy * cos - _rope_partner(dy) * sin

        lo = _lane_iota((tm, LANES)) < HEAD_DIM
        col = 0
        for j in range(A_WIDTH // LANES):
            o_ref[:, col:col + LANES] = (rope_t(dqa_ref[:, j * LANES:(j + 1) * LANES]) * SCALE).astype(BF16)
            col += LANES
        for src, roped in ((dka_ref, True), (dva_ref, False)):
            b0 = src[:, 0:LANES]
            b1 = src[:, LANES:2 * LANES]
            v = jnp.where(lo, b0 + _swap_heads(b0), b1 + _swap_heads(b1))
            if roped:
                v = rope_t(v)
            o_ref[:, col:col + LANES] = v.astype(BF16)
            col += LANES
        for group, roped, scale in ((dqb_refs, True, SCALE), (dkb_refs, True, 1.0), (dvb_refs, False, 1.0)):
            total = _undilate(group[0], DILATIONS[0], stage)
            for r, d in zip(group[1:], DILATIONS[1:]):
                total = total + _undilate(r, d, stage)
            for j in range(B_WIDTH // LANES):
                v = total[:, j * LANES:(j + 1) * LANES]
                if roped:
                    v = rope_t(v) * scale
                o_ref[:, col:col + LANES] = v.astype(BF16)
                col += LANES
        o_ref[:, col:col + C_WIDTH] = dxc_ref[...].astype(BF16)
        o_ref[:, col + C_WIDTH:col + 2 * C_WIDTH] = dgc_ref[...].astype(BF16)

    def row(w):
        return pl.BlockSpec((tm, w), lambda i: (i, 0))

    ins = [dqa, dka2, dva2, *dqb, *dkb, *dvb, dxc, dgc, cos_t, sin_t]
    in_specs = [row(A_WIDTH)] * 3 + [_dilated_spec(tm, d, B_WIDTH) for _ in range(3) for d in DILATIONS]
    in_specs += [row(C_WIDTH)] * 2 + [row(LANES)] * 2
    return pl.pallas_call(
        body, name=name, grid=(T // tm,), in_specs=in_specs,
        out_specs=row(IN_COLS),
        out_shape=jax.ShapeDtypeStruct((T, IN_COLS), BF16),
        scratch_shapes=[_stage(tm, B_WIDTH)],
        compiler_params=_params(("parallel",)),
    )(*ins)


def _band_masks(max_dist):
    row = lax.broadcasted_iota(jnp.int32, (QBLK, 2 * QBLK), 0)
    key = lax.broadcasted_iota(jnp.int32, (QBLK, 2 * QBLK), 1)
    dist = row + QBLK - key
    wide = jnp.logical_and(dist >= 0, dist <= max_dist)
    return wide, wide[:, :QBLK], key >= QBLK


def _head_masks(rows=QBLK):
    lo = _lane_iota((rows, LANES)) < HEAD_DIM
    return lo, jnp.logical_not(lo)


def _keep(hm, x):
    return x * jnp.where(hm, 1.0, 0.0).astype(x.dtype)


def _head_col(x, hm):
    return jnp.max(jnp.where(hm, x, -jnp.inf), axis=1, keepdims=True)


def _attn_specs(R, C):
    chunk = min(ATTN_CHUNK, R)
    nb = chunk // QBLK
    nch = R // chunk
    main = pl.BlockSpec((chunk, LANES), lambda j, c: (c, j))
    prev = pl.BlockSpec((QBLK, LANES), lambda j, c: (jnp.maximum(c * nb - 1, 0), j))
    nxt = pl.BlockSpec((QBLK, LANES), lambda j, c: (jnp.minimum((c + 1) * nb, R // QBLK - 1), j))
    return chunk, nb, nch, main, prev, nxt


def _attn_fwd(q, k, v, max_dist, *, name):
    R, C = q.shape
    chunk, nb, nch, main, prev, _ = _attn_specs(R, C)

    def body(q_ref, k_ref, v_ref, kp_ref, vp_ref, o_ref, lse_ref):
        c = pl.program_id(1)
        wide_mask, _, own_block = _band_masks(max_dist)
        heads = _head_masks()

        def block(q_blk, kk, vv, mask):
            q2 = jnp.concatenate([_keep(hm, q_blk) for hm in heads], axis=0)
            s = jnp.where(jnp.concatenate([mask, mask], axis=0), _dot(q2, kk, NT_DIMS), -jnp.inf)
            m = jnp.max(jnp.maximum(s[:, :QBLK], s[:, QBLK:]), axis=1, keepdims=True)
            p = jnp.exp(s - m)
            l = jnp.sum(p[:, :QBLK] + p[:, QBLK:], axis=1, keepdims=True)
            o2 = _dot(p.astype(BF16), vv) / l
            lse2 = jnp.broadcast_to(m + jnp.log(l), (2 * QBLK, LANES))
            return (jnp.where(heads[0], o2[:QBLK], o2[QBLK:]), jnp.where(heads[0], lse2[:QBLK], lse2[QBLK:]))

        first = pl.ds(0, QBLK)
        o0, l0 = block(q_ref[first, :], jnp.concatenate([kp_ref[...], k_ref[first, :]], axis=0),
                       jnp.concatenate([vp_ref[...], v_ref[first, :]], axis=0),
                       jnp.logical_and(wide_mask, jnp.logical_or(own_block, c > 0)))
        o_ref[first, :] = o0
        lse_ref[first, :] = l0

        def loop(qb, carry):
            cur = pl.ds(pl.multiple_of(qb * QBLK, QBLK), QBLK)
            both = pl.ds(pl.multiple_of((qb - 1) * QBLK, QBLK), 2 * QBLK)
            o, l = block(q_ref[cur, :], k_ref[both, :], v_ref[both, :], wide_mask)
            o_ref[cur, :] = o
            lse_ref[cur, :] = l
            return carry

        if nb > 1:
            lax.fori_loop(1, nb, loop, 0, unroll=ATTN_FWD_UNROLL)

    return pl.pallas_call(
        body, name=name, grid=(C // LANES, nch),
        in_specs=[main, main, main, prev, prev], out_specs=[main, main],
        out_shape=[jax.ShapeDtypeStruct((R, C), F32), jax.ShapeDtypeStruct((R, C), F32)],
        compiler_params=_params(("parallel", "parallel")),
    )(q, k, v, k, v)


def _attn_bwd(q, k, v, do, lse, delta, max_dist, *, name):
    R, C = q.shape
    chunk, nb, nch, main, prev, nxt = _attn_specs(R, C)

    def body(q_ref, k_ref, v_ref, do_ref, lse_ref, dl_ref, kp_ref, vp_ref, qn_ref, don_ref, lsen_ref, dln_ref,
             dq_ref, dk_ref, dv_ref):
        c = pl.program_id(1)
        wide_mask, prev_mask, own_block = _band_masks(max_dist)
        heads = _head_masks()

        def pair(q_blk, do_blk, lse_blk, dl_blk, kk, vv, mask, want_dq=True):
            q2 = jnp.concatenate([_keep(hm, q_blk) for hm in heads], axis=0)
            do2 = jnp.concatenate([_keep(hm, do_blk) for hm in heads], axis=0)
            lse2 = jnp.concatenate([_head_col(lse_blk, hm) for hm in heads], axis=0)
            dl2 = jnp.concatenate([_head_col(dl_blk, hm) for hm in heads], axis=0)
            p = jnp.where(jnp.concatenate([mask, mask], axis=0), jnp.exp(_dot(q2, kk, NT_DIMS) - lse2), 0.0)
            ds = (p * (_dot(do2, vv, NT_DIMS) - dl2)).astype(BF16)
            dq = None
            if want_dq:
                k2 = jnp.concatenate([_keep(khm, kk) for khm in _head_masks(kk.shape[0])], axis=0)
                dq = _dot(jnp.concatenate([ds[:QBLK], ds[QBLK:]], axis=1), k2)
            return dq, _dot(ds, q2, TN_DIMS), _dot(p.astype(BF16), do2, TN_DIMS)

        dk_ref[...] = jnp.zeros_like(dk_ref)
        dv_ref[...] = jnp.zeros_like(dv_ref)

        first = pl.ds(0, QBLK)
        dq0, dkk0, dvv0 = pair(q_ref[first, :], do_ref[first, :], lse_ref[first, :], dl_ref[first, :],
                               jnp.concatenate([kp_ref[...], k_ref[first, :]], axis=0),
                               jnp.concatenate([vp_ref[...], v_ref[first, :]], axis=0),
                               jnp.logical_and(wide_mask, jnp.logical_or(own_block, c > 0)))
        dq_ref[first, :] = dq0
        dk_ref[first, :] += dkk0[QBLK:, :]
        dv_ref[first, :] += dvv0[QBLK:, :]

        def loop(qb, carry):
            cur = pl.ds(pl.multiple_of(qb * QBLK, QBLK), QBLK)
            both = pl.ds(pl.multiple_of((qb - 1) * QBLK, QBLK), 2 * QBLK)
            dq, dkk, dvv = pair(q_ref[cur, :], do_ref[cur, :], lse_ref[cur, :], dl_ref[cur, :],
                                k_ref[both, :], v_ref[both, :], wide_mask)
            dq_ref[cur, :] = dq
            dk_ref[both, :] += dkk
            dv_ref[both, :] += dvv
            return carry

        if nb > 1:
            lax.fori_loop(1, nb, loop, 0, unroll=ATTN_BWD_UNROLL)

        last = pl.ds((nb - 1) * QBLK, QBLK)
        _, dk_n, dv_n = pair(qn_ref[...], don_ref[...], lsen_ref[...], dln_ref[...], k_ref[last, :], v_ref[last, :],
                             jnp.logical_and(prev_mask, c < nch - 1), want_dq=False)
        dk_ref[last, :] += dk_n
        dv_ref[last, :] += dv_n

    return pl.pallas_call(
        body, name=name, grid=(C // LANES, nch),
        in_specs=[main] * 6 + [prev, prev] + [nxt] * 4, out_specs=[main, main, main],
        out_shape=[jax.ShapeDtypeStruct((R, C), F32)] * 3,
        compiler_params=_params(("parallel", "parallel")),
    )(q, k, v, do, lse, delta, k, v, q, do, lse, delta)


def _head_sum(x):
    r = lax.broadcasted_iota(jnp.int32, (LANES, LANES), 0) // HEAD_DIM
    c = lax.broadcasted_iota(jnp.int32, (LANES, LANES), 1) // HEAD_DIM
    ones = jnp.where(r == c, 1.0, 0.0).astype(BF16)
    outs = []
    for j in range(x.shape[1] // LANES):
        rem = x[:, j * LANES:(j + 1) * LANES]
        acc = jnp.zeros(rem.shape, F32)
        for _ in range(3):
            part = rem.astype(BF16)
            acc = acc + _dot(part, ones)
            rem = rem - part.astype(F32)
        outs.append(acc)
    return outs[0] if len(outs) == 1 else jnp.concatenate(outs, axis=1)


def _branch_weights(lses):
    m = functools.reduce(jnp.maximum, lses)
    es = [jnp.exp(l - m) for l in lses]
    den = functools.reduce(lambda a, b: a + b, es)
    return [e / den for e in es]


def _combine_fwd(oa, lsea, sink, obs, lsebs, oc, *, name, tm=256):
    T = oa.shape[0]
    tm = min(tm, T)
    nb = len(obs)

    def body(*refs):
        oa_ref, lsea_ref, sink_ref = refs[:3]
        ob_refs = refs[3:3 + nb]
        lse_refs = refs[3 + nb:3 + 2 * nb]
        oc_ref, out_ref, stage = refs[3 + 2 * nb:]
        out_ref[:, 0:A_WIDTH] = (oa_ref[...] * _sigmoid(lsea_ref[...] - sink_ref[...])).astype(BF16)
        ws = _branch_weights([_undilate(r, d, stage) for r, d in zip(lse_refs, DILATIONS)])
        ob = _undilate(ob_refs[0], DILATIONS[0], stage) * ws[0]
        for r, d, w in zip(ob_refs[1:], DILATIONS[1:], ws[1:]):
            ob = ob + _undilate(r, d, stage) * w
        out_ref[:, A_WIDTH:A_WIDTH + B_WIDTH] = ob.astype(BF16)
        out_ref[:, A_WIDTH + B_WIDTH:MIX_WIDTH] = oc_ref[...].astype(BF16)

    def row(w):
        return pl.BlockSpec((tm, w), lambda i: (i, 0))

    ins = [oa, lsea, sink, *obs, *lsebs, oc]
    in_specs = [row(A_WIDTH), row(A_WIDTH), pl.BlockSpec((1, A_WIDTH), lambda i: (0, 0))]
    in_specs += [_dilated_spec(tm, d, B_WIDTH) for _ in range(2) for d in DILATIONS] + [row(C_WIDTH)]
    return pl.pallas_call(
        body, name=name, grid=(T // tm,), in_specs=in_specs, out_specs=row(MIX_WIDTH),
        out_shape=jax.ShapeDtypeStruct((T, MIX_WIDTH), BF16),
        scratch_shapes=[_stage(tm, B_WIDTH)],
        compiler_params=_params(("parallel",)),
    )(*ins)


def _combine_bwd(dmix, oa, lsea, sink, obs, lsebs, *, name, tm=256):
    T = oa.shape[0]
    tm = min(tm, T)
    nb = len(obs)

    def body(*refs):
        dmix_ref, oa_ref, lsea_ref, sink_ref = refs[:4]
        ob_refs = refs[4:4 + nb]
        lse_refs = refs[4 + nb:4 + 2 * nb]
        outs = refs[4 + 2 * nb:-1]
        stage = refs[-1]
        doa_ref, dla_ref = outs[:2]
        dob_refs = outs[2:2 + nb]
        dlb_refs = outs[2 + nb:2 + 2 * nb]
        doc_ref, dsink_ref = outs[2 + 2 * nb:]

        d_a = dmix_ref[:, 0:A_WIDTH]
        d_b = dmix_ref[:, A_WIDTH:A_WIDTH + B_WIDTH]
        doc_ref[...] = dmix_ref[:, A_WIDTH + B_WIDTH:MIX_WIDTH]

        gate = _sigmoid(lsea_ref[...] - sink_ref[...])
        doa_ref[...] = (d_a * gate).astype(BF16)
        dgate = _head_sum(d_a * oa_ref[...])
        dlse = dgate * gate * (1.0 - gate)
        dla_ref[...] = dgate * gate - dlse

        @pl.when(pl.program_id(0) == 0)
        def _():
            dsink_ref[...] = jnp.zeros_like(dsink_ref)

        dsink_ref[...] -= jnp.sum(dlse, axis=0, keepdims=True)

        ws = _branch_weights([_undilate(r, d, stage) for r, d in zip(lse_refs, DILATIONS)])
        dws = [_head_sum(d_b * _undilate(r, d, stage)) for r, d in zip(ob_refs, DILATIONS)]
        sw = ws[0] * dws[0]
        for w, dw in zip(ws[1:], dws[1:]):
            sw = sw + w * dw
        for w, d, do_ref, dl_ref in zip(ws, DILATIONS, dob_refs, dlb_refs):
            _dilate_into(do_ref, w * d_b, d, stage)
            _dilate_into(dl_ref, w * sw, d, stage)

    def row(w):
        return pl.BlockSpec((tm, w), lambda i: (i, 0))

    vec = pl.BlockSpec((1, A_WIDTH), lambda i: (0, 0))
    dil = [_dilated_spec(tm, d, B_WIDTH) for _ in range(2) for d in DILATIONS]
    ins = [dmix, oa, lsea, sink, *obs, *lsebs]
    in_specs = [row(MIX_WIDTH), row(A_WIDTH), row(A_WIDTH), vec] + dil
    out_specs = [row(A_WIDTH), row(A_WIDTH)] + dil + [row(C_WIDTH), vec]
    out_shape = [jax.ShapeDtypeStruct((T, A_WIDTH), BF16), jax.ShapeDtypeStruct((T, A_WIDTH), F32)]
    out_shape += [jax.ShapeDtypeStruct((T // d, d * B_WIDTH), BF16) for d in DILATIONS]
    out_shape += [jax.ShapeDtypeStruct((T // d, d * B_WIDTH), F32) for d in DILATIONS]
    out_shape += [jax.ShapeDtypeStruct((T, C_WIDTH), F32), jax.ShapeDtypeStruct((1, A_WIDTH), F32)]
    res = pl.pallas_call(
        body, name=name, grid=(T // tm,), in_specs=in_specs, out_specs=out_specs, out_shape=out_shape,
        scratch_shapes=[_stage(tm, B_WIDTH)],
        compiler_params=_params(("arbitrary",)),
    )(*ins)
    return res[0], res[1], list(res[2:2 + nb]), list(res[2 + nb:2 + 2 * nb]), res[2 + 2 * nb], res[3 + 2 * nb]


HIST = 8


def _softplus_neg(lam):
    e = jnp.exp(-jnp.abs(lam))
    log1p = jnp.where(e < 0.01, e * (1.0 - e * (0.5 - e * (1.0 / 3.0))), jnp.log(1.0 + e))
    return jnp.maximum(-lam, 0.0) + log1p


def _neg_expm1(x):
    series = -x * (1.0 + x * (0.5 + x * (1.0 / 6.0 + x * (1.0 / 24.0 + x * (1.0 / 120.0)))))
    return jnp.where(x > -0.1, series, 1.0 - jnp.exp(x))


def _gelu_parts(x):
    k = math.sqrt(2.0 / math.pi)
    t = jnp.tanh(k * (x + 0.044715 * (x * x * x)))
    cdf = 0.5 * (1.0 + t)
    return x * cdf, cdf + 0.5 * x * (1.0 - t * t) * (k * (1.0 + 3.0 * 0.044715 * (x * x)))


def _rglru_gates(y, pos_ref, wr_ref, br_ref, wi_ref, bi_ref, lam_ref):
    yb = y.astype(BF16)
    r = _sigmoid(_dot(yb, wr_ref[...]) + br_ref[...])
    ig = _sigmoid(_dot(yb, wi_ref[...]) + bi_ref[...])
    sp = _softplus_neg(lam_ref[...])
    log_a = -C_EXP * r * sp
    reset = pos_ref[...] == 0
    a = jnp.where(reset, 0.0, jnp.exp(log_a))
    mult = jnp.where(reset, 1.0, jnp.sqrt(_neg_expm1(2.0 * log_a)))
    return yb, r, ig, sp, reset, a, mult


def _conv_fwd(xs_ref, cw_ref, cb_ref, tm):
    y = cb_ref[...] + cw_ref[0:1, :] * xs_ref[HIST:HIST + tm, :]
    for j in range(1, C_CONV):
        y = y + cw_ref[j:j + 1, :] * xs_ref[HIST - j:HIST - j + tm, :]
    return y


SCAN_GROUP = 8


def _blocked_scan(c, d, c_s, d_s, grp_a, grp_h, carry, reverse):
    tm, W = d.shape
    groups = tm // SCAN_GROUP
    order = range(SCAN_GROUP - 1, -1, -1) if reverse else range(SCAN_GROUP)
    outs, lasts = [], []
    for k in range(W // LANES):
        lanes = slice(k * LANES, (k + 1) * LANES)
        ck, dk, ga, gh = c_s.at[k], d_s.at[k], grp_a.at[k], grp_h.at[k]
        ck[...] = c[:, lanes]
        dk[...] = d[:, lanes]
        prod = state = None
        for j in order:
            rows = pl.ds(j, groups, stride=SCAN_GROUP)
            cj, dj = ck[rows, :], dk[rows, :]
            if prod is None:
                prod, state = cj, dj
            else:
                state = cj * state + dj
                prod = cj * prod
            ck[rows, :] = prod
            dk[rows, :] = state
        ga[...] = prod
        gh[...] = state

        def step(i, h, ga=ga, gh=gh):
            row = pl.ds(groups - 1 - i if reverse else i, 1)
            a, t = ga[row, :], gh[row, :]
            ga[row, :] = h
            return a * h + t

        lasts.append(lax.fori_loop(0, groups, step, carry[:, lanes], unroll=8))
        entering = ga[...]
        for j in range(SCAN_GROUP):
            rows = pl.ds(j, groups, stride=SCAN_GROUP)
            dk[rows, :] = dk[rows, :] + ck[rows, :] * entering
        outs.append(dk[...])
    return jnp.concatenate(outs, axis=1), jnp.concatenate(lasts, axis=1)


def _rglru_fwd(xc, gc, pos, cw, cb, wr, br, wi, bi, lam, *, name, tm=512):
    T, W = xc.shape
    tm = min(tm, T)

    def body(xc_ref, gc_ref, pos_ref, cw_ref, cb_ref, wr_ref, br_ref, wi_ref, bi_ref, lam_ref,
             out_ref, hs_ref, xs, a_s, b_s, h_s, grp_a, grp_h):
        @pl.when(pl.program_id(0) == 0)
        def _():
            xs[0:HIST, :] = jnp.zeros((HIST, W), F32)
            h_s[...] = jnp.zeros_like(h_s)

        xv = xc_ref[...]
        xs[HIST:HIST + tm, :] = xv
        y = _conv_fwd(xs, cw_ref, cb_ref, tm)
        xs[0:HIST, :] = xv[tm - HIST:tm, :]
        _, _, ig, _, _, a, mult = _rglru_gates(y, pos_ref, wr_ref, br_ref, wi_ref, bi_ref, lam_ref)
        hs, h_s[...] = _blocked_scan(a, mult * (ig * y), a_s, b_s, grp_a, grp_h, h_s[...], reverse=False)
        hs_ref[...] = hs
        out_ref[...] = hs * _gelu_parts(gc_ref[...])[0]

    row = pl.BlockSpec((tm, W), lambda i: (i, 0))
    full = lambda shape: pl.BlockSpec(shape, lambda i: (0,) * len(shape))
    return pl.pallas_call(
        body, name=name, grid=(T // tm,),
        in_specs=[row, row, pl.BlockSpec((tm, 1), lambda i: (i, 0)), full((C_CONV, W)), full((1, W)),
                  full((W, W)), full((1, W)), full((W, W)), full((1, W)), full((1, W))],
        out_specs=[row, row],
        out_shape=[jax.ShapeDtypeStruct((T, W), F32)] * 2,
        scratch_shapes=[pltpu.VMEM((tm + HIST, W), F32), pltpu.VMEM((W // LANES, tm, LANES), F32),
                        pltpu.VMEM((W // LANES, tm, LANES), F32), pltpu.VMEM((1, W), F32),
                        pltpu.VMEM((W // LANES, tm // SCAN_GROUP, LANES), F32),
                        pltpu.VMEM((W // LANES, tm // SCAN_GROUP, LANES), F32)],
        compiler_params=_params(("arbitrary",)),
    )(xc, gc, pos, cw, cb, wr, br, wi, bi, lam)


def _rglru_bwd(xc, gc, pos, hs, dout, cw, cb, wr, br, wi, bi, lam, *, name, tm=512):
    T, W = xc.shape
    tm = min(tm, T)
    nt = T // tm
    hb = tm // HIST

    def body(xc_ref, gc_ref, pos_ref, hs_ref, dout_ref, xch_ref, hsh_ref,
             cw_ref, cb_ref, wr_ref, br_ref, wi_ref, bi_ref, lam_ref,
             dxc_ref, dgc_ref, dcw_ref, dcb_ref, dwr_ref, dbr_ref, dwi_ref, dbi_ref, dlam_ref,
             xs, hsx, dys, asx, a_s, d_s, carry_s, grp_a, grp_h):
        i = pl.program_id(0)

        @pl.when(i == 0)
        def _():
            for r in (dcw_ref, dcb_ref, dwr_ref, dbr_ref, dwi_ref, dbi_ref, dlam_ref, carry_s):
                r[...] = jnp.zeros_like(r)
            dys[tm:tm + HIST, :] = jnp.zeros((HIST, W), F32)
            asx[tm:tm + HIST, :] = jnp.zeros((HIST, W), F32)

        has_prev = i < nt - 1
        xs[0:HIST, :] = jnp.where(has_prev, xch_ref[...], 0.0)
        hsx[0:HIST, :] = jnp.where(has_prev, hsh_ref[...], 0.0)
        xs[HIST:HIST + tm, :] = xc_ref[...]
        hs = hs_ref[...]
        hsx[HIST:HIST + tm, :] = hs
        y = _conv_fwd(xs, cw_ref, cb_ref, tm)
        yb, r, ig, sp, reset, a, mult = _rglru_gates(y, pos_ref, wr_ref, br_ref, wi_ref, bi_ref, lam_ref)

        gelu, dgelu = _gelu_parts(gc_ref[...])
        dout = dout_ref[...]
        dgc_ref[...] = dout * hs * dgelu
        asx[0:tm, :] = a
        a_up = asx[1:1 + tm, :]
        asx[tm:tm + HIST, :] = a[0:HIST, :]
        dh, carry_s[...] = _blocked_scan(a_up, dout * gelu, a_s, d_s, grp_a, grp_h, carry_s[...], reverse=True)
        hprev = hsx[HIST - 1:HIST - 1 + tm, :]
        igy = ig * y
        dmult = dh * igy
        digy = dh * mult
        dlog_a = jnp.where(reset, 0.0, dh * hprev * a - dmult * a * a / mult)
        dlam_ref[...] += jnp.sum(dlog_a * (C_EXP * r) * _sigmoid(-lam_ref[...]), axis=0, keepdims=True)
        dz_r = dlog_a * (-C_EXP * sp) * r * (1.0 - r)
        dz_i = digy * y * ig * (1.0 - ig)
        dzr_b = dz_r.astype(BF16)
        dzi_b = dz_i.astype(BF16)
        dy = digy * ig + _dot(dzr_b, wr_ref[...], NT_DIMS) + _dot(dzi_b, wi_ref[...], NT_DIMS)
        dwr_ref[...] += _dot(yb, dzr_b, TN_DIMS)
        dwi_ref[...] += _dot(yb, dzi_b, TN_DIMS)
        dbr_ref[...] += jnp.sum(dz_r, axis=0, keepdims=True)
        dbi_ref[...] += jnp.sum(dz_i, axis=0, keepdims=True)

        dys[0:tm, :] = dy
        dxc = cw_ref[0:1, :] * dy
        for j in range(1, C_CONV):
            dxc = dxc + cw_ref[j:j + 1, :] * dys[j:j + tm, :]
        dxc_ref[...] = dxc
        dys[tm:tm + HIST, :] = dy[0:HIST, :]
        dcb_ref[...] += jnp.sum(dy, axis=0, keepdims=True)
        for j in range(C_CONV):
            dcw_ref[j:j + 1, :] += jnp.sum(dy * xs[HIST - j:HIST - j + tm, :], axis=0, keepdims=True)

    row = pl.BlockSpec((tm, W), lambda i: (nt - 1 - i, 0))
    halo = pl.BlockSpec((HIST, W), lambda i: (jnp.maximum((nt - 1 - i) * hb - 1, 0), 0))
    full = lambda shape: pl.BlockSpec(shape, lambda i: (0,) * len(shape))
    out_specs = [row, row, full((C_CONV, W)), full((1, W)), full((W, W)), full((1, W)), full((W, W)), full((1, W)),
                 full((1, W))]
    out_shape = [jax.ShapeDtypeStruct((T, W), F32)] * 2
    out_shape += [jax.ShapeDtypeStruct(s, F32) for s in ((C_CONV, W), (1, W), (W, W), (1, W), (W, W), (1, W), (1, W))]
    return pl.pallas_call(
        body, name=name, grid=(nt,),
        in_specs=[row, row, pl.BlockSpec((tm, 1), lambda i: (nt - 1 - i, 0)), row, row, halo, halo,
                  full((C_CONV, W)), full((1, W)), full((W, W)), full((1, W)), full((W, W)), full((1, W)),
                  full((1, W))],
        out_specs=out_specs, out_shape=out_shape,
        scratch_shapes=[pltpu.VMEM((tm + HIST, W), F32), pltpu.VMEM((tm + HIST, W), F32),
                        pltpu.VMEM((tm + HIST, W), F32), pltpu.VMEM((tm + HIST, W), F32),
                        pltpu.VMEM((W // LANES, tm, LANES), F32), pltpu.VMEM((W // LANES, tm, LANES), F32),
                        pltpu.VMEM((1, W), F32), pltpu.VMEM((W // LANES, tm // SCAN_GROUP, LANES), F32),
                        pltpu.VMEM((W // LANES, tm // SCAN_GROUP, LANES), F32)],
        compiler_params=_params(("arbitrary",)),
    )(xc, gc, pos, hs, dout, xc, hs, cw, cb, wr, br, wi, bi, lam)


def _adam_math(w, g, m, v):
    m = ADAM_B1 * m + (1.0 - ADAM_B1) * g
    v = ADAM_B2 * v + (1.0 - ADAM_B2) * (g * g)
    m_hat = m / (1.0 - ADAM_B1 ** ADAM_STEP)
    v_hat = v / (1.0 - ADAM_B2 ** ADAM_STEP)
    delta = -ADAM_LR * (m_hat / (jnp.sqrt(v_hat) + ADAM_EPS) + ADAM_WD * w)
    return delta, m, v


def _pick_rows(R, cap=512, mult=16):
    for d in range(min(cap, R), 0, -1):
        if R % d == 0 and d % mult == 0:
            return d
    return R


def _adamw(parts, w, m, v, *, name, tr=None, part=0, prev=None):
    R, C = w.shape
    r = parts.shape[1]
    tr = _pick_rows(r) if tr is None else tr
    assert r % tr == 0 and R % r == 0, (name, R, r, tr)
    nt = r // tr

    def body(p_ref, w_ref, m_ref, v_ref, *rest):
        g_ref, d_ref, nm_ref, nv_ref = rest[-4:]
        g = p_ref[0].astype(F32)
        for d in range(1, N_DEV):
            g = g + p_ref[d].astype(F32)
        delta, nm, nv = _adam_math(w_ref[...], g, m_ref[...], v_ref[...])
        g_ref[...] = g
        d_ref[...] = delta
        nm_ref[...] = nm
        nv_ref[...] = nv

    row = pl.BlockSpec((tr, C), lambda i: (part * nt + i, 0))
    in_specs = [pl.BlockSpec((N_DEV, tr, C), lambda i: (0, i, 0)), row, row, row]
    operands = [parts, w, m, v]
    aliases = {}
    if prev is not None:
        in_specs += [pl.BlockSpec(memory_space=pl.ANY)] * 4
        operands += list(prev)
        aliases = {4 + i: i for i in range(4)}
    return pl.pallas_call(
        body, name=name, grid=(nt,), in_specs=in_specs,
        out_specs=[row] * 4, out_shape=[jax.ShapeDtypeStruct((R, C), F32)] * 4,
        input_output_aliases=aliases,
        compiler_params=_params(("parallel",)),
    )(*operands)


def _exchange(srcs, gather, *, name):
    n = len(srcs)
    out_shape = [jax.ShapeDtypeStruct((N_DEV,) + s.shape if gather else s.shape, s.dtype) for s in srcs]

    def body(*refs):
        ins, outs = refs[:n], refs[n:2 * n]
        send_sems, recv_sems, local_sems = refs[2 * n:]
        x, y, c = lax.axis_index("x"), lax.axis_index("y"), lax.axis_index("c")
        me = 4 * x + 2 * y + c
        local_copies, sends, arrivals = [], [], []
        for a in range(n):
            mine = ins[a] if gather else ins[a].at[me]
            local = pltpu.make_async_copy(mine, outs[a].at[me], local_sems.at[a])
            local.start()
            local_copies.append(local)
            for k in range(1, N_DEV):
                px, py, pc = x ^ ((k >> 2) & 1), y ^ ((k >> 1) & 1), c ^ (k & 1)
                peer = 4 * px + 2 * py + pc
                send = pltpu.make_async_remote_copy(
                    src_ref=ins[a] if gather else ins[a].at[peer], dst_ref=outs[a].at[me],
                    send_sem=send_sems.at[a * N_DEV + k], recv_sem=recv_sems.at[a * N_DEV + k],
                    device_id=(px, py, pc), device_id_type=pl.DeviceIdType.MESH)
                send.start()
                sends.append(send)
                arrivals.append(pltpu.make_async_remote_copy(
                    src_ref=mine, dst_ref=outs[a].at[peer],
                    send_sem=send_sems.at[a * N_DEV + k], recv_sem=recv_sems.at[a * N_DEV + k],
                    device_id=(px, py, pc), device_id_type=pl.DeviceIdType.MESH))
        for cp in sends:
            cp.wait_send()
        for cp in arrivals:
            cp.wait_recv()
        for cp in local_copies:
            cp.wait()

    return pl.pallas_call(
        body, name=name,
        in_specs=[pl.BlockSpec(memory_space=pl.ANY)] * n, out_specs=[pl.BlockSpec(memory_space=pl.ANY)] * n,
        out_shape=out_shape,
        scratch_shapes=[pltpu.SemaphoreType.DMA((n * N_DEV,)), pltpu.SemaphoreType.DMA((n * N_DEV,)),
                        pltpu.SemaphoreType.DMA((n,))],
    )(*srcs)


_HBM = pl.BlockSpec(memory_space=pltpu.HBM)
_SEM = pl.BlockSpec(memory_space=pltpu.SEMAPHORE)
_EFFECT = pltpu.SideEffectType.DATAFLOW_SIDE_EFFECTING


def _peers():
    x, y, c = lax.axis_index("x"), lax.axis_index("y"), lax.axis_index("c")
    out = []
    for k in range(1, N_DEV):
        px, py, pc = x ^ ((k >> 2) & 1), y ^ ((k >> 1) & 1), c ^ (k & 1)
        out.append((k, (px, py, pc), 4 * px + 2 * py + pc))
    return 4 * x + 2 * y + c, out


def _split_copies(src_refs, land_refs, send_sems, recv_sems, gather):
    me, peers = _peers()
    out = []
    for a, (src_ref, land_ref) in enumerate(zip(src_refs, land_refs)):
        for k, dev, blk in peers:
            common = dict(send_sem=send_sems.at[a * N_DEV + k], recv_sem=recv_sems.at[a * N_DEV + k], device_id=dev,
                          device_id_type=pl.DeviceIdType.MESH)
            src = src_ref if gather else src_ref.at[blk]
            out.append((pltpu.make_async_remote_copy(src_ref=src, dst_ref=land_ref.at[me], **common),
                        pltpu.make_async_remote_copy(src_ref=src, dst_ref=land_ref.at[blk], **common)))
    return out


def _exchange_start(srcs, gather, *, name, after=None):
    n = len(srcs)
    lands = [lax.empty((N_DEV,) + (s.shape if gather else s.shape[1:]), s.dtype) for s in srcs]

    def body(*refs):
        src_refs, land_refs = refs[:n], refs[n:2 * n]
        send_sems, recv_sems = refs[-2 * n - 3:-2 * n - 1]
        token = refs[-1]
        for outgoing, _ in _split_copies(src_refs, land_refs, send_sems, recv_sems, gather):
            outgoing.start()
        token[...] = jnp.zeros_like(token)

    res = pl.pallas_call(
        body, name=name,
        out_shape=(pltpu.SemaphoreType.DMA((n * N_DEV,)), pltpu.SemaphoreType.DMA((n * N_DEV,)),
                   *[pltpu.HBM(a.shape, a.dtype) for a in srcs + lands], jax.ShapeDtypeStruct((8, LANES), F32)),
        in_specs=(_HBM,) * (2 * n) + ((pl.BlockSpec(memory_space=pl.ANY),) if after is not None else ()),
        out_specs=(_SEM, _SEM) + (_HBM,) * (2 * n) + (pl.BlockSpec(memory_space=pltpu.VMEM),),
        input_output_aliases={i: i + 2 for i in range(2 * n)},
        compiler_params=pltpu.CompilerParams(has_side_effects=_EFFECT),
    )(*[pltpu.with_memory_space_constraint(a, pltpu.HBM) for a in srcs + lands],
      *([after] if after is not None else []))
    return res[0], res[1], list(res[2:2 + n]), list(res[2 + n:2 + 2 * n]), res[-1]


def _exchange_wait(started, after, gather, *, name):
    send_sems, recv_sems, srcs, lands, _ = started
    n = len(srcs)

    def body(*refs):
        src_refs, land_refs = refs[:n], refs[n:2 * n]
        send_sems, recv_sems = refs[2 * n:2 * n + 2]
        for outgoing, incoming in _split_copies(src_refs, land_refs, send_sems, recv_sems, gather):
            outgoing.wait_send()
            incoming.wait_recv()

    res = pl.pallas_call(
        body, name=name,
        out_shape=tuple(pltpu.HBM(a.shape, a.dtype) for a in srcs + lands),
        in_specs=(_HBM,) * (2 * n) + (_SEM, _SEM, pl.BlockSpec(memory_space=pl.ANY)), out_specs=(_HBM,) * (2 * n),
        input_output_aliases={i: i for i in range(2 * n)},
        compiler_params=pltpu.CompilerParams(has_side_effects=_EFFECT),
    )(*srcs, *lands, send_sems, recv_sems, after)
    return list(res[:n]), list(res[n:])


def _cols_to_blocks(g, *, name, tr=128):
    R, C = g.shape
    w = C // N_DEV
    tr = min(tr, R)

    def body(g_ref, o_ref):
        for p in range(N_DEV):
            o_ref[p] = g_ref[:, p * w:(p + 1) * w].astype(BF16)

    return pl.pallas_call(
        body, name=name, grid=(R // tr,),
        in_specs=[pl.BlockSpec((tr, C), lambda i: (i, 0))],
        out_specs=pl.BlockSpec((N_DEV, tr, w), lambda i: (0, i, 0)),
        out_shape=jax.ShapeDtypeStruct((N_DEV, R, w), BF16),
        compiler_params=_params(("parallel",)),
    )(g)


def _blocks_to_cols(b, *, name, tr=128):
    _, R, w = b.shape
    tr = min(tr, R)

    def body(b_ref, o_ref):
        o_ref[...] = jnp.concatenate([b_ref[p].astype(F32) for p in range(N_DEV)], axis=1).astype(o_ref.dtype)

    return pl.pallas_call(
        body, name=name, grid=(R // tr,),
        in_specs=[pl.BlockSpec((N_DEV, tr, w), lambda i: (0, i, 0))],
        out_specs=pl.BlockSpec((tr, N_DEV * w), lambda i: (i, 0)),
        out_shape=jax.ShapeDtypeStruct((R, N_DEV * w), b.dtype),
        compiler_params=_params(("parallel",)),
    )(b)


def _pack_rows(arrays, *, name, pick=None):
    B, _, w = arrays[0].shape
    rows = [a.shape[1] for a in arrays]
    first = 0
    if pick is not None:
        B, first = 1, pick

    def body(*refs):
        o_ref = refs[-1]
        r = 0
        for a_ref, n in zip(refs[:-1], rows):
            o_ref[0, r:r + n, :] = a_ref[0].astype(BF16)
            r += n

    return pl.pallas_call(
        body, name=name, grid=(B,),
        in_specs=[pl.BlockSpec((1, n, w), lambda b: (first + b, 0, 0)) for n in rows],
        out_specs=pl.BlockSpec((1, sum(rows), w), lambda b: (b, 0, 0)),
        out_shape=jax.ShapeDtypeStruct((B, sum(rows), w), BF16),
        compiler_params=_params(("parallel",)),
    )(*arrays)


def _unpack_rows(land, src, rows, *, name):
    _, R, w = land.shape
    src_spec = (pl.BlockSpec((1, R, w), lambda p: (p, 0, 0)) if src.ndim == 3
                else pl.BlockSpec((R, w), lambda p: (0, 0)))

    def body(land_ref, src_ref, *o_refs):
        me = 4 * lax.axis_index("x") + 2 * lax.axis_index("y") + lax.axis_index("c")
        mine = pl.program_id(0) == me
        r = 0
        for o_ref, n in zip(o_refs, rows):
            rows_i = slice(r, r + n)

            @pl.when(mine)
            def _(o_ref=o_ref, rows_i=rows_i):
                o_ref[0] = src_ref[0, rows_i, :] if src.ndim == 3 else src_ref[rows_i, :]

            @pl.when(jnp.logical_not(mine))
            def _(o_ref=o_ref, rows_i=rows_i):
                o_ref[0] = land_ref[0, rows_i, :]

            r += n

    return pl.pallas_call(
        body, name=name, grid=(N_DEV,),
        in_specs=[pl.BlockSpec((1, R, w), lambda p: (p, 0, 0)), src_spec],
        out_specs=[pl.BlockSpec((1, n, w), lambda p: (p, 0, 0)) for n in rows],
        out_shape=[jax.ShapeDtypeStruct((N_DEV, n, w), land.dtype) for n in rows],
        compiler_params=_params(("parallel",)),
    )(land, src)


def _to_blocks(w, axis):
    shape = w.shape
    k = shape[axis] // N_DEV
    w = w.reshape(shape[:axis] + (N_DEV, k) + shape[axis + 1:])
    return jnp.moveaxis(w, axis, 0)


def _from_blocks(wb, axis):
    w = jnp.moveaxis(wb, 0, axis)
    shape = w.shape
    return w.reshape(shape[:axis] + (shape[axis] * shape[axis + 1],) + shape[axis + 2:])


def _block_diag(w):
    n, k, _ = w.shape
    eye = jnp.eye(n, dtype=w.dtype)
    return (eye[:, None, :, None] * w[:, :, None, :]).reshape(n * k, n * k)


def _diag_blocks(wd):
    k = HEAD_DIM
    return jnp.stack([wd[h * k:(h + 1) * k, h * k:(h + 1) * k] for h in range(C_BLOCKS)])


def _pack(arrays):
    rows = []
    for a in arrays:
        flat = a.reshape(-1).astype(F32)
        pad = (-flat.shape[0]) % LANES
        rows.append(jnp.pad(flat, (0, pad)).reshape(-1, LANES))
    out = jnp.concatenate(rows, axis=0)
    return jnp.pad(out, ((0, (-out.shape[0]) % 8), (0, 0)))


def _unpack(packed, shapes):
    outs, r = [], 0
    for s in shapes:
        size = math.prod(s)
        nrows = -(-size // LANES)
        outs.append(packed[r:r + nrows].reshape(-1)[:size].reshape(s))
        r += nrows
    return outs


def _rope_tables(positions):
    inv = 1.0 / (ROPE_THETA ** (jnp.arange(0, HEAD_DIM, 2, dtype=F32) / HEAD_DIM))
    ang = positions.astype(F32)[:, None] * inv
    cos, sin = jnp.cos(ang), jnp.sin(ang)
    return jnp.tile(cos, (1, 4)), jnp.tile(jnp.concatenate([-sin, sin], axis=1), (1, 2))


def _layer_fwd(l, x, pos, cos_t, sin_t, W, before_mixer=None):
    tag = f"l{l}"
    saved = {'x0': x}
    x1, a1, u1 = _ffn_fwd(x, W['norm_ffn1'][l], W['ffn1_gate'][l], W['ffn1_up'][l], W['ffn1_down'][l],
                          name=f"ffn1_fwd_{tag}")
    if before_mixer is not None:
        before_mixer(l, x1)
    h = _rms_fwd(x1, W['norm_mix'][l], name=f"mixnorm_fwd_{tag}")
    proj = _mm(h, W['w_in'][l], 'nn', name=f"proj_{tag}", tm=512, tn=IN_COLS, tk=h.shape[1])
    qa, ka2, va2, qb, kb, vb, xc, gc = _split_rope(proj, cos_t, sin_t, name=f"split_{tag}")
    oa, lsea = _attn_fwd(qa, ka2, va2, A_MAX_DIST, name=f"attn_a_fwd_{tag}")
    obs, lsebs = [], []
    for bi, (window, d) in enumerate(B_BRANCHES):
        o, lse = _attn_fwd(qb[bi], kb[bi], vb[bi], window // d, name=f"attn_b{bi}_fwd_{tag}")
        obs.append(o)
        lsebs.append(lse)
    oc, hs = _rglru_fwd(xc, gc, pos, W['conv_w'][l], W['conv_b'][l], W['rg_w_r'][l], W['rg_b_r'][l],
                        W['rg_w_i'][l], W['rg_b_i'][l], W['rg_lambda'][l], name=f"rglru_fwd_{tag}")
    mix = _combine_fwd(oa, lsea, W['sinks'][l], obs, lsebs, oc, name=f"combine_fwd_{tag}")
    x2 = _mm(mix, W['w_out'][l], 'nn', name=f"outproj_{tag}", tm=512, tn=x.shape[1], tk=MIX_WIDTH, res=x1)
    x3, a2, u2 = _ffn_fwd(x2, W['norm_ffn2'][l], W['ffn2_gate'][l], W['ffn2_up'][l], W['ffn2_down'][l],
                          name=f"ffn2_fwd_{tag}")
    saved.update(a1=a1, u1=u1, x1=x1, h=h, qa=qa, ka2=ka2, va2=va2, qb=qb, kb=kb, vb=vb, xc=xc, gc=gc, oa=oa,
                 lsea=lsea, obs=obs, lsebs=lsebs, hs=hs, mix=mix, x2=x2, a2=a2, u2=u2)
    return x3, saved


def _ffn_grads(tag, which, x, g, dy, a, u, wg, wu, wd):
    T, D = x.shape
    F = wg.shape[1]
    dx, dg, n, act, da, du = _ffn_bwd(x, g, dy, a, u, wg, wu, wd, name=f"{which}_bwd_{tag}")
    fc = _ffn_chunk(F)
    d_gate = _mm(n, da, 'tn', name=f"{which}_dgate_{tag}", tm=D, tn=fc, tk=2048)
    d_up = _mm(n, du, 'tn', name=f"{which}_dup_{tag}", tm=D, tn=fc, tk=2048)
    d_down = _mm(act, dy, 'tn', name=f"{which}_ddown_{tag}", tm=fc, tn=D, tk=1024, alpha=0.5)
    return dx, dg, d_gate, d_up, d_down


def _layer_bwd(l, dx3, pos, cos_t, sin_t, W, S, on_grads=None):
    tag = f"l{l}"
    G = {}
    dx2, G['norm_ffn2'], G['ffn2_gate'], G['ffn2_up'], G['ffn2_down'] = _ffn_grads(
        tag, 'ffn2', S['x2'], W['norm_ffn2'][l], dx3, S['a2'], S['u2'], W['ffn2_gate'][l], W['ffn2_up'][l],
        W['ffn2_down'][l])
    D = dx2.shape[1]
    dmix = _mm(dx2, W['w_out'][l], 'nt', name=f"outproj_dx_{tag}", tm=512, tn=MIX_WIDTH, tk=D)
    G['w_out'] = _mm(S['mix'], dx2, 'tn', name=f"outproj_dw_{tag}", tm=MIX_WIDTH, tn=D, tk=2048)
    doa, dla, dobs, dlbs, doc, dsink = _combine_bwd(dmix, S['oa'], S['lsea'], W['sinks'][l], S['obs'], S['lsebs'],
                                                    name=f"combine_bwd_{tag}")
    G['attn_sinks'] = dsink.reshape(A_WIDTH // HEAD_DIM, HEAD_DIM)[:, 0]
    dqa, dka2, dva2 = _attn_bwd(S['qa'], S['ka2'], S['va2'], doa, S['lsea'], dla, A_MAX_DIST,
                                name=f"attn_a_bwd_{tag}")
    dqb, dkb, dvb = [], [], []
    for bi, (window, d) in enumerate(B_BRANCHES):
        dq, dk, dv = _attn_bwd(S['qb'][bi], S['kb'][bi], S['vb'][bi], dobs[bi], S['lsebs'][bi], dlbs[bi], window // d,
                               name=f"attn_b{bi}_bwd_{tag}")
        dqb.append(dq)
        dkb.append(dk)
        dvb.append(dv)
    (dxc, dgc, G['conv_w'], G['conv_b'], dwr, G['rg_b_r'], dwi, G['rg_b_i'], G['rg_lambda']) = _rglru_bwd(
        S['xc'], S['gc'], pos, S['hs'], doc, W['conv_w'][l], W['conv_b'][l], W['rg_w_r'][l], W['rg_b_r'][l],
        W['rg_w_i'][l], W['rg_b_i'][l], W['rg_lambda'][l], name=f"rglru_bwd_{tag}")
    G['rg_w_r'] = _diag_blocks(dwr)
    G['rg_w_i'] = _diag_blocks(dwi)
    dproj = _merge_dproj(dqa, dka2, dva2, dqb, dkb, dvb, dxc, dgc, cos_t, sin_t, name=f"merge_{tag}")
    dh = _mm(dproj, W['w_in'][l], 'nt', name=f"proj_dx_{tag}", tm=512, tn=D, tk=IN_COLS)
    G['w_in'] = _mm(S['h'], dproj, 'tn', name=f"proj_dw_{tag}", tm=D, tn=IN_COLS, tk=1024)
    g_mix = W['norm_mix'][l]
    if on_grads is not None:
        g_mix = g_mix + on_grads(l, 0, G)
    dx1, G['norm_mix'] = _rms_bwd(S['x1'], g_mix, dh, dx2, name=f"mixnorm_bwd_{tag}")
    dx0, G['norm_ffn1'], G['ffn1_gate'], G['ffn1_up'], G['ffn1_down'] = _ffn_grads(
        tag, 'ffn1', S['x0'], W['norm_ffn1'][l], dx1, S['a1'], S['u1'], W['ffn1_gate'][l], W['ffn1_up'][l],
        W['ffn1_down'][l])
    if on_grads is not None:
        on_grads(l, 1, G)
    return dx0, G


def _device_step(x, positions, loss_target, W, before_layer=None, on_grads=None, before_mixer=None):
    T = x.shape[0]
    pos = positions.reshape(T, 1)
    cos_t, sin_t = _rope_tables(positions)
    saved = []
    for l in range(DEPTH):
        if before_layer is not None:
            before_layer(l, x)
        x, S = _layer_fwd(l, x, pos, cos_t, sin_t, W, before_mixer)
        saved.append(S)
    loss, dx, dg_final = _loss_head(x, W['norm_final'], loss_target, name="loss_head")
    grads = [None] * DEPTH
    for l in reversed(range(DEPTH)):
        dx, grads[l] = _layer_bwd(l, dx, pos, cos_t, sin_t, W, saved[l], on_grads)
    return loss, dx, grads, dg_final


SHARD_AXIS = {'ffn1_gate': 2, 'ffn1_up': 2, 'ffn1_down': 1, 'w_in': 2, 'w_out': 1, 'ffn2_gate': 2, 'ffn2_up': 2,
              'ffn2_down': 1, 'conv_w': 2}


def kernel(x, positions, norm_ffn1, ffn1_gate, ffn1_up, ffn1_down, norm_mix, w_in, attn_sinks, conv_w, conv_b, rg_w_r, rg_b_r, rg_w_i, rg_b_i, rg_lambda, w_out, norm_ffn2, ffn2_gate, ffn2_up, ffn2_down, norm_final, loss_target, m_norm_ffn1, m_ffn1_gate, m_ffn1_up, m_ffn1_down, m_norm_mix, m_w_in, m_attn_sinks, m_conv_w, m_conv_b, m_rg_w_r, m_rg_b_r, m_rg_w_i, m_rg_b_i, m_rg_lambda, m_w_out, m_norm_ffn2, m_ffn2_gate, m_ffn2_up, m_ffn2_down, m_norm_final, v_norm_ffn1, v_ffn1_gate, v_ffn1_up, v_ffn1_down, v_norm_mix, v_w_in, v_attn_sinks, v_conv_w, v_conv_b, v_rg_w_r, v_rg_b_r, v_rg_w_i, v_rg_b_i, v_rg_lambda, v_w_out, v_norm_ffn2, v_ffn2_gate, v_ffn2_up, v_ffn2_down, v_norm_final):
    given = dict(locals())
    me = 4 * lax.axis_index("x") + 2 * lax.axis_index("y") + lax.axis_index("c")

    def by_width(names):
        classes = {}
        for n in names:
            classes.setdefault(given[n].shape[2], []).append(n)
        return list(classes.values())

    def pack(names, get, tag, pick=None):
        return [_pack_rows([get(n) for n in cls], name=f"pack{ci}_{tag}", pick=pick)
                for ci, cls in enumerate(by_width(names))]

    def unpack(names, lands, srcs, tag):
        out = {}
        for ci, cls in enumerate(by_width(names)):
            arrays = _unpack_rows(lands[ci], srcs[ci], [given[n].shape[1] for n in cls], name=f"unpack{ci}_{tag}")
            out.update(zip(cls, arrays))
        return out

    def gather_start(l, names, tag, after=None):
        return _exchange_start([p[0] for p in pack(names, lambda n: given[n], f"w{tag}_l{l}", pick=l)], True,
                               name=f"gather_start{tag}_l{l}", after=after)

    def gather_wait(l, names, tag, started, after):
        srcs, lands = _exchange_wait(started, after, True, name=f"gather_wait{tag}_l{l}")
        blocks = unpack(names, lands, srcs, f"w{tag}_l{l}")
        for n in names:
            if SHARD_AXIS[n] == 2:
                W[n][l] = _blocks_to_cols(blocks[n], name=f"cols_{n}_l{l}")
            else:
                W[n][l] = blocks[n].reshape(-1, blocks[n].shape[2])
        return lands[0]

    W = {n: [None] * DEPTH for n in BIG_NAMES}
    ffn1_names, later_names = SCATTER_STAGES[1], SCATTER_STAGES[0]
    first = gather_start(0, ffn1_names, "a")
    conv_full = _exchange([conv_w], True, name="gather_conv_w")[0]
    landed = gather_wait(0, ffn1_names, "a", first, conv_full)
    second = gather_start(0, later_names, "b", after=landed)
    started = second[4][0, 0]
    gathers = [None] * DEPTH

    def before_layer(l, x_in):
        if l > 0:
            gather_wait(l, BIG_NAMES, "", gathers[l], x_in)

    def before_mixer(l, x1):
        if l == 0:
            landed = gather_wait(0, later_names, "b", second, x1)
            token = 0.0
            for k in range(1, DEPTH):
                gathers[k] = gather_start(k, BIG_NAMES, "", after=landed)
                token = token + gathers[k][4][0, 0]
            W['norm_mix'][0] = W['norm_mix'][0] + token

    scatters = {}

    def on_grads(l, stage, G):
        def blocks_of(n):
            if SHARD_AXIS[n] == 2:
                return _cols_to_blocks(G[n], name=f"blocks_{n}_l{l}")
            return G[n].reshape(N_DEV, -1, G[n].shape[1])

        scatters[l, stage] = _exchange_start(pack(SCATTER_STAGES[stage], blocks_of, f"g{stage}_l{l}"), False,
                                             name=f"scatter_start{stage}_l{l}")
        token = scatters[l, stage][4][0, 0]
        if stage == 1 and l > 0:
            W['norm_ffn2'][l - 1] = W['norm_ffn2'][l - 1] + token
        return token

    W['conv_w'] = [_from_blocks(conv_full[:, l], 1) for l in range(DEPTH)]
    for n in ('norm_ffn1', 'norm_mix', 'norm_ffn2', 'conv_b', 'rg_lambda'):
        W[n] = [given[n][l][None, :] for l in range(DEPTH)]
    W['norm_final'] = norm_final[None, :]
    W['sinks'] = [jnp.repeat(attn_sinks[l], HEAD_DIM)[None, :] for l in range(DEPTH)]
    for n in ('rg_w_r', 'rg_w_i'):
        W[n] = [_block_diag(given[n][l]).astype(BF16) for l in range(DEPTH)]
    for n in ('rg_b_r', 'rg_b_i'):
        W[n] = [given[n][l].reshape(1, C_WIDTH) for l in range(DEPTH)]

    W['norm_ffn1'][0] = W['norm_ffn1'][0] + started

    loss_part, grad_x, grads, dg_final = _device_step(x[0], positions[0], loss_target[0], W, before_layer, on_grads,
                                                      before_mixer)
    loss = lax.psum(loss_part[0, 0], ("x", "y", "c"))

    small_shapes = [given[n].shape for n in SMALL_NAMES] + [(DEPTH, C_CONV, C_WIDTH)]
    small_grads = []
    for n in SMALL_NAMES:
        if n == 'norm_final':
            small_grads.append(dg_final.reshape(-1))
        else:
            small_grads.append(jnp.stack([grads[l][n].reshape(given[n].shape[1:]) for l in range(DEPTH)]))
    small_grads.append(jnp.stack([grads[l]['conv_w'] for l in range(DEPTH)]))
    small_parts = _exchange([_pack(small_grads)], True, name="gather_small_grads")[0]

    out = {}
    for stage in (0, 1):
        parts = {}
        for l in reversed(range(DEPTH)):
            last = stage == 1 and l == 0
            srcs, lands = _exchange_wait(scatters[l, stage], out['w_in'][1] if last else grad_x, False,
                                         name=f"scatter_wait{stage}_l{l}")
            parts[l] = unpack(SCATTER_STAGES[stage], lands, srcs, f"g{stage}_l{l}")
        for n in SCATTER_STAGES[stage]:
            shape = given[n].shape
            two_d = (shape[0] * shape[1], shape[2])
            res = None
            for l in reversed(range(DEPTH)):
                res = _adamw(parts[l][n], given[n].reshape(two_d), given['m_' + n].reshape(two_d),
                             given['v_' + n].reshape(two_d), name=f"adamw_{n}_l{l}", part=l, prev=res)
            out[n] = [r.reshape(shape) for r in res]

    w_small = [given[n] for n in SMALL_NAMES]
    m_small = [given['m_' + n] for n in SMALL_NAMES]
    v_small = [given['v_' + n] for n in SMALL_NAMES]
    zeros_cw = jnp.zeros((DEPTH, C_CONV, C_WIDTH), F32)
    res = _adamw(small_parts, _pack(w_small + [zeros_cw]), _pack(m_small + [zeros_cw]), _pack(v_small + [zeros_cw]),
                 name="adamw_small", tr=8)
    unpacked = [_unpack(r, small_shapes) for r in res]
    for i, n in enumerate(SMALL_NAMES):
        out[n] = [u[i] for u in unpacked]

    k = conv_w.shape[2]
    g_cw = lax.dynamic_slice_in_dim(unpacked[0][-1], me * k, k, axis=2)
    zero_parts = jnp.zeros((N_DEV - 1,) + (8, LANES), F32)
    res = _adamw(jnp.concatenate([_pack([g_cw])[None], zero_parts]), _pack([conv_w]), _pack([m_conv_w]),
                 _pack([v_conv_w]), name="adamw_conv_w", tr=8)
    out['conv_w'] = [_unpack(r, [conv_w.shape])[0] for r in res]

    outputs = [loss, grad_x[None]]
    for i in range(4):
        outputs += [out[n][i] for n in WEIGHT_NAMES]
    return tuple(outputs)
```

```python
import functools
import math

import jax
import jax.numpy as jnp
from jax import lax
from jax.experimental import pallas as pl
from jax.experimental.pallas import tpu as pltpu

F32 = jnp.float32
BF16 = jnp.bfloat16

N_DEV = 8
DEPTH = 4
HEAD_DIM = 64
LANES = 128
QBLK = 128
A_WIDTH = 256
A_KV_WIDTH = 128
B_WIDTH = 384
C_WIDTH = 384
C_BLOCKS = 6
C_CONV = 4
C_EXP = 8.0
MIX_WIDTH = A_WIDTH + B_WIDTH + C_WIDTH
IN_COLS = A_WIDTH + 2 * A_KV_WIDTH + 3 * B_WIDTH + 2 * C_WIDTH
A_MAX_DIST = 127
B_BRANCHES = ((128, 1), (512, 4), (2048, 16))
ROPE_THETA = 10000.0
EPS = 1e-6
SCALE = HEAD_DIM ** -0.5

ADAM_LR = 0.001
ADAM_B1 = 0.9
ADAM_B2 = 0.999
ADAM_EPS = 1e-08
ADAM_WD = 0.01
ADAM_STEP = 10

ATTN_CHUNK = 1024
ATTN_FWD_UNROLL = True
ATTN_BWD_UNROLL = True
VMEM_LIMIT = 56 * 1024 * 1024

NT_DIMS = (((1,), (1,)), ((), ()))
TN_DIMS = (((0,), (0,)), ((), ()))
NN_DIMS = (((1,), (0,)), ((), ()))

WEIGHT_NAMES = ['norm_ffn1', 'ffn1_gate', 'ffn1_up', 'ffn1_down', 'norm_mix', 'w_in', 'attn_sinks', 'conv_w',
                'conv_b', 'rg_w_r', 'rg_b_r', 'rg_w_i', 'rg_b_i', 'rg_lambda', 'w_out', 'norm_ffn2', 'ffn2_gate',
                'ffn2_up', 'ffn2_down', 'norm_final']
BIG_NAMES = ['ffn1_gate', 'ffn1_up', 'ffn1_down', 'w_in', 'w_out', 'ffn2_gate', 'ffn2_up', 'ffn2_down']
SCATTER_STAGES = (['ffn2_gate', 'ffn2_up', 'ffn2_down', 'w_out', 'w_in'], ['ffn1_gate', 'ffn1_up', 'ffn1_down'])
SMALL_NAMES = ['norm_ffn1', 'norm_mix', 'norm_ffn2', 'norm_final', 'attn_sinks', 'conv_b', 'rg_w_r', 'rg_b_r',
               'rg_w_i', 'rg_b_i', 'rg_lambda']


def _params(sem, vmem=VMEM_LIMIT):
    return pltpu.CompilerParams(dimension_semantics=sem, vmem_limit_bytes=vmem)


def _dot(a, b, dims=NN_DIMS):
    return lax.dot_general(a, b, dims, preferred_element_type=F32)


def _sigmoid(x):
    return 1.0 / (1.0 + jnp.exp(-x))


def _mm(a, b, mode, *, name, tm=512, tn=512, tk=512, out_dtype=F32, alpha=1.0, res=None):
    if mode == 'nn':
        (M, K), N = a.shape, b.shape[1]
    elif mode == 'nt':
        (M, K), N = a.shape, b.shape[0]
    else:
        (K, M), N = a.shape, b.shape[1]
    tm, tn, tk = min(tm, M), min(tn, N), min(tk, K)
    ni, nj, nk = M // tm, N // tn, K // tk
    assert ni * tm == M and nj * tn == N and nk * tk == K, (name, a.shape, b.shape, tm, tn, tk)
    if mode == 'tn':
        a_spec = pl.BlockSpec((tk, tm), lambda j, i, k: (k, i))
    else:
        a_spec = pl.BlockSpec((tm, tk), lambda j, i, k: (i, k))
    if mode == 'nt':
        b_spec = pl.BlockSpec((tn, tk), lambda j, i, k: (j, k))
    else:
        b_spec = pl.BlockSpec((tk, tn), lambda j, i, k: (k, j))
    dims = {'nn': NN_DIMS, 'nt': NT_DIMS, 'tn': TN_DIMS}[mode]
    o_spec = pl.BlockSpec((tm, tn), lambda j, i, k: (i, j))
    has_res = res is not None

    def body(*refs):
        if has_res:
            a_ref, b_ref, r_ref, o_ref = refs[:4]
        else:
            a_ref, b_ref, o_ref = refs[:3]
        part = _dot(a_ref[...].astype(BF16), b_ref[...].astype(BF16), dims)

        def finish(acc):
            out = acc * alpha if alpha != 1.0 else acc
            if has_res:
                out = r_ref[...] + out
            o_ref[...] = out.astype(out_dtype)

        if nk == 1:
            finish(part)
        else:
            acc_ref = refs[-1]
            k = pl.program_id(2)

            @pl.when(k == 0)
            def _():
                acc_ref[...] = part

            @pl.when(k > 0)
            def _():
                acc_ref[...] += part

            @pl.when(k == nk - 1)
            def _():
                finish(acc_ref[...])

    in_specs = [a_spec, b_spec] + ([o_spec] if has_res else [])
    operands = [a, b] + ([res] if has_res else [])
    return pl.pallas_call(
        body, name=name, grid=(nj, ni, nk), in_specs=in_specs, out_specs=o_spec,
        out_shape=jax.ShapeDtypeStruct((M, N), out_dtype),
        scratch_shapes=[pltpu.VMEM((tm, tn), F32)] if nk > 1 else [],
        compiler_params=_params(("parallel", "parallel", "arbitrary")),
    )(*operands)


def _rms_fwd(x, g, *, name, tm=512):
    T, D = x.shape
    tm = min(tm, T)

    def body(x_ref, g_ref, o_ref):
        xv = x_ref[...]
        rstd = lax.rsqrt(jnp.mean(xv * xv, axis=-1, keepdims=True) + EPS)
        o_ref[...] = (xv * rstd * g_ref[...]).astype(BF16)

    return pl.pallas_call(
        body, name=name, grid=(T // tm,),
        in_specs=[pl.BlockSpec((tm, D), lambda i: (i, 0)), pl.BlockSpec((1, D), lambda i: (0, 0))],
        out_specs=pl.BlockSpec((tm, D), lambda i: (i, 0)),
        out_shape=jax.ShapeDtypeStruct((T, D), BF16),
        compiler_params=_params(("parallel",)),
    )(x, g)


def _rms_bwd_math(xv, g, dn):
    rstd = lax.rsqrt(jnp.mean(xv * xv, axis=-1, keepdims=True) + EPS)
    xhat = xv * rstd
    dxhat = dn * g
    dx = rstd * (dxhat - xhat * jnp.mean(dxhat * xhat, axis=-1, keepdims=True))
    return dx, jnp.sum(dn * xhat, axis=0, keepdims=True)


def _rms_bwd(x, g, dn, dres, *, name, tm=512):
    T, D = x.shape
    tm = min(tm, T)

    def body(x_ref, g_ref, dn_ref, dres_ref, dx_ref, dg_ref):
        dx, dg = _rms_bwd_math(x_ref[...], g_ref[...], dn_ref[...])
        dx_ref[...] = dres_ref[...] + dx

        @pl.when(pl.program_id(0) == 0)
        def _():
            dg_ref[...] = jnp.zeros_like(dg_ref)

        dg_ref[...] += dg

    row = pl.BlockSpec((tm, D), lambda i: (i, 0))
    vec = pl.BlockSpec((1, D), lambda i: (0, 0))
    return pl.pallas_call(
        body, name=name, grid=(T // tm,),
        in_specs=[row, vec, row, row], out_specs=[row, vec],
        out_shape=[jax.ShapeDtypeStruct((T, D), F32), jax.ShapeDtypeStruct((1, D), F32)],
        compiler_params=_params(("arbitrary",)),
    )(x, g, dn, dres)


def _loss_head(x, g, target, *, name, tm=512):
    T, D = x.shape
    tm = min(tm, T)

    def body(x_ref, g_ref, t_ref, loss_ref, dx_ref, dg_ref):
        xv = x_ref[...]
        g = g_ref[...]
        rstd = lax.rsqrt(jnp.mean(xv * xv, axis=-1, keepdims=True) + EPS)
        y = xv * rstd * g
        err = y - t_ref[...]
        part = 0.5 * jnp.sum(jnp.mean(err * err, axis=-1, keepdims=True), axis=0, keepdims=True)
        dx, dg = _rms_bwd_math(xv, g, err * (1.0 / D))
        dx_ref[...] = dx

        @pl.when(pl.program_id(0) == 0)
        def _():
            dg_ref[...] = jnp.zeros_like(dg_ref)
            loss_ref[...] = jnp.zeros_like(loss_ref)

        dg_ref[...] += dg
        loss_ref[...] += jnp.broadcast_to(part, loss_ref.shape)

    row = pl.BlockSpec((tm, D), lambda i: (i, 0))
    vec = pl.BlockSpec((1, D), lambda i: (0, 0))
    lspec = pl.BlockSpec((1, LANES), lambda i: (0, 0))
    return pl.pallas_call(
        body, name=name, grid=(T // tm,),
        in_specs=[row, vec, row], out_specs=[lspec, row, vec],
        out_shape=[jax.ShapeDtypeStruct((1, LANES), F32), jax.ShapeDtypeStruct((T, D), F32),
                   jax.ShapeDtypeStruct((1, D), F32)],
        compiler_params=_params(("arbitrary",)),
    )(x, g, target)


def _resident(shape):
    return pl.BlockSpec(shape, lambda i: (0,) * len(shape), pipeline_mode=pl.Buffered(1))


def _ffn_chunk(F):
    for c in (1408, 1024, 512, 256, 128):
        if F % c == 0:
            return c
    return F


def _ffn_fwd(x, g, wg, wu, wd, *, name, tm=256):
    T, D = x.shape
    F = wg.shape[1]
    tm = min(tm, T)
    fc = _ffn_chunk(F)

    def body(x_ref, g_ref, wg_ref, wu_ref, wd_ref, o_ref, a_ref, u_ref):
        xv = x_ref[...]
        rstd = lax.rsqrt(jnp.mean(xv * xv, axis=-1, keepdims=True) + EPS)
        n = (xv * rstd * g_ref[...]).astype(BF16)
        acc = jnp.zeros((tm, D), F32)
        for c in range(F // fc):
            sl = slice(c * fc, (c + 1) * fc)
            a = _dot(n, wg_ref[:, sl])
            u = _dot(n, wu_ref[:, sl])
            a_ref[:, sl] = a.astype(BF16)
            u_ref[:, sl] = u.astype(BF16)
            act = (a * _sigmoid(a) * u).astype(BF16)
            acc = acc + _dot(act, wd_ref[sl, :])
        o_ref[...] = xv + 0.5 * acc

    row = pl.BlockSpec((tm, D), lambda i: (i, 0))
    hid = pl.BlockSpec((tm, F), lambda i: (i, 0))
    return pl.pallas_call(
        body, name=name, grid=(T // tm,),
        in_specs=[row, pl.BlockSpec((1, D), lambda i: (0, 0)),
                  _resident((D, F)), _resident((D, F)), _resident((F, D))],
        out_specs=[row, hid, hid],
        out_shape=[jax.ShapeDtypeStruct((T, D), F32), jax.ShapeDtypeStruct((T, F), BF16),
                   jax.ShapeDtypeStruct((T, F), BF16)],
        compiler_params=_params(("parallel",)),
    )(x, g, wg, wu, wd)


def _ffn_bwd(x, g, dy, a, u, wg, wu, wd, *, name, tm=256):
    T, D = x.shape
    F = wg.shape[1]
    tm = min(tm, T)
    fc = _ffn_chunk(F)

    def body(x_ref, g_ref, dy_ref, a_ref, u_ref, wg_ref, wu_ref, wd_ref,
             dx_ref, dg_ref, n_ref, act_ref, da_ref, du_ref):
        xv = x_ref[...]
        g = g_ref[...]
        rstd = lax.rsqrt(jnp.mean(xv * xv, axis=-1, keepdims=True) + EPS)
        n_ref[...] = (xv * rstd * g).astype(BF16)
        dy = dy_ref[...]
        dyh = (0.5 * dy).astype(BF16)
        dn = jnp.zeros((tm, D), F32)
        for c in range(F // fc):
            sl = slice(c * fc, (c + 1) * fc)
            av = a_ref[:, sl].astype(F32)
            uv = u_ref[:, sl].astype(F32)
            dact = _dot(dyh, wd_ref[sl, :], NT_DIMS)
            s = _sigmoid(av)
            silu = av * s
            act_ref[:, sl] = (silu * uv).astype(BF16)
            da = (dact * uv * (s * (1.0 + av * (1.0 - s)))).astype(BF16)
            du = (dact * silu).astype(BF16)
            da_ref[:, sl] = da
            du_ref[:, sl] = du
            dn = dn + _dot(da, wg_ref[:, sl], NT_DIMS) + _dot(du, wu_ref[:, sl], NT_DIMS)
        dx, dg = _rms_bwd_math(xv, g, dn)
        dx_ref[...] = dy + dx

        @pl.when(pl.program_id(0) == 0)
        def _():
            dg_ref[...] = jnp.zeros_like(dg_ref)

        dg_ref[...] += dg

    row = pl.BlockSpec((tm, D), lambda i: (i, 0))
    hid = pl.BlockSpec((tm, F), lambda i: (i, 0))
    vec = pl.BlockSpec((1, D), lambda i: (0, 0))
    return pl.pallas_call(
        body, name=name, grid=(T // tm,),
        in_specs=[row, vec, row, hid, hid,
                  _resident((D, F)), _resident((D, F)), _resident((F, D))],
        out_specs=[row, vec, row, hid, hid, hid],
        out_shape=[jax.ShapeDtypeStruct((T, D), F32), jax.ShapeDtypeStruct((1, D), F32),
                   jax.ShapeDtypeStruct((T, D), BF16), jax.ShapeDtypeStruct((T, F), BF16),
                   jax.ShapeDtypeStruct((T, F), BF16), jax.ShapeDtypeStruct((T, F), BF16)],
        compiler_params=_params(("arbitrary",)),
    )(x, g, dy, a, u, wg, wu, wd)


def _lane_iota(shape):
    return lax.broadcasted_iota(jnp.int32, shape, 1)


def _rope_partner(x):
    first_half = (_lane_iota(x.shape) & (HEAD_DIM - 1)) < HEAD_DIM // 2
    return jnp.where(first_half, pltpu.roll(x, LANES - HEAD_DIM // 2, 1), pltpu.roll(x, HEAD_DIM // 2, 1))


def _swap_heads(x):
    return pltpu.roll(x, HEAD_DIM, 1)


def _undilate(blk_ref, d, stage):
    if d == 1:
        return blk_ref[...]
    n, width = blk_ref.shape
    W = width // d
    for r in range(d):
        for g in range(W // LANES):
            stage.at[g][pl.ds(r, n, stride=d), :] = blk_ref[:, r * W + g * LANES:r * W + (g + 1) * LANES]
    return jnp.concatenate([stage.at[g][...] for g in range(W // LANES)], axis=1)


def _dilate_into(out_ref, value, d, stage):
    if d == 1:
        out_ref[...] = value.astype(out_ref.dtype)
        return
    n = value.shape[0] // d
    W = value.shape[1]
    for g in range(W // LANES):
        stage.at[g][...] = value[:, g * LANES:(g + 1) * LANES]
    for r in range(d):
        for g in range(W // LANES):
            out_ref[:, r * W + g * LANES:r * W + (g + 1) * LANES] = (
                stage.at[g][pl.ds(r, n, stride=d), :].astype(out_ref.dtype))


def _dilated_spec(tm, d, W):
    return pl.BlockSpec((tm // d, d * W), lambda i: (i, 0))


def _stage(tm, W):
    return pltpu.VMEM((W // LANES, tm, LANES), F32)


DILATIONS = tuple(d for _, d in B_BRANCHES)


def _split_rope(proj, cos_t, sin_t, *, name, tm=512):
    T = proj.shape[0]
    tm = min(tm, T)
    nd = len(DILATIONS)

    def body(p_ref, c_ref, s_ref, qa_ref, ka_ref, va_ref, *rest):
        b_refs = rest[:3 * nd]
        xc_ref, gc_ref, stage = rest[3 * nd:]
        cos = c_ref[...]
        sin = s_ref[...]

        def rope(x):
            return x * cos + _rope_partner(x) * sin

        lo = _lane_iota((tm, LANES)) < HEAD_DIM
        col = 0
        for j in range(A_WIDTH // LANES):
            qa_ref[:, j * LANES:(j + 1) * LANES] = (rope(p_ref[:, col:col + LANES]) * SCALE).astype(BF16)
            col += LANES
        kr = rope(p_ref[:, col:col + LANES])
        col += LANES
        vr = p_ref[:, col:col + LANES]
        col += LANES
        for src, dst in ((kr, ka_ref), (vr, va_ref)):
            sw = _swap_heads(src)
            dst[:, 0:LANES] = jnp.where(lo, src, sw).astype(BF16)
            dst[:, LANES:2 * LANES] = jnp.where(lo, sw, src).astype(BF16)
        for which, (roped, scale) in enumerate(((True, SCALE), (True, 1.0), (False, 1.0))):
            parts = []
            for j in range(B_WIDTH // LANES):
                v = p_ref[:, col:col + LANES]
                parts.append(rope(v) * scale if roped else v)
                col += LANES
            value = jnp.concatenate(parts, axis=1)
            for di, d in enumerate(DILATIONS):
                _dilate_into(b_refs[which * nd + di], value, d, stage)
        xc_ref[...] = p_ref[:, col:col + C_WIDTH]
        gc_ref[...] = p_ref[:, col + C_WIDTH:col + 2 * C_WIDTH]

    def row(w):
        return pl.BlockSpec((tm, w), lambda i: (i, 0))

    out_specs = [row(A_WIDTH)] * 3 + [_dilated_spec(tm, d, B_WIDTH) for _ in range(3) for d in DILATIONS]
    out_specs += [row(C_WIDTH)] * 2
    out_shape = [jax.ShapeDtypeStruct((T, A_WIDTH), BF16)] * 3
    out_shape += [jax.ShapeDtypeStruct((T // d, d * B_WIDTH), BF16) for _ in range(3) for d in DILATIONS]
    out_shape += [jax.ShapeDtypeStruct((T, C_WIDTH), F32)] * 2
    res = pl.pallas_call(
        body, name=name, grid=(T // tm,),
        in_specs=[row(IN_COLS), row(LANES), row(LANES)], out_specs=out_specs, out_shape=out_shape,
        scratch_shapes=[_stage(tm, B_WIDTH)],
        compiler_params=_params(("parallel",)),
    )(proj, cos_t, sin_t)
    qa, ka2, va2 = res[:3]
    qb, kb, vb = (list(res[3 + i * nd:3 + (i + 1) * nd]) for i in range(3))
    return qa, ka2, va2, qb, kb, vb, res[-2], res[-1]


def _merge_dproj(dqa, dka2, dva2, dqb, dkb, dvb, dxc, dgc, cos_t, sin_t, *, name, tm=512):
    T = dqa.shape[0]
    tm = min(tm, T)
    nb = len(dqb)

    def body(*refs):
        dqa_ref, dka_ref, dva_ref = refs[:3]
        dqb_refs = refs[3:3 + nb]
        dkb_refs = refs[3 + nb:3 + 2 * nb]
        dvb_refs = refs[3 + 2 * nb:3 + 3 * nb]
        dxc_ref, dgc_ref, c_ref, s_ref, o_ref, stage = refs[3 + 3 * nb:]
        cos = c_ref[...]
        sin = s_ref[...]

        def rope_t(dy):
            return dy * cos - _rope_partner(dy) * sin

        lo = _lane_iota((tm, LANES)) < HEAD_DIM
        col = 0
        for j in range(A_WIDTH // LANES):
            o_ref[:, col:col + LANES] = (rope_t(dqa_ref[:, j * LANES:(j + 1) * LANES]) * SCALE).astype(BF16)
            col += LANES
        for src, roped in ((dka_ref, True), (dva_ref, False)):
            b0 = src[:, 0:LANES]
            b1 = src[:, LANES:2 * LANES]
            v = jnp.where(lo, b0 + _swap_heads(b0), b1 + _swap_heads(b1))
            if roped:
                v = rope_t(v)
            o_ref[:, col:col + LANES] = v.astype(BF16)
            col += LANES
        for group, roped, scale in ((dqb_refs, True, SCALE), (dkb_refs, True, 1.0), (dvb_refs, False, 1.0)):
            total = _undilate(group[0], DILATIONS[0], stage)
            for r, d in zip(group[1:], DILATIONS[1:]):
                total = total + _undilate(r, d, stage)
            for j in range(B_WIDTH // LANES):
                v = total[:, j * LANES:(j + 1) * LANES]
                if roped:
                    v = rope_t(v) * scale
                o_ref[:, col:col + LANES] = v.astype(BF16)
                col += LANES
        o_ref[:, col:col + C_WIDTH] = dxc_ref[...].astype(BF16)
        o_ref[:, col + C_WIDTH:col + 2 * C_WIDTH] = dgc_ref[...].astype(BF16)

    def row(w):
        return pl.BlockSpec((tm, w), lambda i: (i, 0))

    ins = [dqa, dka2, dva2, *dqb, *dkb, *dvb, dxc, dgc, cos_t, sin_t]
    in_specs = [row(A_WIDTH)] * 3 + [_dilated_spec(tm, d, B_WIDTH) for _ in range(3) for d in DILATIONS]
    in_specs += [row(C_WIDTH)] * 2 + [row(LANES)] * 2
    return pl.pallas_call(
        body, name=name, grid=(T // tm,), in_specs=in_specs,
        out_specs=row(IN_COLS),
        out_shape=jax.ShapeDtypeStruct((T, IN_COLS), BF16),
        scratch_shapes=[_stage(tm, B_WIDTH)],
        compiler_params=_params(("parallel",)),
    )(*ins)


def _band_masks(max_dist):
    row = lax.broadcasted_iota(jnp.int32, (QBLK, 2 * QBLK), 0)
    key = lax.broadcasted_iota(jnp.int32, (QBLK, 2 * QBLK), 1)
    dist = row + QBLK - key
    wide = jnp.logical_and(dist >= 0, dist <= max_dist)
    return wide, wide[:, :QBLK], key >= QBLK


def _head_masks(rows=QBLK):
    lo = _lane_iota((rows, LANES)) < HEAD_DIM
    return lo, jnp.logical_not(lo)


def _keep(hm, x):
    return x * jnp.where(hm, 1.0, 0.0).astype(x.dtype)


def _head_col(x, hm):
    return jnp.max(jnp.where(hm, x, -jnp.inf), axis=1, keepdims=True)


def _attn_specs(R, C):
    chunk = min(ATTN_CHUNK, R)
    nb = chunk // QBLK
    nch = R // chunk
    main = pl.BlockSpec((chunk, LANES), lambda j, c: (c, j))
    prev = pl.BlockSpec((QBLK, LANES), lambda j, c: (jnp.maximum(c * nb - 1, 0), j))
    nxt = pl.BlockSpec((QBLK, LANES), lambda j, c: (jnp.minimum((c + 1) * nb, R // QBLK - 1), j))
    return chunk, nb, nch, main, prev, nxt


def _attn_fwd(q, k, v, max_dist, *, name):
    R, C = q.shape
    chunk, nb, nch, main, prev, _ = _attn_specs(R, C)

    def body(q_ref, k_ref, v_ref, kp_ref, vp_ref, o_ref, lse_ref):
        c = pl.program_id(1)
        wide_mask, _, own_block = _band_masks(max_dist)
        heads = _head_masks()

        def block(q_blk, kk, vv, mask):
            q2 = jnp.concatenate([_keep(hm, q_blk) for hm in heads], axis=0)
            s = jnp.where(jnp.concatenate([mask, mask], axis=0), _dot(q2, kk, NT_DIMS), -jnp.inf)
            m = jnp.max(jnp.maximum(s[:, :QBLK], s[:, QBLK:]), axis=1, keepdims=True)
            p = jnp.exp(s - m)
            l = jnp.sum(p[:, :QBLK] + p[:, QBLK:], axis=1, keepdims=True)
            o2 = _dot(p.astype(BF16), vv) / l
            lse2 = jnp.broadcast_to(m + jnp.log(l), (2 * QBLK, LANES))
            return (jnp.where(heads[0], o2[:QBLK], o2[QBLK:]), jnp.where(heads[0], lse2[:QBLK], lse2[QBLK:]))

        first = pl.ds(0, QBLK)
        o0, l0 = block(q_ref[first, :], jnp.concatenate([kp_ref[...], k_ref[first, :]], axis=0),
                       jnp.concatenate([vp_ref[...], v_ref[first, :]], axis=0),
                       jnp.logical_and(wide_mask, jnp.logical_or(own_block, c > 0)))
        o_ref[first, :] = o0
        lse_ref[first, :] = l0

        def loop(qb, carry):
            cur = pl.ds(pl.multiple_of(qb * QBLK, QBLK), QBLK)
            both = pl.ds(pl.multiple_of((qb - 1) * QBLK, QBLK), 2 * QBLK)
            o, l = block(q_ref[cur, :], k_ref[both, :], v_ref[both, :], wide_mask)
            o_ref[cur, :] = o
            lse_ref[cur, :] = l
            return carry

        if nb > 1:
            lax.fori_loop(1, nb, loop, 0, unroll=ATTN_FWD_UNROLL)

    return pl.pallas_call(
        body, name=name, grid=(C // LANES, nch),
        in_specs=[main, main, main, prev, prev], out_specs=[main, main],
        out_shape=[jax.ShapeDtypeStruct((R, C), F32), jax.ShapeDtypeStruct((R, C), F32)],
        compiler_params=_params(("parallel", "parallel")),
    )(q, k, v, k, v)


def _attn_bwd(q, k, v, do, lse, delta, max_dist, *, name):
    R, C = q.shape
    chunk, nb, nch, main, prev, nxt = _attn_specs(R, C)

    def body(q_ref, k_ref, v_ref, do_ref, lse_ref, dl_ref, kp_ref, vp_ref, qn_ref, don_ref, lsen_ref, dln_ref,
             dq_ref, dk_ref, dv_ref):
        c = pl.program_id(1)
        wide_mask, prev_mask, own_block = _band_masks(max_dist)
        heads = _head_masks()

        def pair(q_blk, do_blk, lse_blk, dl_blk, kk, vv, mask, want_dq=True):
            q2 = jnp.concatenate([_keep(hm, q_blk) for hm in heads], axis=0)
            do2 = jnp.concatenate([_keep(hm, do_blk) for hm in heads], axis=0)
            lse2 = jnp.concatenate([_head_col(lse_blk, hm) for hm in heads], axis=0)
            dl2 = jnp.concatenate([_head_col(dl_blk, hm) for hm in heads], axis=0)
            p = jnp.where(jnp.concatenate([mask, mask], axis=0), jnp.exp(_dot(q2, kk, NT_DIMS) - lse2), 0.0)
            ds = (p * (_dot(do2, vv, NT_DIMS) - dl2)).astype(BF16)
            dq = None
            if want_dq:
                k2 = jnp.concatenate([_keep(khm, kk) for khm in _head_masks(kk.shape[0])], axis=0)
                dq = _dot(jnp.concatenate([ds[:QBLK], ds[QBLK:]], axis=1), k2)
            return dq, _dot(ds, q2, TN_DIMS), _dot(p.astype(BF16), do2, TN_DIMS)

        dk_ref[...] = jnp.zeros_like(dk_ref)
        dv_ref[...] = jnp.zeros_like(dv_ref)

        first = pl.ds(0, QBLK)
        dq0, dkk0, dvv0 = pair(q_ref[first, :], do_ref[first, :], lse_ref[first, :], dl_ref[first, :],
                               jnp.concatenate([kp_ref[...], k_ref[first, :]], axis=0),
                               jnp.concatenate([vp_ref[...], v_ref[first, :]], axis=0),
                               jnp.logical_and(wide_mask, jnp.logical_or(own_block, c > 0)))
        dq_ref[first, :] = dq0
        dk_ref[first, :] += dkk0[QBLK:, :]
        dv_ref[first, :] += dvv0[QBLK:, :]

        def loop(qb, carry):
            cur = pl.ds(pl.multiple_of(qb * QBLK, QBLK), QBLK)
            both = pl.ds(pl.multiple_of((qb - 1) * QBLK, QBLK), 2 * QBLK)
            dq, dkk, dvv = pair(q_ref[cur, :], do_ref[cur, :], lse_ref[cur, :], dl_ref[cur, :],
                                k_ref[both, :], v_ref[both, :], wide_mask)
            dq_ref[cur, :] = dq
            dk_ref[both, :] += dkk
            dv_ref[both, :] += dvv
            return carry

        if nb > 1:
            lax.fori_loop(1, nb, loop, 0, unroll=ATTN_BWD_UNROLL)

        last = pl.ds((nb - 1) * QBLK, QBLK)
        _, dk_n, dv_n = pair(qn_ref[...], don_ref[...], lsen_ref[...], dln_ref[...], k_ref[last, :], v_ref[last, :],
                             jnp.logical_and(prev_mask, c < nch - 1), want_dq=False)
        dk_ref[last, :] += dk_n
        dv_ref[last, :] += dv_n

    return pl.pallas_call(
        body, name=name, grid=(C // LANES, nch),
        in_specs=[main] * 6 + [prev, prev] + [nxt] * 4, out_specs=[main, main, main],
        out_shape=[jax.ShapeDtypeStruct((R, C), F32)] * 3,
        compiler_params=_params(("parallel", "parallel")),
    )(q, k, v, do, lse, delta, k, v, q, do, lse, delta)


def _head_sum(x):
    r = lax.broadcasted_iota(jnp.int32, (LANES, LANES), 0) // HEAD_DIM
    c = lax.broadcasted_iota(jnp.int32, (LANES, LANES), 1) // HEAD_DIM
    ones = jnp.where(r == c, 1.0, 0.0).astype(BF16)
    outs = []
    for j in range(x.shape[1] // LANES):
        rem = x[:, j * LANES:(j + 1) * LANES]
        acc = jnp.zeros(rem.shape, F32)
        for _ in range(3):
            part = rem.astype(BF16)
            acc = acc + _dot(part, ones)
            rem = rem - part.astype(F32)
        outs.append(acc)
    return outs[0] if len(outs) == 1 else jnp.concatenate(outs, axis=1)


def _branch_weights(lses):
    m = functools.reduce(jnp.maximum, lses)
    es = [jnp.exp(l - m) for l in lses]
    den = functools.reduce(lambda a, b: a + b, es)
    return [e / den for e in es]


def _combine_fwd(oa, lsea, sink, obs, lsebs, oc, *, name, tm=512):
    T = oa.shape[0]
    tm = min(tm, T)
    nb = len(obs)

    def body(*refs):
        oa_ref, lsea_ref, sink_ref = refs[:3]
        ob_refs = refs[3:3 + nb]
        lse_refs = refs[3 + nb:3 + 2 * nb]
        oc_ref, out_ref, stage = refs[3 + 2 * nb:]
        out_ref[:, 0:A_WIDTH] = (oa_ref[...] * _sigmoid(lsea_ref[...] - sink_ref[...])).astype(BF16)
        ws = _branch_weights([_undilate(r, d, stage) for r, d in zip(lse_refs, DILATIONS)])
        ob = _undilate(ob_refs[0], DILATIONS[0], stage) * ws[0]
        for r, d, w in zip(ob_refs[1:], DILATIONS[1:], ws[1:]):
            ob = ob + _undilate(r, d, stage) * w
        out_ref[:, A_WIDTH:A_WIDTH + B_WIDTH] = ob.astype(BF16)
        out_ref[:, A_WIDTH + B_WIDTH:MIX_WIDTH] = oc_ref[...].astype(BF16)

    def row(w):
        return pl.BlockSpec((tm, w), lambda i: (i, 0))

    ins = [oa, lsea, sink, *obs, *lsebs, oc]
    in_specs = [row(A_WIDTH), row(A_WIDTH), pl.BlockSpec((1, A_WIDTH), lambda i: (0, 0))]
    in_specs += [_dilated_spec(tm, d, B_WIDTH) for _ in range(2) for d in DILATIONS] + [row(C_WIDTH)]
    return pl.pallas_call(
        body, name=name, grid=(T // tm,), in_specs=in_specs, out_specs=row(MIX_WIDTH),
        out_shape=jax.ShapeDtypeStruct((T, MIX_WIDTH), BF16),
        scratch_shapes=[_stage(tm, B_WIDTH)],
        compiler_params=_params(("parallel",)),
    )(*ins)


def _combine_bwd(dmix, oa, lsea, sink, obs, lsebs, *, name, tm=512):
    T = oa.shape[0]
    tm = min(tm, T)
    nb = len(obs)

    def body(*refs):
        dmix_ref, oa_ref, lsea_ref, sink_ref = refs[:4]
        ob_refs = refs[4:4 + nb]
        lse_refs = refs[4 + nb:4 + 2 * nb]
        outs = refs[4 + 2 * nb:-1]
        stage = refs[-1]
        doa_ref, dla_ref = outs[:2]
        dob_refs = outs[2:2 + nb]
        dlb_refs = outs[2 + nb:2 + 2 * nb]
        doc_ref, dsink_ref = outs[2 + 2 * nb:]

        d_a = dmix_ref[:, 0:A_WIDTH]
        d_b = dmix_ref[:, A_WIDTH:A_WIDTH + B_WIDTH]
        doc_ref[...] = dmix_ref[:, A_WIDTH + B_WIDTH:MIX_WIDTH]

        gate = _sigmoid(lsea_ref[...] - sink_ref[...])
        doa_ref[...] = (d_a * gate).astype(BF16)
        dgate = _head_sum(d_a * oa_ref[...])
        dlse = dgate * gate * (1.0 - gate)
        dla_ref[...] = dgate * gate - dlse

        @pl.when(pl.program_id(0) == 0)
        def _():
            dsink_ref[...] = jnp.zeros_like(dsink_ref)

        dsink_ref[...] -= jnp.sum(dlse, axis=0, keepdims=True)

        ws = _branch_weights([_undilate(r, d, stage) for r, d in zip(lse_refs, DILATIONS)])
        dws = [_head_sum(d_b * _undilate(r, d, stage)) for r, d in zip(ob_refs, DILATIONS)]
        sw = ws[0] * dws[0]
        for w, dw in zip(ws[1:], dws[1:]):
            sw = sw + w * dw
        for w, d, do_ref, dl_ref in zip(ws, DILATIONS, dob_refs, dlb_refs):
            _dilate_into(do_ref, w * d_b, d, stage)
            _dilate_into(dl_ref, w * sw, d, stage)

    def row(w):
        return pl.BlockSpec((tm, w), lambda i: (i, 0))

    vec = pl.BlockSpec((1, A_WIDTH), lambda i: (0, 0))
    dil = [_dilated_spec(tm, d, B_WIDTH) for _ in range(2) for d in DILATIONS]
    ins = [dmix, oa, lsea, sink, *obs, *lsebs]
    in_specs = [row(MIX_WIDTH), row(A_WIDTH), row(A_WIDTH), vec] + dil
    out_specs = [row(A_WIDTH), row(A_WIDTH)] + dil + [row(C_WIDTH), vec]
    out_shape = [jax.ShapeDtypeStruct((T, A_WIDTH), BF16), jax.ShapeDtypeStruct((T, A_WIDTH), F32)]
    out_shape += [jax.ShapeDtypeStruct((T // d, d * B_WIDTH), BF16) for d in DILATIONS]
    out_shape += [jax.ShapeDtypeStruct((T // d, d * B_WIDTH), F32) for d in DILATIONS]
    out_shape += [jax.ShapeDtypeStruct((T, C_WIDTH), F32), jax.ShapeDtypeStruct((1, A_WIDTH), F32)]
    res = pl.pallas_call(
        body, name=name, grid=(T // tm,), in_specs=in_specs, out_specs=out_specs, out_shape=out_shape,
        scratch_shapes=[_stage(tm, B_WIDTH)],
        compiler_params=_params(("arbitrary",)),
    )(*ins)
    return res[0], res[1], list(res[2:2 + nb]), list(res[2 + nb:2 + 2 * nb]), res[2 + 2 * nb], res[3 + 2 * nb]


HIST = 8


def _softplus_neg(lam):
    e = jnp.exp(-jnp.abs(lam))
    log1p = jnp.where(e < 0.01, e * (1.0 - e * (0.5 - e * (1.0 / 3.0))), jnp.log(1.0 + e))
    return jnp.maximum(-lam, 0.0) + log1p


def _neg_expm1(x):
    series = -x * (1.0 + x * (0.5 + x * (1.0 / 6.0 + x * (1.0 / 24.0 + x * (1.0 / 120.0)))))
    return jnp.where(x > -0.1, series, 1.0 - jnp.exp(x))


def _gelu_parts(x):
    k = math.sqrt(2.0 / math.pi)
    t = jnp.tanh(k * (x + 0.044715 * (x * x * x)))
    cdf = 0.5 * (1.0 + t)
    return x * cdf, cdf + 0.5 * x * (1.0 - t * t) * (k * (1.0 + 3.0 * 0.044715 * (x * x)))


def _rglru_gates(y, pos_ref, wr_ref, br_ref, wi_ref, bi_ref, lam_ref):
    yb = y.astype(BF16)
    r = _sigmoid(_dot(yb, wr_ref[...]) + br_ref[...])
    ig = _sigmoid(_dot(yb, wi_ref[...]) + bi_ref[...])
    sp = _softplus_neg(lam_ref[...])
    log_a = -C_EXP * r * sp
    reset = pos_ref[...] == 0
    a = jnp.where(reset, 0.0, jnp.exp(log_a))
    mult = jnp.where(reset, 1.0, jnp.sqrt(_neg_expm1(2.0 * log_a)))
    return yb, r, ig, sp, reset, a, mult


def _conv_fwd(xs_ref, cw_ref, cb_ref, tm):
    y = cb_ref[...] + cw_ref[0:1, :] * xs_ref[HIST:HIST + tm, :]
    for j in range(1, C_CONV):
        y = y + cw_ref[j:j + 1, :] * xs_ref[HIST - j:HIST - j + tm, :]
    return y


SCAN_GROUP = 8


def _blocked_scan(c, d, c_s, d_s, grp_a, grp_h, carry, reverse):
    tm, W = d.shape
    groups = tm // SCAN_GROUP
    order = range(SCAN_GROUP - 1, -1, -1) if reverse else range(SCAN_GROUP)
    outs, lasts = [], []
    for k in range(W // LANES):
        lanes = slice(k * LANES, (k + 1) * LANES)
        ck, dk, ga, gh = c_s.at[k], d_s.at[k], grp_a.at[k], grp_h.at[k]
        ck[...] = c[:, lanes]
        dk[...] = d[:, lanes]
        prod = state = None
        for j in order:
            rows = pl.ds(j, groups, stride=SCAN_GROUP)
            cj, dj = ck[rows, :], dk[rows, :]
            if prod is None:
                prod, state = cj, dj
            else:
                state = cj * state + dj
                prod = cj * prod
            ck[rows, :] = prod
            dk[rows, :] = state
        ga[...] = prod
        gh[...] = state

        def step(i, h, ga=ga, gh=gh):
            row = pl.ds(groups - 1 - i if reverse else i, 1)
            a, t = ga[row, :], gh[row, :]
            ga[row, :] = h
            return a * h + t

        lasts.append(lax.fori_loop(0, groups, step, carry[:, lanes], unroll=8))
        entering = ga[...]
        for j in range(SCAN_GROUP):
            rows = pl.ds(j, groups, stride=SCAN_GROUP)
            dk[rows, :] = dk[rows, :] + ck[rows, :] * entering
        outs.append(dk[...])
    return jnp.concatenate(outs, axis=1), jnp.concatenate(lasts, axis=1)


def _rglru_fwd(xc, gc, pos, cw, cb, wr, br, wi, bi, lam, *, name, tm=512):
    T, W = xc.shape
    tm = min(tm, T)

    def body(xc_ref, gc_ref, pos_ref, cw_ref, cb_ref, wr_ref, br_ref, wi_ref, bi_ref, lam_ref,
             out_ref, hs_ref, xs, a_s, b_s, h_s, grp_a, grp_h):
        @pl.when(pl.program_id(0) == 0)
        def _():
            xs[0:HIST, :] = jnp.zeros((HIST, W), F32)
            h_s[...] = jnp.zeros_like(h_s)

        xv = xc_ref[...]
        xs[HIST:HIST + tm, :] = xv
        y = _conv_fwd(xs, cw_ref, cb_ref, tm)
        xs[0:HIST, :] = xv[tm - HIST:tm, :]
        _, _, ig, _, _, a, mult = _rglru_gates(y, pos_ref, wr_ref, br_ref, wi_ref, bi_ref, lam_ref)
        hs, h_s[...] = _blocked_scan(a, mult * (ig * y), a_s, b_s, grp_a, grp_h, h_s[...], reverse=False)
        hs_ref[...] = hs
        out_ref[...] = hs * _gelu_parts(gc_ref[...])[0]

    row = pl.BlockSpec((tm, W), lambda i: (i, 0))
    full = lambda shape: pl.BlockSpec(shape, lambda i: (0,) * len(shape))
    return pl.pallas_call(
        body, name=name, grid=(T // tm,),
        in_specs=[row, row, pl.BlockSpec((tm, 1), lambda i: (i, 0)), full((C_CONV, W)), full((1, W)),
                  full((W, W)), full((1, W)), full((W, W)), full((1, W)), full((1, W))],
        out_specs=[row, row],
        out_shape=[jax.ShapeDtypeStruct((T, W), F32)] * 2,
        scratch_shapes=[pltpu.VMEM((tm + HIST, W), F32), pltpu.VMEM((W // LANES, tm, LANES), F32),
                        pltpu.VMEM((W // LANES, tm, LANES), F32), pltpu.VMEM((1, W), F32),
                        pltpu.VMEM((W // LANES, tm // SCAN_GROUP, LANES), F32),
                        pltpu.VMEM((W // LANES, tm // SCAN_GROUP, LANES), F32)],
        compiler_params=_params(("arbitrary",)),
    )(xc, gc, pos, cw, cb, wr, br, wi, bi, lam)


def _rglru_bwd(xc, gc, pos, hs, dout, cw, cb, wr, br, wi, bi, lam, *, name, tm=512):
    T, W = xc.shape
    tm = min(tm, T)
    nt = T // tm
    hb = tm // HIST

    def body(xc_ref, gc_ref, pos_ref, hs_ref, dout_ref, xch_ref, hsh_ref,
             cw_ref, cb_ref, wr_ref, br_ref, wi_ref, bi_ref, lam_ref,
             dxc_ref, dgc_ref, dcw_ref, dcb_ref, dwr_ref, dbr_ref, dwi_ref, dbi_ref, dlam_ref,
             xs, hsx, dys, asx, a_s, d_s, carry_s, grp_a, grp_h):
        i = pl.program_id(0)

        @pl.when(i == 0)
        def _():
            for r in (dcw_ref, dcb_ref, dwr_ref, dbr_ref, dwi_ref, dbi_ref, dlam_ref, carry_s):
                r[...] = jnp.zeros_like(r)
            dys[tm:tm + HIST, :] = jnp.zeros((HIST, W), F32)
            asx[tm:tm + HIST, :] = jnp.zeros((HIST, W), F32)

        has_prev = i < nt - 1
        xs[0:HIST, :] = jnp.where(has_prev, xch_ref[...], 0.0)
        hsx[0:HIST, :] = jnp.where(has_prev, hsh_ref[...], 0.0)
        xs[HIST:HIST + tm, :] = xc_ref[...]
        hs = hs_ref[...]
        hsx[HIST:HIST + tm, :] = hs
        y = _conv_fwd(xs, cw_ref, cb_ref, tm)
        yb, r, ig, sp, reset, a, mult = _rglru_gates(y, pos_ref, wr_ref, br_ref, wi_ref, bi_ref, lam_ref)

        gelu, dgelu = _gelu_parts(gc_ref[...])
        dout = dout_ref[...]
        dgc_ref[...] = dout * hs * dgelu
        asx[0:tm, :] = a
        a_up = asx[1:1 + tm, :]
        asx[tm:tm + HIST, :] = a[0:HIST, :]
        dh, carry_s[...] = _blocked_scan(a_up, dout * gelu, a_s, d_s, grp_a, grp_h, carry_s[...], reverse=True)
        hprev = hsx[HIST - 1:HIST - 1 + tm, :]
        igy = ig * y
        dmult = dh * igy
        digy = dh * mult
        dlog_a = jnp.where(reset, 0.0, dh * hprev * a - dmult * a * a / mult)
        dlam_ref[...] += jnp.sum(dlog_a * (C_EXP * r) * _sigmoid(-lam_ref[...]), axis=0, keepdims=True)
        dz_r = dlog_a * (-C_EXP * sp) * r * (1.0 - r)
        dz_i = digy * y * ig * (1.0 - ig)
        dzr_b = dz_r.astype(BF16)
        dzi_b = dz_i.astype(BF16)
        dy = digy * ig + _dot(dzr_b, wr_ref[...], NT_DIMS) + _dot(dzi_b, wi_ref[...], NT_DIMS)
        dwr_ref[...] += _dot(yb, dzr_b, TN_DIMS)
        dwi_ref[...] += _dot(yb, dzi_b, TN_DIMS)
        dbr_ref[...] += jnp.sum(dz_r, axis=0, keepdims=True)
        dbi_ref[...] += jnp.sum(dz_i, axis=0, keepdims=True)

        dys[0:tm, :] = dy
        dxc = cw_ref[0:1, :] * dy
        for j in range(1, C_CONV):
            dxc = dxc + cw_ref[j:j + 1, :] * dys[j:j + tm, :]
        dxc_ref[...] = dxc
        dys[tm:tm + HIST, :] = dy[0:HIST, :]
        dcb_ref[...] += jnp.sum(dy, axis=0, keepdims=True)
        for j in range(C_CONV):
            dcw_ref[j:j + 1, :] += jnp.sum(dy * xs[HIST - j:HIST - j + tm, :], axis=0, keepdims=True)

    row = pl.BlockSpec((tm, W), lambda i: (nt - 1 - i, 0))
    halo = pl.BlockSpec((HIST, W), lambda i: (jnp.maximum((nt - 1 - i) * hb - 1, 0), 0))
    full = lambda shape: pl.BlockSpec(shape, lambda i: (0,) * len(shape))
    out_specs = [row, row, full((C_CONV, W)), full((1, W)), full((W, W)), full((1, W)), full((W, W)), full((1, W)),
                 full((1, W))]
    out_shape = [jax.ShapeDtypeStruct((T, W), F32)] * 2
    out_shape += [jax.ShapeDtypeStruct(s, F32) for s in ((C_CONV, W), (1, W), (W, W), (1, W), (W, W), (1, W), (1, W))]
    return pl.pallas_call(
        body, name=name, grid=(nt,),
        in_specs=[row, row, pl.BlockSpec((tm, 1), lambda i: (nt - 1 - i, 0)), row, row, halo, halo,
                  full((C_CONV, W)), full((1, W)), full((W, W)), full((1, W)), full((W, W)), full((1, W)),
                  full((1, W))],
        out_specs=out_specs, out_shape=out_shape,
        scratch_shapes=[pltpu.VMEM((tm + HIST, W), F32), pltpu.VMEM((tm + HIST, W), F32),
                        pltpu.VMEM((tm + HIST, W), F32), pltpu.VMEM((tm + HIST, W), F32),
                        pltpu.VMEM((W // LANES, tm, LANES), F32), pltpu.VMEM((W // LANES, tm, LANES), F32),
                        pltpu.VMEM((1, W), F32), pltpu.VMEM((W // LANES, tm // SCAN_GROUP, LANES), F32),
                        pltpu.VMEM((W // LANES, tm // SCAN_GROUP, LANES), F32)],
        compiler_params=_params(("arbitrary",)),
    )(xc, gc, pos, hs, dout, xc, hs, cw, cb, wr, br, wi, bi, lam)


def _adam_math(w, g, m, v):
    m = ADAM_B1 * m + (1.0 - ADAM_B1) * g
    v = ADAM_B2 * v + (1.0 - ADAM_B2) * (g * g)
    m_hat = m / (1.0 - ADAM_B1 ** ADAM_STEP)
    v_hat = v / (1.0 - ADAM_B2 ** ADAM_STEP)
    delta = -ADAM_LR * (m_hat / (jnp.sqrt(v_hat) + ADAM_EPS) + ADAM_WD * w)
    return delta, m, v


def _pick_rows(R, cap=512, mult=16):
    for d in range(min(cap, R), 0, -1):
        if R % d == 0 and d % mult == 0:
            return d
    return R


def _adamw(parts, w, m, v, *, name, tr=None, part=0, prev=None):
    R, C = w.shape
    r = parts.shape[1]
    tr = _pick_rows(r) if tr is None else tr
    assert r % tr == 0 and R % r == 0, (name, R, r, tr)
    nt = r // tr

    def body(p_ref, w_ref, m_ref, v_ref, *rest):
        g_ref, d_ref, nm_ref, nv_ref = rest[-4:]
        g = p_ref[0].astype(F32)
        for d in range(1, N_DEV):
            g = g + p_ref[d].astype(F32)
        delta, nm, nv = _adam_math(w_ref[...], g, m_ref[...], v_ref[...])
        g_ref[...] = g
        d_ref[...] = delta
        nm_ref[...] = nm
        nv_ref[...] = nv

    row = pl.BlockSpec((tr, C), lambda i: (part * nt + i, 0))
    in_specs = [pl.BlockSpec((N_DEV, tr, C), lambda i: (0, i, 0)), row, row, row]
    operands = [parts, w, m, v]
    aliases = {}
    if prev is not None:
        in_specs += [pl.BlockSpec(memory_space=pl.ANY)] * 4
        operands += list(prev)
        aliases = {4 + i: i for i in range(4)}
    return pl.pallas_call(
        body, name=name, grid=(nt,), in_specs=in_specs,
        out_specs=[row] * 4, out_shape=[jax.ShapeDtypeStruct((R, C), F32)] * 4,
        input_output_aliases=aliases,
        compiler_params=_params(("parallel",)),
    )(*operands)


def _exchange(srcs, gather, *, name):
    n = len(srcs)
    out_shape = [jax.ShapeDtypeStruct((N_DEV,) + s.shape if gather else s.shape, s.dtype) for s in srcs]

    def body(*refs):
        ins, outs = refs[:n], refs[n:2 * n]
        send_sems, recv_sems, local_sems = refs[2 * n:]
        x, y, c = lax.axis_index("x"), lax.axis_index("y"), lax.axis_index("c")
        me = 4 * x + 2 * y + c
        local_copies, sends, arrivals = [], [], []
        for a in range(n):
            mine = ins[a] if gather else ins[a].at[me]
            local = pltpu.make_async_copy(mine, outs[a].at[me], local_sems.at[a])
            local.start()
            local_copies.append(local)
            for k in range(1, N_DEV):
                px, py, pc = x ^ ((k >> 2) & 1), y ^ ((k >> 1) & 1), c ^ (k & 1)
                peer = 4 * px + 2 * py + pc
                send = pltpu.make_async_remote_copy(
                    src_ref=ins[a] if gather else ins[a].at[peer], dst_ref=outs[a].at[me],
                    send_sem=send_sems.at[a * N_DEV + k], recv_sem=recv_sems.at[a * N_DEV + k],
                    device_id=(px, py, pc), device_id_type=pl.DeviceIdType.MESH)
                send.start()
                sends.append(send)
                arrivals.append(pltpu.make_async_remote_copy(
                    src_ref=mine, dst_ref=outs[a].at[peer],
                    send_sem=send_sems.at[a * N_DEV + k], recv_sem=recv_sems.at[a * N_DEV + k],
                    device_id=(px, py, pc), device_id_type=pl.DeviceIdType.MESH))
        for cp in sends:
            cp.wait_send()
        for cp in arrivals:
            cp.wait_recv()
        for cp in local_copies:
            cp.wait()

    return pl.pallas_call(
        body, name=name,
        in_specs=[pl.BlockSpec(memory_space=pl.ANY)] * n, out_specs=[pl.BlockSpec(memory_space=pl.ANY)] * n,
        out_shape=out_shape,
        scratch_shapes=[pltpu.SemaphoreType.DMA((n * N_DEV,)), pltpu.SemaphoreType.DMA((n * N_DEV,)),
                        pltpu.SemaphoreType.DMA((n,))],
    )(*srcs)


_HBM = pl.BlockSpec(memory_space=pltpu.HBM)
_SEM = pl.BlockSpec(memory_space=pltpu.SEMAPHORE)
_EFFECT = pltpu.SideEffectType.DATAFLOW_SIDE_EFFECTING


def _peers():
    x, y, c = lax.axis_index("x"), lax.axis_index("y"), lax.axis_index("c")
    out = []
    for k in range(1, N_DEV):
        px, py, pc = x ^ ((k >> 2) & 1), y ^ ((k >> 1) & 1), c ^ (k & 1)
        out.append((k, (px, py, pc), 4 * px + 2 * py + pc))
    return 4 * x + 2 * y + c, out


def _split_copies(src_refs, land_refs, send_sems, recv_sems, gather):
    me, peers = _peers()
    out = []
    for a, (src_ref, land_ref) in enumerate(zip(src_refs, land_refs)):
        for k, dev, blk in peers:
            common = dict(send_sem=send_sems.at[a * N_DEV + k], recv_sem=recv_sems.at[a * N_DEV + k], device_id=dev,
                          device_id_type=pl.DeviceIdType.MESH)
            src = src_ref if gather else src_ref.at[blk]
            out.append((pltpu.make_async_remote_copy(src_ref=src, dst_ref=land_ref.at[me], **common),
                        pltpu.make_async_remote_copy(src_ref=src, dst_ref=land_ref.at[blk], **common)))
    return out


def _exchange_start(srcs, gather, *, name, after=None):
    n = len(srcs)
    lands = [lax.empty((N_DEV,) + (s.shape if gather else s.shape[1:]), s.dtype) for s in srcs]

    def body(*refs):
        src_refs, land_refs = refs[:n], refs[n:2 * n]
        send_sems, recv_sems = refs[-2 * n - 3:-2 * n - 1]
        token = refs[-1]
        for outgoing, _ in _split_copies(src_refs, land_refs, send_sems, recv_sems, gather):
            outgoing.start()
        token[...] = jnp.zeros_like(token)

    res = pl.pallas_call(
        body, name=name,
        out_shape=(pltpu.SemaphoreType.DMA((n * N_DEV,)), pltpu.SemaphoreType.DMA((n * N_DEV,)),
                   *[pltpu.HBM(a.shape, a.dtype) for a in srcs + lands], jax.ShapeDtypeStruct((8, LANES), F32)),
        in_specs=(_HBM,) * (2 * n) + ((pl.BlockSpec(memory_space=pl.ANY),) if after is not None else ()),
        out_specs=(_SEM, _SEM) + (_HBM,) * (2 * n) + (pl.BlockSpec(memory_space=pltpu.VMEM),),
        input_output_aliases={i: i + 2 for i in range(2 * n)},
        compiler_params=pltpu.CompilerParams(has_side_effects=_EFFECT),
    )(*[pltpu.with_memory_space_constraint(a, pltpu.HBM) for a in srcs + lands],
      *([after] if after is not None else []))
    return res[0], res[1], list(res[2:2 + n]), list(res[2 + n:2 + 2 * n]), res[-1]


def _exchange_wait(started, after, gather, *, name):
    send_sems, recv_sems, srcs, lands, _ = started
    n = len(srcs)

    def body(*refs):
        src_refs, land_refs = refs[:n], refs[n:2 * n]
        send_sems, recv_sems = refs[2 * n:2 * n + 2]
        for outgoing, incoming in _split_copies(src_refs, land_refs, send_sems, recv_sems, gather):
            outgoing.wait_send()
            incoming.wait_recv()

    res = pl.pallas_call(
        body, name=name,
        out_shape=tuple(pltpu.HBM(a.shape, a.dtype) for a in srcs + lands),
        in_specs=(_HBM,) * (2 * n) + (_SEM, _SEM, pl.BlockSpec(memory_space=pl.ANY)), out_specs=(_HBM,) * (2 * n),
        input_output_aliases={i: i for i in range(2 * n)},
        compiler_params=pltpu.CompilerParams(has_side_effects=_EFFECT),
    )(*srcs, *lands, send_sems, recv_sems, after)
    return list(res[:n]), list(res[n:])


def _cols_to_blocks(g, *, name, tr=128):
    R, C = g.shape
    w = C // N_DEV
    tr = min(tr, R)

    def body(g_ref, o_ref):
        for p in range(N_DEV):
            o_ref[p] = g_ref[:, p * w:(p + 1) * w].astype(BF16)

    return pl.pallas_call(
        body, name=name, grid=(R // tr,),
        in_specs=[pl.BlockSpec((tr, C), lambda i: (i, 0))],
        out_specs=pl.BlockSpec((N_DEV, tr, w), lambda i: (0, i, 0)),
        out_shape=jax.ShapeDtypeStruct((N_DEV, R, w), BF16),
        compiler_params=_params(("parallel",)),
    )(g)


def _blocks_to_cols(b, *, name, tr=128):
    _, R, w = b.shape
    tr = min(tr, R)

    def body(b_ref, o_ref):
        o_ref[...] = jnp.concatenate([b_ref[p].astype(F32) for p in range(N_DEV)], axis=1).astype(o_ref.dtype)

    return pl.pallas_call(
        body, name=name, grid=(R // tr,),
        in_specs=[pl.BlockSpec((N_DEV, tr, w), lambda i: (0, i, 0))],
        out_specs=pl.BlockSpec((tr, N_DEV * w), lambda i: (i, 0)),
        out_shape=jax.ShapeDtypeStruct((R, N_DEV * w), b.dtype),
        compiler_params=_params(("parallel",)),
    )(b)


def _pack_rows(arrays, *, name, pick=None):
    B, _, w = arrays[0].shape
    rows = [a.shape[1] for a in arrays]
    first = 0
    if pick is not None:
        B, first = 1, pick

    def body(*refs):
        o_ref = refs[-1]
        r = 0
        for a_ref, n in zip(refs[:-1], rows):
            o_ref[0, r:r + n, :] = a_ref[0].astype(BF16)
            r += n

    return pl.pallas_call(
        body, name=name, grid=(B,),
        in_specs=[pl.BlockSpec((1, n, w), lambda b: (first + b, 0, 0)) for n in rows],
        out_specs=pl.BlockSpec((1, sum(rows), w), lambda b: (b, 0, 0)),
        out_shape=jax.ShapeDtypeStruct((B, sum(rows), w), BF16),
        compiler_params=_params(("parallel",)),
    )(*arrays)


def _unpack_rows(land, src, rows, *, name):
    _, R, w = land.shape
    src_spec = (pl.BlockSpec((1, R, w), lambda p: (p, 0, 0)) if src.ndim == 3
                else pl.BlockSpec((R, w), lambda p: (0, 0)))

    def body(land_ref, src_ref, *o_refs):
        me = 4 * lax.axis_index("x") + 2 * lax.axis_index("y") + lax.axis_index("c")
        mine = pl.program_id(0) == me
        r = 0
        for o_ref, n in zip(o_refs, rows):
            rows_i = slice(r, r + n)

            @pl.when(mine)
            def _(o_ref=o_ref, rows_i=rows_i):
                o_ref[0] = src_ref[0, rows_i, :] if src.ndim == 3 else src_ref[rows_i, :]

            @pl.when(jnp.logical_not(mine))
            def _(o_ref=o_ref, rows_i=rows_i):
                o_ref[0] = land_ref[0, rows_i, :]

            r += n

    return pl.pallas_call(
        body, name=name, grid=(N_DEV,),
        in_specs=[pl.BlockSpec((1, R, w), lambda p: (p, 0, 0)), src_spec],
        out_specs=[pl.BlockSpec((1, n, w), lambda p: (p, 0, 0)) for n in rows],
        out_shape=[jax.ShapeDtypeStruct((N_DEV, n, w), land.dtype) for n in rows],
        compiler_params=_params(("parallel",)),
    )(land, src)


def _to_blocks(w, axis):
    shape = w.shape
    k = shape[axis] // N_DEV
    w = w.reshape(shape[:axis] + (N_DEV, k) + shape[axis + 1:])
    return jnp.moveaxis(w, axis, 0)


def _from_blocks(wb, axis):
    w = jnp.moveaxis(wb, 0, axis)
    shape = w.shape
    return w.reshape(shape[:axis] + (shape[axis] * shape[axis + 1],) + shape[axis + 2:])


def _block_diag(w):
    n, k, _ = w.shape
    eye = jnp.eye(n, dtype=w.dtype)
    return (eye[:, None, :, None] * w[:, :, None, :]).reshape(n * k, n * k)


def _diag_blocks(wd):
    k = HEAD_DIM
    return jnp.stack([wd[h * k:(h + 1) * k, h * k:(h + 1) * k] for h in range(C_BLOCKS)])


def _pack(arrays):
    rows = []
    for a in arrays:
        flat = a.reshape(-1).astype(F32)
        pad = (-flat.shape[0]) % LANES
        rows.append(jnp.pad(flat, (0, pad)).reshape(-1, LANES))
    out = jnp.concatenate(rows, axis=0)
    return jnp.pad(out, ((0, (-out.shape[0]) % 8), (0, 0)))


def _unpack(packed, shapes):
    outs, r = [], 0
    for s in shapes:
        size = math.prod(s)
        nrows = -(-size // LANES)
        outs.append(packed[r:r + nrows].reshape(-1)[:size].reshape(s))
        r += nrows
    return outs


def _rope_tables(positions):
    inv = 1.0 / (ROPE_THETA ** (jnp.arange(0, HEAD_DIM, 2, dtype=F32) / HEAD_DIM))
    ang = positions.astype(F32)[:, None] * inv
    cos, sin = jnp.cos(ang), jnp.sin(ang)
    return jnp.tile(cos, (1, 4)), jnp.tile(jnp.concatenate([-sin, sin], axis=1), (1, 2))


def _layer_fwd(l, x, pos, cos_t, sin_t, W, before_mixer=None):
    tag = f"l{l}"
    saved = {'x0': x}
    x1, a1, u1 = _ffn_fwd(x, W['norm_ffn1'][l], W['ffn1_gate'][l], W['ffn1_up'][l], W['ffn1_down'][l],
                          name=f"ffn1_fwd_{tag}")
    if before_mixer is not None:
        before_mixer(l, x1)
    h = _rms_fwd(x1, W['norm_mix'][l], name=f"mixnorm_fwd_{tag}")
    proj = _mm(h, W['w_in'][l], 'nn', name=f"proj_{tag}", tm=512, tn=IN_COLS, tk=h.shape[1])
    qa, ka2, va2, qb, kb, vb, xc, gc = _split_rope(proj, cos_t, sin_t, name=f"split_{tag}")
    oa, lsea = _attn_fwd(qa, ka2, va2, A_MAX_DIST, name=f"attn_a_fwd_{tag}")
    obs, lsebs = [], []
    for bi, (window, d) in enumerate(B_BRANCHES):
        o, lse = _attn_fwd(qb[bi], kb[bi], vb[bi], window // d, name=f"attn_b{bi}_fwd_{tag}")
        obs.append(o)
        lsebs.append(lse)
    oc, hs = _rglru_fwd(xc, gc, pos, W['conv_w'][l], W['conv_b'][l], W['rg_w_r'][l], W['rg_b_r'][l],
                        W['rg_w_i'][l], W['rg_b_i'][l], W['rg_lambda'][l], name=f"rglru_fwd_{tag}")
    mix = _combine_fwd(oa, lsea, W['sinks'][l], obs, lsebs, oc, name=f"combine_fwd_{tag}")
    x2 = _mm(mix, W['w_out'][l], 'nn', name=f"outproj_{tag}", tm=512, tn=x.shape[1], tk=MIX_WIDTH, res=x1)
    x3, a2, u2 = _ffn_fwd(x2, W['norm_ffn2'][l], W['ffn2_gate'][l], W['ffn2_up'][l], W['ffn2_down'][l],
                          name=f"ffn2_fwd_{tag}")
    saved.update(a1=a1, u1=u1, x1=x1, h=h, qa=qa, ka2=ka2, va2=va2, qb=qb, kb=kb, vb=vb, xc=xc, gc=gc, oa=oa,
                 lsea=lsea, obs=obs, lsebs=lsebs, hs=hs, mix=mix, x2=x2, a2=a2, u2=u2)
    return x3, saved


def _ffn_grads(tag, which, x, g, dy, a, u, wg, wu, wd):
    T, D = x.shape
    F = wg.shape[1]
    dx, dg, n, act, da, du = _ffn_bwd(x, g, dy, a, u, wg, wu, wd, name=f"{which}_bwd_{tag}")
    fc = _ffn_chunk(F)
    d_gate = _mm(n, da, 'tn', name=f"{which}_dgate_{tag}", tm=512, tn=fc, tk=4096)
    d_up = _mm(n, du, 'tn', name=f"{which}_dup_{tag}", tm=512, tn=fc, tk=4096)
    d_down = _mm(act, dy, 'tn', name=f"{which}_ddown_{tag}", tm=fc, tn=D, tk=1024, alpha=0.5)
    return dx, dg, d_gate, d_up, d_down


def _layer_bwd(l, dx3, pos, cos_t, sin_t, W, S, on_grads=None):
    tag = f"l{l}"
    G = {}
    dx2, G['norm_ffn2'], G['ffn2_gate'], G['ffn2_up'], G['ffn2_down'] = _ffn_grads(
        tag, 'ffn2', S['x2'], W['norm_ffn2'][l], dx3, S['a2'], S['u2'], W['ffn2_gate'][l], W['ffn2_up'][l],
        W['ffn2_down'][l])
    D = dx2.shape[1]
    dmix = _mm(dx2, W['w_out'][l], 'nt', name=f"outproj_dx_{tag}", tm=512, tn=MIX_WIDTH, tk=D)
    G['w_out'] = _mm(S['mix'], dx2, 'tn', name=f"outproj_dw_{tag}", tm=MIX_WIDTH, tn=D, tk=2048)
    doa, dla, dobs, dlbs, doc, dsink = _combine_bwd(dmix, S['oa'], S['lsea'], W['sinks'][l], S['obs'], S['lsebs'],
                                                    name=f"combine_bwd_{tag}")
    G['attn_sinks'] = dsink.reshape(A_WIDTH // HEAD_DIM, HEAD_DIM)[:, 0]
    dqa, dka2, dva2 = _attn_bwd(S['qa'], S['ka2'], S['va2'], doa, S['lsea'], dla, A_MAX_DIST,
                                name=f"attn_a_bwd_{tag}")
    dqb, dkb, dvb = [], [], []
    for bi, (window, d) in enumerate(B_BRANCHES):
        dq, dk, dv = _attn_bwd(S['qb'][bi], S['kb'][bi], S['vb'][bi], dobs[bi], S['lsebs'][bi], dlbs[bi], window // d,
                               name=f"attn_b{bi}_bwd_{tag}")
        dqb.append(dq)
        dkb.append(dk)
        dvb.append(dv)
    (dxc, dgc, G['conv_w'], G['conv_b'], dwr, G['rg_b_r'], dwi, G['rg_b_i'], G['rg_lambda']) = _rglru_bwd(
        S['xc'], S['gc'], pos, S['hs'], doc, W['conv_w'][l], W['conv_b'][l], W['rg_w_r'][l], W['rg_b_r'][l],
        W['rg_w_i'][l], W['rg_b_i'][l], W['rg_lambda'][l], name=f"rglru_bwd_{tag}")
    G['rg_w_r'] = _diag_blocks(dwr)
    G['rg_w_i'] = _diag_blocks(dwi)
    dproj = _merge_dproj(dqa, dka2, dva2, dqb, dkb, dvb, dxc, dgc, cos_t, sin_t, name=f"merge_{tag}")
    dh = _mm(dproj, W['w_in'][l], 'nt', name=f"proj_dx_{tag}", tm=512, tn=D, tk=IN_COLS)
    G['w_in'] = _mm(S['h'], dproj, 'tn', name=f"proj_dw_{tag}", tm=512, tn=IN_COLS, tk=2048)
    g_mix = W['norm_mix'][l]
    if on_grads is not None:
        g_mix = g_mix + on_grads(l, 0, G)
    dx1, G['norm_mix'] = _rms_bwd(S['x1'], g_mix, dh, dx2, name=f"mixnorm_bwd_{tag}")
    dx0, G['norm_ffn1'], G['ffn1_gate'], G['ffn1_up'], G['ffn1_down'] = _ffn_grads(
        tag, 'ffn1', S['x0'], W['norm_ffn1'][l], dx1, S['a1'], S['u1'], W['ffn1_gate'][l], W['ffn1_up'][l],
        W['ffn1_down'][l])
    if on_grads is not None:
        on_grads(l, 1, G)
    return dx0, G


def _device_step(x, positions, loss_target, W, before_layer=None, on_grads=None, before_mixer=None):
    T = x.shape[0]
    pos = positions.reshape(T, 1)
    cos_t, sin_t = _rope_tables(positions)
    saved = []
    for l in range(DEPTH):
        if before_layer is not None:
            before_layer(l, x)
        x, S = _layer_fwd(l, x, pos, cos_t, sin_t, W, before_mixer)
        saved.append(S)
    loss, dx, dg_final = _loss_head(x, W['norm_final'], loss_target, name="loss_head")
    grads = [None] * DEPTH
    for l in reversed(range(DEPTH)):
        dx, grads[l] = _layer_bwd(l, dx, pos, cos_t, sin_t, W, saved[l], on_grads)
    return loss, dx, grads, dg_final


SHARD_AXIS = {'ffn1_gate': 2, 'ffn1_up': 2, 'ffn1_down': 1, 'w_in': 2, 'w_out': 1, 'ffn2_gate': 2, 'ffn2_up': 2,
              'ffn2_down': 1, 'conv_w': 2}


def kernel(x, positions, norm_ffn1, ffn1_gate, ffn1_up, ffn1_down, norm_mix, w_in, attn_sinks, conv_w, conv_b, rg_w_r, rg_b_r, rg_w_i, rg_b_i, rg_lambda, w_out, norm_ffn2, ffn2_gate, ffn2_up, ffn2_down, norm_final, loss_target, m_norm_ffn1, m_ffn1_gate, m_ffn1_up, m_ffn1_down, m_norm_mix, m_w_in, m_attn_sinks, m_conv_w, m_conv_b, m_rg_w_r, m_rg_b_r, m_rg_w_i, m_rg_b_i, m_rg_lambda, m_w_out, m_norm_ffn2, m_ffn2_gate, m_ffn2_up, m_ffn2_down, m_norm_final, v_norm_ffn1, v_ffn1_gate, v_ffn1_up, v_ffn1_down, v_norm_mix, v_w_in, v_attn_sinks, v_conv_w, v_conv_b, v_rg_w_r, v_rg_b_r, v_rg_w_i, v_rg_b_i, v_rg_lambda, v_w_out, v_norm_ffn2, v_ffn2_gate, v_ffn2_up, v_ffn2_down, v_norm_final):
    given = dict(locals())
    me = 4 * lax.axis_index("x") + 2 * lax.axis_index("y") + lax.axis_index("c")

    def by_width(names):
        classes = {}
        for n in names:
            classes.setdefault(given[n].shape[2], []).append(n)
        return list(classes.values())

    def pack(names, get, tag, pick=None):
        return [_pack_rows([get(n) for n in cls], name=f"pack{ci}_{tag}", pick=pick)
                for ci, cls in enumerate(by_width(names))]

    def unpack(names, lands, srcs, tag):
        out = {}
        for ci, cls in enumerate(by_width(names)):
            arrays = _unpack_rows(lands[ci], srcs[ci], [given[n].shape[1] for n in cls], name=f"unpack{ci}_{tag}")
            out.update(zip(cls, arrays))
        return out

    def gather_start(l, names, tag, after=None):
        return _exchange_start([p[0] for p in pack(names, lambda n: given[n], f"w{tag}_l{l}", pick=l)], True,
                               name=f"gather_start{tag}_l{l}", after=after)

    def gather_wait(l, names, tag, started, after):
        srcs, lands = _exchange_wait(started, after, True, name=f"gather_wait{tag}_l{l}")
        blocks = unpack(names, lands, srcs, f"w{tag}_l{l}")
        for n in names:
            if SHARD_AXIS[n] == 2:
                W[n][l] = _blocks_to_cols(blocks[n], name=f"cols_{n}_l{l}")
            else:
                W[n][l] = blocks[n].reshape(-1, blocks[n].shape[2])
        return lands[0]

    W = {n: [None] * DEPTH for n in BIG_NAMES}
    ffn1_names, later_names = SCATTER_STAGES[1], SCATTER_STAGES[0]
    first = gather_start(0, ffn1_names, "a")
    conv_full = _exchange([conv_w], True, name="gather_conv_w")[0]
    landed = gather_wait(0, ffn1_names, "a", first, conv_full)
    second = gather_start(0, later_names, "b", after=landed)
    started = second[4][0, 0]
    gathers = [None] * DEPTH

    def before_layer(l, x_in):
        if l > 0:
            gather_wait(l, BIG_NAMES, "", gathers[l], x_in)

    def before_mixer(l, x1):
        if l == 0:
            landed = gather_wait(0, later_names, "b", second, x1)
            token = 0.0
            for k in range(1, DEPTH):
                gathers[k] = gather_start(k, BIG_NAMES, "", after=landed)
                token = token + gathers[k][4][0, 0]
            W['norm_mix'][0] = W['norm_mix'][0] + token

    scatters = {}

    def on_grads(l, stage, G):
        def blocks_of(n):
            if SHARD_AXIS[n] == 2:
                return _cols_to_blocks(G[n], name=f"blocks_{n}_l{l}")
            return G[n].reshape(N_DEV, -1, G[n].shape[1])

        scatters[l, stage] = _exchange_start(pack(SCATTER_STAGES[stage], blocks_of, f"g{stage}_l{l}"), False,
                                             name=f"scatter_start{stage}_l{l}")
        token = scatters[l, stage][4][0, 0]
        if stage == 1 and l > 0:
            W['norm_ffn2'][l - 1] = W['norm_ffn2'][l - 1] + token
        return token

    W['conv_w'] = [_from_blocks(conv_full[:, l], 1) for l in range(DEPTH)]
    for n in ('norm_ffn1', 'norm_mix', 'norm_ffn2', 'conv_b', 'rg_lambda'):
        W[n] = [given[n][l][None, :] for l in range(DEPTH)]
    W['norm_final'] = norm_final[None, :]
    W['sinks'] = [jnp.repeat(attn_sinks[l], HEAD_DIM)[None, :] for l in range(DEPTH)]
    for n in ('rg_w_r', 'rg_w_i'):
        W[n] = [_block_diag(given[n][l]).astype(BF16) for l in range(DEPTH)]
    for n in ('rg_b_r', 'rg_b_i'):
        W[n] = [given[n][l].reshape(1, C_WIDTH) for l in range(DEPTH)]

    W['norm_ffn1'][0] = W['norm_ffn1'][0] + started

    loss_part, grad_x, grads, dg_final = _device_step(x[0], positions[0], loss_target[0], W, before_layer, on_grads,
                                                      before_mixer)
    loss = lax.psum(loss_part[0, 0], ("x", "y", "c"))

    small_shapes = [given[n].shape for n in SMALL_NAMES] + [(DEPTH, C_CONV, C_WIDTH)]
    small_grads = []
    for n in SMALL_NAMES:
        if n == 'norm_final':
            small_grads.append(dg_final.reshape(-1))
        else:
            small_grads.append(jnp.stack([grads[l][n].reshape(given[n].shape[1:]) for l in range(DEPTH)]))
    small_grads.append(jnp.stack([grads[l]['conv_w'] for l in range(DEPTH)]))
    small_parts = _exchange([_pack(small_grads)], True, name="gather_small_grads")[0]

    out = {}
    for stage in (0, 1):
        parts = {}
        for l in reversed(range(DEPTH)):
            last = stage == 1 and l == 0
            srcs, lands = _exchange_wait(scatters[l, stage], out['w_in'][1] if last else grad_x, False,
                                         name=f"scatter_wait{stage}_l{l}")
            parts[l] = unpack(SCATTER_STAGES[stage], lands, srcs, f"g{stage}_l{l}")
        for n in SCATTER_STAGES[stage]:
            shape = given[n].shape
            two_d = (shape[0] * shape[1], shape[2])
            res = None
            for l in reversed(range(DEPTH)):
                res = _adamw(parts[l][n], given[n].reshape(two_d), given['m_' + n].reshape(two_d),
                             given['v_' + n].reshape(two_d), name=f"adamw_{n}_l{l}", part=l, prev=res)
            out[n] = [r.reshape(shape) for r in res]

    w_small = [given[n] for n in SMALL_NAMES]
    m_small = [given['m_' + n] for n in SMALL_NAMES]
    v_small = [given['v_' + n] for n in SMALL_NAMES]
    zeros_cw = jnp.zeros((DEPTH, C_CONV, C_WIDTH), F32)
    res = _adamw(small_parts, _pack(w_small + [zeros_cw]), _pack(m_small + [zeros_cw]), _pack(v_small + [zeros_cw]),
                 name="adamw_small", tr=8)
    unpacked = [_unpack(r, small_shapes) for r in res]
    for i, n in enumerate(SMALL_NAMES):
        out[n] = [u[i] for u in unpacked]

    k = conv_w.shape[2]
    g_cw = lax.dynamic_slice_in_dim(unpacked[0][-1], me * k, k, axis=2)
    zero_parts = jnp.zeros((N_DEV - 1,) + (8, LANES), F32)
    res = _adamw(jnp.concatenate([_pack([g_cw])[None], zero_parts]), _pack([conv_w]), _pack([m_conv_w]),
                 _pack([v_conv_w]), name="adamw_conv_w", tr=8)
    out['conv_w'] = [_unpack(r, [conv_w.shape])[0] for r in res]

    outputs = [loss, grad_x[None]]
    for i in range(4):
        outputs += [out[n][i] for n in WEIGHT_NAMES]
    return tuple(outputs)
```

```python
import functools
import math

import jax
import jax.numpy as jnp
from jax import lax
from jax.experimental import pallas as pl
from jax.experimental.pallas import tpu as pltpu

F32 = jnp.float32
BF16 = jnp.bfloat16

N_DEV = 8
DEPTH = 4
HEAD_DIM = 64
LANES = 128
QBLK = 128
A_WIDTH = 256
A_KV_WIDTH = 128
B_WIDTH = 384
C_WIDTH = 384
C_BLOCKS = 6
C_CONV = 4
C_EXP = 8.0
MIX_WIDTH = A_WIDTH + B_WIDTH + C_WIDTH
IN_COLS = A_WIDTH + 2 * A_KV_WIDTH + 3 * B_WIDTH + 2 * C_WIDTH
A_MAX_DIST = 127
B_BRANCHES = ((128, 1), (512, 4), (2048, 16))
ROPE_THETA = 10000.0
EPS = 1e-6
SCALE = HEAD_DIM ** -0.5

ADAM_LR = 0.001
ADAM_B1 = 0.9
ADAM_B2 = 0.999
ADAM_EPS = 1e-08
ADAM_WD = 0.01
ADAM_STEP = 10

ATTN_CHUNK = 2048
ATTN_FWD_UNROLL = True
ATTN_BWD_UNROLL = True
VMEM_LIMIT = 56 * 1024 * 1024

NT_DIMS = (((1,), (1,)), ((), ()))
TN_DIMS = (((0,), (0,)), ((), ()))
NN_DIMS = (((1,), (0,)), ((), ()))

WEIGHT_NAMES = ['norm_ffn1', 'ffn1_gate', 'ffn1_up', 'ffn1_down', 'norm_mix', 'w_in', 'attn_sinks', 'conv_w',
                'conv_b', 'rg_w_r', 'rg_b_r', 'rg_w_i', 'rg_b_i', 'rg_lambda', 'w_out', 'norm_ffn2', 'ffn2_gate',
                'ffn2_up', 'ffn2_down', 'norm_final']
BIG_NAMES = ['ffn1_gate', 'ffn1_up', 'ffn1_down', 'w_in', 'w_out', 'ffn2_gate', 'ffn2_up', 'ffn2_down']
SCATTER_STAGES = (['ffn2_gate', 'ffn2_up', 'ffn2_down', 'w_out', 'w_in'], ['ffn1_gate', 'ffn1_up', 'ffn1_down'])
SMALL_NAMES = ['norm_ffn1', 'norm_mix', 'norm_ffn2', 'norm_final', 'attn_sinks', 'conv_b', 'rg_w_r', 'rg_b_r',
               'rg_w_i', 'rg_b_i', 'rg_lambda']


def _params(sem, vmem=VMEM_LIMIT):
    return pltpu.CompilerParams(dimension_semantics=sem, vmem_limit_bytes=vmem)


def _dot(a, b, dims=NN_DIMS):
    return lax.dot_general(a, b, dims, preferred_element_type=F32)


def _sigmoid(x):
    return 1.0 / (1.0 + jnp.exp(-x))


def _mm(a, b, mode, *, name, tm=512, tn=512, tk=512, out_dtype=F32, alpha=1.0, res=None):
    if mode == 'nn':
        (M, K), N = a.shape, b.shape[1]
    elif mode == 'nt':
        (M, K), N = a.shape, b.shape[0]
    else:
        (K, M), N = a.shape, b.shape[1]
    tm, tn, tk = min(tm, M), min(tn, N), min(tk, K)
    ni, nj, nk = M // tm, N // tn, K // tk
    assert ni * tm == M and nj * tn == N and nk * tk == K, (name, a.shape, b.shape, tm, tn, tk)
    if mode == 'tn':
        a_spec = pl.BlockSpec((tk, tm), lambda j, i, k: (k, i))
    else:
        a_spec = pl.BlockSpec((tm, tk), lambda j, i, k: (i, k))
    if mode == 'nt':
        b_spec = pl.BlockSpec((tn, tk), lambda j, i, k: (j, k))
    else:
        b_spec = pl.BlockSpec((tk, tn), lambda j, i, k: (k, j))
    dims = {'nn': NN_DIMS, 'nt': NT_DIMS, 'tn': TN_DIMS}[mode]
    o_spec = pl.BlockSpec((tm, tn), lambda j, i, k: (i, j))
    has_res = res is not None

    def body(*refs):
        if has_res:
            a_ref, b_ref, r_ref, o_ref = refs[:4]
        else:
            a_ref, b_ref, o_ref = refs[:3]
        part = _dot(a_ref[...].astype(BF16), b_ref[...].astype(BF16), dims)

        def finish(acc):
            out = acc * alpha if alpha != 1.0 else acc
            if has_res:
                out = r_ref[...] + out
            o_ref[...] = out.astype(out_dtype)

        if nk == 1:
            finish(part)
        else:
            acc_ref = refs[-1]
            k = pl.program_id(2)

            @pl.when(k == 0)
            def _():
                acc_ref[...] = part

            @pl.when(k > 0)
            def _():
                acc_ref[...] += part

            @pl.when(k == nk - 1)
            def _():
                finish(acc_ref[...])

    in_specs = [a_spec, b_spec] + ([o_spec] if has_res else [])
    operands = [a, b] + ([res] if has_res else [])
    return pl.pallas_call(
        body, name=name, grid=(nj, ni, nk), in_specs=in_specs, out_specs=o_spec,
        out_shape=jax.ShapeDtypeStruct((M, N), out_dtype),
        scratch_shapes=[pltpu.VMEM((tm, tn), F32)] if nk > 1 else [],
        compiler_params=_params(("parallel", "parallel", "arbitrary")),
    )(*operands)


def _rms_fwd(x, g, *, name, tm=512):
    T, D = x.shape
    tm = min(tm, T)

    def body(x_ref, g_ref, o_ref):
        xv = x_ref[...]
        rstd = lax.rsqrt(jnp.mean(xv * xv, axis=-1, keepdims=True) + EPS)
        o_ref[...] = (xv * rstd * g_ref[...]).astype(BF16)

    return pl.pallas_call(
        body, name=name, grid=(T // tm,),
        in_specs=[pl.BlockSpec((tm, D), lambda i: (i, 0)), pl.BlockSpec((1, D), lambda i: (0, 0))],
        out_specs=pl.BlockSpec((tm, D), lambda i: (i, 0)),
        out_shape=jax.ShapeDtypeStruct((T, D), BF16),
        compiler_params=_params(("parallel",)),
    )(x, g)


def _rms_bwd_math(xv, g, dn):
    rstd = lax.rsqrt(jnp.mean(xv * xv, axis=-1, keepdims=True) + EPS)
    xhat = xv * rstd
    dxhat = dn * g
    dx = rstd * (dxhat - xhat * jnp.mean(dxhat * xhat, axis=-1, keepdims=True))
    return dx, jnp.sum(dn * xhat, axis=0, keepdims=True)


def _rms_bwd(x, g, dn, dres, *, name, tm=512):
    T, D = x.shape
    tm = min(tm, T)

    def body(x_ref, g_ref, dn_ref, dres_ref, dx_ref, dg_ref):
        dx, dg = _rms_bwd_math(x_ref[...], g_ref[...], dn_ref[...])
        dx_ref[...] = dres_ref[...] + dx

        @pl.when(pl.program_id(0) == 0)
        def _():
            dg_ref[...] = jnp.zeros_like(dg_ref)

        dg_ref[...] += dg

    row = pl.BlockSpec((tm, D), lambda i: (i, 0))
    vec = pl.BlockSpec((1, D), lambda i: (0, 0))
    return pl.pallas_call(
        body, name=name, grid=(T // tm,),
        in_specs=[row, vec, row, row], out_specs=[row, vec],
        out_shape=[jax.ShapeDtypeStruct((T, D), F32), jax.ShapeDtypeStruct((1, D), F32)],
        compiler_params=_params(("arbitrary",)),
    )(x, g, dn, dres)


def _loss_head(x, g, target, *, name, tm=512):
    T, D = x.shape
    tm = min(tm, T)

    def body(x_ref, g_ref, t_ref, loss_ref, dx_ref, dg_ref):
        xv = x_ref[...]
        g = g_ref[...]
        rstd = lax.rsqrt(jnp.mean(xv * xv, axis=-1, keepdims=True) + EPS)
        y = xv * rstd * g
        err = y - t_ref[...]
        part = 0.5 * jnp.sum(jnp.mean(err * err, axis=-1, keepdims=True), axis=0, keepdims=True)
        dx, dg = _rms_bwd_math(xv, g, err * (1.0 / D))
        dx_ref[...] = dx

        @pl.when(pl.program_id(0) == 0)
        def _():
            dg_ref[...] = jnp.zeros_like(dg_ref)
            loss_ref[...] = jnp.zeros_like(loss_ref)

        dg_ref[...] += dg
        loss_ref[...] += jnp.broadcast_to(part, loss_ref.shape)

    row = pl.BlockSpec((tm, D), lambda i: (i, 0))
    vec = pl.BlockSpec((1, D), lambda i: (0, 0))
    lspec = pl.BlockSpec((1, LANES), lambda i: (0, 0))
    return pl.pallas_call(
        body, name=name, grid=(T // tm,),
        in_specs=[row, vec, row], out_specs=[lspec, row, vec],
        out_shape=[jax.ShapeDtypeStruct((1, LANES), F32), jax.ShapeDtypeStruct((T, D), F32),
                   jax.ShapeDtypeStruct((1, D), F32)],
        compiler_params=_params(("arbitrary",)),
    )(x, g, target)


def _resident(shape):
    return pl.BlockSpec(shape, lambda i: (0,) * len(shape), pipeline_mode=pl.Buffered(1))


def _ffn_chunk(F):
    for c in (1408, 1024, 512, 256, 128):
        if F % c == 0:
            return c
    return F


def _ffn_fwd(x, g, wg, wu, wd, *, name, tm=256):
    T, D = x.shape
    F = wg.shape[1]
    tm = min(tm, T)
    fc = _ffn_chunk(F)

    def body(x_ref, g_ref, wg_ref, wu_ref, wd_ref, o_ref, a_ref, u_ref):
        xv = x_ref[...]
        rstd = lax.rsqrt(jnp.mean(xv * xv, axis=-1, keepdims=True) + EPS)
        n = (xv * rstd * g_ref[...]).astype(BF16)
        acc = jnp.zeros((tm, D), F32)
        for c in range(F // fc):
            sl = slice(c * fc, (c + 1) * fc)
            a = _dot(n, wg_ref[:, sl])
            u = _dot(n, wu_ref[:, sl])
            a_ref[:, sl] = a.astype(BF16)
            u_ref[:, sl] = u.astype(BF16)
            act = (a * _sigmoid(a) * u).astype(BF16)
            acc = acc + _dot(act, wd_ref[sl, :])
        o_ref[...] = xv + 0.5 * acc

    row = pl.BlockSpec((tm, D), lambda i: (i, 0))
    hid = pl.BlockSpec((tm, F), lambda i: (i, 0))
    return pl.pallas_call(
        body, name=name, grid=(T // tm,),
        in_specs=[row, pl.BlockSpec((1, D), lambda i: (0, 0)),
                  _resident((D, F)), _resident((D, F)), _resident((F, D))],
        out_specs=[row, hid, hid],
        out_shape=[jax.ShapeDtypeStruct((T, D), F32), jax.ShapeDtypeStruct((T, F), BF16),
                   jax.ShapeDtypeStruct((T, F), BF16)],
        compiler_params=_params(("parallel",)),
    )(x, g, wg, wu, wd)


def _ffn_bwd(x, g, dy, a, u, wg, wu, wd, *, name, tm=256):
    T, D = x.shape
    F = wg.shape[1]
    tm = min(tm, T)
    fc = _ffn_chunk(F)

    def body(x_ref, g_ref, dy_ref, a_ref, u_ref, wg_ref, wu_ref, wd_ref,
             dx_ref, dg_ref, n_ref, act_ref, da_ref, du_ref):
        xv = x_ref[...]
        g = g_ref[...]
        rstd = lax.rsqrt(jnp.mean(xv * xv, axis=-1, keepdims=True) + EPS)
        n_ref[...] = (xv * rstd * g).astype(BF16)
        dy = dy_ref[...]
        dyh = (0.5 * dy).astype(BF16)
        dn = jnp.zeros((tm, D), F32)
        for c in range(F // fc):
            sl = slice(c * fc, (c + 1) * fc)
            av = a_ref[:, sl].astype(F32)
            uv = u_ref[:, sl].astype(F32)
            dact = _dot(dyh, wd_ref[sl, :], NT_DIMS)
            s = _sigmoid(av)
            silu = av * s
            act_ref[:, sl] = (silu * uv).astype(BF16)
            da = (dact * uv * (s * (1.0 + av * (1.0 - s)))).astype(BF16)
            du = (dact * silu).astype(BF16)
            da_ref[:, sl] = da
            du_ref[:, sl] = du
            dn = dn + _dot(da, wg_ref[:, sl], NT_DIMS) + _dot(du, wu_ref[:, sl], NT_DIMS)
        dx, dg = _rms_bwd_math(xv, g, dn)
        dx_ref[...] = dy + dx

        @pl.when(pl.program_id(0) == 0)
        def _():
            dg_ref[...] = jnp.zeros_like(dg_ref)

        dg_ref[...] += dg

    row = pl.BlockSpec((tm, D), lambda i: (i, 0))
    hid = pl.BlockSpec((tm, F), lambda i: (i, 0))
    vec = pl.BlockSpec((1, D), lambda i: (0, 0))
    return pl.pallas_call(
        body, name=name, grid=(T // tm,),
        in_specs=[row, vec, row, hid, hid,
                  _resident((D, F)), _resident((D, F)), _resident((F, D))],
        out_specs=[row, vec, row, hid, hid, hid],
        out_shape=[jax.ShapeDtypeStruct((T, D), F32), jax.ShapeDtypeStruct((1, D), F32),
                   jax.ShapeDtypeStruct((T, D), BF16), jax.ShapeDtypeStruct((T, F), BF16),
                   jax.ShapeDtypeStruct((T, F), BF16), jax.ShapeDtypeStruct((T, F), BF16)],
        compiler_params=_params(("arbitrary",)),
    )(x, g, dy, a, u, wg, wu, wd)


def _lane_iota(shape):
    return lax.broadcasted_iota(jnp.int32, shape, 1)


def _rope_partner(x):
    first_half = (_lane_iota(x.shape) & (HEAD_DIM - 1)) < HEAD_DIM // 2
    return jnp.where(first_half, pltpu.roll(x, LANES - HEAD_DIM // 2, 1), pltpu.roll(x, HEAD_DIM // 2, 1))


def _swap_heads(x):
    return pltpu.roll(x, HEAD_DIM, 1)


def _undilate(blk_ref, d, stage):
    if d == 1:
        return blk_ref[...]
    n, width = blk_ref.shape
    W = width // d
    for r in range(d):
        for g in range(W // LANES):
            stage.at[g][pl.ds(r, n, stride=d), :] = blk_ref[:, r * W + g * LANES:r * W + (g + 1) * LANES]
    return jnp.concatenate([stage.at[g][...] for g in range(W // LANES)], axis=1)


def _dilate_into(out_ref, value, d, stage):
    if d == 1:
        out_ref[...] = value.astype(out_ref.dtype)
        return
    n = value.shape[0] // d
    W = value.shape[1]
    for g in range(W // LANES):
        stage.at[g][...] = value[:, g * LANES:(g + 1) * LANES]
    for r in range(d):
        for g in range(W // LANES):
            out_ref[:, r * W + g * LANES:r * W + (g + 1) * LANES] = (
                stage.at[g][pl.ds(r, n, stride=d), :].astype(out_ref.dtype))


def _dilated_spec(tm, d, W):
    return pl.BlockSpec((tm // d, d * W), lambda i: (i, 0))


def _stage(tm, W):
    return pltpu.VMEM((W // LANES, tm, LANES), F32)


DILATIONS = tuple(d for _, d in B_BRANCHES)


def _split_rope(proj, cos_t, sin_t, *, name, tm=512):
    T = proj.shape[0]
    tm = min(tm, T)
    nd = len(DILATIONS)

    def body(p_ref, c_ref, s_ref, qa_ref, ka_ref, va_ref, *rest):
        b_refs = rest[:3 * nd]
        xc_ref, gc_ref, stage = rest[3 * nd:]
        cos = c_ref[...]
        sin = s_ref[...]

        def rope(x):
            return x * cos + _rope_partner(x) * sin

        lo = _lane_iota((tm, LANES)) < HEAD_DIM
        col = 0
        for j in range(A_WIDTH // LANES):
            qa_ref[:, j * LANES:(j + 1) * LANES] = (rope(p_ref[:, col:col + LANES]) * SCALE).astype(BF16)
            col += LANES
        kr = rope(p_ref[:, col:col + LANES])
        col += LANES
        vr = p_ref[:, col:col + LANES]
        col += LANES
        for src, dst in ((kr, ka_ref), (vr, va_ref)):
            sw = _swap_heads(src)
            dst[:, 0:LANES] = jnp.where(lo, src, sw).astype(BF16)
            dst[:, LANES:2 * LANES] = jnp.where(lo, sw, src).astype(BF16)
        for which, (roped, scale) in enumerate(((True, SCALE), (True, 1.0), (False, 1.0))):
            parts = []
            for j in range(B_WIDTH // LANES):
                v = p_ref[:, col:col + LANES]
                parts.append(rope(v) * scale if roped else v)
                col += LANES
            value = jnp.concatenate(parts, axis=1)
            for di, d in enumerate(DILATIONS):
                _dilate_into(b_refs[which * nd + di], value, d, stage)
        xc_ref[...] = p_ref[:, col:col + C_WIDTH]
        gc_ref[...] = p_ref[:, col + C_WIDTH:col + 2 * C_WIDTH]

    def row(w):
        return pl.BlockSpec((tm, w), lambda i: (i, 0))

    out_specs = [row(A_WIDTH)] * 3 + [_dilated_spec(tm, d, B_WIDTH) for _ in range(3) for d in DILATIONS]
    out_specs += [row(C_WIDTH)] * 2
    out_shape = [jax.ShapeDtypeStruct((T, A_WIDTH), BF16)] * 3
    out_shape += [jax.ShapeDtypeStruct((T // d, d * B_WIDTH), BF16) for _ in range(3) for d in DILATIONS]
    out_shape += [jax.ShapeDtypeStruct((T, C_WIDTH), F32)] * 2
    res = pl.pallas_call(
        body, name=name, grid=(T // tm,),
        in_specs=[row(IN_COLS), row(LANES), row(LANES)], out_specs=out_specs, out_shape=out_shape,
        scratch_shapes=[_stage(tm, B_WIDTH)],
        compiler_params=_params(("parallel",)),
    )(proj, cos_t, sin_t)
    qa, ka2, va2 = res[:3]
    qb, kb, vb = (list(res[3 + i * nd:3 + (i + 1) * nd]) for i in range(3))
    return qa, ka2, va2, qb, kb, vb, res[-2], res[-1]


def _merge_dproj(dqa, dka2, dva2, dqb, dkb, dvb, dxc, dgc, cos_t, sin_t, *, name, tm=512):
    T = dqa.shape[0]
    tm = min(tm, T)
    nb = len(dqb)

    def body(*refs):
        dqa_ref, dka_ref, dva_ref = refs[:3]
        dqb_refs = refs[3:3 + nb]
        dkb_refs = refs[3 + nb:3 + 2 * nb]
        dvb_refs = refs[3 + 2 * nb:3 + 3 * nb]
        dxc_ref, dgc_ref, c_ref, s_ref, o_ref, stage = refs[3 + 3 * nb:]
        cos = c_ref[...]
        sin = s_ref[...]

        def rope_t(dy):
            return dy * cos - _rope_partner(dy) * sin

        lo = _lane_iota((tm, LANES)) < HEAD_DIM
        col = 0
        for j in range(A_WIDTH // LANES):
            o_ref[:, col:col + LANES] = (rope_t(dqa_ref[:, j * LANES:(j + 1) * LANES]) * SCALE).astype(BF16)
            col += LANES
        for src, roped in ((dka_ref, True), (dva_ref, False)):
            b0 = src[:, 0:LANES]
            b1 = src[:, LANES:2 * LANES]
            v = jnp.where(lo, b0 + _swap_heads(b0), b1 + _swap_heads(b1))
            if roped:
                v = rope_t(v)
            o_ref[:, col:col + LANES] = v.astype(BF16)
            col += LANES
        for group, roped, scale in ((dqb_refs, True, SCALE), (dkb_refs, True, 1.0), (dvb_refs, False, 1.0)):
            total = _undilate(group[0], DILATIONS[0], stage)
            for r, d in zip(group[1:], DILATIONS[1:]):
                total = total + _undilate(r, d, stage)
            for j in range(B_WIDTH // LANES):
                v = total[:, j * LANES:(j + 1) * LANES]
                if roped:
                    v = rope_t(v) * scale
                o_ref[:, col:col + LANES] = v.astype(BF16)
                col += LANES
        o_ref[:, col:col + C_WIDTH] = dxc_ref[...].astype(BF16)
        o_ref[:, col + C_WIDTH:col + 2 * C_WIDTH] = dgc_ref[...].astype(BF16)

    def row(w):
        return pl.BlockSpec((tm, w), lambda i: (i, 0))

    ins = [dqa, dka2, dva2, *dqb, *dkb, *dvb, dxc, dgc, cos_t, sin_t]
    in_specs = [row(A_WIDTH)] * 3 + [_dilated_spec(tm, d, B_WIDTH) for _ in range(3) for d in DILATIONS]
    in_specs += [row(C_WIDTH)] * 2 + [row(LANES)] * 2
    return pl.pallas_call(
        body, name=name, grid=(T // tm,), in_specs=in_specs,
        out_specs=row(IN_COLS),
        out_shape=jax.ShapeDtypeStruct((T, IN_COLS), BF16),
        scratch_shapes=[_stage(tm, B_WIDTH)],
        compiler_params=_params(("parallel",)),
    )(*ins)


def _band_masks(max_dist):
    row = lax.broadcasted_iota(jnp.int32, (QBLK, 2 * QBLK), 0)
    key = lax.broadcasted_iota(jnp.int32, (QBLK, 2 * QBLK), 1)
    dist = row + QBLK - key
    wide = jnp.logical_and(dist >= 0, dist <= max_dist)
    return wide, wide[:, :QBLK], key >= QBLK


def _head_masks(rows=QBLK):
    lo = _lane_iota((rows, LANES)) < HEAD_DIM
    return lo, jnp.logical_not(lo)


def _keep(hm, x):
    return x * jnp.where(hm, 1.0, 0.0).astype(x.dtype)


def _head_col(x, hm):
    return jnp.max(jnp.where(hm, x, -jnp.inf), axis=1, keepdims=True)


def _attn_specs(R, C):
    chunk = min(ATTN_CHUNK, R)
    nb = chunk // QBLK
    nch = R // chunk
    main = pl.BlockSpec((chunk, LANES), lambda j, c: (c, j))
    prev = pl.BlockSpec((QBLK, LANES), lambda j, c: (jnp.maximum(c * nb - 1, 0), j))
    nxt = pl.BlockSpec((QBLK, LANES), lambda j, c: (jnp.minimum((c + 1) * nb, R // QBLK - 1), j))
    return chunk, nb, nch, main, prev, nxt


def _attn_fwd(q, k, v, max_dist, *, name):
    R, C = q.shape
    chunk, nb, nch, main, prev, _ = _attn_specs(R, C)

    def body(q_ref, k_ref, v_ref, kp_ref, vp_ref, o_ref, lse_ref):
        c = pl.program_id(1)
        wide_mask, _, own_block = _band_masks(max_dist)
        heads = _head_masks()

        def block(q_blk, kk, vv, mask):
            q2 = jnp.concatenate([_keep(hm, q_blk) for hm in heads], axis=0)
            s = jnp.where(jnp.concatenate([mask, mask], axis=0), _dot(q2, kk, NT_DIMS), -jnp.inf)
            m = jnp.max(jnp.maximum(s[:, :QBLK], s[:, QBLK:]), axis=1, keepdims=True)
            p = jnp.exp(s - m)
            l = jnp.sum(p[:, :QBLK] + p[:, QBLK:], axis=1, keepdims=True)
            o2 = _dot(p.astype(BF16), vv) / l
            lse2 = jnp.broadcast_to(m + jnp.log(l), (2 * QBLK, LANES))
            return (jnp.where(heads[0], o2[:QBLK], o2[QBLK:]), jnp.where(heads[0], lse2[:QBLK], lse2[QBLK:]))

        first = pl.ds(0, QBLK)
        o0, l0 = block(q_ref[first, :], jnp.concatenate([kp_ref[...], k_ref[first, :]], axis=0),
                       jnp.concatenate([vp_ref[...], v_ref[first, :]], axis=0),
                       jnp.logical_and(wide_mask, jnp.logical_or(own_block, c > 0)))
        o_ref[first, :] = o0
        lse_ref[first, :] = l0

        def loop(qb, carry):
            cur = pl.ds(pl.multiple_of(qb * QBLK, QBLK), QBLK)
            both = pl.ds(pl.multiple_of((qb - 1) * QBLK, QBLK), 2 * QBLK)
            o, l = block(q_ref[cur, :], k_ref[both, :], v_ref[both, :], wide_mask)
            o_ref[cur, :] = o
            lse_ref[cur, :] = l
            return carry

        if nb > 1:
            lax.fori_loop(1, nb, loop, 0, unroll=ATTN_FWD_UNROLL)

    return pl.pallas_call(
        body, name=name, grid=(C // LANES, nch),
        in_specs=[main, main, main, prev, prev], out_specs=[main, main],
        out_shape=[jax.ShapeDtypeStruct((R, C), F32), jax.ShapeDtypeStruct((R, C), F32)],
        compiler_params=_params(("parallel", "parallel")),
    )(q, k, v, k, v)


def _attn_bwd(q, k, v, do, lse, delta, max_dist, *, name):
    R, C = q.shape
    chunk, nb, nch, main, prev, nxt = _attn_specs(R, C)

    def body(q_ref, k_ref, v_ref, do_ref, lse_ref, dl_ref, kp_ref, vp_ref, qn_ref, don_ref, lsen_ref, dln_ref,
             dq_ref, dk_ref, dv_ref):
        c = pl.program_id(1)
        wide_mask, prev_mask, own_block = _band_masks(max_dist)
        heads = _head_masks()

        def pair(q_blk, do_blk, lse_blk, dl_blk, kk, vv, mask, want_dq=True):
            q2 = jnp.concatenate([_keep(hm, q_blk) for hm in heads], axis=0)
            do2 = jnp.concatenate([_keep(hm, do_blk) for hm in heads], axis=0)
            lse2 = jnp.concatenate([_head_col(lse_blk, hm) for hm in heads], axis=0)
            dl2 = jnp.concatenate([_head_col(dl_blk, hm) for hm in heads], axis=0)
            p = jnp.where(jnp.concatenate([mask, mask], axis=0), jnp.exp(_dot(q2, kk, NT_DIMS) - lse2), 0.0)
            ds = (p * (_dot(do2, vv, NT_DIMS) - dl2)).astype(BF16)
            dq = None
            if want_dq:
                k2 = jnp.concatenate([_keep(khm, kk) for khm in _head_masks(kk.shape[0])], axis=0)
                dq = _dot(jnp.concatenate([ds[:QBLK], ds[QBLK:]], axis=1), k2)
            return dq, _dot(ds, q2, TN_DIMS), _dot(p.astype(BF16), do2, TN_DIMS)

        dk_ref[...] = jnp.zeros_like(dk_ref)
        dv_ref[...] = jnp.zeros_like(dv_ref)

        first = pl.ds(0, QBLK)
        dq0, dkk0, dvv0 = pair(q_ref[first, :], do_ref[first, :], lse_ref[first, :], dl_ref[first, :],
                               jnp.concatenate([kp_ref[...], k_ref[first, :]], axis=0),
                               jnp.concatenate([vp_ref[...], v_ref[first, :]], axis=0),
                               jnp.logical_and(wide_mask, jnp.logical_or(own_block, c > 0)))
        dq_ref[first, :] = dq0
        dk_ref[first, :] += dkk0[QBLK:, :]
        dv_ref[first, :] += dvv0[QBLK:, :]

        def loop(qb, carry):
            cur = pl.ds(pl.multiple_of(qb * QBLK, QBLK), QBLK)
            both = pl.ds(pl.multiple_of((qb - 1) * QBLK, QBLK), 2 * QBLK)
            dq, dkk, dvv = pair(q_ref[cur, :], do_ref[cur, :], lse_ref[cur, :], dl_ref[cur, :],
                                k_ref[both, :], v_ref[both, :], wide_mask)
            dq_ref[cur, :] = dq
            dk_ref[both, :] += dkk
            dv_ref[both, :] += dvv
            return carry

        if nb > 1:
            lax.fori_loop(1, nb, loop, 0, unroll=ATTN_BWD_UNROLL)

        last = pl.ds((nb - 1) * QBLK, QBLK)
        _, dk_n, dv_n = pair(qn_ref[...], don_ref[...], lsen_ref[...], dln_ref[...], k_ref[last, :], v_ref[last, :],
                             jnp.logical_and(prev_mask, c < nch - 1), want_dq=False)
        dk_ref[last, :] += dk_n
        dv_ref[last, :] += dv_n

    return pl.pallas_call(
        body, name=name, grid=(C // LANES, nch),
        in_specs=[main] * 6 + [prev, prev] + [nxt] * 4, out_specs=[main, main, main],
        out_shape=[jax.ShapeDtypeStruct((R, C), F32)] * 3,
        compiler_params=_params(("parallel", "parallel")),
    )(q, k, v, do, lse, delta, k, v, q, do, lse, delta)


def _head_sum(x):
    r = lax.broadcasted_iota(jnp.int32, (LANES, LANES), 0) // HEAD_DIM
    c = lax.broadcasted_iota(jnp.int32, (LANES, LANES), 1) // HEAD_DIM
    ones = jnp.where(r == c, 1.0, 0.0).astype(BF16)
    outs = []
    for j in range(x.shape[1] // LANES):
        rem = x[:, j * LANES:(j + 1) * LANES]
        acc = jnp.zeros(rem.shape, F32)
        for _ in range(3):
            part = rem.astype(BF16)
            acc = acc + _dot(part, ones)
            rem = rem - part.astype(F32)
        outs.append(acc)
    return outs[0] if len(outs) == 1 else jnp.concatenate(outs, axis=1)


def _branch_weights(lses):
    m = functools.reduce(jnp.maximum, lses)
    es = [jnp.exp(l - m) for l in lses]
    den = functools.reduce(lambda a, b: a + b, es)
    return [e / den for e in es]


def _combine_fwd(oa, lsea, sink, obs, lsebs, oc, *, name, tm=512):
    T = oa.shape[0]
    tm = min(tm, T)
    nb = len(obs)

    def body(*refs):
        oa_ref, lsea_ref, sink_ref = refs[:3]
        ob_refs = refs[3:3 + nb]
        lse_refs = refs[3 + nb:3 + 2 * nb]
        oc_ref, out_ref, stage = refs[3 + 2 * nb:]
        out_ref[:, 0:A_WIDTH] = (oa_ref[...] * _sigmoid(lsea_ref[...] - sink_ref[...])).astype(BF16)
        ws = _branch_weights([_undilate(r, d, stage) for r, d in zip(lse_refs, DILATIONS)])
        ob = _undilate(ob_refs[0], DILATIONS[0], stage) * ws[0]
        for r, d, w in zip(ob_refs[1:], DILATIONS[1:], ws[1:]):
            ob = ob + _undilate(r, d, stage) * w
        out_ref[:, A_WIDTH:A_WIDTH + B_WIDTH] = ob.astype(BF16)
        out_ref[:, A_WIDTH + B_WIDTH:MIX_WIDTH] = oc_ref[...].astype(BF16)

    def row(w):
        return pl.BlockSpec((tm, w), lambda i: (i, 0))

    ins = [oa, lsea, sink, *obs, *lsebs, oc]
    in_specs = [row(A_WIDTH), row(A_WIDTH), pl.BlockSpec((1, A_WIDTH), lambda i: (0, 0))]
    in_specs += [_dilated_spec(tm, d, B_WIDTH) for _ in range(2) for d in DILATIONS] + [row(C_WIDTH)]
    return pl.pallas_call(
        body, name=name, grid=(T // tm,), in_specs=in_specs, out_specs=row(MIX_WIDTH),
        out_shape=jax.ShapeDtypeStruct((T, MIX_WIDTH), BF16),
        scratch_shapes=[_stage(tm, B_WIDTH)],
        compiler_params=_params(("parallel",)),
    )(*ins)


def _combine_bwd(dmix, oa, lsea, sink, obs, lsebs, *, name, tm=512):
    T = oa.shape[0]
    tm = min(tm, T)
    nb = len(obs)

    def body(*refs):
        dmix_ref, oa_ref, lsea_ref, sink_ref = refs[:4]
        ob_refs = refs[4:4 + nb]
        lse_refs = refs[4 + nb:4 + 2 * nb]
        outs = refs[4 + 2 * nb:-1]
        stage = refs[-1]
        doa_ref, dla_ref = outs[:2]
        dob_refs = outs[2:2 + nb]
        dlb_refs = outs[2 + nb:2 + 2 * nb]
        doc_ref, dsink_ref = outs[2 + 2 * nb:]

        d_a = dmix_ref[:, 0:A_WIDTH]
        d_b = dmix_ref[:, A_WIDTH:A_WIDTH + B_WIDTH]
        doc_ref[...] = dmix_ref[:, A_WIDTH + B_WIDTH:MIX_WIDTH]

        gate = _sigmoid(lsea_ref[...] - sink_ref[...])
        doa_ref[...] = (d_a * gate).astype(BF16)
        dgate = _head_sum(d_a * oa_ref[...])
        dlse = dgate * gate * (1.0 - gate)
        dla_ref[...] = dgate * gate - dlse

        @pl.when(pl.program_id(0) == 0)
        def _():
            dsink_ref[...] = jnp.zeros_like(dsink_ref)

        dsink_ref[...] -= jnp.sum(dlse, axis=0, keepdims=True)

        ws = _branch_weights([_undilate(r, d, stage) for r, d in zip(lse_refs, DILATIONS)])
        dws = [_head_sum(d_b * _undilate(r, d, stage)) for r, d in zip(ob_refs, DILATIONS)]
        sw = ws[0] * dws[0]
        for w, dw in zip(ws[1:], dws[1:]):
            sw = sw + w * dw
        for w, d, do_ref, dl_ref in zip(ws, DILATIONS, dob_refs, dlb_refs):
            _dilate_into(do_ref, w * d_b, d, stage)
            _dilate_into(dl_ref, w * sw, d, stage)

    def row(w):
        return pl.BlockSpec((tm, w), lambda i: (i, 0))

    vec = pl.BlockSpec((1, A_WIDTH), lambda i: (0, 0))
    dil = [_dilated_spec(tm, d, B_WIDTH) for _ in range(2) for d in DILATIONS]
    ins = [dmix, oa, lsea, sink, *obs, *lsebs]
    in_specs = [row(MIX_WIDTH), row(A_WIDTH), row(A_WIDTH), vec] + dil
    out_specs = [row(A_WIDTH), row(A_WIDTH)] + dil + [row(C_WIDTH), vec]
    out_shape = [jax.ShapeDtypeStruct((T, A_WIDTH), BF16), jax.ShapeDtypeStruct((T, A_WIDTH), F32)]
    out_shape += [jax.ShapeDtypeStruct((T // d, d * B_WIDTH), BF16) for d in DILATIONS]
    out_shape += [jax.ShapeDtypeStruct((T // d, d * B_WIDTH), F32) for d in DILATIONS]
    out_shape += [jax.ShapeDtypeStruct((T, C_WIDTH), F32), jax.ShapeDtypeStruct((1, A_WIDTH), F32)]
    res = pl.pallas_call(
        body, name=name, grid=(T // tm,), in_specs=in_specs, out_specs=out_specs, out_shape=out_shape,
        scratch_shapes=[_stage(tm, B_WIDTH)],
        compiler_params=_params(("arbitrary",)),
    )(*ins)
    return res[0], res[1], list(res[2:2 + nb]), list(res[2 + nb:2 + 2 * nb]), res[2 + 2 * nb], res[3 + 2 * nb]


HIST = 8


def _softplus_neg(lam):
    e = jnp.exp(-jnp.abs(lam))
    log1p = jnp.where(e < 0.01, e * (1.0 - e * (0.5 - e * (1.0 / 3.0))), jnp.log(1.0 + e))
    return jnp.maximum(-lam, 0.0) + log1p


def _neg_expm1(x):
    series = -x * (1.0 + x * (0.5 + x * (1.0 / 6.0 + x * (1.0 / 24.0 + x * (1.0 / 120.0)))))
    return jnp.where(x > -0.1, series, 1.0 - jnp.exp(x))


def _gelu_parts(x):
    k = math.sqrt(2.0 / math.pi)
    t = jnp.tanh(k * (x + 0.044715 * (x * x * x)))
    cdf = 0.5 * (1.0 + t)
    return x * cdf, cdf + 0.5 * x * (1.0 - t * t) * (k * (1.0 + 3.0 * 0.044715 * (x * x)))


def _rglru_gates(y, pos_ref, wr_ref, br_ref, wi_ref, bi_ref, lam_ref):
    yb = y.astype(BF16)
    r = _sigmoid(_dot(yb, wr_ref[...]) + br_ref[...])
    ig = _sigmoid(_dot(yb, wi_ref[...]) + bi_ref[...])
    sp = _softplus_neg(lam_ref[...])
    log_a = -C_EXP * r * sp
    reset = pos_ref[...] == 0
    a = jnp.where(reset, 0.0, jnp.exp(log_a))
    mult = jnp.where(reset, 1.0, jnp.sqrt(_neg_expm1(2.0 * log_a)))
    return yb, r, ig, sp, reset, a, mult


def _conv_fwd(xs_ref, cw_ref, cb_ref, tm):
    y = cb_ref[...] + cw_ref[0:1, :] * xs_ref[HIST:HIST + tm, :]
    for j in range(1, C_CONV):
        y = y + cw_ref[j:j + 1, :] * xs_ref[HIST - j:HIST - j + tm, :]
    return y


SCAN_GROUP = 8


def _blocked_scan(c, d, c_s, d_s, grp_a, grp_h, carry, reverse):
    tm, W = d.shape
    groups = tm // SCAN_GROUP
    order = range(SCAN_GROUP - 1, -1, -1) if reverse else range(SCAN_GROUP)
    outs, lasts = [], []
    for k in range(W // LANES):
        lanes = slice(k * LANES, (k + 1) * LANES)
        ck, dk, ga, gh = c_s.at[k], d_s.at[k], grp_a.at[k], grp_h.at[k]
        ck[...] = c[:, lanes]
        dk[...] = d[:, lanes]
        prod = state = None
        for j in order:
            rows = pl.ds(j, groups, stride=SCAN_GROUP)
            cj, dj = ck[rows, :], dk[rows, :]
            if prod is None:
                prod, state = cj, dj
            else:
                state = cj * state + dj
                prod = cj * prod
            ck[rows, :] = prod
            dk[rows, :] = state
        ga[...] = prod
        gh[...] = state

        def step(i, h, ga=ga, gh=gh):
            row = pl.ds(groups - 1 - i if reverse else i, 1)
            a, t = ga[row, :], gh[row, :]
            ga[row, :] = h
            return a * h + t

        lasts.append(lax.fori_loop(0, groups, step, carry[:, lanes], unroll=8))
        entering = ga[...]
        for j in range(SCAN_GROUP):
            rows = pl.ds(j, groups, stride=SCAN_GROUP)
            dk[rows, :] = dk[rows, :] + ck[rows, :] * entering
        outs.append(dk[...])
    return jnp.concatenate(outs, axis=1), jnp.concatenate(lasts, axis=1)


def _rglru_fwd(xc, gc, pos, cw, cb, wr, br, wi, bi, lam, *, name, tm=512):
    T, W = xc.shape
    tm = min(tm, T)

    def body(xc_ref, gc_ref, pos_ref, cw_ref, cb_ref, wr_ref, br_ref, wi_ref, bi_ref, lam_ref,
             out_ref, hs_ref, xs, a_s, b_s, h_s, grp_a, grp_h):
        @pl.when(pl.program_id(0) == 0)
        def _():
            xs[0:HIST, :] = jnp.zeros((HIST, W), F32)
            h_s[...] = jnp.zeros_like(h_s)

        xv = xc_ref[...]
        xs[HIST:HIST + tm, :] = xv
        y = _conv_fwd(xs, cw_ref, cb_ref, tm)
        xs[0:HIST, :] = xv[tm - HIST:tm, :]
        _, _, ig, _, _, a, mult = _rglru_gates(y, pos_ref, wr_ref, br_ref, wi_ref, bi_ref, lam_ref)
        hs, h_s[...] = _blocked_scan(a, mult * (ig * y), a_s, b_s, grp_a, grp_h, h_s[...], reverse=False)
        hs_ref[...] = hs
        out_ref[...] = hs * _gelu_parts(gc_ref[...])[0]

    row = pl.BlockSpec((tm, W), lambda i: (i, 0))
    full = lambda shape: pl.BlockSpec(shape, lambda i: (0,) * len(shape))
    return pl.pallas_call(
        body, name=name, grid=(T // tm,),
        in_specs=[row, row, pl.BlockSpec((tm, 1), lambda i: (i, 0)), full((C_CONV, W)), full((1, W)),
                  full((W, W)), full((1, W)), full((W, W)), full((1, W)), full((1, W))],
        out_specs=[row, row],
        out_shape=[jax.ShapeDtypeStruct((T, W), F32)] * 2,
        scratch_shapes=[pltpu.VMEM((tm + HIST, W), F32), pltpu.VMEM((W // LANES, tm, LANES), F32),
                        pltpu.VMEM((W // LANES, tm, LANES), F32), pltpu.VMEM((1, W), F32),
                        pltpu.VMEM((W // LANES, tm // SCAN_GROUP, LANES), F32),
                        pltpu.VMEM((W // LANES, tm // SCAN_GROUP, LANES), F32)],
        compiler_params=_params(("arbitrary",)),
    )(xc, gc, pos, cw, cb, wr, br, wi, bi, lam)


def _rglru_bwd(xc, gc, pos, hs, dout, cw, cb, wr, br, wi, bi, lam, *, name, tm=512):
    T, W = xc.shape
    tm = min(tm, T)
    nt = T // tm
    hb = tm // HIST

    def body(xc_ref, gc_ref, pos_ref, hs_ref, dout_ref, xch_ref, hsh_ref,
             cw_ref, cb_ref, wr_ref, br_ref, wi_ref, bi_ref, lam_ref,
             dxc_ref, dgc_ref, dcw_ref, dcb_ref, dwr_ref, dbr_ref, dwi_ref, dbi_ref, dlam_ref,
             xs, hsx, dys, asx, a_s, d_s, carry_s, grp_a, grp_h):
        i = pl.program_id(0)

        @pl.when(i == 0)
        def _():
            for r in (dcw_ref, dcb_ref, dwr_ref, dbr_ref, dwi_ref, dbi_ref, dlam_ref, carry_s):
                r[...] = jnp.zeros_like(r)
            dys[tm:tm + HIST, :] = jnp.zeros((HIST, W), F32)
            asx[tm:tm + HIST, :] = jnp.zeros((HIST, W), F32)

        has_prev = i < nt - 1
        xs[0:HIST, :] = jnp.where(has_prev, xch_ref[...], 0.0)
        hsx[0:HIST, :] = jnp.where(has_prev, hsh_ref[...], 0.0)
        xs[HIST:HIST + tm, :] = xc_ref[...]
        hs = hs_ref[...]
        hsx[HIST:HIST + tm, :] = hs
        y = _conv_fwd(xs, cw_ref, cb_ref, tm)
        yb, r, ig, sp, reset, a, mult = _rglru_gates(y, pos_ref, wr_ref, br_ref, wi_ref, bi_ref, lam_ref)

        gelu, dgelu = _gelu_parts(gc_ref[...])
        dout = dout_ref[...]
        dgc_ref[...] = dout * hs * dgelu
        asx[0:tm, :] = a
        a_up = asx[1:1 + tm, :]
        asx[tm:tm + HIST, :] = a[0:HIST, :]
        dh, carry_s[...] = _blocked_scan(a_up, dout * gelu, a_s, d_s, grp_a, grp_h, carry_s[...], reverse=True)
        hprev = hsx[HIST - 1:HIST - 1 + tm, :]
        igy = ig * y
        dmult = dh * igy
        digy = dh * mult
        dlog_a = jnp.where(reset, 0.0, dh * hprev * a - dmult * a * a / mult)
        dlam_ref[...] += jnp.sum(dlog_a * (C_EXP * r) * _sigmoid(-lam_ref[...]), axis=0, keepdims=True)
        dz_r = dlog_a * (-C_EXP * sp) * r * (1.0 - r)
        dz_i = digy * y * ig * (1.0 - ig)
        dzr_b = dz_r.astype(BF16)
        dzi_b = dz_i.astype(BF16)
        dy = digy * ig + _dot(dzr_b, wr_ref[...], NT_DIMS) + _dot(dzi_b, wi_ref[...], NT_DIMS)
        dwr_ref[...] += _dot(yb, dzr_b, TN_DIMS)
        dwi_ref[...] += _dot(yb, dzi_b, TN_DIMS)
        dbr_ref[...] += jnp.sum(dz_r, axis=0, keepdims=True)
        dbi_ref[...] += jnp.sum(dz_i, axis=0, keepdims=True)

        dys[0:tm, :] = dy
        dxc = cw_ref[0:1, :] * dy
        for j in range(1, C_CONV):
            dxc = dxc + cw_ref[j:j + 1, :] * dys[j:j + tm, :]
        dxc_ref[...] = dxc
        dys[tm:tm + HIST, :] = dy[0:HIST, :]
        dcb_ref[...] += jnp.sum(dy, axis=0, keepdims=True)
        for j in range(C_CONV):
            dcw_ref[j:j + 1, :] += jnp.sum(dy * xs[HIST - j:HIST - j + tm, :], axis=0, keepdims=True)

    row = pl.BlockSpec((tm, W), lambda i: (nt - 1 - i, 0))
    halo = pl.BlockSpec((HIST, W), lambda i: (jnp.maximum((nt - 1 - i) * hb - 1, 0), 0))
    full = lambda shape: pl.BlockSpec(shape, lambda i: (0,) * len(shape))
    out_specs = [row, row, full((C_CONV, W)), full((1, W)), full((W, W)), full((1, W)), full((W, W)), full((1, W)),
                 full((1, W))]
    out_shape = [jax.ShapeDtypeStruct((T, W), F32)] * 2
    out_shape += [jax.ShapeDtypeStruct(s, F32) for s in ((C_CONV, W), (1, W), (W, W), (1, W), (W, W), (1, W), (1, W))]
    return pl.pallas_call(
        body, name=name, grid=(nt,),
        in_specs=[row, row, pl.BlockSpec((tm, 1), lambda i: (nt - 1 - i, 0)), row, row, halo, halo,
                  full((C_CONV, W)), full((1, W)), full((W, W)), full((1, W)), full((W, W)), full((1, W)),
                  full((1, W))],
        out_specs=out_specs, out_shape=out_shape,
        scratch_shapes=[pltpu.VMEM((tm + HIST, W), F32), pltpu.VMEM((tm + HIST, W), F32),
                        pltpu.VMEM((tm + HIST, W), F32), pltpu.VMEM((tm + HIST, W), F32),
                        pltpu.VMEM((W // LANES, tm, LANES), F32), pltpu.VMEM((W // LANES, tm, LANES), F32),
                        pltpu.VMEM((1, W), F32), pltpu.VMEM((W // LANES, tm // SCAN_GROUP, LANES), F32),
                        pltpu.VMEM((W // LANES, tm // SCAN_GROUP, LANES), F32)],
        compiler_params=_params(("arbitrary",)),
    )(xc, gc, pos, hs, dout, xc, hs, cw, cb, wr, br, wi, bi, lam)


def _adam_math(w, g, m, v):
    m = ADAM_B1 * m + (1.0 - ADAM_B1) * g
    v = ADAM_B2 * v + (1.0 - ADAM_B2) * (g * g)
    m_hat = m / (1.0 - ADAM_B1 ** ADAM_STEP)
    v_hat = v / (1.0 - ADAM_B2 ** ADAM_STEP)
    delta = -ADAM_LR * (m_hat / (jnp.sqrt(v_hat) + ADAM_EPS) + ADAM_WD * w)
    return delta, m, v


def _pick_rows(R, cap=512, mult=16):
    for d in range(min(cap, R), 0, -1):
        if R % d == 0 and d % mult == 0:
            return d
    return R


def _adamw(parts, w, m, v, *, name, tr=None, part=0, prev=None):
    R, C = w.shape
    r = parts.shape[1]
    tr = _pick_rows(r) if tr is None else tr
    assert r % tr == 0 and R % r == 0, (name, R, r, tr)
    nt = r // tr

    def body(p_ref, w_ref, m_ref, v_ref, *rest):
        g_ref, d_ref, nm_ref, nv_ref = rest[-4:]
        g = p_ref[0].astype(F32)
        for d in range(1, N_DEV):
            g = g + p_ref[d].astype(F32)
        delta, nm, nv = _adam_math(w_ref[...], g, m_ref[...], v_ref[...])
        g_ref[...] = g
        d_ref[...] = delta
        nm_ref[...] = nm
        nv_ref[...] = nv

    row = pl.BlockSpec((tr, C), lambda i: (part * nt + i, 0))
    in_specs = [pl.BlockSpec((N_DEV, tr, C), lambda i: (0, i, 0)), row, row, row]
    operands = [parts, w, m, v]
    aliases = {}
    if prev is not None:
        in_specs += [pl.BlockSpec(memory_space=pl.ANY)] * 4
        operands += list(prev)
        aliases = {4 + i: i for i in range(4)}
    return pl.pallas_call(
        body, name=name, grid=(nt,), in_specs=in_specs,
        out_specs=[row] * 4, out_shape=[jax.ShapeDtypeStruct((R, C), F32)] * 4,
        input_output_aliases=aliases,
        compiler_params=_params(("parallel",)),
    )(*operands)


def _exchange(srcs, gather, *, name):
    n = len(srcs)
    out_shape = [jax.ShapeDtypeStruct((N_DEV,) + s.shape if gather else s.shape, s.dtype) for s in srcs]

    def body(*refs):
        ins, outs = refs[:n], refs[n:2 * n]
        send_sems, recv_sems, local_sems = refs[2 * n:]
        x, y, c = lax.axis_index("x"), lax.axis_index("y"), lax.axis_index("c")
        me = 4 * x + 2 * y + c
        local_copies, sends, arrivals = [], [], []
        for a in range(n):
            mine = ins[a] if gather else ins[a].at[me]
            local = pltpu.make_async_copy(mine, outs[a].at[me], local_sems.at[a])
            local.start()
            local_copies.append(local)
            for k in range(1, N_DEV):
                px, py, pc = x ^ ((k >> 2) & 1), y ^ ((k >> 1) & 1), c ^ (k & 1)
                peer = 4 * px + 2 * py + pc
                send = pltpu.make_async_remote_copy(
                    src_ref=ins[a] if gather else ins[a].at[peer], dst_ref=outs[a].at[me],
                    send_sem=send_sems.at[a * N_DEV + k], recv_sem=recv_sems.at[a * N_DEV + k],
                    device_id=(px, py, pc), device_id_type=pl.DeviceIdType.MESH)
                send.start()
                sends.append(send)
                arrivals.append(pltpu.make_async_remote_copy(
                    src_ref=mine, dst_ref=outs[a].at[peer],
                    send_sem=send_sems.at[a * N_DEV + k], recv_sem=recv_sems.at[a * N_DEV + k],
                    device_id=(px, py, pc), device_id_type=pl.DeviceIdType.MESH))
        for cp in sends:
            cp.wait_send()
        for cp in arrivals:
            cp.wait_recv()
        for cp in local_copies:
            cp.wait()

    return pl.pallas_call(
        body, name=name,
        in_specs=[pl.BlockSpec(memory_space=pl.ANY)] * n, out_specs=[pl.BlockSpec(memory_space=pl.ANY)] * n,
        out_shape=out_shape,
        scratch_shapes=[pltpu.SemaphoreType.DMA((n * N_DEV,)), pltpu.SemaphoreType.DMA((n * N_DEV,)),
                        pltpu.SemaphoreType.DMA((n,))],
    )(*srcs)


_HBM = pl.BlockSpec(memory_space=pltpu.HBM)
_SEM = pl.BlockSpec(memory_space=pltpu.SEMAPHORE)
_EFFECT = pltpu.SideEffectType.DATAFLOW_SIDE_EFFECTING


def _peers():
    x, y, c = lax.axis_index("x"), lax.axis_index("y"), lax.axis_index("c")
    out = []
    for k in range(1, N_DEV):
        px, py, pc = x ^ ((k >> 2) & 1), y ^ ((k >> 1) & 1), c ^ (k & 1)
        out.append((k, (px, py, pc), 4 * px + 2 * py + pc))
    return 4 * x + 2 * y + c, out


def _split_copies(src_refs, land_refs, send_sems, recv_sems, gather):
    me, peers = _peers()
    out = []
    for a, (src_ref, land_ref) in enumerate(zip(src_refs, land_refs)):
        for k, dev, blk in peers:
            common = dict(send_sem=send_sems.at[a * N_DEV + k], recv_sem=recv_sems.at[a * N_DEV + k], device_id=dev,
                          device_id_type=pl.DeviceIdType.MESH)
            src = src_ref if gather else src_ref.at[blk]
            out.append((pltpu.make_async_remote_copy(src_ref=src, dst_ref=land_ref.at[me], **common),
                        pltpu.make_async_remote_copy(src_ref=src, dst_ref=land_ref.at[blk], **common)))
    return out


def _exchange_start(srcs, gather, *, name, after=None):
    n = len(srcs)
    lands = [lax.empty((N_DEV,) + (s.shape if gather else s.shape[1:]), s.dtype) for s in srcs]

    def body(*refs):
        src_refs, land_refs = refs[:n], refs[n:2 * n]
        send_sems, recv_sems = refs[-2 * n - 3:-2 * n - 1]
        token = refs[-1]
        for outgoing, _ in _split_copies(src_refs, land_refs, send_sems, recv_sems, gather):
            outgoing.start()
        token[...] = jnp.zeros_like(token)

    res = pl.pallas_call(
        body, name=name,
        out_shape=(pltpu.SemaphoreType.DMA((n * N_DEV,)), pltpu.SemaphoreType.DMA((n * N_DEV,)),
                   *[pltpu.HBM(a.shape, a.dtype) for a in srcs + lands], jax.ShapeDtypeStruct((8, LANES), F32)),
        in_specs=(_HBM,) * (2 * n) + ((pl.BlockSpec(memory_space=pl.ANY),) if after is not None else ()),
        out_specs=(_SEM, _SEM) + (_HBM,) * (2 * n) + (pl.BlockSpec(memory_space=pltpu.VMEM),),
        input_output_aliases={i: i + 2 for i in range(2 * n)},
        compiler_params=pltpu.CompilerParams(has_side_effects=_EFFECT),
    )(*[pltpu.with_memory_space_constraint(a, pltpu.HBM) for a in srcs + lands],
      *([after] if after is not None else []))
    return res[0], res[1], list(res[2:2 + n]), list(res[2 + n:2 + 2 * n]), res[-1]


def _exchange_wait(started, after, gather, *, name):
    send_sems, recv_sems, srcs, lands, _ = started
    n = len(srcs)

    def body(*refs):
        src_refs, land_refs = refs[:n], refs[n:2 * n]
        send_sems, recv_sems = refs[2 * n:2 * n + 2]
        for outgoing, incoming in _split_copies(src_refs, land_refs, send_sems, recv_sems, gather):
            outgoing.wait_send()
            incoming.wait_recv()

    res = pl.pallas_call(
        body, name=name,
        out_shape=tuple(pltpu.HBM(a.shape, a.dtype) for a in srcs + lands),
        in_specs=(_HBM,) * (2 * n) + (_SEM, _SEM, pl.BlockSpec(memory_space=pl.ANY)), out_specs=(_HBM,) * (2 * n),
        input_output_aliases={i: i for i in range(2 * n)},
        compiler_params=pltpu.CompilerParams(has_side_effects=_EFFECT),
    )(*srcs, *lands, send_sems, recv_sems, after)
    return list(res[:n]), list(res[n:])


def _cols_to_blocks(g, *, name, tr=128):
    R, C = g.shape
    w = C // N_DEV
    tr = min(tr, R)

    def body(g_ref, o_ref):
        for p in range(N_DEV):
            o_ref[p] = g_ref[:, p * w:(p + 1) * w].astype(BF16)

    return pl.pallas_call(
        body, name=name, grid=(R // tr,),
        in_specs=[pl.BlockSpec((tr, C), lambda i: (i, 0))],
        out_specs=pl.BlockSpec((N_DEV, tr, w), lambda i: (0, i, 0)),
        out_shape=jax.ShapeDtypeStruct((N_DEV, R, w), BF16),
        compiler_params=_params(("parallel",)),
    )(g)


def _blocks_to_cols(b, *, name, tr=128):
    _, R, w = b.shape
    tr = min(tr, R)

    def body(b_ref, o_ref):
        o_ref[...] = jnp.concatenate([b_ref[p].astype(F32) for p in range(N_DEV)], axis=1).astype(o_ref.dtype)

    return pl.pallas_call(
        body, name=name, grid=(R // tr,),
        in_specs=[pl.BlockSpec((N_DEV, tr, w), lambda i: (0, i, 0))],
        out_specs=pl.BlockSpec((tr, N_DEV * w), lambda i: (i, 0)),
        out_shape=jax.ShapeDtypeStruct((R, N_DEV * w), b.dtype),
        compiler_params=_params(("parallel",)),
    )(b)


def _pack_rows(arrays, *, name, pick=None):
    B, _, w = arrays[0].shape
    rows = [a.shape[1] for a in arrays]
    first = 0
    if pick is not None:
        B, first = 1, pick

    def body(*refs):
        o_ref = refs[-1]
        r = 0
        for a_ref, n in zip(refs[:-1], rows):
            o_ref[0, r:r + n, :] = a_ref[0].astype(BF16)
            r += n

    return pl.pallas_call(
        body, name=name, grid=(B,),
        in_specs=[pl.BlockSpec((1, n, w), lambda b: (first + b, 0, 0)) for n in rows],
        out_specs=pl.BlockSpec((1, sum(rows), w), lambda b: (b, 0, 0)),
        out_shape=jax.ShapeDtypeStruct((B, sum(rows), w), BF16),
        compiler_params=_params(("parallel",)),
    )(*arrays)


def _unpack_rows(land, src, rows, *, name):
    _, R, w = land.shape
    src_spec = (pl.BlockSpec((1, R, w), lambda p: (p, 0, 0)) if src.ndim == 3
                else pl.BlockSpec((R, w), lambda p: (0, 0)))

    def body(land_ref, src_ref, *o_refs):
        me = 4 * lax.axis_index("x") + 2 * lax.axis_index("y") + lax.axis_index("c")
        mine = pl.program_id(0) == me
        r = 0
        for o_ref, n in zip(o_refs, rows):
            rows_i = slice(r, r + n)

            @pl.when(mine)
            def _(o_ref=o_ref, rows_i=rows_i):
                o_ref[0] = src_ref[0, rows_i, :] if src.ndim == 3 else src_ref[rows_i, :]

            @pl.when(jnp.logical_not(mine))
            def _(o_ref=o_ref, rows_i=rows_i):
                o_ref[0] = land_ref[0, rows_i, :]

            r += n

    return pl.pallas_call(
        body, name=name, grid=(N_DEV,),
        in_specs=[pl.BlockSpec((1, R, w), lambda p: (p, 0, 0)), src_spec],
        out_specs=[pl.BlockSpec((1, n, w), lambda p: (p, 0, 0)) for n in rows],
        out_shape=[jax.ShapeDtypeStruct((N_DEV, n, w), land.dtype) for n in rows],
        compiler_params=_params(("parallel",)),
    )(land, src)


def _to_blocks(w, axis):
    shape = w.shape
    k = shape[axis] // N_DEV
    w = w.reshape(shape[:axis] + (N_DEV, k) + shape[axis + 1:])
    return jnp.moveaxis(w, axis, 0)


def _from_blocks(wb, axis):
    w = jnp.moveaxis(wb, 0, axis)
    shape = w.shape
    return w.reshape(shape[:axis] + (shape[axis] * shape[axis + 1],) + shape[axis + 2:])


def _block_diag(w):
    n, k, _ = w.shape
    eye = jnp.eye(n, dtype=w.dtype)
    return (eye[:, None, :, None] * w[:, :, None, :]).reshape(n * k, n * k)


def _diag_blocks(wd):
    k = HEAD_DIM
    return jnp.stack([wd[h * k:(h + 1) * k, h * k:(h + 1) * k] for h in range(C_BLOCKS)])


def _pack(arrays):
    rows = []
    for a in arrays:
        flat = a.reshape(-1).astype(F32)
        pad = (-flat.shape[0]) % LANES
        rows.append(jnp.pad(flat, (0, pad)).reshape(-1, LANES))
    out = jnp.concatenate(rows, axis=0)
    return jnp.pad(out, ((0, (-out.shape[0]) % 8), (0, 0)))


def _unpack(packed, shapes):
    outs, r = [], 0
    for s in shapes:
        size = math.prod(s)
        nrows = -(-size // LANES)
        outs.append(packed[r:r + nrows].reshape(-1)[:size].reshape(s))
        r += nrows
    return outs


def _rope_tables(positions):
    inv = 1.0 / (ROPE_THETA ** (jnp.arange(0, HEAD_DIM, 2, dtype=F32) / HEAD_DIM))
    ang = positions.astype(F32)[:, None] * inv
    cos, sin = jnp.cos(ang), jnp.sin(ang)
    return jnp.tile(cos, (1, 4)), jnp.tile(jnp.concatenate([-sin, sin], axis=1), (1, 2))


def _layer_fwd(l, x, pos, cos_t, sin_t, W, before_mixer=None):
    tag = f"l{l}"
    saved = {'x0': x}
    x1, a1, u1 = _ffn_fwd(x, W['norm_ffn1'][l], W['ffn1_gate'][l], W['ffn1_up'][l], W['ffn1_down'][l],
                          name=f"ffn1_fwd_{tag}")
    if before_mixer is not None:
        before_mixer(l, x1)
    h = _rms_fwd(x1, W['norm_mix'][l], name=f"mixnorm_fwd_{tag}")
    proj = _mm(h, W['w_in'][l], 'nn', name=f"proj_{tag}", tm=512, tn=IN_COLS, tk=h.shape[1])
    qa, ka2, va2, qb, kb, vb, xc, gc = _split_rope(proj, cos_t, sin_t, name=f"split_{tag}")
    oa, lsea = _attn_fwd(qa, ka2, va2, A_MAX_DIST, name=f"attn_a_fwd_{tag}")
    obs, lsebs = [], []
    for bi, (window, d) in enumerate(B_BRANCHES):
        o, lse = _attn_fwd(qb[bi], kb[bi], vb[bi], window // d, name=f"attn_b{bi}_fwd_{tag}")
        obs.append(o)
        lsebs.append(lse)
    oc, hs = _rglru_fwd(xc, gc, pos, W['conv_w'][l], W['conv_b'][l], W['rg_w_r'][l], W['rg_b_r'][l],
                        W['rg_w_i'][l], W['rg_b_i'][l], W['rg_lambda'][l], name=f"rglru_fwd_{tag}")
    mix = _combine_fwd(oa, lsea, W['sinks'][l], obs, lsebs, oc, name=f"combine_fwd_{tag}")
    x2 = _mm(mix, W['w_out'][l], 'nn', name=f"outproj_{tag}", tm=512, tn=x.shape[1], tk=MIX_WIDTH, res=x1)
    x3, a2, u2 = _ffn_fwd(x2, W['norm_ffn2'][l], W['ffn2_gate'][l], W['ffn2_up'][l], W['ffn2_down'][l],
                          name=f"ffn2_fwd_{tag}")
    saved.update(a1=a1, u1=u1, x1=x1, h=h, qa=qa, ka2=ka2, va2=va2, qb=qb, kb=kb, vb=vb, xc=xc, gc=gc, oa=oa,
                 lsea=lsea, obs=obs, lsebs=lsebs, hs=hs, mix=mix, x2=x2, a2=a2, u2=u2)
    return x3, saved


def _ffn_grads(tag, which, x, g, dy, a, u, wg, wu, wd):
    T, D = x.shape
    F = wg.shape[1]
    dx, dg, n, act, da, du = _ffn_bwd(x, g, dy, a, u, wg, wu, wd, name=f"{which}_bwd_{tag}")
    fc = _ffn_chunk(F)
    d_gate = _mm(n, da, 'tn', name=f"{which}_dgate_{tag}", tm=512, tn=fc, tk=4096)
    d_up = _mm(n, du, 'tn', name=f"{which}_dup_{tag}", tm=512, tn=fc, tk=4096)
    d_down = _mm(act, dy, 'tn', name=f"{which}_ddown_{tag}", tm=fc, tn=D, tk=1024, alpha=0.5)
    return dx, dg, d_gate, d_up, d_down


def _layer_bwd(l, dx3, pos, cos_t, sin_t, W, S, on_grads=None):
    tag = f"l{l}"
    G = {}
    dx2, G['norm_ffn2'], G['ffn2_gate'], G['ffn2_up'], G['ffn2_down'] = _ffn_grads(
        tag, 'ffn2', S['x2'], W['norm_ffn2'][l], dx3, S['a2'], S['u2'], W['ffn2_gate'][l], W['ffn2_up'][l],
        W['ffn2_down'][l])
    D = dx2.shape[1]
    dmix = _mm(dx2, W['w_out'][l], 'nt', name=f"outproj_dx_{tag}", tm=512, tn=MIX_WIDTH, tk=D)
    G['w_out'] = _mm(S['mix'], dx2, 'tn', name=f"outproj_dw_{tag}", tm=MIX_WIDTH, tn=D, tk=2048)
    doa, dla, dobs, dlbs, doc, dsink = _combine_bwd(dmix, S['oa'], S['lsea'], W['sinks'][l], S['obs'], S['lsebs'],
                                                    name=f"combine_bwd_{tag}")
    G['attn_sinks'] = dsink.reshape(A_WIDTH // HEAD_DIM, HEAD_DIM)[:, 0]
    dqa, dka2, dva2 = _attn_bwd(S['qa'], S['ka2'], S['va2'], doa, S['lsea'], dla, A_MAX_DIST,
                                name=f"attn_a_bwd_{tag}")
    dqb, dkb, dvb = [], [], []
    for bi, (window, d) in enumerate(B_BRANCHES):
        dq, dk, dv = _attn_bwd(S['qb'][bi], S['kb'][bi], S['vb'][bi], dobs[bi], S['lsebs'][bi], dlbs[bi], window // d,
                               name=f"attn_b{bi}_bwd_{tag}")
        dqb.append(dq)
        dkb.append(dk)
        dvb.append(dv)
    (dxc, dgc, G['conv_w'], G['conv_b'], dwr, G['rg_b_r'], dwi, G['rg_b_i'], G['rg_lambda']) = _rglru_bwd(
        S['xc'], S['gc'], pos, S['hs'], doc, W['conv_w'][l], W['conv_b'][l], W['rg_w_r'][l], W['rg_b_r'][l],
        W['rg_w_i'][l], W['rg_b_i'][l], W['rg_lambda'][l], name=f"rglru_bwd_{tag}")
    G['rg_w_r'] = _diag_blocks(dwr)
    G['rg_w_i'] = _diag_blocks(dwi)
    dproj = _merge_dproj(dqa, dka2, dva2, dqb, dkb, dvb, dxc, dgc, cos_t, sin_t, name=f"merge_{tag}")
    dh = _mm(dproj, W['w_in'][l], 'nt', name=f"proj_dx_{tag}", tm=512, tn=D, tk=IN_COLS)
    G['w_in'] = _mm(S['h'], dproj, 'tn', name=f"proj_dw_{tag}", tm=512, tn=IN_COLS, tk=2048)
    g_mix = W['norm_mix'][l]
    if on_grads is not None:
        g_mix = g_mix + on_grads(l, 0, G)
    dx1, G['norm_mix'] = _rms_bwd(S['x1'], g_mix, dh, dx2, name=f"mixnorm_bwd_{tag}")
    dx0, G['norm_ffn1'], G['ffn1_gate'], G['ffn1_up'], G['ffn1_down'] = _ffn_grads(
        tag, 'ffn1', S['x0'], W['norm_ffn1'][l], dx1, S['a1'], S['u1'], W['ffn1_gate'][l], W['ffn1_up'][l],
        W['ffn1_down'][l])
    if on_grads is not None:
        on_grads(l, 1, G)
    return dx0, G


def _device_step(x, positions, loss_target, W, before_layer=None, on_grads=None, before_mixer=None):
    T = x.shape[0]
    pos = positions.reshape(T, 1)
    cos_t, sin_t = _rope_tables(positions)
    saved = []
    for l in range(DEPTH):
        if before_layer is not None:
            before_layer(l, x)
        x, S = _layer_fwd(l, x, pos, cos_t, sin_t, W, before_mixer)
        saved.append(S)
    loss, dx, dg_final = _loss_head(x, W['norm_final'], loss_target, name="loss_head")
    grads = [None] * DEPTH
    for l in reversed(range(DEPTH)):
        dx, grads[l] = _layer_bwd(l, dx, pos, cos_t, sin_t, W, saved[l], on_grads)
    return loss, dx, grads, dg_final


SHARD_AXIS = {'ffn1_gate': 2, 'ffn1_up': 2, 'ffn1_down': 1, 'w_in': 2, 'w_out': 1, 'ffn2_gate': 2, 'ffn2_up': 2,
              'ffn2_down': 1, 'conv_w': 2}


def kernel(x, positions, norm_ffn1, ffn1_gate, ffn1_up, ffn1_down, norm_mix, w_in, attn_sinks, conv_w, conv_b, rg_w_r, rg_b_r, rg_w_i, rg_b_i, rg_lambda, w_out, norm_ffn2, ffn2_gate, ffn2_up, ffn2_down, norm_final, loss_target, m_norm_ffn1, m_ffn1_gate, m_ffn1_up, m_ffn1_down, m_norm_mix, m_w_in, m_attn_sinks, m_conv_w, m_conv_b, m_rg_w_r, m_rg_b_r, m_rg_w_i, m_rg_b_i, m_rg_lambda, m_w_out, m_norm_ffn2, m_ffn2_gate, m_ffn2_up, m_ffn2_down, m_norm_final, v_norm_ffn1, v_ffn1_gate, v_ffn1_up, v_ffn1_down, v_norm_mix, v_w_in, v_attn_sinks, v_conv_w, v_conv_b, v_rg_w_r, v_rg_b_r, v_rg_w_i, v_rg_b_i, v_rg_lambda, v_w_out, v_norm_ffn2, v_ffn2_gate, v_ffn2_up, v_ffn2_down, v_norm_final):
    given = dict(locals())
    me = 4 * lax.axis_index("x") + 2 * lax.axis_index("y") + lax.axis_index("c")

    def by_width(names):
        classes = {}
        for n in names:
            classes.setdefault(given[n].shape[2], []).append(n)
        return list(classes.values())

    def pack(names, get, tag, pick=None):
        return [_pack_rows([get(n) for n in cls], name=f"pack{ci}_{tag}", pick=pick)
                for ci, cls in enumerate(by_width(names))]

    def unpack(names, lands, srcs, tag):
        out = {}
        for ci, cls in enumerate(by_width(names)):
            arrays = _unpack_rows(lands[ci], srcs[ci], [given[n].shape[1] for n in cls], name=f"unpack{ci}_{tag}")
            out.update(zip(cls, arrays))
        return out

    def gather_start(l, names, tag, after=None):
        return _exchange_start([p[0] for p in pack(names, lambda n: given[n], f"w{tag}_l{l}", pick=l)], True,
                               name=f"gather_start{tag}_l{l}", after=after)

    def gather_wait(l, names, tag, started, after):
        srcs, lands = _exchange_wait(started, after, True, name=f"gather_wait{tag}_l{l}")
        blocks = unpack(names, lands, srcs, f"w{tag}_l{l}")
        for n in names:
            if SHARD_AXIS[n] == 2:
                W[n][l] = _blocks_to_cols(blocks[n], name=f"cols_{n}_l{l}")
            else:
                W[n][l] = blocks[n].reshape(-1, blocks[n].shape[2])
        return lands[0]

    W = {n: [None] * DEPTH for n in BIG_NAMES}
    ffn1_names, later_names = SCATTER_STAGES[1], SCATTER_STAGES[0]
    first = gather_start(0, ffn1_names, "a")
    conv_full = _exchange([conv_w], True, name="gather_conv_w")[0]
    landed = gather_wait(0, ffn1_names, "a", first, conv_full)
    second = gather_start(0, later_names, "b", after=landed)
    started = second[4][0, 0]
    gathers = [None] * DEPTH

    def before_layer(l, x_in):
        if l > 0:
            gather_wait(l, BIG_NAMES, "", gathers[l], x_in)

    def before_mixer(l, x1):
        if l == 0:
            landed = gather_wait(0, later_names, "b", second, x1)
            token = 0.0
            for k in range(1, DEPTH):
                gathers[k] = gather_start(k, BIG_NAMES, "", after=landed)
                token = token + gathers[k][4][0, 0]
            W['norm_mix'][0] = W['norm_mix'][0] + token

    scatters = {}

    def on_grads(l, stage, G):
        def blocks_of(n):
            if SHARD_AXIS[n] == 2:
                return _cols_to_blocks(G[n], name=f"blocks_{n}_l{l}")
            return G[n].reshape(N_DEV, -1, G[n].shape[1])

        scatters[l, stage] = _exchange_start(pack(SCATTER_STAGES[stage], blocks_of, f"g{stage}_l{l}"), False,
                                             name=f"scatter_start{stage}_l{l}")
        token = scatters[l, stage][4][0, 0]
        if stage == 1 and l > 0:
            W['norm_ffn2'][l - 1] = W['norm_ffn2'][l - 1] + token
        return token

    W['conv_w'] = [_from_blocks(conv_full[:, l], 1) for l in range(DEPTH)]
    for n in ('norm_ffn1', 'norm_mix', 'norm_ffn2', 'conv_b', 'rg_lambda'):
        W[n] = [given[n][l][None, :] for l in range(DEPTH)]
    W['norm_final'] = norm_final[None, :]
    W['sinks'] = [jnp.repeat(attn_sinks[l], HEAD_DIM)[None, :] for l in range(DEPTH)]
    for n in ('rg_w_r', 'rg_w_i'):
        W[n] = [_block_diag(given[n][l]).astype(BF16) for l in range(DEPTH)]
    for n in ('rg_b_r', 'rg_b_i'):
        W[n] = [given[n][l].reshape(1, C_WIDTH) for l in range(DEPTH)]

    W['norm_ffn1'][0] = W['norm_ffn1'][0] + started

    loss_part, grad_x, grads, dg_final = _device_step(x[0], positions[0], loss_target[0], W, before_layer, on_grads,
                                                      before_mixer)
    loss = lax.psum(loss_part[0, 0], ("x", "y", "c"))

    small_shapes = [given[n].shape for n in SMALL_NAMES] + [(DEPTH, C_CONV, C_WIDTH)]
    small_grads = []
    for n in SMALL_NAMES:
        if n == 'norm_final':
            small_grads.append(dg_final.reshape(-1))
        else:
            small_grads.append(jnp.stack([grads[l][n].reshape(given[n].shape[1:]) for l in range(DEPTH)]))
    small_grads.append(jnp.stack([grads[l]['conv_w'] for l in range(DEPTH)]))
    small_parts = _exchange([_pack(small_grads)], True, name="gather_small_grads")[0]

    out = {}
    for stage in (0, 1):
        parts = {}
        for l in reversed(range(DEPTH)):
            last = stage == 1 and l == 0
            srcs, lands = _exchange_wait(scatters[l, stage], out['w_in'][1] if last else grad_x, False,
                                         name=f"scatter_wait{stage}_l{l}")
            parts[l] = unpack(SCATTER_STAGES[stage], lands, srcs, f"g{stage}_l{l}")
        for n in SCATTER_STAGES[stage]:
            shape = given[n].shape
            two_d = (shape[0] * shape[1], shape[2])
            res = None
            for l in reversed(range(DEPTH)):
                res = _adamw(parts[l][n], given[n].reshape(two_d), given['m_' + n].reshape(two_d),
                             given['v_' + n].reshape(two_d), name=f"adamw_{n}_l{l}", part=l, prev=res)
            out[n] = [r.reshape(shape) for r in res]

    w_small = [given[n] for n in SMALL_NAMES]
    m_small = [given['m_' + n] for n in SMALL_NAMES]
    v_small = [given['v_' + n] for n in SMALL_NAMES]
    zeros_cw = jnp.zeros((DEPTH, C_CONV, C_WIDTH), F32)
    res = _adamw(small_parts, _pack(w_small + [zeros_cw]), _pack(m_small + [zeros_cw]), _pack(v_small + [zeros_cw]),
                 name="adamw_small", tr=8)
    unpacked = [_unpack(r, small_shapes) for r in res]
    for i, n in enumerate(SMALL_NAMES):
        out[n] = [u[i] for u in unpacked]

    k = conv_w.shape[2]
    g_cw = lax.dynamic_slice_in_dim(unpacked[0][-1], me * k, k, axis=2)
    zero_parts = jnp.zeros((N_DEV - 1,) + (8, LANES), F32)
    res = _adamw(jnp.concatenate([_pack([g_cw])[None], zero_parts]), _pack([conv_w]), _pack([m_conv_w]),
                 _pack([v_conv_w]), name="adamw_conv_w", tr=8)
    out['conv_w'] = [_unpack(r, [conv_w.shape])[0] for r in res]

    outputs = [loss, grad_x[None]]
    for i in range(4):
        outputs += [out[n][i] for n in WEIGHT_NAMES]
    return tuple(outputs)
```

```python
import functools
import math

import jax
import jax.numpy as jnp
from jax import lax
from jax.experimental import pallas as pl
from jax.experimental.pallas import tpu as pltpu

F32 = jnp.float32
BF16 = jnp.bfloat16

N_DEV = 8
DEPTH = 4
HEAD_DIM = 64
LANES = 128
QBLK = 128
A_WIDTH = 256
A_KV_WIDTH = 128
B_WIDTH = 384
C_WIDTH = 384
C_BLOCKS = 6
C_CONV = 4
C_EXP = 8.0
MIX_WIDTH = A_WIDTH + B_WIDTH + C_WIDTH
IN_COLS = A_WIDTH + 2 * A_KV_WIDTH + 3 * B_WIDTH + 2 * C_WIDTH
A_MAX_DIST = 127
B_BRANCHES = ((128, 1), (512, 4), (2048, 16))
ROPE_THETA = 10000.0
EPS = 1e-6
SCALE = HEAD_DIM ** -0.5

ADAM_LR = 0.001
ADAM_B1 = 0.9
ADAM_B2 = 0.999
ADAM_EPS = 1e-08
ADAM_WD = 0.01
ADAM_STEP = 10

ATTN_CHUNK = 2048
ATTN_FWD_UNROLL = True
ATTN_BWD_UNROLL = True
VMEM_LIMIT = 56 * 1024 * 1024
SMALL_TILE = 256

NT_DIMS = (((1,), (1,)), ((), ()))
TN_DIMS = (((0,), (0,)), ((), ()))
NN_DIMS = (((1,), (0,)), ((), ()))

WEIGHT_NAMES = ['norm_ffn1', 'ffn1_gate', 'ffn1_up', 'ffn1_down', 'norm_mix', 'w_in', 'attn_sinks', 'conv_w',
                'conv_b', 'rg_w_r', 'rg_b_r', 'rg_w_i', 'rg_b_i', 'rg_lambda', 'w_out', 'norm_ffn2', 'ffn2_gate',
                'ffn2_up', 'ffn2_down', 'norm_final']
BIG_NAMES = ['ffn1_gate', 'ffn1_up', 'ffn1_down', 'w_in', 'w_out', 'ffn2_gate', 'ffn2_up', 'ffn2_down']
SCATTER_STAGES = (['ffn2_gate', 'ffn2_up', 'ffn2_down', 'w_out', 'w_in'], ['ffn1_gate', 'ffn1_up', 'ffn1_down'])
SMALL_NAMES = ['norm_ffn1', 'norm_mix', 'norm_ffn2', 'norm_final', 'attn_sinks', 'conv_b', 'rg_w_r', 'rg_b_r',
               'rg_w_i', 'rg_b_i', 'rg_lambda']


def _params(sem, vmem=VMEM_LIMIT):
    return pltpu.CompilerParams(dimension_semantics=sem, vmem_limit_bytes=vmem)


def _dot(a, b, dims=NN_DIMS):
    return lax.dot_general(a, b, dims, preferred_element_type=F32)


def _sigmoid(x):
    return 1.0 / (1.0 + jnp.exp(-x))


def _mm(a, b, mode, *, name, tm=512, tn=512, tk=512, out_dtype=F32, alpha=1.0, res=None):
    if mode == 'nn':
        (M, K), N = a.shape, b.shape[1]
    elif mode == 'nt':
        (M, K), N = a.shape, b.shape[0]
    else:
        (K, M), N = a.shape, b.shape[1]
    tm, tn, tk = min(tm, M), min(tn, N), min(tk, K)
    ni, nj, nk = M // tm, N // tn, K // tk
    assert ni * tm == M and nj * tn == N and nk * tk == K, (name, a.shape, b.shape, tm, tn, tk)
    if mode == 'tn':
        a_spec = pl.BlockSpec((tk, tm), lambda j, i, k: (k, i))
    else:
        a_spec = pl.BlockSpec((tm, tk), lambda j, i, k: (i, k))
    if mode == 'nt':
        b_spec = pl.BlockSpec((tn, tk), lambda j, i, k: (j, k))
    else:
        b_spec = pl.BlockSpec((tk, tn), lambda j, i, k: (k, j))
    dims = {'nn': NN_DIMS, 'nt': NT_DIMS, 'tn': TN_DIMS}[mode]
    o_spec = pl.BlockSpec((tm, tn), lambda j, i, k: (i, j))
    has_res = res is not None

    def body(*refs):
        if has_res:
            a_ref, b_ref, r_ref, o_ref = refs[:4]
        else:
            a_ref, b_ref, o_ref = refs[:3]
        part = _dot(a_ref[...].astype(BF16), b_ref[...].astype(BF16), dims)

        def finish(acc):
            out = acc * alpha if alpha != 1.0 else acc
            if has_res:
                out = r_ref[...] + out
            o_ref[...] = out.astype(out_dtype)

        if nk == 1:
            finish(part)
        else:
            acc_ref = refs[-1]
            k = pl.program_id(2)

            @pl.when(k == 0)
            def _():
                acc_ref[...] = part

            @pl.when(k > 0)
            def _():
                acc_ref[...] += part

            @pl.when(k == nk - 1)
            def _():
                finish(acc_ref[...])

    in_specs = [a_spec, b_spec] + ([o_spec] if has_res else [])
    operands = [a, b] + ([res] if has_res else [])
    return pl.pallas_call(
        body, name=name, grid=(nj, ni, nk), in_specs=in_specs, out_specs=o_spec,
        out_shape=jax.ShapeDtypeStruct((M, N), out_dtype),
        scratch_shapes=[pltpu.VMEM((tm, tn), F32)] if nk > 1 else [],
        compiler_params=_params(("parallel", "parallel", "arbitrary")),
    )(*operands)


def _rms_fwd(x, g, *, name, tm=512):
    T, D = x.shape
    tm = min(tm, T)

    def body(x_ref, g_ref, o_ref):
        xv = x_ref[...]
        rstd = lax.rsqrt(jnp.mean(xv * xv, axis=-1, keepdims=True) + EPS)
        o_ref[...] = (xv * rstd * g_ref[...]).astype(BF16)

    return pl.pallas_call(
        body, name=name, grid=(T // tm,),
        in_specs=[pl.BlockSpec((tm, D), lambda i: (i, 0)), pl.BlockSpec((1, D), lambda i: (0, 0))],
        out_specs=pl.BlockSpec((tm, D), lambda i: (i, 0)),
        out_shape=jax.ShapeDtypeStruct((T, D), BF16),
        compiler_params=_params(("parallel",)),
    )(x, g)


def _rms_bwd_math(xv, g, dn):
    rstd = lax.rsqrt(jnp.mean(xv * xv, axis=-1, keepdims=True) + EPS)
    xhat = xv * rstd
    dxhat = dn * g
    dx = rstd * (dxhat - xhat * jnp.mean(dxhat * xhat, axis=-1, keepdims=True))
    return dx, jnp.sum(dn * xhat, axis=0, keepdims=True)


def _rms_bwd(x, g, dn, dres, *, name, tm=512):
    T, D = x.shape
    tm = min(tm, T)

    def body(x_ref, g_ref, dn_ref, dres_ref, dx_ref, dg_ref):
        dx, dg = _rms_bwd_math(x_ref[...], g_ref[...], dn_ref[...])
        dx_ref[...] = dres_ref[...] + dx

        @pl.when(pl.program_id(0) == 0)
        def _():
            dg_ref[...] = jnp.zeros_like(dg_ref)

        dg_ref[...] += dg

    row = pl.BlockSpec((tm, D), lambda i: (i, 0))
    vec = pl.BlockSpec((1, D), lambda i: (0, 0))
    return pl.pallas_call(
        body, name=name, grid=(T // tm,),
        in_specs=[row, vec, row, row], out_specs=[row, vec],
        out_shape=[jax.ShapeDtypeStruct((T, D), F32), jax.ShapeDtypeStruct((1, D), F32)],
        compiler_params=_params(("arbitrary",)),
    )(x, g, dn, dres)


def _loss_head(x, g, target, *, name, tm=512):
    T, D = x.shape
    tm = min(tm, T)

    def body(x_ref, g_ref, t_ref, loss_ref, dx_ref, dg_ref):
        xv = x_ref[...]
        g = g_ref[...]
        rstd = lax.rsqrt(jnp.mean(xv * xv, axis=-1, keepdims=True) + EPS)
        y = xv * rstd * g
        err = y - t_ref[...]
        part = 0.5 * jnp.sum(jnp.mean(err * err, axis=-1, keepdims=True), axis=0, keepdims=True)
        dx, dg = _rms_bwd_math(xv, g, err * (1.0 / D))
        dx_ref[...] = dx

        @pl.when(pl.program_id(0) == 0)
        def _():
            dg_ref[...] = jnp.zeros_like(dg_ref)
            loss_ref[...] = jnp.zeros_like(loss_ref)

        dg_ref[...] += dg
        loss_ref[...] += jnp.broadcast_to(part, loss_ref.shape)

    row = pl.BlockSpec((tm, D), lambda i: (i, 0))
    vec = pl.BlockSpec((1, D), lambda i: (0, 0))
    lspec = pl.BlockSpec((1, LANES), lambda i: (0, 0))
    return pl.pallas_call(
        body, name=name, grid=(T // tm,),
        in_specs=[row, vec, row], out_specs=[lspec, row, vec],
        out_shape=[jax.ShapeDtypeStruct((1, LANES), F32), jax.ShapeDtypeStruct((T, D), F32),
                   jax.ShapeDtypeStruct((1, D), F32)],
        compiler_params=_params(("arbitrary",)),
    )(x, g, target)


def _resident(shape):
    return pl.BlockSpec(shape, lambda i: (0,) * len(shape), pipeline_mode=pl.Buffered(1))


def _ffn_chunk(F):
    for c in (1408, 1024, 512, 256, 128):
        if F % c == 0:
            return c
    return F


def _ffn_fwd(x, g, wg, wu, wd, *, name, tm=256):
    T, D = x.shape
    F = wg.shape[1]
    tm = min(tm, T)
    fc = _ffn_chunk(F)

    def body(x_ref, g_ref, wg_ref, wu_ref, wd_ref, o_ref, a_ref, u_ref):
        xv = x_ref[...]
        rstd = lax.rsqrt(jnp.mean(xv * xv, axis=-1, keepdims=True) + EPS)
        n = (xv * rstd * g_ref[...]).astype(BF16)
        acc = jnp.zeros((tm, D), F32)
        for c in range(F // fc):
            sl = slice(c * fc, (c + 1) * fc)
            a = _dot(n, wg_ref[:, sl])
            u = _dot(n, wu_ref[:, sl])
            a_ref[:, sl] = a.astype(BF16)
            u_ref[:, sl] = u.astype(BF16)
            act = (a * _sigmoid(a) * u).astype(BF16)
            acc = acc + _dot(act, wd_ref[sl, :])
        o_ref[...] = xv + 0.5 * acc

    row = pl.BlockSpec((tm, D), lambda i: (i, 0))
    hid = pl.BlockSpec((tm, F), lambda i: (i, 0))
    return pl.pallas_call(
        body, name=name, grid=(T // tm,),
        in_specs=[row, pl.BlockSpec((1, D), lambda i: (0, 0)),
                  _resident((D, F)), _resident((D, F)), _resident((F, D))],
        out_specs=[row, hid, hid],
        out_shape=[jax.ShapeDtypeStruct((T, D), F32), jax.ShapeDtypeStruct((T, F), BF16),
                   jax.ShapeDtypeStruct((T, F), BF16)],
        compiler_params=_params(("parallel",)),
    )(x, g, wg, wu, wd)


def _ffn_bwd(x, g, dy, a, u, wg, wu, wd, *, name, tm=256):
    T, D = x.shape
    F = wg.shape[1]
    tm = min(tm, T)
    fc = _ffn_chunk(F)

    def body(x_ref, g_ref, dy_ref, a_ref, u_ref, wg_ref, wu_ref, wd_ref,
             dx_ref, dg_ref, n_ref, act_ref, da_ref, du_ref):
        xv = x_ref[...]
        g = g_ref[...]
        rstd = lax.rsqrt(jnp.mean(xv * xv, axis=-1, keepdims=True) + EPS)
        n_ref[...] = (xv * rstd * g).astype(BF16)
        dy = dy_ref[...]
        dyh = (0.5 * dy).astype(BF16)
        dn = jnp.zeros((tm, D), F32)
        for c in range(F // fc):
            sl = slice(c * fc, (c + 1) * fc)
            av = a_ref[:, sl].astype(F32)
            uv = u_ref[:, sl].astype(F32)
            dact = _dot(dyh, wd_ref[sl, :], NT_DIMS)
            s = _sigmoid(av)
            silu = av * s
            act_ref[:, sl] = (silu * uv).astype(BF16)
            da = (dact * uv * (s * (1.0 + av * (1.0 - s)))).astype(BF16)
            du = (dact * silu).astype(BF16)
            da_ref[:, sl] = da
            du_ref[:, sl] = du
            dn = dn + _dot(da, wg_ref[:, sl], NT_DIMS) + _dot(du, wu_ref[:, sl], NT_DIMS)
        dx, dg = _rms_bwd_math(xv, g, dn)
        dx_ref[...] = dy + dx

        @pl.when(pl.program_id(0) == 0)
        def _():
            dg_ref[...] = jnp.zeros_like(dg_ref)

        dg_ref[...] += dg

    row = pl.BlockSpec((tm, D), lambda i: (i, 0))
    hid = pl.BlockSpec((tm, F), lambda i: (i, 0))
    vec = pl.BlockSpec((1, D), lambda i: (0, 0))
    return pl.pallas_call(
        body, name=name, grid=(T // tm,),
        in_specs=[row, vec, row, hid, hid,
                  _resident((D, F)), _resident((D, F)), _resident((F, D))],
        out_specs=[row, vec, row, hid, hid, hid],
        out_shape=[jax.ShapeDtypeStruct((T, D), F32), jax.ShapeDtypeStruct((1, D), F32),
                   jax.ShapeDtypeStruct((T, D), BF16), jax.ShapeDtypeStruct((T, F), BF16),
                   jax.ShapeDtypeStruct((T, F), BF16), jax.ShapeDtypeStruct((T, F), BF16)],
        compiler_params=_params(("arbitrary",)),
    )(x, g, dy, a, u, wg, wu, wd)


def _lane_iota(shape):
    return lax.broadcasted_iota(jnp.int32, shape, 1)


def _rope_partner(x):
    first_half = (_lane_iota(x.shape) & (HEAD_DIM - 1)) < HEAD_DIM // 2
    return jnp.where(first_half, pltpu.roll(x, LANES - HEAD_DIM // 2, 1), pltpu.roll(x, HEAD_DIM // 2, 1))


def _swap_heads(x):
    return pltpu.roll(x, HEAD_DIM, 1)


def _undilate(blk_ref, d, stage):
    if d == 1:
        return blk_ref[...]
    n, width = blk_ref.shape
    W = width // d
    for r in range(d):
        for g in range(W // LANES):
            stage.at[g][pl.ds(r, n, stride=d), :] = blk_ref[:, r * W + g * LANES:r * W + (g + 1) * LANES]
    return jnp.concatenate([stage.at[g][...] for g in range(W // LANES)], axis=1)


def _dilate_into(out_ref, value, d, stage):
    if d == 1:
        out_ref[...] = value.astype(out_ref.dtype)
        return
    n = value.shape[0] // d
    W = value.shape[1]
    for g in range(W // LANES):
        stage.at[g][...] = value[:, g * LANES:(g + 1) * LANES]
    for r in range(d):
        for g in range(W // LANES):
            out_ref[:, r * W + g * LANES:r * W + (g + 1) * LANES] = (
                stage.at[g][pl.ds(r, n, stride=d), :].astype(out_ref.dtype))


def _dilated_spec(tm, d, W):
    return pl.BlockSpec((tm // d, d * W), lambda i: (i, 0))


def _stage(tm, W):
    return pltpu.VMEM((W // LANES, tm, LANES), F32)


DILATIONS = tuple(d for _, d in B_BRANCHES)


def _split_rope(proj, cos_t, sin_t, *, name, tm=512):
    T = proj.shape[0]
    tm = min(tm, T)
    nd = len(DILATIONS)

    def body(p_ref, c_ref, s_ref, qa_ref, ka_ref, va_ref, *rest):
        b_refs = rest[:3 * nd]
        xc_ref, gc_ref, stage = rest[3 * nd:]
        cos = c_ref[...]
        sin = s_ref[...]

        def rope(x):
            return x * cos + _rope_partner(x) * sin

        lo = _lane_iota((tm, LANES)) < HEAD_DIM
        col = 0
        for j in range(A_WIDTH // LANES):
            qa_ref[:, j * LANES:(j + 1) * LANES] = (rope(p_ref[:, col:col + LANES]) * SCALE).astype(BF16)
            col += LANES
        kr = rope(p_ref[:, col:col + LANES])
        col += LANES
        vr = p_ref[:, col:col + LANES]
        col += LANES
        for src, dst in ((kr, ka_ref), (vr, va_ref)):
            sw = _swap_heads(src)
            dst[:, 0:LANES] = jnp.where(lo, src, sw).astype(BF16)
            dst[:, LANES:2 * LANES] = jnp.where(lo, sw, src).astype(BF16)
        for which, (roped, scale) in enumerate(((True, SCALE), (True, 1.0), (False, 1.0))):
            parts = []
            for j in range(B_WIDTH // LANES):
                v = p_ref[:, col:col + LANES]
                parts.append(rope(v) * scale if roped else v)
                col += LANES
            value = jnp.concatenate(parts, axis=1)
            for di, d in enumerate(DILATIONS):
                _dilate_into(b_refs[which * nd + di], value, d, stage)
        xc_ref[...] = p_ref[:, col:col + C_WIDTH]
        gc_ref[...] = p_ref[:, col + C_WIDTH:col + 2 * C_WIDTH]

    def row(w):
        return pl.BlockSpec((tm, w), lambda i: (i, 0))

    out_specs = [row(A_WIDTH)] * 3 + [_dilated_spec(tm, d, B_WIDTH) for _ in range(3) for d in DILATIONS]
    out_specs += [row(C_WIDTH)] * 2
    out_shape = [jax.ShapeDtypeStruct((T, A_WIDTH), BF16)] * 3
    out_shape += [jax.ShapeDtypeStruct((T // d, d * B_WIDTH), BF16) for _ in range(3) for d in DILATIONS]
    out_shape += [jax.ShapeDtypeStruct((T, C_WIDTH), F32)] * 2
    res = pl.pallas_call(
        body, name=name, grid=(T // tm,),
        in_specs=[row(IN_COLS), row(LANES), row(LANES)], out_specs=out_specs, out_shape=out_shape,
        scratch_shapes=[_stage(tm, B_WIDTH)],
        compiler_params=_params(("parallel",)),
    )(proj, cos_t, sin_t)
    qa, ka2, va2 = res[:3]
    qb, kb, vb = (list(res[3 + i * nd:3 + (i + 1) * nd]) for i in range(3))
    return qa, ka2, va2, qb, kb, vb, res[-2], res[-1]


def _merge_dproj(dqa, dka2, dva2, dqb, dkb, dvb, dxc, dgc, cos_t, sin_t, *, name, tm=512):
    T = dqa.shape[0]
    tm = min(tm, T)
    nb = len(dqb)

    def body(*refs):
        dqa_ref, dka_ref, dva_ref = refs[:3]
        dqb_refs = refs[3:3 + nb]
        dkb_refs = refs[3 + nb:3 + 2 * nb]
        dvb_refs = refs[3 + 2 * nb:3 + 3 * nb]
        dxc_ref, dgc_ref, c_ref, s_ref, o_ref, stage = refs[3 + 3 * nb:]
        cos = c_ref[...]
        sin = s_ref[...]

        def rope_t(dy):
            return dy * cos - _rope_partner(dy) * sin

        lo = _lane_iota((tm, LANES)) < HEAD_DIM
        col = 0
        for j in range(A_WIDTH // LANES):
            o_ref[:, col:col + LANES] = (rope_t(dqa_ref[:, j * LANES:(j + 1) * LANES]) * SCALE).astype(BF16)
            col += LANES
        for src, roped in ((dka_ref, True), (dva_ref, False)):
            b0 = src[:, 0:LANES]
            b1 = src[:, LANES:2 * LANES]
            v = jnp.where(lo, b0 + _swap_heads(b0), b1 + _swap_heads(b1))
            if roped:
                v = rope_t(v)
            o_ref[:, col:col + LANES] = v.astype(BF16)
            col += LANES
        for group, roped, scale in ((dqb_refs, True, SCALE), (dkb_refs, True, 1.0), (dvb_refs, False, 1.0)):
            total = _undilate(group[0], DILATIONS[0], stage)
            for r, d in zip(group[1:], DILATIONS[1:]):
                total = total + _undilate(r, d, stage)
            for j in range(B_WIDTH // LANES):
                v = total[:, j * LANES:(j + 1) * LANES]
                if roped:
                    v = rope_t(v) * scale
                o_ref[:, col:col + LANES] = v.astype(BF16)
                col += LANES
        o_ref[:, col:col + C_WIDTH] = dxc_ref[...].astype(BF16)
        o_ref[:, col + C_WIDTH:col + 2 * C_WIDTH] = dgc_ref[...].astype(BF16)

    def row(w):
        return pl.BlockSpec((tm, w), lambda i: (i, 0))

    ins = [dqa, dka2, dva2, *dqb, *dkb, *dvb, dxc, dgc, cos_t, sin_t]
    in_specs = [row(A_WIDTH)] * 3 + [_dilated_spec(tm, d, B_WIDTH) for _ in range(3) for d in DILATIONS]
    in_specs += [row(C_WIDTH)] * 2 + [row(LANES)] * 2
    return pl.pallas_call(
        body, name=name, grid=(T // tm,), in_specs=in_specs,
        out_specs=row(IN_COLS),
        out_shape=jax.ShapeDtypeStruct((T, IN_COLS), BF16),
        scratch_shapes=[_stage(tm, B_WIDTH)],
        compiler_params=_params(("parallel",)),
    )(*ins)


def _band_masks(max_dist):
    row = lax.broadcasted_iota(jnp.int32, (QBLK, 2 * QBLK), 0)
    key = lax.broadcasted_iota(jnp.int32, (QBLK, 2 * QBLK), 1)
    dist = row + QBLK - key
    wide = jnp.logical_and(dist >= 0, dist <= max_dist)
    return wide, wide[:, :QBLK], key >= QBLK


def _head_masks(rows=QBLK):
    lo = _lane_iota((rows, LANES)) < HEAD_DIM
    return lo, jnp.logical_not(lo)


def _keep(hm, x):
    return x * jnp.where(hm, 1.0, 0.0).astype(x.dtype)


def _head_col(x, hm):
    return jnp.max(jnp.where(hm, x, -jnp.inf), axis=1, keepdims=True)


def _attn_specs(R, C):
    chunk = min(ATTN_CHUNK, R)
    nb = chunk // QBLK
    nch = R // chunk
    main = pl.BlockSpec((chunk, LANES), lambda j, c: (c, j))
    prev = pl.BlockSpec((QBLK, LANES), lambda j, c: (jnp.maximum(c * nb - 1, 0), j))
    nxt = pl.BlockSpec((QBLK, LANES), lambda j, c: (jnp.minimum((c + 1) * nb, R // QBLK - 1), j))
    return chunk, nb, nch, main, prev, nxt


def _attn_fwd(q, k, v, max_dist, *, name):
    R, C = q.shape
    chunk, nb, nch, main, prev, _ = _attn_specs(R, C)

    def body(q_ref, k_ref, v_ref, kp_ref, vp_ref, o_ref, lse_ref):
        c = pl.program_id(1)
        wide_mask, _, own_block = _band_masks(max_dist)
        heads = _head_masks()

        def block(q_blk, kk, vv, mask):
            q2 = jnp.concatenate([_keep(hm, q_blk) for hm in heads], axis=0)
            s = jnp.where(jnp.concatenate([mask, mask], axis=0), _dot(q2, kk, NT_DIMS), -jnp.inf)
            m = jnp.max(jnp.maximum(s[:, :QBLK], s[:, QBLK:]), axis=1, keepdims=True)
            p = jnp.exp(s - m)
            l = jnp.sum(p[:, :QBLK] + p[:, QBLK:], axis=1, keepdims=True)
            o2 = _dot(p.astype(BF16), vv) / l
            lse2 = jnp.broadcast_to(m + jnp.log(l), (2 * QBLK, LANES))
            return (jnp.where(heads[0], o2[:QBLK], o2[QBLK:]), jnp.where(heads[0], lse2[:QBLK], lse2[QBLK:]))

        first = pl.ds(0, QBLK)
        o0, l0 = block(q_ref[first, :], jnp.concatenate([kp_ref[...], k_ref[first, :]], axis=0),
                       jnp.concatenate([vp_ref[...], v_ref[first, :]], axis=0),
                       jnp.logical_and(wide_mask, jnp.logical_or(own_block, c > 0)))
        o_ref[first, :] = o0
        lse_ref[first, :] = l0

        def loop(qb, carry):
            cur = pl.ds(pl.multiple_of(qb * QBLK, QBLK), QBLK)
            both = pl.ds(pl.multiple_of((qb - 1) * QBLK, QBLK), 2 * QBLK)
            o, l = block(q_ref[cur, :], k_ref[both, :], v_ref[both, :], wide_mask)
            o_ref[cur, :] = o
            lse_ref[cur, :] = l
            return carry

        if nb > 1:
            lax.fori_loop(1, nb, loop, 0, unroll=ATTN_FWD_UNROLL)

    return pl.pallas_call(
        body, name=name, grid=(C // LANES, nch),
        in_specs=[main, main, main, prev, prev], out_specs=[main, main],
        out_shape=[jax.ShapeDtypeStruct((R, C), F32), jax.ShapeDtypeStruct((R, C), F32)],
        compiler_params=_params(("parallel", "parallel")),
    )(q, k, v, k, v)


def _attn_bwd(q, k, v, do, lse, delta, max_dist, *, name):
    R, C = q.shape
    chunk, nb, nch, main, prev, nxt = _attn_specs(R, C)

    def body(q_ref, k_ref, v_ref, do_ref, lse_ref, dl_ref, kp_ref, vp_ref, qn_ref, don_ref, lsen_ref, dln_ref,
             dq_ref, dk_ref, dv_ref):
        c = pl.program_id(1)
        wide_mask, prev_mask, own_block = _band_masks(max_dist)
        heads = _head_masks()

        def pair(q_blk, do_blk, lse_blk, dl_blk, kk, vv, mask, want_dq=True):
            q2 = jnp.concatenate([_keep(hm, q_blk) for hm in heads], axis=0)
            do2 = jnp.concatenate([_keep(hm, do_blk) for hm in heads], axis=0)
            lse2 = jnp.concatenate([_head_col(lse_blk, hm) for hm in heads], axis=0)
            dl2 = jnp.concatenate([_head_col(dl_blk, hm) for hm in heads], axis=0)
            p = jnp.where(jnp.concatenate([mask, mask], axis=0), jnp.exp(_dot(q2, kk, NT_DIMS) - lse2), 0.0)
            ds = (p * (_dot(do2, vv, NT_DIMS) - dl2)).astype(BF16)
            dq = None
            if want_dq:
                k2 = jnp.concatenate([_keep(khm, kk) for khm in _head_masks(kk.shape[0])], axis=0)
                dq = _dot(jnp.concatenate([ds[:QBLK], ds[QBLK:]], axis=1), k2)
            return dq, _dot(ds, q2, TN_DIMS), _dot(p.astype(BF16), do2, TN_DIMS)

        dk_ref[...] = jnp.zeros_like(dk_ref)
        dv_ref[...] = jnp.zeros_like(dv_ref)

        first = pl.ds(0, QBLK)
        dq0, dkk0, dvv0 = pair(q_ref[first, :], do_ref[first, :], lse_ref[first, :], dl_ref[first, :],
                               jnp.concatenate([kp_ref[...], k_ref[first, :]], axis=0),
                               jnp.concatenate([vp_ref[...], v_ref[first, :]], axis=0),
                               jnp.logical_and(wide_mask, jnp.logical_or(own_block, c > 0)))
        dq_ref[first, :] = dq0
        dk_ref[first, :] += dkk0[QBLK:, :]
        dv_ref[first, :] += dvv0[QBLK:, :]

        def loop(qb, carry):
            cur = pl.ds(pl.multiple_of(qb * QBLK, QBLK), QBLK)
            both = pl.ds(pl.multiple_of((qb - 1) * QBLK, QBLK), 2 * QBLK)
            dq, dkk, dvv = pair(q_ref[cur, :], do_ref[cur, :], lse_ref[cur, :], dl_ref[cur, :],
                                k_ref[both, :], v_ref[both, :], wide_mask)
            dq_ref[cur, :] = dq
            dk_ref[both, :] += dkk
            dv_ref[both, :] += dvv
            return carry

        if nb > 1:
            lax.fori_loop(1, nb, loop, 0, unroll=ATTN_BWD_UNROLL)

        last = pl.ds((nb - 1) * QBLK, QBLK)
        _, dk_n, dv_n = pair(qn_ref[...], don_ref[...], lsen_ref[...], dln_ref[...], k_ref[last, :], v_ref[last, :],
                             jnp.logical_and(prev_mask, c < nch - 1), want_dq=False)
        dk_ref[last, :] += dk_n
        dv_ref[last, :] += dv_n

    return pl.pallas_call(
        body, name=name, grid=(C // LANES, nch),
        in_specs=[main] * 6 + [prev, prev] + [nxt] * 4, out_specs=[main, main, main],
        out_shape=[jax.ShapeDtypeStruct((R, C), F32)] * 3,
        compiler_params=_params(("parallel", "parallel")),
    )(q, k, v, do, lse, delta, k, v, q, do, lse, delta)


def _head_sum(x):
    r = lax.broadcasted_iota(jnp.int32, (LANES, LANES), 0) // HEAD_DIM
    c = lax.broadcasted_iota(jnp.int32, (LANES, LANES), 1) // HEAD_DIM
    ones = jnp.where(r == c, 1.0, 0.0).astype(BF16)
    outs = []
    for j in range(x.shape[1] // LANES):
        rem = x[:, j * LANES:(j + 1) * LANES]
        acc = jnp.zeros(rem.shape, F32)
        for _ in range(3):
            part = rem.astype(BF16)
            acc = acc + _dot(part, ones)
            rem = rem - part.astype(F32)
        outs.append(acc)
    return outs[0] if len(outs) == 1 else jnp.concatenate(outs, axis=1)


def _branch_weights(lses):
    m = functools.reduce(jnp.maximum, lses)
    es = [jnp.exp(l - m) for l in lses]
    den = functools.reduce(lambda a, b: a + b, es)
    return [e / den for e in es]


def _combine_fwd(oa, lsea, sink, obs, lsebs, oc, *, name, tm=512):
    T = oa.shape[0]
    tm = min(tm, T)
    nb = len(obs)

    def body(*refs):
        oa_ref, lsea_ref, sink_ref = refs[:3]
        ob_refs = refs[3:3 + nb]
        lse_refs = refs[3 + nb:3 + 2 * nb]
        oc_ref, out_ref, stage = refs[3 + 2 * nb:]
        out_ref[:, 0:A_WIDTH] = (oa_ref[...] * _sigmoid(lsea_ref[...] - sink_ref[...])).astype(BF16)
        ws = _branch_weights([_undilate(r, d, stage) for r, d in zip(lse_refs, DILATIONS)])
        ob = _undilate(ob_refs[0], DILATIONS[0], stage) * ws[0]
        for r, d, w in zip(ob_refs[1:], DILATIONS[1:], ws[1:]):
            ob = ob + _undilate(r, d, stage) * w
        out_ref[:, A_WIDTH:A_WIDTH + B_WIDTH] = ob.astype(BF16)
        out_ref[:, A_WIDTH + B_WIDTH:MIX_WIDTH] = oc_ref[...].astype(BF16)

    def row(w):
        return pl.BlockSpec((tm, w), lambda i: (i, 0))

    ins = [oa, lsea, sink, *obs, *lsebs, oc]
    in_specs = [row(A_WIDTH), row(A_WIDTH), pl.BlockSpec((1, A_WIDTH), lambda i: (0, 0))]
    in_specs += [_dilated_spec(tm, d, B_WIDTH) for _ in range(2) for d in DILATIONS] + [row(C_WIDTH)]
    return pl.pallas_call(
        body, name=name, grid=(T // tm,), in_specs=in_specs, out_specs=row(MIX_WIDTH),
        out_shape=jax.ShapeDtypeStruct((T, MIX_WIDTH), BF16),
        scratch_shapes=[_stage(tm, B_WIDTH)],
        compiler_params=_params(("parallel",)),
    )(*ins)


def _combine_bwd(dmix, oa, lsea, sink, obs, lsebs, *, name, tm=512):
    T = oa.shape[0]
    tm = min(tm, T)
    nb = len(obs)

    def body(*refs):
        dmix_ref, oa_ref, lsea_ref, sink_ref = refs[:4]
        ob_refs = refs[4:4 + nb]
        lse_refs = refs[4 + nb:4 + 2 * nb]
        outs = refs[4 + 2 * nb:-1]
        stage = refs[-1]
        doa_ref, dla_ref = outs[:2]
        dob_refs = outs[2:2 + nb]
        dlb_refs = outs[2 + nb:2 + 2 * nb]
        doc_ref, dsink_ref = outs[2 + 2 * nb:]

        d_a = dmix_ref[:, 0:A_WIDTH]
        d_b = dmix_ref[:, A_WIDTH:A_WIDTH + B_WIDTH]
        doc_ref[...] = dmix_ref[:, A_WIDTH + B_WIDTH:MIX_WIDTH]

        gate = _sigmoid(lsea_ref[...] - sink_ref[...])
        doa_ref[...] = (d_a * gate).astype(BF16)
        dgate = _head_sum(d_a * oa_ref[...])
        dlse = dgate * gate * (1.0 - gate)
        dla_ref[...] = dgate * gate - dlse

        @pl.when(pl.program_id(0) == 0)
        def _():
            dsink_ref[...] = jnp.zeros_like(dsink_ref)

        dsink_ref[...] -= jnp.sum(dlse, axis=0, keepdims=True)

        ws = _branch_weights([_undilate(r, d, stage) for r, d in zip(lse_refs, DILATIONS)])
        dws = [_head_sum(d_b * _undilate(r, d, stage)) for r, d in zip(ob_refs, DILATIONS)]
        sw = ws[0] * dws[0]
        for w, dw in zip(ws[1:], dws[1:]):
            sw = sw + w * dw
        for w, d, do_ref, dl_ref in zip(ws, DILATIONS, dob_refs, dlb_refs):
            _dilate_into(do_ref, w * d_b, d, stage)
            _dilate_into(dl_ref, w * sw, d, stage)

    def row(w):
        return pl.BlockSpec((tm, w), lambda i: (i, 0))

    vec = pl.BlockSpec((1, A_WIDTH), lambda i: (0, 0))
    dil = [_dilated_spec(tm, d, B_WIDTH) for _ in range(2) for d in DILATIONS]
    ins = [dmix, oa, lsea, sink, *obs, *lsebs]
    in_specs = [row(MIX_WIDTH), row(A_WIDTH), row(A_WIDTH), vec] + dil
    out_specs = [row(A_WIDTH), row(A_WIDTH)] + dil + [row(C_WIDTH), vec]
    out_shape = [jax.ShapeDtypeStruct((T, A_WIDTH), BF16), jax.ShapeDtypeStruct((T, A_WIDTH), F32)]
    out_shape += [jax.ShapeDtypeStruct((T // d, d * B_WIDTH), BF16) for d in DILATIONS]
    out_shape += [jax.ShapeDtypeStruct((T // d, d * B_WIDTH), F32) for d in DILATIONS]
    out_shape += [jax.ShapeDtypeStruct((T, C_WIDTH), F32), jax.ShapeDtypeStruct((1, A_WIDTH), F32)]
    res = pl.pallas_call(
        body, name=name, grid=(T // tm,), in_specs=in_specs, out_specs=out_specs, out_shape=out_shape,
        scratch_shapes=[_stage(tm, B_WIDTH)],
        compiler_params=_params(("arbitrary",)),
    )(*ins)
    return res[0], res[1], list(res[2:2 + nb]), list(res[2 + nb:2 + 2 * nb]), res[2 + 2 * nb], res[3 + 2 * nb]


HIST = 8


def _softplus_neg(lam):
    e = jnp.exp(-jnp.abs(lam))
    log1p = jnp.where(e < 0.01, e * (1.0 - e * (0.5 - e * (1.0 / 3.0))), jnp.log(1.0 + e))
    return jnp.maximum(-lam, 0.0) + log1p


def _neg_expm1(x):
    series = -x * (1.0 + x * (0.5 + x * (1.0 / 6.0 + x * (1.0 / 24.0 + x * (1.0 / 120.0)))))
    return jnp.where(x > -0.1, series, 1.0 - jnp.exp(x))


def _gelu_parts(x):
    k = math.sqrt(2.0 / math.pi)
    t = jnp.tanh(k * (x + 0.044715 * (x * x * x)))
    cdf = 0.5 * (1.0 + t)
    return x * cdf, cdf + 0.5 * x * (1.0 - t * t) * (k * (1.0 + 3.0 * 0.044715 * (x * x)))


def _rglru_gates(y, pos_ref, wr_ref, br_ref, wi_ref, bi_ref, lam_ref):
    yb = y.astype(BF16)
    r = _sigmoid(_dot(yb, wr_ref[...]) + br_ref[...])
    ig = _sigmoid(_dot(yb, wi_ref[...]) + bi_ref[...])
    sp = _softplus_neg(lam_ref[...])
    log_a = -C_EXP * r * sp
    reset = pos_ref[...] == 0
    a = jnp.where(reset, 0.0, jnp.exp(log_a))
    mult = jnp.where(reset, 1.0, jnp.sqrt(_neg_expm1(2.0 * log_a)))
    return yb, r, ig, sp, reset, a, mult


def _conv_fwd(xs_ref, cw_ref, cb_ref, tm):
    y = cb_ref[...] + cw_ref[0:1, :] * xs_ref[HIST:HIST + tm, :]
    for j in range(1, C_CONV):
        y = y + cw_ref[j:j + 1, :] * xs_ref[HIST - j:HIST - j + tm, :]
    return y


SCAN_GROUP = 8


def _blocked_scan(c, d, c_s, d_s, grp_a, grp_h, carry, reverse):
    tm, W = d.shape
    groups = tm // SCAN_GROUP
    order = range(SCAN_GROUP - 1, -1, -1) if reverse else range(SCAN_GROUP)
    outs, lasts = [], []
    for k in range(W // LANES):
        lanes = slice(k * LANES, (k + 1) * LANES)
        ck, dk, ga, gh = c_s.at[k], d_s.at[k], grp_a.at[k], grp_h.at[k]
        ck[...] = c[:, lanes]
        dk[...] = d[:, lanes]
        prod = state = None
        for j in order:
            rows = pl.ds(j, groups, stride=SCAN_GROUP)
            cj, dj = ck[rows, :], dk[rows, :]
            if prod is None:
                prod, state = cj, dj
            else:
                state = cj * state + dj
                prod = cj * prod
            ck[rows, :] = prod
            dk[rows, :] = state
        ga[...] = prod
        gh[...] = state

        def step(i, h, ga=ga, gh=gh):
            row = pl.ds(groups - 1 - i if reverse else i, 1)
            a, t = ga[row, :], gh[row, :]
            ga[row, :] = h
            return a * h + t

        lasts.append(lax.fori_loop(0, groups, step, carry[:, lanes], unroll=8))
        entering = ga[...]
        for j in range(SCAN_GROUP):
            rows = pl.ds(j, groups, stride=SCAN_GROUP)
            dk[rows, :] = dk[rows, :] + ck[rows, :] * entering
        outs.append(dk[...])
    return jnp.concatenate(outs, axis=1), jnp.concatenate(lasts, axis=1)


def _rglru_fwd(xc, gc, pos, cw, cb, wr, br, wi, bi, lam, *, name, tm=512):
    T, W = xc.shape
    tm = min(tm, T)

    def body(xc_ref, gc_ref, pos_ref, cw_ref, cb_ref, wr_ref, br_ref, wi_ref, bi_ref, lam_ref,
             out_ref, hs_ref, xs, a_s, b_s, h_s, grp_a, grp_h):
        @pl.when(pl.program_id(0) == 0)
        def _():
            xs[0:HIST, :] = jnp.zeros((HIST, W), F32)
            h_s[...] = jnp.zeros_like(h_s)

        xv = xc_ref[...]
        xs[HIST:HIST + tm, :] = xv
        y = _conv_fwd(xs, cw_ref, cb_ref, tm)
        xs[0:HIST, :] = xv[tm - HIST:tm, :]
        _, _, ig, _, _, a, mult = _rglru_gates(y, pos_ref, wr_ref, br_ref, wi_ref, bi_ref, lam_ref)
        hs, h_s[...] = _blocked_scan(a, mult * (ig * y), a_s, b_s, grp_a, grp_h, h_s[...], reverse=False)
        hs_ref[...] = hs
        out_ref[...] = hs * _gelu_parts(gc_ref[...])[0]

    row = pl.BlockSpec((tm, W), lambda i: (i, 0))
    full = lambda shape: pl.BlockSpec(shape, lambda i: (0,) * len(shape))
    return pl.pallas_call(
        body, name=name, grid=(T // tm,),
        in_specs=[row, row, pl.BlockSpec((tm, 1), lambda i: (i, 0)), full((C_CONV, W)), full((1, W)),
                  full((W, W)), full((1, W)), full((W, W)), full((1, W)), full((1, W))],
        out_specs=[row, row],
        out_shape=[jax.ShapeDtypeStruct((T, W), F32)] * 2,
        scratch_shapes=[pltpu.VMEM((tm + HIST, W), F32), pltpu.VMEM((W // LANES, tm, LANES), F32),
                        pltpu.VMEM((W // LANES, tm, LANES), F32), pltpu.VMEM((1, W), F32),
                        pltpu.VMEM((W // LANES, tm // SCAN_GROUP, LANES), F32),
                        pltpu.VMEM((W // LANES, tm // SCAN_GROUP, LANES), F32)],
        compiler_params=_params(("arbitrary",)),
    )(xc, gc, pos, cw, cb, wr, br, wi, bi, lam)


def _rglru_bwd(xc, gc, pos, hs, dout, cw, cb, wr, br, wi, bi, lam, *, name, tm=512):
    T, W = xc.shape
    tm = min(tm, T)
    nt = T // tm
    hb = tm // HIST

    def body(xc_ref, gc_ref, pos_ref, hs_ref, dout_ref, xch_ref, hsh_ref,
             cw_ref, cb_ref, wr_ref, br_ref, wi_ref, bi_ref, lam_ref,
             dxc_ref, dgc_ref, dcw_ref, dcb_ref, dwr_ref, dbr_ref, dwi_ref, dbi_ref, dlam_ref,
             xs, hsx, dys, asx, a_s, d_s, carry_s, grp_a, grp_h):
        i = pl.program_id(0)

        @pl.when(i == 0)
        def _():
            for r in (dcw_ref, dcb_ref, dwr_ref, dbr_ref, dwi_ref, dbi_ref, dlam_ref, carry_s):
                r[...] = jnp.zeros_like(r)
            dys[tm:tm + HIST, :] = jnp.zeros((HIST, W), F32)
            asx[tm:tm + HIST, :] = jnp.zeros((HIST, W), F32)

        has_prev = i < nt - 1
        xs[0:HIST, :] = jnp.where(has_prev, xch_ref[...], 0.0)
        hsx[0:HIST, :] = jnp.where(has_prev, hsh_ref[...], 0.0)
        xs[HIST:HIST + tm, :] = xc_ref[...]
        hs = hs_ref[...]
        hsx[HIST:HIST + tm, :] = hs
        y = _conv_fwd(xs, cw_ref, cb_ref, tm)
        yb, r, ig, sp, reset, a, mult = _rglru_gates(y, pos_ref, wr_ref, br_ref, wi_ref, bi_ref, lam_ref)

        gelu, dgelu = _gelu_parts(gc_ref[...])
        dout = dout_ref[...]
        dgc_ref[...] = dout * hs * dgelu
        asx[0:tm, :] = a
        a_up = asx[1:1 + tm, :]
        asx[tm:tm + HIST, :] = a[0:HIST, :]
        dh, carry_s[...] = _blocked_scan(a_up, dout * gelu, a_s, d_s, grp_a, grp_h, carry_s[...], reverse=True)
        hprev = hsx[HIST - 1:HIST - 1 + tm, :]
        igy = ig * y
        dmult = dh * igy
        digy = dh * mult
        dlog_a = jnp.where(reset, 0.0, dh * hprev * a - dmult * a * a / mult)
        dlam_ref[...] += jnp.sum(dlog_a * (C_EXP * r) * _sigmoid(-lam_ref[...]), axis=0, keepdims=True)
        dz_r = dlog_a * (-C_EXP * sp) * r * (1.0 - r)
        dz_i = digy * y * ig * (1.0 - ig)
        dzr_b = dz_r.astype(BF16)
        dzi_b = dz_i.astype(BF16)
        dy = digy * ig + _dot(dzr_b, wr_ref[...], NT_DIMS) + _dot(dzi_b, wi_ref[...], NT_DIMS)
        dwr_ref[...] += _dot(yb, dzr_b, TN_DIMS)
        dwi_ref[...] += _dot(yb, dzi_b, TN_DIMS)
        dbr_ref[...] += jnp.sum(dz_r, axis=0, keepdims=True)
        dbi_ref[...] += jnp.sum(dz_i, axis=0, keepdims=True)

        dys[0:tm, :] = dy
        dxc = cw_ref[0:1, :] * dy
        for j in range(1, C_CONV):
            dxc = dxc + cw_ref[j:j + 1, :] * dys[j:j + tm, :]
        dxc_ref[...] = dxc
        dys[tm:tm + HIST, :] = dy[0:HIST, :]
        dcb_ref[...] += jnp.sum(dy, axis=0, keepdims=True)
        for j in range(C_CONV):
            dcw_ref[j:j + 1, :] += jnp.sum(dy * xs[HIST - j:HIST - j + tm, :], axis=0, keepdims=True)

    row = pl.BlockSpec((tm, W), lambda i: (nt - 1 - i, 0))
    halo = pl.BlockSpec((HIST, W), lambda i: (jnp.maximum((nt - 1 - i) * hb - 1, 0), 0))
    full = lambda shape: pl.BlockSpec(shape, lambda i: (0,) * len(shape))
    out_specs = [row, row, full((C_CONV, W)), full((1, W)), full((W, W)), full((1, W)), full((W, W)), full((1, W)),
                 full((1, W))]
    out_shape = [jax.ShapeDtypeStruct((T, W), F32)] * 2
    out_shape += [jax.ShapeDtypeStruct(s, F32) for s in ((C_CONV, W), (1, W), (W, W), (1, W), (W, W), (1, W), (1, W))]
    return pl.pallas_call(
        body, name=name, grid=(nt,),
        in_specs=[row, row, pl.BlockSpec((tm, 1), lambda i: (nt - 1 - i, 0)), row, row, halo, halo,
                  full((C_CONV, W)), full((1, W)), full((W, W)), full((1, W)), full((W, W)), full((1, W)),
                  full((1, W))],
        out_specs=out_specs, out_shape=out_shape,
        scratch_shapes=[pltpu.VMEM((tm + HIST, W), F32), pltpu.VMEM((tm + HIST, W), F32),
                        pltpu.VMEM((tm + HIST, W), F32), pltpu.VMEM((tm + HIST, W), F32),
                        pltpu.VMEM((W // LANES, tm, LANES), F32), pltpu.VMEM((W // LANES, tm, LANES), F32),
                        pltpu.VMEM((1, W), F32), pltpu.VMEM((W // LANES, tm // SCAN_GROUP, LANES), F32),
                        pltpu.VMEM((W // LANES, tm // SCAN_GROUP, LANES), F32)],
        compiler_params=_params(("arbitrary",)),
    )(xc, gc, pos, hs, dout, xc, hs, cw, cb, wr, br, wi, bi, lam)


def _adam_math(w, g, m, v):
    m = ADAM_B1 * m + (1.0 - ADAM_B1) * g
    v = ADAM_B2 * v + (1.0 - ADAM_B2) * (g * g)
    m_hat = m / (1.0 - ADAM_B1 ** ADAM_STEP)
    v_hat = v / (1.0 - ADAM_B2 ** ADAM_STEP)
    delta = -ADAM_LR * (m_hat / (jnp.sqrt(v_hat) + ADAM_EPS) + ADAM_WD * w)
    return delta, m, v


def _pick_rows(R, cap=512, mult=16):
    for d in range(min(cap, R), 0, -1):
        if R % d == 0 and d % mult == 0:
            return d
    return R


def _adamw(parts, w, m, v, *, name, tr=None, part=0, prev=None):
    R, C = w.shape
    r = parts.shape[1]
    tr = _pick_rows(r) if tr is None else tr
    assert r % tr == 0 and R % r == 0, (name, R, r, tr)
    nt = r // tr

    def body(p_ref, w_ref, m_ref, v_ref, *rest):
        g_ref, d_ref, nm_ref, nv_ref = rest[-4:]
        g = p_ref[0].astype(F32)
        for d in range(1, N_DEV):
            g = g + p_ref[d].astype(F32)
        delta, nm, nv = _adam_math(w_ref[...], g, m_ref[...], v_ref[...])
        g_ref[...] = g
        d_ref[...] = delta
        nm_ref[...] = nm
        nv_ref[...] = nv

    row = pl.BlockSpec((tr, C), lambda i: (part * nt + i, 0))
    in_specs = [pl.BlockSpec((N_DEV, tr, C), lambda i: (0, i, 0)), row, row, row]
    operands = [parts, w, m, v]
    aliases = {}
    if prev is not None:
        in_specs += [pl.BlockSpec(memory_space=pl.ANY)] * 4
        operands += list(prev)
        aliases = {4 + i: i for i in range(4)}
    return pl.pallas_call(
        body, name=name, grid=(nt,), in_specs=in_specs,
        out_specs=[row] * 4, out_shape=[jax.ShapeDtypeStruct((R, C), F32)] * 4,
        input_output_aliases=aliases,
        compiler_params=_params(("parallel",)),
    )(*operands)


def _exchange(srcs, gather, *, name):
    n = len(srcs)
    out_shape = [jax.ShapeDtypeStruct((N_DEV,) + s.shape if gather else s.shape, s.dtype) for s in srcs]

    def body(*refs):
        ins, outs = refs[:n], refs[n:2 * n]
        send_sems, recv_sems, local_sems = refs[2 * n:]
        x, y, c = lax.axis_index("x"), lax.axis_index("y"), lax.axis_index("c")
        me = 4 * x + 2 * y + c
        local_copies, sends, arrivals = [], [], []
        for a in range(n):
            mine = ins[a] if gather else ins[a].at[me]
            local = pltpu.make_async_copy(mine, outs[a].at[me], local_sems.at[a])
            local.start()
            local_copies.append(local)
            for k in range(1, N_DEV):
                px, py, pc = x ^ ((k >> 2) & 1), y ^ ((k >> 1) & 1), c ^ (k & 1)
                peer = 4 * px + 2 * py + pc
                send = pltpu.make_async_remote_copy(
                    src_ref=ins[a] if gather else ins[a].at[peer], dst_ref=outs[a].at[me],
                    send_sem=send_sems.at[a * N_DEV + k], recv_sem=recv_sems.at[a * N_DEV + k],
                    device_id=(px, py, pc), device_id_type=pl.DeviceIdType.MESH)
                send.start()
                sends.append(send)
                arrivals.append(pltpu.make_async_remote_copy(
                    src_ref=mine, dst_ref=outs[a].at[peer],
                    send_sem=send_sems.at[a * N_DEV + k], recv_sem=recv_sems.at[a * N_DEV + k],
                    device_id=(px, py, pc), device_id_type=pl.DeviceIdType.MESH))
        for cp in sends:
            cp.wait_send()
        for cp in arrivals:
            cp.wait_recv()
        for cp in local_copies:
            cp.wait()

    return pl.pallas_call(
        body, name=name,
        in_specs=[pl.BlockSpec(memory_space=pl.ANY)] * n, out_specs=[pl.BlockSpec(memory_space=pl.ANY)] * n,
        out_shape=out_shape,
        scratch_shapes=[pltpu.SemaphoreType.DMA((n * N_DEV,)), pltpu.SemaphoreType.DMA((n * N_DEV,)),
                        pltpu.SemaphoreType.DMA((n,))],
    )(*srcs)


_HBM = pl.BlockSpec(memory_space=pltpu.HBM)
_SEM = pl.BlockSpec(memory_space=pltpu.SEMAPHORE)
_EFFECT = pltpu.SideEffectType.DATAFLOW_SIDE_EFFECTING


def _peers():
    x, y, c = lax.axis_index("x"), lax.axis_index("y"), lax.axis_index("c")
    out = []
    for k in range(1, N_DEV):
        px, py, pc = x ^ ((k >> 2) & 1), y ^ ((k >> 1) & 1), c ^ (k & 1)
        out.append((k, (px, py, pc), 4 * px + 2 * py + pc))
    return 4 * x + 2 * y + c, out


def _split_copies(src_refs, land_refs, send_sems, recv_sems, gather):
    me, peers = _peers()
    out = []
    for a, (src_ref, land_ref) in enumerate(zip(src_refs, land_refs)):
        for k, dev, blk in peers:
            common = dict(send_sem=send_sems.at[a * N_DEV + k], recv_sem=recv_sems.at[a * N_DEV + k], device_id=dev,
                          device_id_type=pl.DeviceIdType.MESH)
            src = src_ref if gather else src_ref.at[blk]
            out.append((pltpu.make_async_remote_copy(src_ref=src, dst_ref=land_ref.at[me], **common),
                        pltpu.make_async_remote_copy(src_ref=src, dst_ref=land_ref.at[blk], **common)))
    return out


def _exchange_start(srcs, gather, *, name, after=None):
    n = len(srcs)
    lands = [lax.empty((N_DEV,) + (s.shape if gather else s.shape[1:]), s.dtype) for s in srcs]

    def body(*refs):
        src_refs, land_refs = refs[:n], refs[n:2 * n]
        send_sems, recv_sems = refs[-2 * n - 3:-2 * n - 1]
        token = refs[-1]
        for outgoing, _ in _split_copies(src_refs, land_refs, send_sems, recv_sems, gather):
            outgoing.start()
        token[...] = jnp.zeros_like(token)

    res = pl.pallas_call(
        body, name=name,
        out_shape=(pltpu.SemaphoreType.DMA((n * N_DEV,)), pltpu.SemaphoreType.DMA((n * N_DEV,)),
                   *[pltpu.HBM(a.shape, a.dtype) for a in srcs + lands], jax.ShapeDtypeStruct((8, LANES), F32)),
        in_specs=(_HBM,) * (2 * n) + ((pl.BlockSpec(memory_space=pl.ANY),) if after is not None else ()),
        out_specs=(_SEM, _SEM) + (_HBM,) * (2 * n) + (pl.BlockSpec(memory_space=pltpu.VMEM),),
        input_output_aliases={i: i + 2 for i in range(2 * n)},
        compiler_params=pltpu.CompilerParams(has_side_effects=_EFFECT),
    )(*[pltpu.with_memory_space_constraint(a, pltpu.HBM) for a in srcs + lands],
      *([after] if after is not None else []))
    return res[0], res[1], list(res[2:2 + n]), list(res[2 + n:2 + 2 * n]), res[-1]


def _exchange_wait(started, after, gather, *, name):
    send_sems, recv_sems, srcs, lands, _ = started
    n = len(srcs)

    def body(*refs):
        src_refs, land_refs = refs[:n], refs[n:2 * n]
        send_sems, recv_sems = refs[2 * n:2 * n + 2]
        for outgoing, incoming in _split_copies(src_refs, land_refs, send_sems, recv_sems, gather):
            outgoing.wait_send()
            incoming.wait_recv()

    res = pl.pallas_call(
        body, name=name,
        out_shape=tuple(pltpu.HBM(a.shape, a.dtype) for a in srcs + lands),
        in_specs=(_HBM,) * (2 * n) + (_SEM, _SEM, pl.BlockSpec(memory_space=pl.ANY)), out_specs=(_HBM,) * (2 * n),
        input_output_aliases={i: i for i in range(2 * n)},
        compiler_params=pltpu.CompilerParams(has_side_effects=_EFFECT),
    )(*srcs, *lands, send_sems, recv_sems, after)
    return list(res[:n]), list(res[n:])


def _cols_to_blocks(g, *, name, tr=128):
    R, C = g.shape
    w = C // N_DEV
    tr = min(tr, R)

    def body(g_ref, o_ref):
        for p in range(N_DEV):
            o_ref[p] = g_ref[:, p * w:(p + 1) * w].astype(BF16)

    return pl.pallas_call(
        body, name=name, grid=(R // tr,),
        in_specs=[pl.BlockSpec((tr, C), lambda i: (i, 0))],
        out_specs=pl.BlockSpec((N_DEV, tr, w), lambda i: (0, i, 0)),
        out_shape=jax.ShapeDtypeStruct((N_DEV, R, w), BF16),
        compiler_params=_params(("parallel",)),
    )(g)


def _blocks_to_cols(b, *, name, tr=128):
    _, R, w = b.shape
    tr = min(tr, R)

    def body(b_ref, o_ref):
        o_ref[...] = jnp.concatenate([b_ref[p].astype(F32) for p in range(N_DEV)], axis=1).astype(o_ref.dtype)

    return pl.pallas_call(
        body, name=name, grid=(R // tr,),
        in_specs=[pl.BlockSpec((N_DEV, tr, w), lambda i: (0, i, 0))],
        out_specs=pl.BlockSpec((tr, N_DEV * w), lambda i: (i, 0)),
        out_shape=jax.ShapeDtypeStruct((R, N_DEV * w), b.dtype),
        compiler_params=_params(("parallel",)),
    )(b)


def _pack_rows(arrays, *, name, pick=None):
    B, _, w = arrays[0].shape
    rows = [a.shape[1] for a in arrays]
    first = 0
    if pick is not None:
        B, first = 1, pick

    def body(*refs):
        o_ref = refs[-1]
        r = 0
        for a_ref, n in zip(refs[:-1], rows):
            o_ref[0, r:r + n, :] = a_ref[0].astype(BF16)
            r += n

    return pl.pallas_call(
        body, name=name, grid=(B,),
        in_specs=[pl.BlockSpec((1, n, w), lambda b: (first + b, 0, 0)) for n in rows],
        out_specs=pl.BlockSpec((1, sum(rows), w), lambda b: (b, 0, 0)),
        out_shape=jax.ShapeDtypeStruct((B, sum(rows), w), BF16),
        compiler_params=_params(("parallel",)),
    )(*arrays)


def _unpack_rows(land, src, rows, *, name):
    _, R, w = land.shape
    src_spec = (pl.BlockSpec((1, R, w), lambda p: (p, 0, 0)) if src.ndim == 3
                else pl.BlockSpec((R, w), lambda p: (0, 0)))

    def body(land_ref, src_ref, *o_refs):
        me = 4 * lax.axis_index("x") + 2 * lax.axis_index("y") + lax.axis_index("c")
        mine = pl.program_id(0) == me
        r = 0
        for o_ref, n in zip(o_refs, rows):
            rows_i = slice(r, r + n)

            @pl.when(mine)
            def _(o_ref=o_ref, rows_i=rows_i):
                o_ref[0] = src_ref[0, rows_i, :] if src.ndim == 3 else src_ref[rows_i, :]

            @pl.when(jnp.logical_not(mine))
            def _(o_ref=o_ref, rows_i=rows_i):
                o_ref[0] = land_ref[0, rows_i, :]

            r += n

    return pl.pallas_call(
        body, name=name, grid=(N_DEV,),
        in_specs=[pl.BlockSpec((1, R, w), lambda p: (p, 0, 0)), src_spec],
        out_specs=[pl.BlockSpec((1, n, w), lambda p: (p, 0, 0)) for n in rows],
        out_shape=[jax.ShapeDtypeStruct((N_DEV, n, w), land.dtype) for n in rows],
        compiler_params=_params(("parallel",)),
    )(land, src)


def _to_blocks(w, axis):
    shape = w.shape
    k = shape[axis] // N_DEV
    w = w.reshape(shape[:axis] + (N_DEV, k) + shape[axis + 1:])
    return jnp.moveaxis(w, axis, 0)


def _from_blocks(wb, axis):
    w = jnp.moveaxis(wb, 0, axis)
    shape = w.shape
    return w.reshape(shape[:axis] + (shape[axis] * shape[axis + 1],) + shape[axis + 2:])


def _block_diag(w):
    n, k, _ = w.shape
    eye = jnp.eye(n, dtype=w.dtype)
    return (eye[:, None, :, None] * w[:, :, None, :]).reshape(n * k, n * k)


def _diag_blocks(wd):
    k = HEAD_DIM
    return jnp.stack([wd[h * k:(h + 1) * k, h * k:(h + 1) * k] for h in range(C_BLOCKS)])


def _pack(arrays, row_multiple=8):
    rows = []
    for a in arrays:
        flat = a.reshape(-1).astype(F32)
        pad = (-flat.shape[0]) % LANES
        rows.append(jnp.pad(flat, (0, pad)).reshape(-1, LANES))
    out = jnp.concatenate(rows, axis=0)
    return jnp.pad(out, ((0, (-out.shape[0]) % row_multiple), (0, 0)))


def _unpack(packed, shapes):
    outs, r = [], 0
    for s in shapes:
        size = math.prod(s)
        nrows = -(-size // LANES)
        outs.append(packed[r:r + nrows].reshape(-1)[:size].reshape(s))
        r += nrows
    return outs


def _rope_tables(positions):
    inv = 1.0 / (ROPE_THETA ** (jnp.arange(0, HEAD_DIM, 2, dtype=F32) / HEAD_DIM))
    ang = positions.astype(F32)[:, None] * inv
    cos, sin = jnp.cos(ang), jnp.sin(ang)
    return jnp.tile(cos, (1, 4)), jnp.tile(jnp.concatenate([-sin, sin], axis=1), (1, 2))


def _layer_fwd(l, x, pos, cos_t, sin_t, W, before_mixer=None):
    tag = f"l{l}"
    saved = {'x0': x}
    x1, a1, u1 = _ffn_fwd(x, W['norm_ffn1'][l], W['ffn1_gate'][l], W['ffn1_up'][l], W['ffn1_down'][l],
                          name=f"ffn1_fwd_{tag}")
    if before_mixer is not None:
        before_mixer(l, x1)
    h = _rms_fwd(x1, W['norm_mix'][l], name=f"mixnorm_fwd_{tag}")
    proj = _mm(h, W['w_in'][l], 'nn', name=f"proj_{tag}", tm=512, tn=IN_COLS, tk=h.shape[1])
    qa, ka2, va2, qb, kb, vb, xc, gc = _split_rope(proj, cos_t, sin_t, name=f"split_{tag}")
    oa, lsea = _attn_fwd(qa, ka2, va2, A_MAX_DIST, name=f"attn_a_fwd_{tag}")
    obs, lsebs = [], []
    for bi, (window, d) in enumerate(B_BRANCHES):
        o, lse = _attn_fwd(qb[bi], kb[bi], vb[bi], window // d, name=f"attn_b{bi}_fwd_{tag}")
        obs.append(o)
        lsebs.append(lse)
    oc, hs = _rglru_fwd(xc, gc, pos, W['conv_w'][l], W['conv_b'][l], W['rg_w_r'][l], W['rg_b_r'][l],
                        W['rg_w_i'][l], W['rg_b_i'][l], W['rg_lambda'][l], name=f"rglru_fwd_{tag}")
    mix = _combine_fwd(oa, lsea, W['sinks'][l], obs, lsebs, oc, name=f"combine_fwd_{tag}")
    x2 = _mm(mix, W['w_out'][l], 'nn', name=f"outproj_{tag}", tm=512, tn=x.shape[1], tk=MIX_WIDTH, res=x1)
    x3, a2, u2 = _ffn_fwd(x2, W['norm_ffn2'][l], W['ffn2_gate'][l], W['ffn2_up'][l], W['ffn2_down'][l],
                          name=f"ffn2_fwd_{tag}")
    saved.update(a1=a1, u1=u1, x1=x1, h=h, qa=qa, ka2=ka2, va2=va2, qb=qb, kb=kb, vb=vb, xc=xc, gc=gc, oa=oa,
                 lsea=lsea, obs=obs, lsebs=lsebs, hs=hs, mix=mix, x2=x2, a2=a2, u2=u2)
    return x3, saved


def _ffn_grads(tag, which, x, g, dy, a, u, wg, wu, wd):
    T, D = x.shape
    F = wg.shape[1]
    dx, dg, n, act, da, du = _ffn_bwd(x, g, dy, a, u, wg, wu, wd, name=f"{which}_bwd_{tag}")
    fc = _ffn_chunk(F)
    d_gate = _mm(n, da, 'tn', name=f"{which}_dgate_{tag}", tm=512, tn=fc, tk=4096)
    d_up = _mm(n, du, 'tn', name=f"{which}_dup_{tag}", tm=512, tn=fc, tk=4096)
    d_down = _mm(act, dy, 'tn', name=f"{which}_ddown_{tag}", tm=fc, tn=D, tk=1024, alpha=0.5)
    return dx, dg, d_gate, d_up, d_down


def _layer_bwd(l, dx3, pos, cos_t, sin_t, W, S, on_grads=None):
    tag = f"l{l}"
    G = {}
    dx2, G['norm_ffn2'], G['ffn2_gate'], G['ffn2_up'], G['ffn2_down'] = _ffn_grads(
        tag, 'ffn2', S['x2'], W['norm_ffn2'][l], dx3, S['a2'], S['u2'], W['ffn2_gate'][l], W['ffn2_up'][l],
        W['ffn2_down'][l])
    D = dx2.shape[1]
    dmix = _mm(dx2, W['w_out'][l], 'nt', name=f"outproj_dx_{tag}", tm=512, tn=MIX_WIDTH, tk=D)
    G['w_out'] = _mm(S['mix'], dx2, 'tn', name=f"outproj_dw_{tag}", tm=MIX_WIDTH, tn=D, tk=2048)
    doa, dla, dobs, dlbs, doc, dsink = _combine_bwd(dmix, S['oa'], S['lsea'], W['sinks'][l], S['obs'], S['lsebs'],
                                                    name=f"combine_bwd_{tag}")
    G['attn_sinks'] = dsink.reshape(A_WIDTH // HEAD_DIM, HEAD_DIM)[:, 0]
    dqa, dka2, dva2 = _attn_bwd(S['qa'], S['ka2'], S['va2'], doa, S['lsea'], dla, A_MAX_DIST,
                                name=f"attn_a_bwd_{tag}")
    dqb, dkb, dvb = [], [], []
    for bi, (window, d) in enumerate(B_BRANCHES):
        dq, dk, dv = _attn_bwd(S['qb'][bi], S['kb'][bi], S['vb'][bi], dobs[bi], S['lsebs'][bi], dlbs[bi], window // d,
                               name=f"attn_b{bi}_bwd_{tag}")
        dqb.append(dq)
        dkb.append(dk)
        dvb.append(dv)
    (dxc, dgc, G['conv_w'], G['conv_b'], dwr, G['rg_b_r'], dwi, G['rg_b_i'], G['rg_lambda']) = _rglru_bwd(
        S['xc'], S['gc'], pos, S['hs'], doc, W['conv_w'][l], W['conv_b'][l], W['rg_w_r'][l], W['rg_b_r'][l],
        W['rg_w_i'][l], W['rg_b_i'][l], W['rg_lambda'][l], name=f"rglru_bwd_{tag}")
    G['rg_w_r'] = _diag_blocks(dwr)
    G['rg_w_i'] = _diag_blocks(dwi)
    dproj = _merge_dproj(dqa, dka2, dva2, dqb, dkb, dvb, dxc, dgc, cos_t, sin_t, name=f"merge_{tag}")
    dh = _mm(dproj, W['w_in'][l], 'nt', name=f"proj_dx_{tag}", tm=512, tn=D, tk=IN_COLS)
    G['w_in'] = _mm(S['h'], dproj, 'tn', name=f"proj_dw_{tag}", tm=512, tn=IN_COLS, tk=2048)
    g_mix = W['norm_mix'][l]
    if on_grads is not None:
        g_mix = g_mix + on_grads(l, 0, G)
    dx1, G['norm_mix'] = _rms_bwd(S['x1'], g_mix, dh, dx2, name=f"mixnorm_bwd_{tag}")
    dx0, G['norm_ffn1'], G['ffn1_gate'], G['ffn1_up'], G['ffn1_down'] = _ffn_grads(
        tag, 'ffn1', S['x0'], W['norm_ffn1'][l], dx1, S['a1'], S['u1'], W['ffn1_gate'][l], W['ffn1_up'][l],
        W['ffn1_down'][l])
    if on_grads is not None:
        on_grads(l, 1, G)
    return dx0, G


def _device_step(x, positions, loss_target, W, before_layer=None, on_grads=None, before_mixer=None):
    T = x.shape[0]
    pos = positions.reshape(T, 1)
    cos_t, sin_t = _rope_tables(positions)
    saved = []
    for l in range(DEPTH):
        if before_layer is not None:
            before_layer(l, x)
        x, S = _layer_fwd(l, x, pos, cos_t, sin_t, W, before_mixer)
        saved.append(S)
    loss, dx, dg_final = _loss_head(x, W['norm_final'], loss_target, name="loss_head")
    grads = [None] * DEPTH
    for l in reversed(range(DEPTH)):
        dx, grads[l] = _layer_bwd(l, dx, pos, cos_t, sin_t, W, saved[l], on_grads)
    return loss, dx, grads, dg_final


SHARD_AXIS = {'ffn1_gate': 2, 'ffn1_up': 2, 'ffn1_down': 1, 'w_in': 2, 'w_out': 1, 'ffn2_gate': 2, 'ffn2_up': 2,
              'ffn2_down': 1, 'conv_w': 2}


def kernel(x, positions, norm_ffn1, ffn1_gate, ffn1_up, ffn1_down, norm_mix, w_in, attn_sinks, conv_w, conv_b, rg_w_r, rg_b_r, rg_w_i, rg_b_i, rg_lambda, w_out, norm_ffn2, ffn2_gate, ffn2_up, ffn2_down, norm_final, loss_target, m_norm_ffn1, m_ffn1_gate, m_ffn1_up, m_ffn1_down, m_norm_mix, m_w_in, m_attn_sinks, m_conv_w, m_conv_b, m_rg_w_r, m_rg_b_r, m_rg_w_i, m_rg_b_i, m_rg_lambda, m_w_out, m_norm_ffn2, m_ffn2_gate, m_ffn2_up, m_ffn2_down, m_norm_final, v_norm_ffn1, v_ffn1_gate, v_ffn1_up, v_ffn1_down, v_norm_mix, v_w_in, v_attn_sinks, v_conv_w, v_conv_b, v_rg_w_r, v_rg_b_r, v_rg_w_i, v_rg_b_i, v_rg_lambda, v_w_out, v_norm_ffn2, v_ffn2_gate, v_ffn2_up, v_ffn2_down, v_norm_final):
    given = dict(locals())
    me = 4 * lax.axis_index("x") + 2 * lax.axis_index("y") + lax.axis_index("c")

    def by_width(names):
        classes = {}
        for n in names:
            classes.setdefault(given[n].shape[2], []).append(n)
        return list(classes.values())

    def pack(names, get, tag, pick=None):
        return [_pack_rows([get(n) for n in cls], name=f"pack{ci}_{tag}", pick=pick)
                for ci, cls in enumerate(by_width(names))]

    def unpack(names, lands, srcs, tag):
        out = {}
        for ci, cls in enumerate(by_width(names)):
            arrays = _unpack_rows(lands[ci], srcs[ci], [given[n].shape[1] for n in cls], name=f"unpack{ci}_{tag}")
            out.update(zip(cls, arrays))
        return out

    def gather_start(l, names, tag, after=None):
        return _exchange_start([p[0] for p in pack(names, lambda n: given[n], f"w{tag}_l{l}", pick=l)], True,
                               name=f"gather_start{tag}_l{l}", after=after)

    def gather_wait(l, names, tag, started, after):
        srcs, lands = _exchange_wait(started, after, True, name=f"gather_wait{tag}_l{l}")
        blocks = unpack(names, lands, srcs, f"w{tag}_l{l}")
        for n in names:
            if SHARD_AXIS[n] == 2:
                W[n][l] = _blocks_to_cols(blocks[n], name=f"cols_{n}_l{l}")
            else:
                W[n][l] = blocks[n].reshape(-1, blocks[n].shape[2])
        return lands[0]

    W = {n: [None] * DEPTH for n in BIG_NAMES}
    ffn1_names, later_names = SCATTER_STAGES[1], SCATTER_STAGES[0]
    first = gather_start(0, ffn1_names, "a")
    conv_full = _exchange([conv_w], True, name="gather_conv_w")[0]
    landed = gather_wait(0, ffn1_names, "a", first, conv_full)
    second = gather_start(0, later_names, "b", after=landed)
    started = second[4][0, 0]
    gathers = [None] * DEPTH

    def before_layer(l, x_in):
        if l > 0:
            gather_wait(l, BIG_NAMES, "", gathers[l], x_in)

    def before_mixer(l, x1):
        if l == 0:
            landed = gather_wait(0, later_names, "b", second, x1)
            token = 0.0
            for k in range(1, DEPTH):
                gathers[k] = gather_start(k, BIG_NAMES, "", after=landed)
                token = token + gathers[k][4][0, 0]
            W['norm_mix'][0] = W['norm_mix'][0] + token

    scatters = {}

    def on_grads(l, stage, G):
        def blocks_of(n):
            if SHARD_AXIS[n] == 2:
                return _cols_to_blocks(G[n], name=f"blocks_{n}_l{l}")
            return G[n].reshape(N_DEV, -1, G[n].shape[1])

        scatters[l, stage] = _exchange_start(pack(SCATTER_STAGES[stage], blocks_of, f"g{stage}_l{l}"), False,
                                             name=f"scatter_start{stage}_l{l}")
        token = scatters[l, stage][4][0, 0]
        if stage == 1 and l > 0:
            W['norm_ffn2'][l - 1] = W['norm_ffn2'][l - 1] + token
        return token

    W['conv_w'] = [_from_blocks(conv_full[:, l], 1) for l in range(DEPTH)]
    for n in ('norm_ffn1', 'norm_mix', 'norm_ffn2', 'conv_b', 'rg_lambda'):
        W[n] = [given[n][l][None, :] for l in range(DEPTH)]
    W['norm_final'] = norm_final[None, :]
    W['sinks'] = [jnp.repeat(attn_sinks[l], HEAD_DIM)[None, :] for l in range(DEPTH)]
    for n in ('rg_w_r', 'rg_w_i'):
        W[n] = [_block_diag(given[n][l]).astype(BF16) for l in range(DEPTH)]
    for n in ('rg_b_r', 'rg_b_i'):
        W[n] = [given[n][l].reshape(1, C_WIDTH) for l in range(DEPTH)]

    W['norm_ffn1'][0] = W['norm_ffn1'][0] + started

    loss_part, grad_x, grads, dg_final = _device_step(x[0], positions[0], loss_target[0], W, before_layer, on_grads,
                                                      before_mixer)
    loss = lax.psum(loss_part[0, 0], ("x", "y", "c"))

    small_shapes = [given[n].shape for n in SMALL_NAMES] + [(DEPTH, C_CONV, C_WIDTH)]
    small_grads = []
    for n in SMALL_NAMES:
        if n == 'norm_final':
            small_grads.append(dg_final.reshape(-1))
        else:
            small_grads.append(jnp.stack([grads[l][n].reshape(given[n].shape[1:]) for l in range(DEPTH)]))
    small_grads.append(jnp.stack([grads[l]['conv_w'] for l in range(DEPTH)]))
    small_parts = _exchange([_pack(small_grads, SMALL_TILE)], True, name="gather_small_grads")[0]

    out = {}
    for stage in (0, 1):
        parts = {}
        for l in reversed(range(DEPTH)):
            last = stage == 1 and l == 0
            srcs, lands = _exchange_wait(scatters[l, stage], out['w_in'][1] if last else grad_x, False,
                                         name=f"scatter_wait{stage}_l{l}")
            parts[l] = unpack(SCATTER_STAGES[stage], lands, srcs, f"g{stage}_l{l}")
        for n in SCATTER_STAGES[stage]:
            shape = given[n].shape
            two_d = (shape[0] * shape[1], shape[2])
            res = None
            for l in reversed(range(DEPTH)):
                res = _adamw(parts[l][n], given[n].reshape(two_d), given['m_' + n].reshape(two_d),
                             given['v_' + n].reshape(two_d), name=f"adamw_{n}_l{l}", part=l, prev=res)
            out[n] = [r.reshape(shape) for r in res]

    w_small = [given[n] for n in SMALL_NAMES]
    m_small = [given['m_' + n] for n in SMALL_NAMES]
    v_small = [given['v_' + n] for n in SMALL_NAMES]
    zeros_cw = jnp.zeros((DEPTH, C_CONV, C_WIDTH), F32)
    res = _adamw(small_parts, _pack(w_small + [zeros_cw], SMALL_TILE), _pack(m_small + [zeros_cw], SMALL_TILE),
                 _pack(v_small + [zeros_cw], SMALL_TILE), name="adamw_small", tr=SMALL_TILE)
    unpacked = [_unpack(r, small_shapes) for r in res]
    for i, n in enumerate(SMALL_NAMES):
        out[n] = [u[i] for u in unpacked]

    k = conv_w.shape[2]
    g_cw = lax.dynamic_slice_in_dim(unpacked[0][-1], me * k, k, axis=2)
    zero_parts = jnp.zeros((N_DEV - 1,) + (8, LANES), F32)
    res = _adamw(jnp.concatenate([_pack([g_cw])[None], zero_parts]), _pack([conv_w]), _pack([m_conv_w]),
                 _pack([v_conv_w]), name="adamw_conv_w", tr=8)
    out['conv_w'] = [_unpack(r, [conv_w.shape])[0] for r in res]

    outputs = [loss, grad_x[None]]
    for i in range(4):
        outputs += [out[n][i] for n in WEIGHT_NAMES]
    return tuple(outputs)
```

```python
import functools
import math

import jax
import jax.numpy as jnp
from jax import lax
from jax.experimental import pallas as pl
from jax.experimental.pallas import tpu as pltpu

F32 = jnp.float32
BF16 = jnp.bfloat16

N_DEV = 8
DEPTH = 4
HEAD_DIM = 64
LANES = 128
QBLK = 128
A_WIDTH = 256
A_KV_WIDTH = 128
B_WIDTH = 384
C_WIDTH = 384
C_BLOCKS = 6
C_CONV = 4
C_EXP = 8.0
MIX_WIDTH = A_WIDTH + B_WIDTH + C_WIDTH
IN_COLS = A_WIDTH + 2 * A_KV_WIDTH + 3 * B_WIDTH + 2 * C_WIDTH
A_MAX_DIST = 127
B_BRANCHES = ((128, 1), (512, 4), (2048, 16))
ROPE_THETA = 10000.0
EPS = 1e-6
SCALE = HEAD_DIM ** -0.5

ADAM_LR = 0.001
ADAM_B1 = 0.9
ADAM_B2 = 0.999
ADAM_EPS = 1e-08
ADAM_WD = 0.01
ADAM_STEP = 10

ATTN_CHUNK = 4096
ATTN_FWD_UNROLL = True
ATTN_BWD_UNROLL = True
VMEM_LIMIT = 56 * 1024 * 1024
SMALL_TILE = 256

NT_DIMS = (((1,), (1,)), ((), ()))
TN_DIMS = (((0,), (0,)), ((), ()))
NN_DIMS = (((1,), (0,)), ((), ()))

WEIGHT_NAMES = ['norm_ffn1', 'ffn1_gate', 'ffn1_up', 'ffn1_down', 'norm_mix', 'w_in', 'attn_sinks', 'conv_w',
                'conv_b', 'rg_w_r', 'rg_b_r', 'rg_w_i', 'rg_b_i', 'rg_lambda', 'w_out', 'norm_ffn2', 'ffn2_gate',
                'ffn2_up', 'ffn2_down', 'norm_final']
BIG_NAMES = ['ffn1_gate', 'ffn1_up', 'ffn1_down', 'w_in', 'w_out', 'ffn2_gate', 'ffn2_up', 'ffn2_down']
SCATTER_STAGES = (['ffn2_gate', 'ffn2_up', 'ffn2_down', 'w_out', 'w_in'], ['ffn1_gate', 'ffn1_up', 'ffn1_down'])
SMALL_NAMES = ['norm_ffn1', 'norm_mix', 'norm_ffn2', 'norm_final', 'attn_sinks', 'conv_b', 'rg_w_r', 'rg_b_r',
               'rg_w_i', 'rg_b_i', 'rg_lambda']


def _params(sem, vmem=VMEM_LIMIT):
    return pltpu.CompilerParams(dimension_semantics=sem, vmem_limit_bytes=vmem)


def _dot(a, b, dims=NN_DIMS):
    return lax.dot_general(a, b, dims, preferred_element_type=F32)


def _sigmoid(x):
    return 1.0 / (1.0 + jnp.exp(-x))


def _mm(a, b, mode, *, name, tm=512, tn=512, tk=512, out_dtype=F32, alpha=1.0, res=None):
    if mode == 'nn':
        (M, K), N = a.shape, b.shape[1]
    elif mode == 'nt':
        (M, K), N = a.shape, b.shape[0]
    else:
        (K, M), N = a.shape, b.shape[1]
    tm, tn, tk = min(tm, M), min(tn, N), min(tk, K)
    ni, nj, nk = M // tm, N // tn, K // tk
    assert ni * tm == M and nj * tn == N and nk * tk == K, (name, a.shape, b.shape, tm, tn, tk)
    if mode == 'tn':
        a_spec = pl.BlockSpec((tk, tm), lambda j, i, k: (k, i))
    else:
        a_spec = pl.BlockSpec((tm, tk), lambda j, i, k: (i, k))
    if mode == 'nt':
        b_spec = pl.BlockSpec((tn, tk), lambda j, i, k: (j, k))
    else:
        b_spec = pl.BlockSpec((tk, tn), lambda j, i, k: (k, j))
    dims = {'nn': NN_DIMS, 'nt': NT_DIMS, 'tn': TN_DIMS}[mode]
    o_spec = pl.BlockSpec((tm, tn), lambda j, i, k: (i, j))
    has_res = res is not None

    def body(*refs):
        if has_res:
            a_ref, b_ref, r_ref, o_ref = refs[:4]
        else:
            a_ref, b_ref, o_ref = refs[:3]
        part = _dot(a_ref[...].astype(BF16), b_ref[...].astype(BF16), dims)

        def finish(acc):
            out = acc * alpha if alpha != 1.0 else acc
            if has_res:
                out = r_ref[...] + out
            o_ref[...] = out.astype(out_dtype)

        if nk == 1:
            finish(part)
        else:
            acc_ref = refs[-1]
            k = pl.program_id(2)

            @pl.when(k == 0)
            def _():
                acc_ref[...] = part

            @pl.when(k > 0)
            def _():
                acc_ref[...] += part

            @pl.when(k == nk - 1)
            def _():
                finish(acc_ref[...])

    in_specs = [a_spec, b_spec] + ([o_spec] if has_res else [])
    operands = [a, b] + ([res] if has_res else [])
    return pl.pallas_call(
        body, name=name, grid=(nj, ni, nk), in_specs=in_specs, out_specs=o_spec,
        out_shape=jax.ShapeDtypeStruct((M, N), out_dtype),
        scratch_shapes=[pltpu.VMEM((tm, tn), F32)] if nk > 1 else [],
        compiler_params=_params(("parallel", "parallel", "arbitrary")),
    )(*operands)


def _rms_fwd(x, g, *, name, tm=512):
    T, D = x.shape
    tm = min(tm, T)

    def body(x_ref, g_ref, o_ref):
        xv = x_ref[...]
        rstd = lax.rsqrt(jnp.mean(xv * xv, axis=-1, keepdims=True) + EPS)
        o_ref[...] = (xv * rstd * g_ref[...]).astype(BF16)

    return pl.pallas_call(
        body, name=name, grid=(T // tm,),
        in_specs=[pl.BlockSpec((tm, D), lambda i: (i, 0)), pl.BlockSpec((1, D), lambda i: (0, 0))],
        out_specs=pl.BlockSpec((tm, D), lambda i: (i, 0)),
        out_shape=jax.ShapeDtypeStruct((T, D), BF16),
        compiler_params=_params(("parallel",)),
    )(x, g)


def _rms_bwd_math(xv, g, dn):
    rstd = lax.rsqrt(jnp.mean(xv * xv, axis=-1, keepdims=True) + EPS)
    xhat = xv * rstd
    dxhat = dn * g
    dx = rstd * (dxhat - xhat * jnp.mean(dxhat * xhat, axis=-1, keepdims=True))
    return dx, jnp.sum(dn * xhat, axis=0, keepdims=True)


def _rms_bwd(x, g, dn, dres, *, name, tm=512):
    T, D = x.shape
    tm = min(tm, T)

    def body(x_ref, g_ref, dn_ref, dres_ref, dx_ref, dg_ref):
        dx, dg = _rms_bwd_math(x_ref[...], g_ref[...], dn_ref[...])
        dx_ref[...] = dres_ref[...] + dx

        @pl.when(pl.program_id(0) == 0)
        def _():
            dg_ref[...] = jnp.zeros_like(dg_ref)

        dg_ref[...] += dg

    row = pl.BlockSpec((tm, D), lambda i: (i, 0))
    vec = pl.BlockSpec((1, D), lambda i: (0, 0))
    return pl.pallas_call(
        body, name=name, grid=(T // tm,),
        in_specs=[row, vec, row, row], out_specs=[row, vec],
        out_shape=[jax.ShapeDtypeStruct((T, D), F32), jax.ShapeDtypeStruct((1, D), F32)],
        compiler_params=_params(("arbitrary",)),
    )(x, g, dn, dres)


def _loss_head(x, g, target, *, name, tm=512):
    T, D = x.shape
    tm = min(tm, T)

    def body(x_ref, g_ref, t_ref, loss_ref, dx_ref, dg_ref):
        xv = x_ref[...]
        g = g_ref[...]
        rstd = lax.rsqrt(jnp.mean(xv * xv, axis=-1, keepdims=True) + EPS)
        y = xv * rstd * g
        err = y - t_ref[...]
        part = 0.5 * jnp.sum(jnp.mean(err * err, axis=-1, keepdims=True), axis=0, keepdims=True)
        dx, dg = _rms_bwd_math(xv, g, err * (1.0 / D))
        dx_ref[...] = dx

        @pl.when(pl.program_id(0) == 0)
        def _():
            dg_ref[...] = jnp.zeros_like(dg_ref)
            loss_ref[...] = jnp.zeros_like(loss_ref)

        dg_ref[...] += dg
        loss_ref[...] += jnp.broadcast_to(part, loss_ref.shape)

    row = pl.BlockSpec((tm, D), lambda i: (i, 0))
    vec = pl.BlockSpec((1, D), lambda i: (0, 0))
    lspec = pl.BlockSpec((1, LANES), lambda i: (0, 0))
    return pl.pallas_call(
        body, name=name, grid=(T // tm,),
        in_specs=[row, vec, row], out_specs=[lspec, row, vec],
        out_shape=[jax.ShapeDtypeStruct((1, LANES), F32), jax.ShapeDtypeStruct((T, D), F32),
                   jax.ShapeDtypeStruct((1, D), F32)],
        compiler_params=_params(("arbitrary",)),
    )(x, g, target)


def _resident(shape):
    return pl.BlockSpec(shape, lambda i: (0,) * len(shape), pipeline_mode=pl.Buffered(1))


def _ffn_chunk(F):
    for c in (1408, 1024, 512, 256, 128):
        if F % c == 0:
            return c
    return F


def _ffn_fwd(x, g, wg, wu, wd, *, name, tm=256):
    T, D = x.shape
    F = wg.shape[1]
    tm = min(tm, T)
    fc = _ffn_chunk(F)

    def body(x_ref, g_ref, wg_ref, wu_ref, wd_ref, o_ref, a_ref, u_ref):
        xv = x_ref[...]
        rstd = lax.rsqrt(jnp.mean(xv * xv, axis=-1, keepdims=True) + EPS)
        n = (xv * rstd * g_ref[...]).astype(BF16)
        acc = jnp.zeros((tm, D), F32)
        for c in range(F // fc):
            sl = slice(c * fc, (c + 1) * fc)
            a = _dot(n, wg_ref[:, sl])
            u = _dot(n, wu_ref[:, sl])
            a_ref[:, sl] = a.astype(BF16)
            u_ref[:, sl] = u.astype(BF16)
            act = (a * _sigmoid(a) * u).astype(BF16)
            acc = acc + _dot(act, wd_ref[sl, :])
        o_ref[...] = xv + 0.5 * acc

    row = pl.BlockSpec((tm, D), lambda i: (i, 0))
    hid = pl.BlockSpec((tm, F), lambda i: (i, 0))
    return pl.pallas_call(
        body, name=name, grid=(T // tm,),
        in_specs=[row, pl.BlockSpec((1, D), lambda i: (0, 0)),
                  _resident((D, F)), _resident((D, F)), _resident((F, D))],
        out_specs=[row, hid, hid],
        out_shape=[jax.ShapeDtypeStruct((T, D), F32), jax.ShapeDtypeStruct((T, F), BF16),
                   jax.ShapeDtypeStruct((T, F), BF16)],
        compiler_params=_params(("parallel",)),
    )(x, g, wg, wu, wd)


def _ffn_bwd(x, g, dy, a, u, wg, wu, wd, *, name, tm=256):
    T, D = x.shape
    F = wg.shape[1]
    tm = min(tm, T)
    fc = _ffn_chunk(F)

    def body(x_ref, g_ref, dy_ref, a_ref, u_ref, wg_ref, wu_ref, wd_ref,
             dx_ref, dg_ref, n_ref, act_ref, da_ref, du_ref):
        xv = x_ref[...]
        g = g_ref[...]
        rstd = lax.rsqrt(jnp.mean(xv * xv, axis=-1, keepdims=True) + EPS)
        n_ref[...] = (xv * rstd * g).astype(BF16)
        dy = dy_ref[...]
        dyh = (0.5 * dy).astype(BF16)
        dn = jnp.zeros((tm, D), F32)
        for c in range(F // fc):
            sl = slice(c * fc, (c + 1) * fc)
            av = a_ref[:, sl].astype(F32)
            uv = u_ref[:, sl].astype(F32)
            dact = _dot(dyh, wd_ref[sl, :], NT_DIMS)
            s = _sigmoid(av)
            silu = av * s
            act_ref[:, sl] = (silu * uv).astype(BF16)
            da = (dact * uv * (s * (1.0 + av * (1.0 - s)))).astype(BF16)
            du = (dact * silu).astype(BF16)
            da_ref[:, sl] = da
            du_ref[:, sl] = du
            dn = dn + _dot(da, wg_ref[:, sl], NT_DIMS) + _dot(du, wu_ref[:, sl], NT_DIMS)
        dx, dg = _rms_bwd_math(xv, g, dn)
        dx_ref[...] = dy + dx

        @pl.when(pl.program_id(0) == 0)
        def _():
            dg_ref[...] = jnp.zeros_like(dg_ref)

        dg_ref[...] += dg

    row = pl.BlockSpec((tm, D), lambda i: (i, 0))
    hid = pl.BlockSpec((tm, F), lambda i: (i, 0))
    vec = pl.BlockSpec((1, D), lambda i: (0, 0))
    return pl.pallas_call(
        body, name=name, grid=(T // tm,),
        in_specs=[row, vec, row, hid, hid,
                  _resident((D, F)), _resident((D, F)), _resident((F, D))],
        out_specs=[row, vec, row, hid, hid, hid],
        out_shape=[jax.ShapeDtypeStruct((T, D), F32), jax.ShapeDtypeStruct((1, D), F32),
                   jax.ShapeDtypeStruct((T, D), BF16), jax.ShapeDtypeStruct((T, F), BF16),
                   jax.ShapeDtypeStruct((T, F), BF16), jax.ShapeDtypeStruct((T, F), BF16)],
        compiler_params=_params(("arbitrary",)),
    )(x, g, dy, a, u, wg, wu, wd)


def _lane_iota(shape):
    return lax.broadcasted_iota(jnp.int32, shape, 1)


def _rope_partner(x):
    first_half = (_lane_iota(x.shape) & (HEAD_DIM - 1)) < HEAD_DIM // 2
    return jnp.where(first_half, pltpu.roll(x, LANES - HEAD_DIM // 2, 1), pltpu.roll(x, HEAD_DIM // 2, 1))


def _swap_heads(x):
    return pltpu.roll(x, HEAD_DIM, 1)


def _undilate(blk_ref, d, stage):
    if d == 1:
        return blk_ref[...]
    n, width = blk_ref.shape
    W = width // d
    for r in range(d):
        for g in range(W // LANES):
            stage.at[g][pl.ds(r, n, stride=d), :] = blk_ref[:, r * W + g * LANES:r * W + (g + 1) * LANES]
    return jnp.concatenate([stage.at[g][...] for g in range(W // LANES)], axis=1)


def _dilate_into(out_ref, value, d, stage):
    if d == 1:
        out_ref[...] = value.astype(out_ref.dtype)
        return
    n = value.shape[0] // d
    W = value.shape[1]
    for g in range(W // LANES):
        stage.at[g][...] = value[:, g * LANES:(g + 1) * LANES]
    for r in range(d):
        for g in range(W // LANES):
            out_ref[:, r * W + g * LANES:r * W + (g + 1) * LANES] = (
                stage.at[g][pl.ds(r, n, stride=d), :].astype(out_ref.dtype))


def _dilated_spec(tm, d, W):
    return pl.BlockSpec((tm // d, d * W), lambda i: (i, 0))


def _stage(tm, W):
    return pltpu.VMEM((W // LANES, tm, LANES), F32)


DILATIONS = tuple(d for _, d in B_BRANCHES)


def _split_rope(proj, cos_t, sin_t, *, name, tm=512):
    T = proj.shape[0]
    tm = min(tm, T)
    nd = len(DILATIONS)

    def body(p_ref, c_ref, s_ref, qa_ref, ka_ref, va_ref, *rest):
        b_refs = rest[:3 * nd]
        xc_ref, gc_ref, stage = rest[3 * nd:]
        cos = c_ref[...]
        sin = s_ref[...]

        def rope(x):
            return x * cos + _rope_partner(x) * sin

        lo = _lane_iota((tm, LANES)) < HEAD_DIM
        col = 0
        for j in range(A_WIDTH // LANES):
            qa_ref[:, j * LANES:(j + 1) * LANES] = (rope(p_ref[:, col:col + LANES]) * SCALE).astype(BF16)
            col += LANES
        kr = rope(p_ref[:, col:col + LANES])
        col += LANES
        vr = p_ref[:, col:col + LANES]
        col += LANES
        for src, dst in ((kr, ka_ref), (vr, va_ref)):
            sw = _swap_heads(src)
            dst[:, 0:LANES] = jnp.where(lo, src, sw).astype(BF16)
            dst[:, LANES:2 * LANES] = jnp.where(lo, sw, src).astype(BF16)
        for which, (roped, scale) in enumerate(((True, SCALE), (True, 1.0), (False, 1.0))):
            parts = []
            for j in range(B_WIDTH // LANES):
                v = p_ref[:, col:col + LANES]
                parts.append(rope(v) * scale if roped else v)
                col += LANES
            value = jnp.concatenate(parts, axis=1)
            for di, d in enumerate(DILATIONS):
                _dilate_into(b_refs[which * nd + di], value, d, stage)
        xc_ref[...] = p_ref[:, col:col + C_WIDTH]
        gc_ref[...] = p_ref[:, col + C_WIDTH:col + 2 * C_WIDTH]

    def row(w):
        return pl.BlockSpec((tm, w), lambda i: (i, 0))

    out_specs = [row(A_WIDTH)] * 3 + [_dilated_spec(tm, d, B_WIDTH) for _ in range(3) for d in DILATIONS]
    out_specs += [row(C_WIDTH)] * 2
    out_shape = [jax.ShapeDtypeStruct((T, A_WIDTH), BF16)] * 3
    out_shape += [jax.ShapeDtypeStruct((T // d, d * B_WIDTH), BF16) for _ in range(3) for d in DILATIONS]
    out_shape += [jax.ShapeDtypeStruct((T, C_WIDTH), F32)] * 2
    res = pl.pallas_call(
        body, name=name, grid=(T // tm,),
        in_specs=[row(IN_COLS), row(LANES), row(LANES)], out_specs=out_specs, out_shape=out_shape,
        scratch_shapes=[_stage(tm, B_WIDTH)],
        compiler_params=_params(("parallel",)),
    )(proj, cos_t, sin_t)
    qa, ka2, va2 = res[:3]
    qb, kb, vb = (list(res[3 + i * nd:3 + (i + 1) * nd]) for i in range(3))
    return qa, ka2, va2, qb, kb, vb, res[-2], res[-1]


def _merge_dproj(dqa, dka2, dva2, dqb, dkb, dvb, dxc, dgc, cos_t, sin_t, *, name, tm=512):
    T = dqa.shape[0]
    tm = min(tm, T)
    nb = len(dqb)

    def body(*refs):
        dqa_ref, dka_ref, dva_ref = refs[:3]
        dqb_refs = refs[3:3 + nb]
        dkb_refs = refs[3 + nb:3 + 2 * nb]
        dvb_refs = refs[3 + 2 * nb:3 + 3 * nb]
        dxc_ref, dgc_ref, c_ref, s_ref, o_ref, stage = refs[3 + 3 * nb:]
        cos = c_ref[...]
        sin = s_ref[...]

        def rope_t(dy):
            return dy * cos - _rope_partner(dy) * sin

        lo = _lane_iota((tm, LANES)) < HEAD_DIM
        col = 0
        for j in range(A_WIDTH // LANES):
            o_ref[:, col:col + LANES] = (rope_t(dqa_ref[:, j * LANES:(j + 1) * LANES]) * SCALE).astype(BF16)
            col += LANES
        for src, roped in ((dka_ref, True), (dva_ref, False)):
            b0 = src[:, 0:LANES]
            b1 = src[:, LANES:2 * LANES]
            v = jnp.where(lo, b0 + _swap_heads(b0), b1 + _swap_heads(b1))
            if roped:
                v = rope_t(v)
            o_ref[:, col:col + LANES] = v.astype(BF16)
            col += LANES
        for group, roped, scale in ((dqb_refs, True, SCALE), (dkb_refs, True, 1.0), (dvb_refs, False, 1.0)):
            total = _undilate(group[0], DILATIONS[0], stage)
            for r, d in zip(group[1:], DILATIONS[1:]):
                total = total + _undilate(r, d, stage)
            for j in range(B_WIDTH // LANES):
                v = total[:, j * LANES:(j + 1) * LANES]
                if roped:
                    v = rope_t(v) * scale
                o_ref[:, col:col + LANES] = v.astype(BF16)
                col += LANES
        o_ref[:, col:col + C_WIDTH] = dxc_ref[...].astype(BF16)
        o_ref[:, col + C_WIDTH:col + 2 * C_WIDTH] = dgc_ref[...].astype(BF16)

    def row(w):
        return pl.BlockSpec((tm, w), lambda i: (i, 0))

    ins = [dqa, dka2, dva2, *dqb, *dkb, *dvb, dxc, dgc, cos_t, sin_t]
    in_specs = [row(A_WIDTH)] * 3 + [_dilated_spec(tm, d, B_WIDTH) for _ in range(3) for d in DILATIONS]
    in_specs += [row(C_WIDTH)] * 2 + [row(LANES)] * 2
    return pl.pallas_call(
        body, name=name, grid=(T // tm,), in_specs=in_specs,
        out_specs=row(IN_COLS),
        out_shape=jax.ShapeDtypeStruct((T, IN_COLS), BF16),
        scratch_shapes=[_stage(tm, B_WIDTH)],
        compiler_params=_params(("parallel",)),
    )(*ins)


def _band_masks(max_dist):
    row = lax.broadcasted_iota(jnp.int32, (QBLK, 2 * QBLK), 0)
    key = lax.broadcasted_iota(jnp.int32, (QBLK, 2 * QBLK), 1)
    dist = row + QBLK - key
    wide = jnp.logical_and(dist >= 0, dist <= max_dist)
    return wide, wide[:, :QBLK], key >= QBLK


def _head_masks(rows=QBLK):
    lo = _lane_iota((rows, LANES)) < HEAD_DIM
    return lo, jnp.logical_not(lo)


def _keep(hm, x):
    return x * jnp.where(hm, 1.0, 0.0).astype(x.dtype)


def _head_col(x, hm):
    return jnp.max(jnp.where(hm, x, -jnp.inf), axis=1, keepdims=True)


def _attn_specs(R, C):
    chunk = min(ATTN_CHUNK, R)
    nb = chunk // QBLK
    nch = R // chunk
    main = pl.BlockSpec((chunk, LANES), lambda j, c: (c, j))
    prev = pl.BlockSpec((QBLK, LANES), lambda j, c: (jnp.maximum(c * nb - 1, 0), j))
    nxt = pl.BlockSpec((QBLK, LANES), lambda j, c: (jnp.minimum((c + 1) * nb, R // QBLK - 1), j))
    return chunk, nb, nch, main, prev, nxt


def _attn_fwd(q, k, v, max_dist, *, name):
    R, C = q.shape
    chunk, nb, nch, main, prev, _ = _attn_specs(R, C)

    def body(q_ref, k_ref, v_ref, kp_ref, vp_ref, o_ref, lse_ref):
        c = pl.program_id(1)
        wide_mask, _, own_block = _band_masks(max_dist)
        heads = _head_masks()

        def block(q_blk, kk, vv, mask):
            q2 = jnp.concatenate([_keep(hm, q_blk) for hm in heads], axis=0)
            s = jnp.where(jnp.concatenate([mask, mask], axis=0), _dot(q2, kk, NT_DIMS), -jnp.inf)
            m = jnp.max(jnp.maximum(s[:, :QBLK], s[:, QBLK:]), axis=1, keepdims=True)
            p = jnp.exp(s - m)
            l = jnp.sum(p[:, :QBLK] + p[:, QBLK:], axis=1, keepdims=True)
            o2 = _dot(p.astype(BF16), vv) / l
            lse2 = jnp.broadcast_to(m + jnp.log(l), (2 * QBLK, LANES))
            return (jnp.where(heads[0], o2[:QBLK], o2[QBLK:]), jnp.where(heads[0], lse2[:QBLK], lse2[QBLK:]))

        first = pl.ds(0, QBLK)
        o0, l0 = block(q_ref[first, :], jnp.concatenate([kp_ref[...], k_ref[first, :]], axis=0),
                       jnp.concatenate([vp_ref[...], v_ref[first, :]], axis=0),
                       jnp.logical_and(wide_mask, jnp.logical_or(own_block, c > 0)))
        o_ref[first, :] = o0
        lse_ref[first, :] = l0

        def loop(qb, carry):
            cur = pl.ds(pl.multiple_of(qb * QBLK, QBLK), QBLK)
            both = pl.ds(pl.multiple_of((qb - 1) * QBLK, QBLK), 2 * QBLK)
            o, l = block(q_ref[cur, :], k_ref[both, :], v_ref[both, :], wide_mask)
            o_ref[cur, :] = o
            lse_ref[cur, :] = l
            return carry

        if nb > 1:
            lax.fori_loop(1, nb, loop, 0, unroll=ATTN_FWD_UNROLL)

    return pl.pallas_call(
        body, name=name, grid=(C // LANES, nch),
        in_specs=[main, main, main, prev, prev], out_specs=[main, main],
        out_shape=[jax.ShapeDtypeStruct((R, C), F32), jax.ShapeDtypeStruct((R, C), F32)],
        compiler_params=_params(("parallel", "parallel")),
    )(q, k, v, k, v)


def _attn_bwd(q, k, v, do, lse, delta, max_dist, *, name):
    R, C = q.shape
    chunk, nb, nch, main, prev, nxt = _attn_specs(R, C)

    def body(q_ref, k_ref, v_ref, do_ref, lse_ref, dl_ref, kp_ref, vp_ref, qn_ref, don_ref, lsen_ref, dln_ref,
             dq_ref, dk_ref, dv_ref):
        c = pl.program_id(1)
        wide_mask, prev_mask, own_block = _band_masks(max_dist)
        heads = _head_masks()

        def pair(q_blk, do_blk, lse_blk, dl_blk, kk, vv, mask, want_dq=True):
            q2 = jnp.concatenate([_keep(hm, q_blk) for hm in heads], axis=0)
            do2 = jnp.concatenate([_keep(hm, do_blk) for hm in heads], axis=0)
            lse2 = jnp.concatenate([_head_col(lse_blk, hm) for hm in heads], axis=0)
            dl2 = jnp.concatenate([_head_col(dl_blk, hm) for hm in heads], axis=0)
            p = jnp.where(jnp.concatenate([mask, mask], axis=0), jnp.exp(_dot(q2, kk, NT_DIMS) - lse2), 0.0)
            ds = (p * (_dot(do2, vv, NT_DIMS) - dl2)).astype(BF16)
            dq = None
            if want_dq:
                k2 = jnp.concatenate([_keep(khm, kk) for khm in _head_masks(kk.shape[0])], axis=0)
                dq = _dot(jnp.concatenate([ds[:QBLK], ds[QBLK:]], axis=1), k2)
            return dq, _dot(ds, q2, TN_DIMS), _dot(p.astype(BF16), do2, TN_DIMS)

        dk_ref[...] = jnp.zeros_like(dk_ref)
        dv_ref[...] = jnp.zeros_like(dv_ref)

        first = pl.ds(0, QBLK)
        dq0, dkk0, dvv0 = pair(q_ref[first, :], do_ref[first, :], lse_ref[first, :], dl_ref[first, :],
                               jnp.concatenate([kp_ref[...], k_ref[first, :]], axis=0),
                               jnp.concatenate([vp_ref[...], v_ref[first, :]], axis=0),
                               jnp.logical_and(wide_mask, jnp.logical_or(own_block, c > 0)))
        dq_ref[first, :] = dq0
        dk_ref[first, :] += dkk0[QBLK:, :]
        dv_ref[first, :] += dvv0[QBLK:, :]

        def loop(qb, carry):
            cur = pl.ds(pl.multiple_of(qb * QBLK, QBLK), QBLK)
            both = pl.ds(pl.multiple_of((qb - 1) * QBLK, QBLK), 2 * QBLK)
            dq, dkk, dvv = pair(q_ref[cur, :], do_ref[cur, :], lse_ref[cur, :], dl_ref[cur, :],
                                k_ref[both, :], v_ref[both, :], wide_mask)
            dq_ref[cur, :] = dq
            dk_ref[both, :] += dkk
            dv_ref[both, :] += dvv
            return carry

        if nb > 1:
            lax.fori_loop(1, nb, loop, 0, unroll=ATTN_BWD_UNROLL)

        last = pl.ds((nb - 1) * QBLK, QBLK)
        _, dk_n, dv_n = pair(qn_ref[...], don_ref[...], lsen_ref[...], dln_ref[...], k_ref[last, :], v_ref[last, :],
                             jnp.logical_and(prev_mask, c < nch - 1), want_dq=False)
        dk_ref[last, :] += dk_n
        dv_ref[last, :] += dv_n

    return pl.pallas_call(
        body, name=name, grid=(C // LANES, nch),
        in_specs=[main] * 6 + [prev, prev] + [nxt] * 4, out_specs=[main, main, main],
        out_shape=[jax.ShapeDtypeStruct((R, C), F32)] * 3,
        compiler_params=_params(("parallel", "parallel")),
    )(q, k, v, do, lse, delta, k, v, q, do, lse, delta)


def _head_sum(x):
    r = lax.broadcasted_iota(jnp.int32, (LANES, LANES), 0) // HEAD_DIM
    c = lax.broadcasted_iota(jnp.int32, (LANES, LANES), 1) // HEAD_DIM
    ones = jnp.where(r == c, 1.0, 0.0).astype(BF16)
    outs = []
    for j in range(x.shape[1] // LANES):
        rem = x[:, j * LANES:(j + 1) * LANES]
        acc = jnp.zeros(rem.shape, F32)
        for _ in range(3):
            part = rem.astype(BF16)
            acc = acc + _dot(part, ones)
            rem = rem - part.astype(F32)
        outs.append(acc)
    return outs[0] if len(outs) == 1 else jnp.concatenate(outs, axis=1)


def _branch_weights(lses):
    m = functools.reduce(jnp.maximum, lses)
    es = [jnp.exp(l - m) for l in lses]
    den = functools.reduce(lambda a, b: a + b, es)
    return [e / den for e in es]


def _combine_fwd(oa, lsea, sink, obs, lsebs, oc, *, name, tm=512):
    T = oa.shape[0]
    tm = min(tm, T)
    nb = len(obs)

    def body(*refs):
        oa_ref, lsea_ref, sink_ref = refs[:3]
        ob_refs = refs[3:3 + nb]
        lse_refs = refs[3 + nb:3 + 2 * nb]
        oc_ref, out_ref, stage = refs[3 + 2 * nb:]
        out_ref[:, 0:A_WIDTH] = (oa_ref[...] * _sigmoid(lsea_ref[...] - sink_ref[...])).astype(BF16)
        ws = _branch_weights([_undilate(r, d, stage) for r, d in zip(lse_refs, DILATIONS)])
        ob = _undilate(ob_refs[0], DILATIONS[0], stage) * ws[0]
        for r, d, w in zip(ob_refs[1:], DILATIONS[1:], ws[1:]):
            ob = ob + _undilate(r, d, stage) * w
        out_ref[:, A_WIDTH:A_WIDTH + B_WIDTH] = ob.astype(BF16)
        out_ref[:, A_WIDTH + B_WIDTH:MIX_WIDTH] = oc_ref[...].astype(BF16)

    def row(w):
        return pl.BlockSpec((tm, w), lambda i: (i, 0))

    ins = [oa, lsea, sink, *obs, *lsebs, oc]
    in_specs = [row(A_WIDTH), row(A_WIDTH), pl.BlockSpec((1, A_WIDTH), lambda i: (0, 0))]
    in_specs += [_dilated_spec(tm, d, B_WIDTH) for _ in range(2) for d in DILATIONS] + [row(C_WIDTH)]
    return pl.pallas_call(
        body, name=name, grid=(T // tm,), in_specs=in_specs, out_specs=row(MIX_WIDTH),
        out_shape=jax.ShapeDtypeStruct((T, MIX_WIDTH), BF16),
        scratch_shapes=[_stage(tm, B_WIDTH)],
        compiler_params=_params(("parallel",)),
    )(*ins)


def _combine_bwd(dmix, oa, lsea, sink, obs, lsebs, *, name, tm=512):
    T = oa.shape[0]
    tm = min(tm, T)
    nb = len(obs)

    def body(*refs):
        dmix_ref, oa_ref, lsea_ref, sink_ref = refs[:4]
        ob_refs = refs[4:4 + nb]
        lse_refs = refs[4 + nb:4 + 2 * nb]
        outs = refs[4 + 2 * nb:-1]
        stage = refs[-1]
        doa_ref, dla_ref = outs[:2]
        dob_refs = outs[2:2 + nb]
        dlb_refs = outs[2 + nb:2 + 2 * nb]
        doc_ref, dsink_ref = outs[2 + 2 * nb:]

        d_a = dmix_ref[:, 0:A_WIDTH]
        d_b = dmix_ref[:, A_WIDTH:A_WIDTH + B_WIDTH]
        doc_ref[...] = dmix_ref[:, A_WIDTH + B_WIDTH:MIX_WIDTH]

        gate = _sigmoid(lsea_ref[...] - sink_ref[...])
        doa_ref[...] = (d_a * gate).astype(BF16)
        dgate = _head_sum(d_a * oa_ref[...])
        dlse = dgate * gate * (1.0 - gate)
        dla_ref[...] = dgate * gate - dlse

        @pl.when(pl.program_id(0) == 0)
        def _():
            dsink_ref[...] = jnp.zeros_like(dsink_ref)

        dsink_ref[...] -= jnp.sum(dlse, axis=0, keepdims=True)

        ws = _branch_weights([_undilate(r, d, stage) for r, d in zip(lse_refs, DILATIONS)])
        dws = [_head_sum(d_b * _undilate(r, d, stage)) for r, d in zip(ob_refs, DILATIONS)]
        sw = ws[0] * dws[0]
        for w, dw in zip(ws[1:], dws[1:]):
            sw = sw + w * dw
        for w, d, do_ref, dl_ref in zip(ws, DILATIONS, dob_refs, dlb_refs):
            _dilate_into(do_ref, w * d_b, d, stage)
            _dilate_into(dl_ref, w * sw, d, stage)

    def row(w):
        return pl.BlockSpec((tm, w), lambda i: (i, 0))

    vec = pl.BlockSpec((1, A_WIDTH), lambda i: (0, 0))
    dil = [_dilated_spec(tm, d, B_WIDTH) for _ in range(2) for d in DILATIONS]
    ins = [dmix, oa, lsea, sink, *obs, *lsebs]
    in_specs = [row(MIX_WIDTH), row(A_WIDTH), row(A_WIDTH), vec] + dil
    out_specs = [row(A_WIDTH), row(A_WIDTH)] + dil + [row(C_WIDTH), vec]
    out_shape = [jax.ShapeDtypeStruct((T, A_WIDTH), BF16), jax.ShapeDtypeStruct((T, A_WIDTH), F32)]
    out_shape += [jax.ShapeDtypeStruct((T // d, d * B_WIDTH), BF16) for d in DILATIONS]
    out_shape += [jax.ShapeDtypeStruct((T // d, d * B_WIDTH), F32) for d in DILATIONS]
    out_shape += [jax.ShapeDtypeStruct((T, C_WIDTH), F32), jax.ShapeDtypeStruct((1, A_WIDTH), F32)]
    res = pl.pallas_call(
        body, name=name, grid=(T // tm,), in_specs=in_specs, out_specs=out_specs, out_shape=out_shape,
        scratch_shapes=[_stage(tm, B_WIDTH)],
        compiler_params=_params(("arbitrary",)),
    )(*ins)
    return res[0], res[1], list(res[2:2 + nb]), list(res[2 + nb:2 + 2 * nb]), res[2 + 2 * nb], res[3 + 2 * nb]


HIST = 8


def _softplus_neg(lam):
    e = jnp.exp(-jnp.abs(lam))
    log1p = jnp.where(e < 0.01, e * (1.0 - e * (0.5 - e * (1.0 / 3.0))), jnp.log(1.0 + e))
    return jnp.maximum(-lam, 0.0) + log1p


def _neg_expm1(x):
    series = -x * (1.0 + x * (0.5 + x * (1.0 / 6.0 + x * (1.0 / 24.0 + x * (1.0 / 120.0)))))
    return jnp.where(x > -0.1, series, 1.0 - jnp.exp(x))


def _gelu_parts(x):
    k = math.sqrt(2.0 / math.pi)
    t = jnp.tanh(k * (x + 0.044715 * (x * x * x)))
    cdf = 0.5 * (1.0 + t)
    return x * cdf, cdf + 0.5 * x * (1.0 - t * t) * (k * (1.0 + 3.0 * 0.044715 * (x * x)))


def _rglru_gates(y, pos_ref, wr_ref, br_ref, wi_ref, bi_ref, lam_ref):
    yb = y.astype(BF16)
    r = _sigmoid(_dot(yb, wr_ref[...]) + br_ref[...])
    ig = _sigmoid(_dot(yb, wi_ref[...]) + bi_ref[...])
    sp = _softplus_neg(lam_ref[...])
    log_a = -C_EXP * r * sp
    reset = pos_ref[...] == 0
    a = jnp.where(reset, 0.0, jnp.exp(log_a))
    mult = jnp.where(reset, 1.0, jnp.sqrt(_neg_expm1(2.0 * log_a)))
    return yb, r, ig, sp, reset, a, mult


def _conv_fwd(xs_ref, cw_ref, cb_ref, tm):
    y = cb_ref[...] + cw_ref[0:1, :] * xs_ref[HIST:HIST + tm, :]
    for j in range(1, C_CONV):
        y = y + cw_ref[j:j + 1, :] * xs_ref[HIST - j:HIST - j + tm, :]
    return y


SCAN_GROUP = 8


def _blocked_scan(c, d, c_s, d_s, grp_a, grp_h, carry, reverse):
    tm, W = d.shape
    groups = tm // SCAN_GROUP
    order = range(SCAN_GROUP - 1, -1, -1) if reverse else range(SCAN_GROUP)
    outs, lasts = [], []
    for k in range(W // LANES):
        lanes = slice(k * LANES, (k + 1) * LANES)
        ck, dk, ga, gh = c_s.at[k], d_s.at[k], grp_a.at[k], grp_h.at[k]
        ck[...] = c[:, lanes]
        dk[...] = d[:, lanes]
        prod = state = None
        for j in order:
            rows = pl.ds(j, groups, stride=SCAN_GROUP)
            cj, dj = ck[rows, :], dk[rows, :]
            if prod is None:
                prod, state = cj, dj
            else:
                state = cj * state + dj
                prod = cj * prod
            ck[rows, :] = prod
            dk[rows, :] = state
        ga[...] = prod
        gh[...] = state

        def step(i, h, ga=ga, gh=gh):
            row = pl.ds(groups - 1 - i if reverse else i, 1)
            a, t = ga[row, :], gh[row, :]
            ga[row, :] = h
            return a * h + t

        lasts.append(lax.fori_loop(0, groups, step, carry[:, lanes], unroll=8))
        entering = ga[...]
        for j in range(SCAN_GROUP):
            rows = pl.ds(j, groups, stride=SCAN_GROUP)
            dk[rows, :] = dk[rows, :] + ck[rows, :] * entering
        outs.append(dk[...])
    return jnp.concatenate(outs, axis=1), jnp.concatenate(lasts, axis=1)


def _rglru_fwd(xc, gc, pos, cw, cb, wr, br, wi, bi, lam, *, name, tm=512):
    T, W = xc.shape
    tm = min(tm, T)

    def body(xc_ref, gc_ref, pos_ref, cw_ref, cb_ref, wr_ref, br_ref, wi_ref, bi_ref, lam_ref,
             out_ref, hs_ref, xs, a_s, b_s, h_s, grp_a, grp_h):
        @pl.when(pl.program_id(0) == 0)
        def _():
            xs[0:HIST, :] = jnp.zeros((HIST, W), F32)
            h_s[...] = jnp.zeros_like(h_s)

        xv = xc_ref[...]
        xs[HIST:HIST + tm, :] = xv
        y = _conv_fwd(xs, cw_ref, cb_ref, tm)
        xs[0:HIST, :] = xv[tm - HIST:tm, :]
        _, _, ig, _, _, a, mult = _rglru_gates(y, pos_ref, wr_ref, br_ref, wi_ref, bi_ref, lam_ref)
        hs, h_s[...] = _blocked_scan(a, mult * (ig * y), a_s, b_s, grp_a, grp_h, h_s[...], reverse=False)
        hs_ref[...] = hs
        out_ref[...] = hs * _gelu_parts(gc_ref[...])[0]

    row = pl.BlockSpec((tm, W), lambda i: (i, 0))
    full = lambda shape: pl.BlockSpec(shape, lambda i: (0,) * len(shape))
    return pl.pallas_call(
        body, name=name, grid=(T // tm,),
        in_specs=[row, row, pl.BlockSpec((tm, 1), lambda i: (i, 0)), full((C_CONV, W)), full((1, W)),
                  full((W, W)), full((1, W)), full((W, W)), full((1, W)), full((1, W))],
        out_specs=[row, row],
        out_shape=[jax.ShapeDtypeStruct((T, W), F32)] * 2,
        scratch_shapes=[pltpu.VMEM((tm + HIST, W), F32), pltpu.VMEM((W // LANES, tm, LANES), F32),
                        pltpu.VMEM((W // LANES, tm, LANES), F32), pltpu.VMEM((1, W), F32),
                        pltpu.VMEM((W // LANES, tm // SCAN_GROUP, LANES), F32),
                        pltpu.VMEM((W // LANES, tm // SCAN_GROUP, LANES), F32)],
        compiler_params=_params(("arbitrary",)),
    )(xc, gc, pos, cw, cb, wr, br, wi, bi, lam)


def _rglru_bwd(xc, gc, pos, hs, dout, cw, cb, wr, br, wi, bi, lam, *, name, tm=512):
    T, W = xc.shape
    tm = min(tm, T)
    nt = T // tm
    hb = tm // HIST

    def body(xc_ref, gc_ref, pos_ref, hs_ref, dout_ref, xch_ref, hsh_ref,
             cw_ref, cb_ref, wr_ref, br_ref, wi_ref, bi_ref, lam_ref,
             dxc_ref, dgc_ref, dcw_ref, dcb_ref, dwr_ref, dbr_ref, dwi_ref, dbi_ref, dlam_ref,
             xs, hsx, dys, asx, a_s, d_s, carry_s, grp_a, grp_h):
        i = pl.program_id(0)

        @pl.when(i == 0)
        def _():
            for r in (dcw_ref, dcb_ref, dwr_ref, dbr_ref, dwi_ref, dbi_ref, dlam_ref, carry_s):
                r[...] = jnp.zeros_like(r)
            dys[tm:tm + HIST, :] = jnp.zeros((HIST, W), F32)
            asx[tm:tm + HIST, :] = jnp.zeros((HIST, W), F32)

        has_prev = i < nt - 1
        xs[0:HIST, :] = jnp.where(has_prev, xch_ref[...], 0.0)
        hsx[0:HIST, :] = jnp.where(has_prev, hsh_ref[...], 0.0)
        xs[HIST:HIST + tm, :] = xc_ref[...]
        hs = hs_ref[...]
        hsx[HIST:HIST + tm, :] = hs
        y = _conv_fwd(xs, cw_ref, cb_ref, tm)
        yb, r, ig, sp, reset, a, mult = _rglru_gates(y, pos_ref, wr_ref, br_ref, wi_ref, bi_ref, lam_ref)

        gelu, dgelu = _gelu_parts(gc_ref[...])
        dout = dout_ref[...]
        dgc_ref[...] = dout * hs * dgelu
        asx[0:tm, :] = a
        a_up = asx[1:1 + tm, :]
        asx[tm:tm + HIST, :] = a[0:HIST, :]
        dh, carry_s[...] = _blocked_scan(a_up, dout * gelu, a_s, d_s, grp_a, grp_h, carry_s[...], reverse=True)
        hprev = hsx[HIST - 1:HIST - 1 + tm, :]
        igy = ig * y
        dmult = dh * igy
        digy = dh * mult
        dlog_a = jnp.where(reset, 0.0, dh * hprev * a - dmult * a * a / mult)
        dlam_ref[...] += jnp.sum(dlog_a * (C_EXP * r) * _sigmoid(-lam_ref[...]), axis=0, keepdims=True)
        dz_r = dlog_a * (-C_EXP * sp) * r * (1.0 - r)
        dz_i = digy * y * ig * (1.0 - ig)
        dzr_b = dz_r.astype(BF16)
        dzi_b = dz_i.astype(BF16)
        dy = digy * ig + _dot(dzr_b, wr_ref[...], NT_DIMS) + _dot(dzi_b, wi_ref[...], NT_DIMS)
        dwr_ref[...] += _dot(yb, dzr_b, TN_DIMS)
        dwi_ref[...] += _dot(yb, dzi_b, TN_DIMS)
        dbr_ref[...] += jnp.sum(dz_r, axis=0, keepdims=True)
        dbi_ref[...] += jnp.sum(dz_i, axis=0, keepdims=True)

        dys[0:tm, :] = dy
        dxc = cw_ref[0:1, :] * dy
        for j in range(1, C_CONV):
            dxc = dxc + cw_ref[j:j + 1, :] * dys[j:j + tm, :]
        dxc_ref[...] = dxc
        dys[tm:tm + HIST, :] = dy[0:HIST, :]
        dcb_ref[...] += jnp.sum(dy, axis=0, keepdims=True)
        for j in range(C_CONV):
            dcw_ref[j:j + 1, :] += jnp.sum(dy * xs[HIST - j:HIST - j + tm, :], axis=0, keepdims=True)

    row = pl.BlockSpec((tm, W), lambda i: (nt - 1 - i, 0))
    halo = pl.BlockSpec((HIST, W), lambda i: (jnp.maximum((nt - 1 - i) * hb - 1, 0), 0))
    full = lambda shape: pl.BlockSpec(shape, lambda i: (0,) * len(shape))
    out_specs = [row, row, full((C_CONV, W)), full((1, W)), full((W, W)), full((1, W)), full((W, W)), full((1, W)),
                 full((1, W))]
    out_shape = [jax.ShapeDtypeStruct((T, W), F32)] * 2
    out_shape += [jax.ShapeDtypeStruct(s, F32) for s in ((C_CONV, W), (1, W), (W, W), (1, W), (W, W), (1, W), (1, W))]
    return pl.pallas_call(
        body, name=name, grid=(nt,),
        in_specs=[row, row, pl.BlockSpec((tm, 1), lambda i: (nt - 1 - i, 0)), row, row, halo, halo,
                  full((C_CONV, W)), full((1, W)), full((W, W)), full((1, W)), full((W, W)), full((1, W)),
                  full((1, W))],
        out_specs=out_specs, out_shape=out_shape,
        scratch_shapes=[pltpu.VMEM((tm + HIST, W), F32), pltpu.VMEM((tm + HIST, W), F32),
                        pltpu.VMEM((tm + HIST, W), F32), pltpu.VMEM((tm + HIST, W), F32),
                        pltpu.VMEM((W // LANES, tm, LANES), F32), pltpu.VMEM((W // LANES, tm, LANES), F32),
                        pltpu.VMEM((1, W), F32), pltpu.VMEM((W // LANES, tm // SCAN_GROUP, LANES), F32),
                        pltpu.VMEM((W // LANES, tm // SCAN_GROUP, LANES), F32)],
        compiler_params=_params(("arbitrary",)),
    )(xc, gc, pos, hs, dout, xc, hs, cw, cb, wr, br, wi, bi, lam)


def _adam_math(w, g, m, v):
    m = ADAM_B1 * m + (1.0 - ADAM_B1) * g
    v = ADAM_B2 * v + (1.0 - ADAM_B2) * (g * g)
    m_hat = m / (1.0 - ADAM_B1 ** ADAM_STEP)
    v_hat = v / (1.0 - ADAM_B2 ** ADAM_STEP)
    delta = -ADAM_LR * (m_hat / (jnp.sqrt(v_hat) + ADAM_EPS) + ADAM_WD * w)
    return delta, m, v


def _pick_rows(R, cap=512, mult=16):
    for d in range(min(cap, R), 0, -1):
        if R % d == 0 and d % mult == 0:
            return d
    return R


def _adamw(parts, w, m, v, *, name, tr=None, part=0, prev=None):
    R, C = w.shape
    r = parts.shape[1]
    tr = _pick_rows(r) if tr is None else tr
    assert r % tr == 0 and R % r == 0, (name, R, r, tr)
    nt = r // tr

    def body(p_ref, w_ref, m_ref, v_ref, *rest):
        g_ref, d_ref, nm_ref, nv_ref = rest[-4:]
        g = p_ref[0].astype(F32)
        for d in range(1, N_DEV):
            g = g + p_ref[d].astype(F32)
        delta, nm, nv = _adam_math(w_ref[...], g, m_ref[...], v_ref[...])
        g_ref[...] = g
        d_ref[...] = delta
        nm_ref[...] = nm
        nv_ref[...] = nv

    row = pl.BlockSpec((tr, C), lambda i: (part * nt + i, 0))
    in_specs = [pl.BlockSpec((N_DEV, tr, C), lambda i: (0, i, 0)), row, row, row]
    operands = [parts, w, m, v]
    aliases = {}
    if prev is not None:
        in_specs += [pl.BlockSpec(memory_space=pl.ANY)] * 4
        operands += list(prev)
        aliases = {4 + i: i for i in range(4)}
    return pl.pallas_call(
        body, name=name, grid=(nt,), in_specs=in_specs,
        out_specs=[row] * 4, out_shape=[jax.ShapeDtypeStruct((R, C), F32)] * 4,
        input_output_aliases=aliases,
        compiler_params=_params(("parallel",)),
    )(*operands)


def _exchange(srcs, gather, *, name):
    n = len(srcs)
    out_shape = [jax.ShapeDtypeStruct((N_DEV,) + s.shape if gather else s.shape, s.dtype) for s in srcs]

    def body(*refs):
        ins, outs = refs[:n], refs[n:2 * n]
        send_sems, recv_sems, local_sems = refs[2 * n:]
        x, y, c = lax.axis_index("x"), lax.axis_index("y"), lax.axis_index("c")
        me = 4 * x + 2 * y + c
        local_copies, sends, arrivals = [], [], []
        for a in range(n):
            mine = ins[a] if gather else ins[a].at[me]
            local = pltpu.make_async_copy(mine, outs[a].at[me], local_sems.at[a])
            local.start()
            local_copies.append(local)
            for k in range(1, N_DEV):
                px, py, pc = x ^ ((k >> 2) & 1), y ^ ((k >> 1) & 1), c ^ (k & 1)
                peer = 4 * px + 2 * py + pc
                send = pltpu.make_async_remote_copy(
                    src_ref=ins[a] if gather else ins[a].at[peer], dst_ref=outs[a].at[me],
                    send_sem=send_sems.at[a * N_DEV + k], recv_sem=recv_sems.at[a * N_DEV + k],
                    device_id=(px, py, pc), device_id_type=pl.DeviceIdType.MESH)
                send.start()
                sends.append(send)
                arrivals.append(pltpu.make_async_remote_copy(
                    src_ref=mine, dst_ref=outs[a].at[peer],
                    send_sem=send_sems.at[a * N_DEV + k], recv_sem=recv_sems.at[a * N_DEV + k],
                    device_id=(px, py, pc), device_id_type=pl.DeviceIdType.MESH))
        for cp in sends:
            cp.wait_send()
        for cp in arrivals:
            cp.wait_recv()
        for cp in local_copies:
            cp.wait()

    return pl.pallas_call(
        body, name=name,
        in_specs=[pl.BlockSpec(memory_space=pl.ANY)] * n, out_specs=[pl.BlockSpec(memory_space=pl.ANY)] * n,
        out_shape=out_shape,
        scratch_shapes=[pltpu.SemaphoreType.DMA((n * N_DEV,)), pltpu.SemaphoreType.DMA((n * N_DEV,)),
                        pltpu.SemaphoreType.DMA((n,))],
    )(*srcs)


_HBM = pl.BlockSpec(memory_space=pltpu.HBM)
_SEM = pl.BlockSpec(memory_space=pltpu.SEMAPHORE)
_EFFECT = pltpu.SideEffectType.DATAFLOW_SIDE_EFFECTING


def _peers():
    x, y, c = lax.axis_index("x"), lax.axis_index("y"), lax.axis_index("c")
    out = []
    for k in range(1, N_DEV):
        px, py, pc = x ^ ((k >> 2) & 1), y ^ ((k >> 1) & 1), c ^ (k & 1)
        out.append((k, (px, py, pc), 4 * px + 2 * py + pc))
    return 4 * x + 2 * y + c, out


def _split_copies(src_refs, land_refs, send_sems, recv_sems, gather):
    me, peers = _peers()
    out = []
    for a, (src_ref, land_ref) in enumerate(zip(src_refs, land_refs)):
        for k, dev, blk in peers:
            common = dict(send_sem=send_sems.at[a * N_DEV + k], recv_sem=recv_sems.at[a * N_DEV + k], device_id=dev,
                          device_id_type=pl.DeviceIdType.MESH)
            src = src_ref if gather else src_ref.at[blk]
            out.append((pltpu.make_async_remote_copy(src_ref=src, dst_ref=land_ref.at[me], **common),
                        pltpu.make_async_remote_copy(src_ref=src, dst_ref=land_ref.at[blk], **common)))
    return out


def _exchange_start(srcs, gather, *, name, after=None):
    n = len(srcs)
    lands = [lax.empty((N_DEV,) + (s.shape if gather else s.shape[1:]), s.dtype) for s in srcs]

    def body(*refs):
        src_refs, land_refs = refs[:n], refs[n:2 * n]
        send_sems, recv_sems = refs[-2 * n - 3:-2 * n - 1]
        token = refs[-1]
        for outgoing, _ in _split_copies(src_refs, land_refs, send_sems, recv_sems, gather):
            outgoing.start()
        token[...] = jnp.zeros_like(token)

    res = pl.pallas_call(
        body, name=name,
        out_shape=(pltpu.SemaphoreType.DMA((n * N_DEV,)), pltpu.SemaphoreType.DMA((n * N_DEV,)),
                   *[pltpu.HBM(a.shape, a.dtype) for a in srcs + lands], jax.ShapeDtypeStruct((8, LANES), F32)),
        in_specs=(_HBM,) * (2 * n) + ((pl.BlockSpec(memory_space=pl.ANY),) if after is not None else ()),
        out_specs=(_SEM, _SEM) + (_HBM,) * (2 * n) + (pl.BlockSpec(memory_space=pltpu.VMEM),),
        input_output_aliases={i: i + 2 for i in range(2 * n)},
        compiler_params=pltpu.CompilerParams(has_side_effects=_EFFECT),
    )(*[pltpu.with_memory_space_constraint(a, pltpu.HBM) for a in srcs + lands],
      *([after] if after is not None else []))
    return res[0], res[1], list(res[2:2 + n]), list(res[2 + n:2 + 2 * n]), res[-1]


def _exchange_wait(started, after, gather, *, name):
    send_sems, recv_sems, srcs, lands, _ = started
    n = len(srcs)

    def body(*refs):
        src_refs, land_refs = refs[:n], refs[n:2 * n]
        send_sems, recv_sems = refs[2 * n:2 * n + 2]
        for outgoing, incoming in _split_copies(src_refs, land_refs, send_sems, recv_sems, gather):
            outgoing.wait_send()
            incoming.wait_recv()

    res = pl.pallas_call(
        body, name=name,
        out_shape=tuple(pltpu.HBM(a.shape, a.dtype) for a in srcs + lands),
        in_specs=(_HBM,) * (2 * n) + (_SEM, _SEM, pl.BlockSpec(memory_space=pl.ANY)), out_specs=(_HBM,) * (2 * n),
        input_output_aliases={i: i for i in range(2 * n)},
        compiler_params=pltpu.CompilerParams(has_side_effects=_EFFECT),
    )(*srcs, *lands, send_sems, recv_sems, after)
    return list(res[:n]), list(res[n:])


def _cols_to_blocks(g, *, name, tr=128):
    R, C = g.shape
    w = C // N_DEV
    tr = min(tr, R)

    def body(g_ref, o_ref):
        for p in range(N_DEV):
            o_ref[p] = g_ref[:, p * w:(p + 1) * w].astype(BF16)

    return pl.pallas_call(
        body, name=name, grid=(R // tr,),
        in_specs=[pl.BlockSpec((tr, C), lambda i: (i, 0))],
        out_specs=pl.BlockSpec((N_DEV, tr, w), lambda i: (0, i, 0)),
        out_shape=jax.ShapeDtypeStruct((N_DEV, R, w), BF16),
        compiler_params=_params(("parallel",)),
    )(g)


def _blocks_to_cols(b, *, name, tr=128):
    _, R, w = b.shape
    tr = min(tr, R)

    def body(b_ref, o_ref):
        o_ref[...] = jnp.concatenate([b_ref[p].astype(F32) for p in range(N_DEV)], axis=1).astype(o_ref.dtype)

    return pl.pallas_call(
        body, name=name, grid=(R // tr,),
        in_specs=[pl.BlockSpec((N_DEV, tr, w), lambda i: (0, i, 0))],
        out_specs=pl.BlockSpec((tr, N_DEV * w), lambda i: (i, 0)),
        out_shape=jax.ShapeDtypeStruct((R, N_DEV * w), b.dtype),
        compiler_params=_params(("parallel",)),
    )(b)


def _pack_rows(arrays, *, name, pick=None):
    B, _, w = arrays[0].shape
    rows = [a.shape[1] for a in arrays]
    first = 0
    if pick is not None:
        B, first = 1, pick

    def body(*refs):
        o_ref = refs[-1]
        r = 0
        for a_ref, n in zip(refs[:-1], rows):
            o_ref[0, r:r + n, :] = a_ref[0].astype(BF16)
            r += n

    return pl.pallas_call(
        body, name=name, grid=(B,),
        in_specs=[pl.BlockSpec((1, n, w), lambda b: (first + b, 0, 0)) for n in rows],
        out_specs=pl.BlockSpec((1, sum(rows), w), lambda b: (b, 0, 0)),
        out_shape=jax.ShapeDtypeStruct((B, sum(rows), w), BF16),
        compiler_params=_params(("parallel",)),
    )(*arrays)


def _unpack_rows(land, src, rows, *, name):
    _, R, w = land.shape
    src_spec = (pl.BlockSpec((1, R, w), lambda p: (p, 0, 0)) if src.ndim == 3
                else pl.BlockSpec((R, w), lambda p: (0, 0)))

    def body(land_ref, src_ref, *o_refs):
        me = 4 * lax.axis_index("x") + 2 * lax.axis_index("y") + lax.axis_index("c")
        mine = pl.program_id(0) == me
        r = 0
        for o_ref, n in zip(o_refs, rows):
            rows_i = slice(r, r + n)

            @pl.when(mine)
            def _(o_ref=o_ref, rows_i=rows_i):
                o_ref[0] = src_ref[0, rows_i, :] if src.ndim == 3 else src_ref[rows_i, :]

            @pl.when(jnp.logical_not(mine))
            def _(o_ref=o_ref, rows_i=rows_i):
                o_ref[0] = land_ref[0, rows_i, :]

            r += n

    return pl.pallas_call(
        body, name=name, grid=(N_DEV,),
        in_specs=[pl.BlockSpec((1, R, w), lambda p: (p, 0, 0)), src_spec],
        out_specs=[pl.BlockSpec((1, n, w), lambda p: (p, 0, 0)) for n in rows],
        out_shape=[jax.ShapeDtypeStruct((N_DEV, n, w), land.dtype) for n in rows],
        compiler_params=_params(("parallel",)),
    )(land, src)


def _to_blocks(w, axis):
    shape = w.shape
    k = shape[axis] // N_DEV
    w = w.reshape(shape[:axis] + (N_DEV, k) + shape[axis + 1:])
    return jnp.moveaxis(w, axis, 0)


def _from_blocks(wb, axis):
    w = jnp.moveaxis(wb, 0, axis)
    shape = w.shape
    return w.reshape(shape[:axis] + (shape[axis] * shape[axis + 1],) + shape[axis + 2:])


def _block_diag(w):
    n, k, _ = w.shape
    eye = jnp.eye(n, dtype=w.dtype)
    return (eye[:, None, :, None] * w[:, :, None, :]).reshape(n * k, n * k)


def _diag_blocks(wd):
    k = HEAD_DIM
    return jnp.stack([wd[h * k:(h + 1) * k, h * k:(h + 1) * k] for h in range(C_BLOCKS)])


def _pack(arrays, row_multiple=8):
    rows = []
    for a in arrays:
        flat = a.reshape(-1).astype(F32)
        pad = (-flat.shape[0]) % LANES
        rows.append(jnp.pad(flat, (0, pad)).reshape(-1, LANES))
    out = jnp.concatenate(rows, axis=0)
    return jnp.pad(out, ((0, (-out.shape[0]) % row_multiple), (0, 0)))


def _unpack(packed, shapes):
    outs, r = [], 0
    for s in shapes:
        size = math.prod(s)
        nrows = -(-size // LANES)
        outs.append(packed[r:r + nrows].reshape(-1)[:size].reshape(s))
        r += nrows
    return outs


def _rope_tables(positions):
    inv = 1.0 / (ROPE_THETA ** (jnp.arange(0, HEAD_DIM, 2, dtype=F32) / HEAD_DIM))
    ang = positions.astype(F32)[:, None] * inv
    cos, sin = jnp.cos(ang), jnp.sin(ang)
    return jnp.tile(cos, (1, 4)), jnp.tile(jnp.concatenate([-sin, sin], axis=1), (1, 2))


def _layer_fwd(l, x, pos, cos_t, sin_t, W, before_mixer=None):
    tag = f"l{l}"
    saved = {'x0': x}
    x1, a1, u1 = _ffn_fwd(x, W['norm_ffn1'][l], W['ffn1_gate'][l], W['ffn1_up'][l], W['ffn1_down'][l],
                          name=f"ffn1_fwd_{tag}")
    if before_mixer is not None:
        before_mixer(l, x1)
    h = _rms_fwd(x1, W['norm_mix'][l], name=f"mixnorm_fwd_{tag}")
    proj = _mm(h, W['w_in'][l], 'nn', name=f"proj_{tag}", tm=512, tn=IN_COLS, tk=h.shape[1])
    qa, ka2, va2, qb, kb, vb, xc, gc = _split_rope(proj, cos_t, sin_t, name=f"split_{tag}")
    oa, lsea = _attn_fwd(qa, ka2, va2, A_MAX_DIST, name=f"attn_a_fwd_{tag}")
    obs, lsebs = [], []
    for bi, (window, d) in enumerate(B_BRANCHES):
        o, lse = _attn_fwd(qb[bi], kb[bi], vb[bi], window // d, name=f"attn_b{bi}_fwd_{tag}")
        obs.append(o)
        lsebs.append(lse)
    oc, hs = _rglru_fwd(xc, gc, pos, W['conv_w'][l], W['conv_b'][l], W['rg_w_r'][l], W['rg_b_r'][l],
                        W['rg_w_i'][l], W['rg_b_i'][l], W['rg_lambda'][l], name=f"rglru_fwd_{tag}")
    mix = _combine_fwd(oa, lsea, W['sinks'][l], obs, lsebs, oc, name=f"combine_fwd_{tag}")
    x2 = _mm(mix, W['w_out'][l], 'nn', name=f"outproj_{tag}", tm=512, tn=x.shape[1], tk=MIX_WIDTH, res=x1)
    x3, a2, u2 = _ffn_fwd(x2, W['norm_ffn2'][l], W['ffn2_gate'][l], W['ffn2_up'][l], W['ffn2_down'][l],
                          name=f"ffn2_fwd_{tag}")
    saved.update(a1=a1, u1=u1, x1=x1, h=h, qa=qa, ka2=ka2, va2=va2, qb=qb, kb=kb, vb=vb, xc=xc, gc=gc, oa=oa,
                 lsea=lsea, obs=obs, lsebs=lsebs, hs=hs, mix=mix, x2=x2, a2=a2, u2=u2)
    return x3, saved


def _ffn_grads(tag, which, x, g, dy, a, u, wg, wu, wd):
    T, D = x.shape
    F = wg.shape[1]
    dx, dg, n, act, da, du = _ffn_bwd(x, g, dy, a, u, wg, wu, wd, name=f"{which}_bwd_{tag}")
    fc = _ffn_chunk(F)
    d_gate = _mm(n, da, 'tn', name=f"{which}_dgate_{tag}", tm=512, tn=fc, tk=4096)
    d_up = _mm(n, du, 'tn', name=f"{which}_dup_{tag}", tm=512, tn=fc, tk=4096)
    d_down = _mm(act, dy, 'tn', name=f"{which}_ddown_{tag}", tm=fc, tn=D, tk=1024, alpha=0.5)
    return dx, dg, d_gate, d_up, d_down


def _layer_bwd(l, dx3, pos, cos_t, sin_t, W, S, on_grads=None):
    tag = f"l{l}"
    G = {}
    dx2, G['norm_ffn2'], G['ffn2_gate'], G['ffn2_up'], G['ffn2_down'] = _ffn_grads(
        tag, 'ffn2', S['x2'], W['norm_ffn2'][l], dx3, S['a2'], S['u2'], W['ffn2_gate'][l], W['ffn2_up'][l],
        W['ffn2_down'][l])
    D = dx2.shape[1]
    dmix = _mm(dx2, W['w_out'][l], 'nt', name=f"outproj_dx_{tag}", tm=512, tn=MIX_WIDTH, tk=D)
    G['w_out'] = _mm(S['mix'], dx2, 'tn', name=f"outproj_dw_{tag}", tm=MIX_WIDTH, tn=D, tk=2048)
    doa, dla, dobs, dlbs, doc, dsink = _combine_bwd(dmix, S['oa'], S['lsea'], W['sinks'][l], S['obs'], S['lsebs'],
                                                    name=f"combine_bwd_{tag}")
    G['attn_sinks'] = dsink.reshape(A_WIDTH // HEAD_DIM, HEAD_DIM)[:, 0]
    dqa, dka2, dva2 = _attn_bwd(S['qa'], S['ka2'], S['va2'], doa, S['lsea'], dla, A_MAX_DIST,
                                name=f"attn_a_bwd_{tag}")
    dqb, dkb, dvb = [], [], []
    for bi, (window, d) in enumerate(B_BRANCHES):
        dq, dk, dv = _attn_bwd(S['qb'][bi], S['kb'][bi], S['vb'][bi], dobs[bi], S['lsebs'][bi], dlbs[bi], window // d,
                               name=f"attn_b{bi}_bwd_{tag}")
        dqb.append(dq)
        dkb.append(dk)
        dvb.append(dv)
    (dxc, dgc, G['conv_w'], G['conv_b'], dwr, G['rg_b_r'], dwi, G['rg_b_i'], G['rg_lambda']) = _rglru_bwd(
        S['xc'], S['gc'], pos, S['hs'], doc, W['conv_w'][l], W['conv_b'][l], W['rg_w_r'][l], W['rg_b_r'][l],
        W['rg_w_i'][l], W['rg_b_i'][l], W['rg_lambda'][l], name=f"rglru_bwd_{tag}")
    G['rg_w_r'] = _diag_blocks(dwr)
    G['rg_w_i'] = _diag_blocks(dwi)
    dproj = _merge_dproj(dqa, dka2, dva2, dqb, dkb, dvb, dxc, dgc, cos_t, sin_t, name=f"merge_{tag}")
    dh = _mm(dproj, W['w_in'][l], 'nt', name=f"proj_dx_{tag}", tm=512, tn=D, tk=IN_COLS)
    G['w_in'] = _mm(S['h'], dproj, 'tn', name=f"proj_dw_{tag}", tm=512, tn=IN_COLS, tk=2048)
    g_mix = W['norm_mix'][l]
    if on_grads is not None:
        g_mix = g_mix + on_grads(l, 0, G)
    dx1, G['norm_mix'] = _rms_bwd(S['x1'], g_mix, dh, dx2, name=f"mixnorm_bwd_{tag}")
    dx0, G['norm_ffn1'], G['ffn1_gate'], G['ffn1_up'], G['ffn1_down'] = _ffn_grads(
        tag, 'ffn1', S['x0'], W['norm_ffn1'][l], dx1, S['a1'], S['u1'], W['ffn1_gate'][l], W['ffn1_up'][l],
        W['ffn1_down'][l])
    if on_grads is not None:
        on_grads(l, 1, G)
    return dx0, G


def _device_step(x, positions, loss_target, W, before_layer=None, on_grads=None, before_mixer=None):
    T = x.shape[0]
    pos = positions.reshape(T, 1)
    cos_t, sin_t = _rope_tables(positions)
    saved = []
    for l in range(DEPTH):
        if before_layer is not None:
            before_layer(l, x)
        x, S = _layer_fwd(l, x, pos, cos_t, sin_t, W, before_mixer)
        saved.append(S)
    loss, dx, dg_final = _loss_head(x, W['norm_final'], loss_target, name="loss_head")
    grads = [None] * DEPTH
    for l in reversed(range(DEPTH)):
        dx, grads[l] = _layer_bwd(l, dx, pos, cos_t, sin_t, W, saved[l], on_grads)
    return loss, dx, grads, dg_final


SHARD_AXIS = {'ffn1_gate': 2, 'ffn1_up': 2, 'ffn1_down': 1, 'w_in': 2, 'w_out': 1, 'ffn2_gate': 2, 'ffn2_up': 2,
              'ffn2_down': 1, 'conv_w': 2}


def kernel(x, positions, norm_ffn1, ffn1_gate, ffn1_up, ffn1_down, norm_mix, w_in, attn_sinks, conv_w, conv_b, rg_w_r, rg_b_r, rg_w_i, rg_b_i, rg_lambda, w_out, norm_ffn2, ffn2_gate, ffn2_up, ffn2_down, norm_final, loss_target, m_norm_ffn1, m_ffn1_gate, m_ffn1_up, m_ffn1_down, m_norm_mix, m_w_in, m_attn_sinks, m_conv_w, m_conv_b, m_rg_w_r, m_rg_b_r, m_rg_w_i, m_rg_b_i, m_rg_lambda, m_w_out, m_norm_ffn2, m_ffn2_gate, m_ffn2_up, m_ffn2_down, m_norm_final, v_norm_ffn1, v_ffn1_gate, v_ffn1_up, v_ffn1_down, v_norm_mix, v_w_in, v_attn_sinks, v_conv_w, v_conv_b, v_rg_w_r, v_rg_b_r, v_rg_w_i, v_rg_b_i, v_rg_lambda, v_w_out, v_norm_ffn2, v_ffn2_gate, v_ffn2_up, v_ffn2_down, v_norm_final):
    given = dict(locals())
    me = 4 * lax.axis_index("x") + 2 * lax.axis_index("y") + lax.axis_index("c")

    def by_width(names):
        classes = {}
        for n in names:
            classes.setdefault(given[n].shape[2], []).append(n)
        return list(classes.values())

    def pack(names, get, tag, pick=None):
        return [_pack_rows([get(n) for n in cls], name=f"pack{ci}_{tag}", pick=pick)
                for ci, cls in enumerate(by_width(names))]

    def unpack(names, lands, srcs, tag):
        out = {}
        for ci, cls in enumerate(by_width(names)):
            arrays = _unpack_rows(lands[ci], srcs[ci], [given[n].shape[1] for n in cls], name=f"unpack{ci}_{tag}")
            out.update(zip(cls, arrays))
        return out

    def gather_start(l, names, tag, after=None):
        return _exchange_start([p[0] for p in pack(names, lambda n: given[n], f"w{tag}_l{l}", pick=l)], True,
                               name=f"gather_start{tag}_l{l}", after=after)

    def gather_wait(l, names, tag, started, after):
        srcs, lands = _exchange_wait(started, after, True, name=f"gather_wait{tag}_l{l}")
        blocks = unpack(names, lands, srcs, f"w{tag}_l{l}")
        for n in names:
            if SHARD_AXIS[n] == 2:
                W[n][l] = _blocks_to_cols(blocks[n], name=f"cols_{n}_l{l}")
            else:
                W[n][l] = blocks[n].reshape(-1, blocks[n].shape[2])
        return lands[0]

    W = {n: [None] * DEPTH for n in BIG_NAMES}
    ffn1_names, later_names = SCATTER_STAGES[1], SCATTER_STAGES[0]
    first = gather_start(0, ffn1_names, "a")
    conv_full = _exchange([conv_w], True, name="gather_conv_w")[0]
    landed = gather_wait(0, ffn1_names, "a", first, conv_full)
    second = gather_start(0, later_names, "b", after=landed)
    started = second[4][0, 0]
    gathers = [None] * DEPTH

    def before_layer(l, x_in):
        if l > 0:
            gather_wait(l, BIG_NAMES, "", gathers[l], x_in)

    def before_mixer(l, x1):
        if l == 0:
            landed = gather_wait(0, later_names, "b", second, x1)
            token = 0.0
            for k in range(1, DEPTH):
                gathers[k] = gather_start(k, BIG_NAMES, "", after=landed)
                token = token + gathers[k][4][0, 0]
            W['norm_mix'][0] = W['norm_mix'][0] + token

    scatters = {}

    def on_grads(l, stage, G):
        def blocks_of(n):
            if SHARD_AXIS[n] == 2:
                return _cols_to_blocks(G[n], name=f"blocks_{n}_l{l}")
            return G[n].reshape(N_DEV, -1, G[n].shape[1])

        scatters[l, stage] = _exchange_start(pack(SCATTER_STAGES[stage], blocks_of, f"g{stage}_l{l}"), False,
                                             name=f"scatter_start{stage}_l{l}")
        token = scatters[l, stage][4][0, 0]
        if stage == 1 and l > 0:
            W['norm_ffn2'][l - 1] = W['norm_ffn2'][l - 1] + token
        return token

    W['conv_w'] = [_from_blocks(conv_full[:, l], 1) for l in range(DEPTH)]
    for n in ('norm_ffn1', 'norm_mix', 'norm_ffn2', 'conv_b', 'rg_lambda'):
        W[n] = [given[n][l][None, :] for l in range(DEPTH)]
    W['norm_final'] = norm_final[None, :]
    W['sinks'] = [jnp.repeat(attn_sinks[l], HEAD_DIM)[None, :] for l in range(DEPTH)]
    for n in ('rg_w_r', 'rg_w_i'):
        W[n] = [_block_diag(given[n][l]).astype(BF16) for l in range(DEPTH)]
    for n in ('rg_b_r', 'rg_b_i'):
        W[n] = [given[n][l].reshape(1, C_WIDTH) for l in range(DEPTH)]

    W['norm_ffn1'][0] = W['norm_ffn1'][0] + started

    loss_part, grad_x, grads, dg_final = _device_step(x[0], positions[0], loss_target[0], W, before_layer, on_grads,
                                                      before_mixer)
    loss = lax.psum(loss_part[0, 0], ("x", "y", "c"))

    small_shapes = [given[n].shape for n in SMALL_NAMES] + [(DEPTH, C_CONV, C_WIDTH)]
    small_grads = []
    for n in SMALL_NAMES:
        if n == 'norm_final':
            small_grads.append(dg_final.reshape(-1))
        else:
            small_grads.append(jnp.stack([grads[l][n].reshape(given[n].shape[1:]) for l in range(DEPTH)]))
    small_grads.append(jnp.stack([grads[l]['conv_w'] for l in range(DEPTH)]))
    small_parts = _exchange([_pack(small_grads, SMALL_TILE)], True, name="gather_small_grads")[0]

    out = {}
    for stage in (0, 1):
        parts = {}
        for l in reversed(range(DEPTH)):
            last = stage == 1 and l == 0
            srcs, lands = _exchange_wait(scatters[l, stage], out['w_in'][1] if last else grad_x, False,
                                         name=f"scatter_wait{stage}_l{l}")
            parts[l] = unpack(SCATTER_STAGES[stage], lands, srcs, f"g{stage}_l{l}")
        for n in SCATTER_STAGES[stage]:
            shape = given[n].shape
            two_d = (shape[0] * shape[1], shape[2])
            res = None
            for l in reversed(range(DEPTH)):
                res = _adamw(parts[l][n], given[n].reshape(two_d), given['m_' + n].reshape(two_d),
                             given['v_' + n].reshape(two_d), name=f"adamw_{n}_l{l}", part=l, prev=res)
            out[n] = [r.reshape(shape) for r in res]

    w_small = [given[n] for n in SMALL_NAMES]
    m_small = [given['m_' + n] for n in SMALL_NAMES]
    v_small = [given['v_' + n] for n in SMALL_NAMES]
    zeros_cw = jnp.zeros((DEPTH, C_CONV, C_WIDTH), F32)
    res = _adamw(small_parts, _pack(w_small + [zeros_cw], SMALL_TILE), _pack(m_small + [zeros_cw], SMALL_TILE),
                 _pack(v_small + [zeros_cw], SMALL_TILE), name="adamw_small", tr=SMALL_TILE)
    unpacked = [_unpack(r, small_shapes) for r in res]
    for i, n in enumerate(SMALL_NAMES):
        out[n] = [u[i] for u in unpacked]

    k = conv_w.shape[2]
    g_cw = lax.dynamic_slice_in_dim(unpacked[0][-1], me * k, k, axis=2)
    zero_parts = jnp.zeros((N_DEV - 1,) + (8, LANES), F32)
    res = _adamw(jnp.concatenate([_pack([g_cw])[None], zero_parts]), _pack([conv_w]), _pack([m_conv_w]),
                 _pack([v_conv_w]), name="adamw_conv_w", tr=8)
    out['conv_w'] = [_unpack(r, [conv_w.shape])[0] for r in res]

    outputs = [loss, grad_x[None]]
    for i in range(4):
        outputs += [out[n][i] for n in WEIGHT_NAMES]
    return tuple(outputs)
```

```python
import functools
import math

import jax
import jax.numpy as jnp
from jax import lax
from jax.experimental import pallas as pl
from jax.experimental.pallas import tpu as pltpu

F32 = jnp.float32
BF16 = jnp.bfloat16

N_DEV = 8
DEPTH = 4
HEAD_DIM = 64
LANES = 128
QBLK = 128
A_WIDTH = 256
A_KV_WIDTH = 128
B_WIDTH = 384
C_WIDTH = 384
C_BLOCKS = 6
C_CONV = 4
C_EXP = 8.0
MIX_WIDTH = A_WIDTH + B_WIDTH + C_WIDTH
IN_COLS = A_WIDTH + 2 * A_KV_WIDTH + 3 * B_WIDTH + 2 * C_WIDTH
A_MAX_DIST = 127
B_BRANCHES = ((128, 1), (512, 4), (2048, 16))
ROPE_THETA = 10000.0
EPS = 1e-6
SCALE = HEAD_DIM ** -0.5

ADAM_LR = 0.001
ADAM_B1 = 0.9
ADAM_B2 = 0.999
ADAM_EPS = 1e-08
ADAM_WD = 0.01
ADAM_STEP = 10

ATTN_CHUNK = 4096
ATTN_FWD_UNROLL = True
ATTN_BWD_UNROLL = True
VMEM_LIMIT = 56 * 1024 * 1024
SMALL_TILE = 256

NT_DIMS = (((1,), (1,)), ((), ()))
TN_DIMS = (((0,), (0,)), ((), ()))
NN_DIMS = (((1,), (0,)), ((), ()))

WEIGHT_NAMES = ['norm_ffn1', 'ffn1_gate', 'ffn1_up', 'ffn1_down', 'norm_mix', 'w_in', 'attn_sinks', 'conv_w',
                'conv_b', 'rg_w_r', 'rg_b_r', 'rg_w_i', 'rg_b_i', 'rg_lambda', 'w_out', 'norm_ffn2', 'ffn2_gate',
                'ffn2_up', 'ffn2_down', 'norm_final']
BIG_NAMES = ['ffn1_gate', 'ffn1_up', 'ffn1_down', 'w_in', 'w_out', 'ffn2_gate', 'ffn2_up', 'ffn2_down']
SCATTER_STAGES = (['ffn2_gate', 'ffn2_up', 'ffn2_down', 'w_out', 'w_in'], ['ffn1_gate', 'ffn1_up', 'ffn1_down'])
SMALL_NAMES = ['norm_ffn1', 'norm_mix', 'norm_ffn2', 'norm_final', 'attn_sinks', 'conv_b', 'rg_w_r', 'rg_b_r',
               'rg_w_i', 'rg_b_i', 'rg_lambda']


def _params(sem, vmem=VMEM_LIMIT):
    return pltpu.CompilerParams(dimension_semantics=sem, vmem_limit_bytes=vmem)


def _dot(a, b, dims=NN_DIMS):
    return lax.dot_general(a, b, dims, preferred_element_type=F32)


def _sigmoid(x):
    return 1.0 / (1.0 + jnp.exp(-x))


def _mm(a, b, mode, *, name, tm=512, tn=512, tk=512, out_dtype=F32, alpha=1.0, res=None):
    if mode == 'nn':
        (M, K), N = a.shape, b.shape[1]
    elif mode == 'nt':
        (M, K), N = a.shape, b.shape[0]
    else:
        (K, M), N = a.shape, b.shape[1]
    tm, tn, tk = min(tm, M), min(tn, N), min(tk, K)
    ni, nj, nk = M // tm, N // tn, K // tk
    assert ni * tm == M and nj * tn == N and nk * tk == K, (name, a.shape, b.shape, tm, tn, tk)
    if mode == 'tn':
        a_spec = pl.BlockSpec((tk, tm), lambda j, i, k: (k, i))
    else:
        a_spec = pl.BlockSpec((tm, tk), lambda j, i, k: (i, k))
    if mode == 'nt':
        b_spec = pl.BlockSpec((tn, tk), lambda j, i, k: (j, k))
    else:
        b_spec = pl.BlockSpec((tk, tn), lambda j, i, k: (k, j))
    dims = {'nn': NN_DIMS, 'nt': NT_DIMS, 'tn': TN_DIMS}[mode]
    o_spec = pl.BlockSpec((tm, tn), lambda j, i, k: (i, j))
    has_res = res is not None

    def body(*refs):
        if has_res:
            a_ref, b_ref, r_ref, o_ref = refs[:4]
        else:
            a_ref, b_ref, o_ref = refs[:3]
        part = _dot(a_ref[...].astype(BF16), b_ref[...].astype(BF16), dims)

        def finish(acc):
            out = acc * alpha if alpha != 1.0 else acc
            if has_res:
                out = r_ref[...] + out
            o_ref[...] = out.astype(out_dtype)

        if nk == 1:
            finish(part)
        else:
            acc_ref = refs[-1]
            k = pl.program_id(2)

            @pl.when(k == 0)
            def _():
                acc_ref[...] = part

            @pl.when(k > 0)
            def _():
                acc_ref[...] += part

            @pl.when(k == nk - 1)
            def _():
                finish(acc_ref[...])

    in_specs = [a_spec, b_spec] + ([o_spec] if has_res else [])
    operands = [a, b] + ([res] if has_res else [])
    return pl.pallas_call(
        body, name=name, grid=(nj, ni, nk), in_specs=in_specs, out_specs=o_spec,
        out_shape=jax.ShapeDtypeStruct((M, N), out_dtype),
        scratch_shapes=[pltpu.VMEM((tm, tn), F32)] if nk > 1 else [],
        compiler_params=_params(("parallel", "parallel", "arbitrary")),
    )(*operands)


def _rms_fwd(x, g, *, name, tm=512):
    T, D = x.shape
    tm = min(tm, T)

    def body(x_ref, g_ref, o_ref):
        xv = x_ref[...]
        rstd = lax.rsqrt(jnp.mean(xv * xv, axis=-1, keepdims=True) + EPS)
        o_ref[...] = (xv * rstd * g_ref[...]).astype(BF16)

    return pl.pallas_call(
        body, name=name, grid=(T // tm,),
        in_specs=[pl.BlockSpec((tm, D), lambda i: (i, 0)), pl.BlockSpec((1, D), lambda i: (0, 0))],
        out_specs=pl.BlockSpec((tm, D), lambda i: (i, 0)),
        out_shape=jax.ShapeDtypeStruct((T, D), BF16),
        compiler_params=_params(("parallel",)),
    )(x, g)


def _rms_bwd_math(xv, g, dn):
    rstd = lax.rsqrt(jnp.mean(xv * xv, axis=-1, keepdims=True) + EPS)
    xhat = xv * rstd
    dxhat = dn * g
    dx = rstd * (dxhat - xhat * jnp.mean(dxhat * xhat, axis=-1, keepdims=True))
    return dx, jnp.sum(dn * xhat, axis=0, keepdims=True)


def _rms_bwd(x, g, dn, dres, *, name, tm=512):
    T, D = x.shape
    tm = min(tm, T)

    def body(x_ref, g_ref, dn_ref, dres_ref, dx_ref, dg_ref):
        dx, dg = _rms_bwd_math(x_ref[...], g_ref[...], dn_ref[...])
        dx_ref[...] = dres_ref[...] + dx

        @pl.when(pl.program_id(0) == 0)
        def _():
            dg_ref[...] = jnp.zeros_like(dg_ref)

        dg_ref[...] += dg

    row = pl.BlockSpec((tm, D), lambda i: (i, 0))
    vec = pl.BlockSpec((1, D), lambda i: (0, 0))
    return pl.pallas_call(
        body, name=name, grid=(T // tm,),
        in_specs=[row, vec, row, row], out_specs=[row, vec],
        out_shape=[jax.ShapeDtypeStruct((T, D), F32), jax.ShapeDtypeStruct((1, D), F32)],
        compiler_params=_params(("arbitrary",)),
    )(x, g, dn, dres)


def _loss_head(x, g, target, *, name, tm=512):
    T, D = x.shape
    tm = min(tm, T)

    def body(x_ref, g_ref, t_ref, loss_ref, dx_ref, dg_ref):
        xv = x_ref[...]
        g = g_ref[...]
        rstd = lax.rsqrt(jnp.mean(xv * xv, axis=-1, keepdims=True) + EPS)
        y = xv * rstd * g
        err = y - t_ref[...]
        part = 0.5 * jnp.sum(jnp.mean(err * err, axis=-1, keepdims=True), axis=0, keepdims=True)
        dx, dg = _rms_bwd_math(xv, g, err * (1.0 / D))
        dx_ref[...] = dx

        @pl.when(pl.program_id(0) == 0)
        def _():
            dg_ref[...] = jnp.zeros_like(dg_ref)
            loss_ref[...] = jnp.zeros_like(loss_ref)

        dg_ref[...] += dg
        loss_ref[...] += jnp.broadcast_to(part, loss_ref.shape)

    row = pl.BlockSpec((tm, D), lambda i: (i, 0))
    vec = pl.BlockSpec((1, D), lambda i: (0, 0))
    lspec = pl.BlockSpec((1, LANES), lambda i: (0, 0))
    return pl.pallas_call(
        body, name=name, grid=(T // tm,),
        in_specs=[row, vec, row], out_specs=[lspec, row, vec],
        out_shape=[jax.ShapeDtypeStruct((1, LANES), F32), jax.ShapeDtypeStruct((T, D), F32),
                   jax.ShapeDtypeStruct((1, D), F32)],
        compiler_params=_params(("arbitrary",)),
    )(x, g, target)


def _resident(shape):
    return pl.BlockSpec(shape, lambda i: (0,) * len(shape), pipeline_mode=pl.Buffered(1))


def _ffn_chunk(F):
    for c in (1408, 1024, 512, 256, 128):
        if F % c == 0:
            return c
    return F


def _ffn_fwd(x, g, wg, wu, wd, *, name, tm=256):
    T, D = x.shape
    F = wg.shape[1]
    tm = min(tm, T)
    fc = 256 if F % 256 == 0 else _ffn_chunk(F)

    def body(x_ref, g_ref, wg_ref, wu_ref, wd_ref, o_ref, a_ref, u_ref):
        xv = x_ref[...]
        rstd = lax.rsqrt(jnp.mean(xv * xv, axis=-1, keepdims=True) + EPS)
        n = (xv * rstd * g_ref[...]).astype(BF16)
        acc = jnp.zeros((tm, D), F32)
        for c in range(F // fc):
            sl = slice(c * fc, (c + 1) * fc)
            a = _dot(n, wg_ref[:, sl])
            u = _dot(n, wu_ref[:, sl])
            a_ref[:, sl] = a.astype(BF16)
            u_ref[:, sl] = u.astype(BF16)
            act = (a * _sigmoid(a) * u).astype(BF16)
            acc = acc + _dot(act, wd_ref[sl, :])
        o_ref[...] = xv + 0.5 * acc

    row = pl.BlockSpec((tm, D), lambda i: (i, 0))
    hid = pl.BlockSpec((tm, F), lambda i: (i, 0))
    return pl.pallas_call(
        body, name=name, grid=(T // tm,),
        in_specs=[row, pl.BlockSpec((1, D), lambda i: (0, 0)),
                  _resident((D, F)), _resident((D, F)), _resident((F, D))],
        out_specs=[row, hid, hid],
        out_shape=[jax.ShapeDtypeStruct((T, D), F32), jax.ShapeDtypeStruct((T, F), BF16),
                   jax.ShapeDtypeStruct((T, F), BF16)],
        compiler_params=_params(("parallel",)),
    )(x, g, wg, wu, wd)


def _ffn_bwd(x, g, dy, a, u, wg, wu, wd, *, name, tm=256):
    T, D = x.shape
    F = wg.shape[1]
    tm = min(tm, T)
    fc = _ffn_chunk(F)

    def body(x_ref, g_ref, dy_ref, a_ref, u_ref, wg_ref, wu_ref, wd_ref,
             dx_ref, dg_ref, n_ref, act_ref, da_ref, du_ref):
        xv = x_ref[...]
        g = g_ref[...]
        rstd = lax.rsqrt(jnp.mean(xv * xv, axis=-1, keepdims=True) + EPS)
        n_ref[...] = (xv * rstd * g).astype(BF16)
        dy = dy_ref[...]
        dyh = (0.5 * dy).astype(BF16)
        dn = jnp.zeros((tm, D), F32)
        for c in range(F // fc):
            sl = slice(c * fc, (c + 1) * fc)
            av = a_ref[:, sl].astype(F32)
            uv = u_ref[:, sl].astype(F32)
            dact = _dot(dyh, wd_ref[sl, :], NT_DIMS)
            s = _sigmoid(av)
            silu = av * s
            act_ref[:, sl] = (silu * uv).astype(BF16)
            da = (dact * uv * (s * (1.0 + av * (1.0 - s)))).astype(BF16)
            du = (dact * silu).astype(BF16)
            da_ref[:, sl] = da
            du_ref[:, sl] = du
            dn = dn + _dot(da, wg_ref[:, sl], NT_DIMS) + _dot(du, wu_ref[:, sl], NT_DIMS)
        dx, dg = _rms_bwd_math(xv, g, dn)
        dx_ref[...] = dy + dx

        @pl.when(pl.program_id(0) == 0)
        def _():
            dg_ref[...] = jnp.zeros_like(dg_ref)

        dg_ref[...] += dg

    row = pl.BlockSpec((tm, D), lambda i: (i, 0))
    hid = pl.BlockSpec((tm, F), lambda i: (i, 0))
    vec = pl.BlockSpec((1, D), lambda i: (0, 0))
    return pl.pallas_call(
        body, name=name, grid=(T // tm,),
        in_specs=[row, vec, row, hid, hid,
                  _resident((D, F)), _resident((D, F)), _resident((F, D))],
        out_specs=[row, vec, row, hid, hid, hid],
        out_shape=[jax.ShapeDtypeStruct((T, D), F32), jax.ShapeDtypeStruct((1, D), F32),
                   jax.ShapeDtypeStruct((T, D), BF16), jax.ShapeDtypeStruct((T, F), BF16),
                   jax.ShapeDtypeStruct((T, F), BF16), jax.ShapeDtypeStruct((T, F), BF16)],
        compiler_params=_params(("arbitrary",)),
    )(x, g, dy, a, u, wg, wu, wd)


def _lane_iota(shape):
    return lax.broadcasted_iota(jnp.int32, shape, 1)


def _rope_partner(x):
    first_half = (_lane_iota(x.shape) & (HEAD_DIM - 1)) < HEAD_DIM // 2
    return jnp.where(first_half, pltpu.roll(x, LANES - HEAD_DIM // 2, 1), pltpu.roll(x, HEAD_DIM // 2, 1))


def _swap_heads(x):
    return pltpu.roll(x, HEAD_DIM, 1)


def _undilate(blk_ref, d, stage):
    if d == 1:
        return blk_ref[...]
    n, width = blk_ref.shape
    W = width // d
    for r in range(d):
        for g in range(W // LANES):
            stage.at[g][pl.ds(r, n, stride=d), :] = blk_ref[:, r * W + g * LANES:r * W + (g + 1) * LANES]
    return jnp.concatenate([stage.at[g][...] for g in range(W // LANES)], axis=1)


def _dilate_into(out_ref, value, d, stage):
    if d == 1:
        out_ref[...] = value.astype(out_ref.dtype)
        return
    n = value.shape[0] // d
    W = value.shape[1]
    for g in range(W // LANES):
        stage.at[g][...] = value[:, g * LANES:(g + 1) * LANES]
    for r in range(d):
        for g in range(W // LANES):
            out_ref[:, r * W + g * LANES:r * W + (g + 1) * LANES] = (
                stage.at[g][pl.ds(r, n, stride=d), :].astype(out_ref.dtype))


def _dilated_spec(tm, d, W):
    return pl.BlockSpec((tm // d, d * W), lambda i: (i, 0))


def _stage(tm, W):
    return pltpu.VMEM((W // LANES, tm, LANES), F32)


DILATIONS = tuple(d for _, d in B_BRANCHES)


def _split_rope(proj, cos_t, sin_t, *, name, tm=512):
    T = proj.shape[0]
    tm = min(tm, T)
    nd = len(DILATIONS)

    def body(p_ref, c_ref, s_ref, qa_ref, ka_ref, va_ref, *rest):
        b_refs = rest[:3 * nd]
        xc_ref, gc_ref, stage = rest[3 * nd:]
        cos = c_ref[...]
        sin = s_ref[...]

        def rope(x):
            return x * cos + _rope_partner(x) * sin

        lo = _lane_iota((tm, LANES)) < HEAD_DIM
        col = 0
        for j in range(A_WIDTH // LANES):
            qa_ref[:, j * LANES:(j + 1) * LANES] = (rope(p_ref[:, col:col + LANES]) * SCALE).astype(BF16)
            col += LANES
        kr = rope(p_ref[:, col:col + LANES])
        col += LANES
        vr = p_ref[:, col:col + LANES]
        col += LANES
        for src, dst in ((kr, ka_ref), (vr, va_ref)):
            sw = _swap_heads(src)
            dst[:, 0:LANES] = jnp.where(lo, src, sw).astype(BF16)
            dst[:, LANES:2 * LANES] = jnp.where(lo, sw, src).astype(BF16)
        for which, (roped, scale) in enumerate(((True, SCALE), (True, 1.0), (False, 1.0))):
            parts = []
            for j in range(B_WIDTH // LANES):
                v = p_ref[:, col:col + LANES]
                parts.append(rope(v) * scale if roped else v)
                col += LANES
            value = jnp.concatenate(parts, axis=1)
            for di, d in enumerate(DILATIONS):
                _dilate_into(b_refs[which * nd + di], value, d, stage)
        xc_ref[...] = p_ref[:, col:col + C_WIDTH]
        gc_ref[...] = p_ref[:, col + C_WIDTH:col + 2 * C_WIDTH]

    def row(w):
        return pl.BlockSpec((tm, w), lambda i: (i, 0))

    out_specs = [row(A_WIDTH)] * 3 + [_dilated_spec(tm, d, B_WIDTH) for _ in range(3) for d in DILATIONS]
    out_specs += [row(C_WIDTH)] * 2
    out_shape = [jax.ShapeDtypeStruct((T, A_WIDTH), BF16)] * 3
    out_shape += [jax.ShapeDtypeStruct((T // d, d * B_WIDTH), BF16) for _ in range(3) for d in DILATIONS]
    out_shape += [jax.ShapeDtypeStruct((T, C_WIDTH), F32)] * 2
    res = pl.pallas_call(
        body, name=name, grid=(T // tm,),
        in_specs=[row(IN_COLS), row(LANES), row(LANES)], out_specs=out_specs, out_shape=out_shape,
        scratch_shapes=[_stage(tm, B_WIDTH)],
        compiler_params=_params(("parallel",)),
    )(proj, cos_t, sin_t)
    qa, ka2, va2 = res[:3]
    qb, kb, vb = (list(res[3 + i * nd:3 + (i + 1) * nd]) for i in range(3))
    return qa, ka2, va2, qb, kb, vb, res[-2], res[-1]


def _merge_dproj(dqa, dka2, dva2, dqb, dkb, dvb, dxc, dgc, cos_t, sin_t, *, name, tm=512):
    T = dqa.shape[0]
    tm = min(tm, T)
    nb = len(dqb)

    def body(*refs):
        dqa_ref, dka_ref, dva_ref = refs[:3]
        dqb_refs = refs[3:3 + nb]
        dkb_refs = refs[3 + nb:3 + 2 * nb]
        dvb_refs = refs[3 + 2 * nb:3 + 3 * nb]
        dxc_ref, dgc_ref, c_ref, s_ref, o_ref, stage = refs[3 + 3 * nb:]
        cos = c_ref[...]
        sin = s_ref[...]

        def rope_t(dy):
            return dy * cos - _rope_partner(dy) * sin

        lo = _lane_iota((tm, LANES)) < HEAD_DIM
        col = 0
        for j in range(A_WIDTH // LANES):
            o_ref[:, col:col + LANES] = (rope_t(dqa_ref[:, j * LANES:(j + 1) * LANES]) * SCALE).astype(BF16)
            col += LANES
        for src, roped in ((dka_ref, True), (dva_ref, False)):
            b0 = src[:, 0:LANES]
            b1 = src[:, LANES:2 * LANES]
            v = jnp.where(lo, b0 + _swap_heads(b0), b1 + _swap_heads(b1))
            if roped:
                v = rope_t(v)
            o_ref[:, col:col + LANES] = v.astype(BF16)
            col += LANES
        for group, roped, scale in ((dqb_refs, True, SCALE), (dkb_refs, True, 1.0), (dvb_refs, False, 1.0)):
            total = _undilate(group[0], DILATIONS[0], stage)
            for r, d in zip(group[1:], DILATIONS[1:]):
                total = total + _undilate(r, d, stage)
            for j in range(B_WIDTH // LANES):
                v = total[:, j * LANES:(j + 1) * LANES]
                if roped:
                    v = rope_t(v) * scale
                o_ref[:, col:col + LANES] = v.astype(BF16)
                col += LANES
        o_ref[:, col:col + C_WIDTH] = dxc_ref[...].astype(BF16)
        o_ref[:, col + C_WIDTH:col + 2 * C_WIDTH] = dgc_ref[...].astype(BF16)

    def row(w):
        return pl.BlockSpec((tm, w), lambda i: (i, 0))

    ins = [dqa, dka2, dva2, *dqb, *dkb, *dvb, dxc, dgc, cos_t, sin_t]
    in_specs = [row(A_WIDTH)] * 3 + [_dilated_spec(tm, d, B_WIDTH) for _ in range(3) for d in DILATIONS]
    in_specs += [row(C_WIDTH)] * 2 + [row(LANES)] * 2
    return pl.pallas_call(
        body, name=name, grid=(T // tm,), in_specs=in_specs,
        out_specs=row(IN_COLS),
        out_shape=jax.ShapeDtypeStruct((T, IN_COLS), BF16),
        scratch_shapes=[_stage(tm, B_WIDTH)],
        compiler_params=_params(("parallel",)),
    )(*ins)


def _band_masks(max_dist):
    row = lax.broadcasted_iota(jnp.int32, (QBLK, 2 * QBLK), 0)
    key = lax.broadcasted_iota(jnp.int32, (QBLK, 2 * QBLK), 1)
    dist = row + QBLK - key
    wide = jnp.logical_and(dist >= 0, dist <= max_dist)
    return wide, wide[:, :QBLK], key >= QBLK


def _head_masks(rows=QBLK):
    lo = _lane_iota((rows, LANES)) < HEAD_DIM
    return lo, jnp.logical_not(lo)


def _keep(hm, x):
    return x * jnp.where(hm, 1.0, 0.0).astype(x.dtype)


def _head_col(x, hm):
    return jnp.max(jnp.where(hm, x, -jnp.inf), axis=1, keepdims=True)


def _attn_specs(R, C):
    chunk = min(ATTN_CHUNK, R)
    nb = chunk // QBLK
    nch = R // chunk
    main = pl.BlockSpec((chunk, LANES), lambda j, c: (c, j))
    prev = pl.BlockSpec((QBLK, LANES), lambda j, c: (jnp.maximum(c * nb - 1, 0), j))
    nxt = pl.BlockSpec((QBLK, LANES), lambda j, c: (jnp.minimum((c + 1) * nb, R // QBLK - 1), j))
    return chunk, nb, nch, main, prev, nxt


def _attn_fwd(q, k, v, max_dist, *, name):
    R, C = q.shape
    chunk, nb, nch, main, prev, _ = _attn_specs(R, C)

    def body(q_ref, k_ref, v_ref, kp_ref, vp_ref, o_ref, lse_ref):
        c = pl.program_id(1)
        wide_mask, _, own_block = _band_masks(max_dist)
        heads = _head_masks()

        def block(q_blk, kk, vv, mask):
            q2 = jnp.concatenate([_keep(hm, q_blk) for hm in heads], axis=0)
            s = jnp.where(jnp.concatenate([mask, mask], axis=0), _dot(q2, kk, NT_DIMS), -jnp.inf)
            m = jnp.max(jnp.maximum(s[:, :QBLK], s[:, QBLK:]), axis=1, keepdims=True)
            p = jnp.exp(s - m)
            l = jnp.sum(p[:, :QBLK] + p[:, QBLK:], axis=1, keepdims=True)
            o2 = _dot(p.astype(BF16), vv) / l
            lse2 = jnp.broadcast_to(m + jnp.log(l), (2 * QBLK, LANES))
            return (jnp.where(heads[0], o2[:QBLK], o2[QBLK:]), jnp.where(heads[0], lse2[:QBLK], lse2[QBLK:]))

        first = pl.ds(0, QBLK)
        o0, l0 = block(q_ref[first, :], jnp.concatenate([kp_ref[...], k_ref[first, :]], axis=0),
                       jnp.concatenate([vp_ref[...], v_ref[first, :]], axis=0),
                       jnp.logical_and(wide_mask, jnp.logical_or(own_block, c > 0)))
        o_ref[first, :] = o0
        lse_ref[first, :] = l0

        def loop(qb, carry):
            cur = pl.ds(pl.multiple_of(qb * QBLK, QBLK), QBLK)
            both = pl.ds(pl.multiple_of((qb - 1) * QBLK, QBLK), 2 * QBLK)
            o, l = block(q_ref[cur, :], k_ref[both, :], v_ref[both, :], wide_mask)
            o_ref[cur, :] = o
            lse_ref[cur, :] = l
            return carry

        if nb > 1:
            lax.fori_loop(1, nb, loop, 0, unroll=ATTN_FWD_UNROLL)

    return pl.pallas_call(
        body, name=name, grid=(C // LANES, nch),
        in_specs=[main, main, main, prev, prev], out_specs=[main, main],
        out_shape=[jax.ShapeDtypeStruct((R, C), F32), jax.ShapeDtypeStruct((R, C), F32)],
        compiler_params=_params(("parallel", "parallel")),
    )(q, k, v, k, v)


def _attn_bwd(q, k, v, do, lse, delta, max_dist, *, name):
    R, C = q.shape
    chunk, nb, nch, main, prev, nxt = _attn_specs(R, C)

    def body(q_ref, k_ref, v_ref, do_ref, lse_ref, dl_ref, kp_ref, vp_ref, qn_ref, don_ref, lsen_ref, dln_ref,
             dq_ref, dk_ref, dv_ref):
        c = pl.program_id(1)
        wide_mask, prev_mask, own_block = _band_masks(max_dist)
        heads = _head_masks()

        def pair(q_blk, do_blk, lse_blk, dl_blk, kk, vv, mask, want_dq=True):
            q2 = jnp.concatenate([_keep(hm, q_blk) for hm in heads], axis=0)
            do2 = jnp.concatenate([_keep(hm, do_blk) for hm in heads], axis=0)
            lse2 = jnp.concatenate([_head_col(lse_blk, hm) for hm in heads], axis=0)
            dl2 = jnp.concatenate([_head_col(dl_blk, hm) for hm in heads], axis=0)
            p = jnp.where(jnp.concatenate([mask, mask], axis=0), jnp.exp(_dot(q2, kk, NT_DIMS) - lse2), 0.0)
            ds = (p * (_dot(do2, vv, NT_DIMS) - dl2)).astype(BF16)
            dq = None
            if want_dq:
                k2 = jnp.concatenate([_keep(khm, kk) for khm in _head_masks(kk.shape[0])], axis=0)
                dq = _dot(jnp.concatenate([ds[:QBLK], ds[QBLK:]], axis=1), k2)
            return dq, _dot(ds, q2, TN_DIMS), _dot(p.astype(BF16), do2, TN_DIMS)

        dk_ref[...] = jnp.zeros_like(dk_ref)
        dv_ref[...] = jnp.zeros_like(dv_ref)

        first = pl.ds(0, QBLK)
        dq0, dkk0, dvv0 = pair(q_ref[first, :], do_ref[first, :], lse_ref[first, :], dl_ref[first, :],
                               jnp.concatenate([kp_ref[...], k_ref[first, :]], axis=0),
                               jnp.concatenate([vp_ref[...], v_ref[first, :]], axis=0),
                               jnp.logical_and(wide_mask, jnp.logical_or(own_block, c > 0)))
        dq_ref[first, :] = dq0
        dk_ref[first, :] += dkk0[QBLK:, :]
        dv_ref[first, :] += dvv0[QBLK:, :]

        def loop(qb, carry):
            cur = pl.ds(pl.multiple_of(qb * QBLK, QBLK), QBLK)
            both = pl.ds(pl.multiple_of((qb - 1) * QBLK, QBLK), 2 * QBLK)
            dq, dkk, dvv = pair(q_ref[cur, :], do_ref[cur, :], lse_ref[cur, :], dl_ref[cur, :],
                                k_ref[both, :], v_ref[both, :], wide_mask)
            dq_ref[cur, :] = dq
            dk_ref[both, :] += dkk
            dv_ref[both, :] += dvv
            return carry

        if nb > 1:
            lax.fori_loop(1, nb, loop, 0, unroll=ATTN_BWD_UNROLL)

        last = pl.ds((nb - 1) * QBLK, QBLK)
        _, dk_n, dv_n = pair(qn_ref[...], don_ref[...], lsen_ref[...], dln_ref[...], k_ref[last, :], v_ref[last, :],
                             jnp.logical_and(prev_mask, c < nch - 1), want_dq=False)
        dk_ref[last, :] += dk_n
        dv_ref[last, :] += dv_n

    return pl.pallas_call(
        body, name=name, grid=(C // LANES, nch),
        in_specs=[main] * 6 + [prev, prev] + [nxt] * 4, out_specs=[main, main, main],
        out_shape=[jax.ShapeDtypeStruct((R, C), F32)] * 3,
        compiler_params=_params(("parallel", "parallel")),
    )(q, k, v, do, lse, delta, k, v, q, do, lse, delta)


def _head_sum(x):
    r = lax.broadcasted_iota(jnp.int32, (LANES, LANES), 0) // HEAD_DIM
    c = lax.broadcasted_iota(jnp.int32, (LANES, LANES), 1) // HEAD_DIM
    ones = jnp.where(r == c, 1.0, 0.0).astype(BF16)
    outs = []
    for j in range(x.shape[1] // LANES):
        rem = x[:, j * LANES:(j + 1) * LANES]
        acc = jnp.zeros(rem.shape, F32)
        for _ in range(3):
            part = rem.astype(BF16)
            acc = acc + _dot(part, ones)
            rem = rem - part.astype(F32)
        outs.append(acc)
    return outs[0] if len(outs) == 1 else jnp.concatenate(outs, axis=1)


def _branch_weights(lses):
    m = functools.reduce(jnp.maximum, lses)
    es = [jnp.exp(l - m) for l in lses]
    den = functools.reduce(lambda a, b: a + b, es)
    return [e / den for e in es]


def _combine_fwd(oa, lsea, sink, obs, lsebs, oc, *, name, tm=512):
    T = oa.shape[0]
    tm = min(tm, T)
    nb = len(obs)

    def body(*refs):
        oa_ref, lsea_ref, sink_ref = refs[:3]
        ob_refs = refs[3:3 + nb]
        lse_refs = refs[3 + nb:3 + 2 * nb]
        oc_ref, out_ref, stage = refs[3 + 2 * nb:]
        out_ref[:, 0:A_WIDTH] = (oa_ref[...] * _sigmoid(lsea_ref[...] - sink_ref[...])).astype(BF16)
        ws = _branch_weights([_undilate(r, d, stage) for r, d in zip(lse_refs, DILATIONS)])
        ob = _undilate(ob_refs[0], DILATIONS[0], stage) * ws[0]
        for r, d, w in zip(ob_refs[1:], DILATIONS[1:], ws[1:]):
            ob = ob + _undilate(r, d, stage) * w
        out_ref[:, A_WIDTH:A_WIDTH + B_WIDTH] = ob.astype(BF16)
        out_ref[:, A_WIDTH + B_WIDTH:MIX_WIDTH] = oc_ref[...].astype(BF16)

    def row(w):
        return pl.BlockSpec((tm, w), lambda i: (i, 0))

    ins = [oa, lsea, sink, *obs, *lsebs, oc]
    in_specs = [row(A_WIDTH), row(A_WIDTH), pl.BlockSpec((1, A_WIDTH), lambda i: (0, 0))]
    in_specs += [_dilated_spec(tm, d, B_WIDTH) for _ in range(2) for d in DILATIONS] + [row(C_WIDTH)]
    return pl.pallas_call(
        body, name=name, grid=(T // tm,), in_specs=in_specs, out_specs=row(MIX_WIDTH),
        out_shape=jax.ShapeDtypeStruct((T, MIX_WIDTH), BF16),
        scratch_shapes=[_stage(tm, B_WIDTH)],
        compiler_params=_params(("parallel",)),
    )(*ins)


def _combine_bwd(dmix, oa, lsea, sink, obs, lsebs, *, name, tm=512):
    T = oa.shape[0]
    tm = min(tm, T)
    nb = len(obs)

    def body(*refs):
        dmix_ref, oa_ref, lsea_ref, sink_ref = refs[:4]
        ob_refs = refs[4:4 + nb]
        lse_refs = refs[4 + nb:4 + 2 * nb]
        outs = refs[4 + 2 * nb:-1]
        stage = refs[-1]
        doa_ref, dla_ref = outs[:2]
        dob_refs = outs[2:2 + nb]
        dlb_refs = outs[2 + nb:2 + 2 * nb]
        doc_ref, dsink_ref = outs[2 + 2 * nb:]

        d_a = dmix_ref[:, 0:A_WIDTH]
        d_b = dmix_ref[:, A_WIDTH:A_WIDTH + B_WIDTH]
        doc_ref[...] = dmix_ref[:, A_WIDTH + B_WIDTH:MIX_WIDTH]

        gate = _sigmoid(lsea_ref[...] - sink_ref[...])
        doa_ref[...] = (d_a * gate).astype(BF16)
        dgate = _head_sum(d_a * oa_ref[...])
        dlse = dgate * gate * (1.0 - gate)
        dla_ref[...] = dgate * gate - dlse

        @pl.when(pl.program_id(0) == 0)
        def _():
            dsink_ref[...] = jnp.zeros_like(dsink_ref)

        dsink_ref[...] -= jnp.sum(dlse, axis=0, keepdims=True)

        ws = _branch_weights([_undilate(r, d, stage) for r, d in zip(lse_refs, DILATIONS)])
        dws = [_head_sum(d_b * _undilate(r, d, stage)) for r, d in zip(ob_refs, DILATIONS)]
        sw = ws[0] * dws[0]
        for w, dw in zip(ws[1:], dws[1:]):
            sw = sw + w * dw
        for w, d, do_ref, dl_ref in zip(ws, DILATIONS, dob_refs, dlb_refs):
            _dilate_into(do_ref, w * d_b, d, stage)
            _dilate_into(dl_ref, w * sw, d, stage)

    def row(w):
        return pl.BlockSpec((tm, w), lambda i: (i, 0))

    vec = pl.BlockSpec((1, A_WIDTH), lambda i: (0, 0))
    dil = [_dilated_spec(tm, d, B_WIDTH) for _ in range(2) for d in DILATIONS]
    ins = [dmix, oa, lsea, sink, *obs, *lsebs]
    in_specs = [row(MIX_WIDTH), row(A_WIDTH), row(A_WIDTH), vec] + dil
    out_specs = [row(A_WIDTH), row(A_WIDTH)] + dil + [row(C_WIDTH), vec]
    out_shape = [jax.ShapeDtypeStruct((T, A_WIDTH), BF16), jax.ShapeDtypeStruct((T, A_WIDTH), F32)]
    out_shape += [jax.ShapeDtypeStruct((T // d, d * B_WIDTH), BF16) for d in DILATIONS]
    out_shape += [jax.ShapeDtypeStruct((T // d, d * B_WIDTH), F32) for d in DILATIONS]
    out_shape += [jax.ShapeDtypeStruct((T, C_WIDTH), F32), jax.ShapeDtypeStruct((1, A_WIDTH), F32)]
    res = pl.pallas_call(
        body, name=name, grid=(T // tm,), in_specs=in_specs, out_specs=out_specs, out_shape=out_shape,
        scratch_shapes=[_stage(tm, B_WIDTH)],
        compiler_params=_params(("arbitrary",)),
    )(*ins)
    return res[0], res[1], list(res[2:2 + nb]), list(res[2 + nb:2 + 2 * nb]), res[2 + 2 * nb], res[3 + 2 * nb]


HIST = 8


def _softplus_neg(lam):
    e = jnp.exp(-jnp.abs(lam))
    log1p = jnp.where(e < 0.01, e * (1.0 - e * (0.5 - e * (1.0 / 3.0))), jnp.log(1.0 + e))
    return jnp.maximum(-lam, 0.0) + log1p


def _neg_expm1(x):
    series = -x * (1.0 + x * (0.5 + x * (1.0 / 6.0 + x * (1.0 / 24.0 + x * (1.0 / 120.0)))))
    return jnp.where(x > -0.1, series, 1.0 - jnp.exp(x))


def _gelu_parts(x):
    k = math.sqrt(2.0 / math.pi)
    t = jnp.tanh(k * (x + 0.044715 * (x * x * x)))
    cdf = 0.5 * (1.0 + t)
    return x * cdf, cdf + 0.5 * x * (1.0 - t * t) * (k * (1.0 + 3.0 * 0.044715 * (x * x)))


def _rglru_gates(y, pos_ref, wr_ref, br_ref, wi_ref, bi_ref, lam_ref):
    yb = y.astype(BF16)
    r = _sigmoid(_dot(yb, wr_ref[...]) + br_ref[...])
    ig = _sigmoid(_dot(yb, wi_ref[...]) + bi_ref[...])
    sp = _softplus_neg(lam_ref[...])
    log_a = -C_EXP * r * sp
    reset = pos_ref[...] == 0
    a = jnp.where(reset, 0.0, jnp.exp(log_a))
    mult = jnp.where(reset, 1.0, jnp.sqrt(_neg_expm1(2.0 * log_a)))
    return yb, r, ig, sp, reset, a, mult


def _conv_fwd(xs_ref, cw_ref, cb_ref, tm):
    y = cb_ref[...] + cw_ref[0:1, :] * xs_ref[HIST:HIST + tm, :]
    for j in range(1, C_CONV):
        y = y + cw_ref[j:j + 1, :] * xs_ref[HIST - j:HIST - j + tm, :]
    return y


SCAN_GROUP = 8


def _blocked_scan(c, d, c_s, d_s, grp_a, grp_h, carry, reverse):
    tm, W = d.shape
    groups = tm // SCAN_GROUP
    order = range(SCAN_GROUP - 1, -1, -1) if reverse else range(SCAN_GROUP)
    outs, lasts = [], []
    for k in range(W // LANES):
        lanes = slice(k * LANES, (k + 1) * LANES)
        ck, dk, ga, gh = c_s.at[k], d_s.at[k], grp_a.at[k], grp_h.at[k]
        ck[...] = c[:, lanes]
        dk[...] = d[:, lanes]
        prod = state = None
        for j in order:
            rows = pl.ds(j, groups, stride=SCAN_GROUP)
            cj, dj = ck[rows, :], dk[rows, :]
            if prod is None:
                prod, state = cj, dj
            else:
                state = cj * state + dj
                prod = cj * prod
            ck[rows, :] = prod
            dk[rows, :] = state
        ga[...] = prod
        gh[...] = state

        def step(i, h, ga=ga, gh=gh):
            row = pl.ds(groups - 1 - i if reverse else i, 1)
            a, t = ga[row, :], gh[row, :]
            ga[row, :] = h
            return a * h + t

        lasts.append(lax.fori_loop(0, groups, step, carry[:, lanes], unroll=8))
        entering = ga[...]
        for j in range(SCAN_GROUP):
            rows = pl.ds(j, groups, stride=SCAN_GROUP)
            dk[rows, :] = dk[rows, :] + ck[rows, :] * entering
        outs.append(dk[...])
    return jnp.concatenate(outs, axis=1), jnp.concatenate(lasts, axis=1)


def _rglru_fwd(xc, gc, pos, cw, cb, wr, br, wi, bi, lam, *, name, tm=512):
    T, W = xc.shape
    tm = min(tm, T)

    def body(xc_ref, gc_ref, pos_ref, cw_ref, cb_ref, wr_ref, br_ref, wi_ref, bi_ref, lam_ref,
             out_ref, hs_ref, xs, a_s, b_s, h_s, grp_a, grp_h):
        @pl.when(pl.program_id(0) == 0)
        def _():
            xs[0:HIST, :] = jnp.zeros((HIST, W), F32)
            h_s[...] = jnp.zeros_like(h_s)

        xv = xc_ref[...]
        xs[HIST:HIST + tm, :] = xv
        y = _conv_fwd(xs, cw_ref, cb_ref, tm)
        xs[0:HIST, :] = xv[tm - HIST:tm, :]
        _, _, ig, _, _, a, mult = _rglru_gates(y, pos_ref, wr_ref, br_ref, wi_ref, bi_ref, lam_ref)
        hs, h_s[...] = _blocked_scan(a, mult * (ig * y), a_s, b_s, grp_a, grp_h, h_s[...], reverse=False)
        hs_ref[...] = hs
        out_ref[...] = hs * _gelu_parts(gc_ref[...])[0]

    row = pl.BlockSpec((tm, W), lambda i: (i, 0))
    full = lambda shape: pl.BlockSpec(shape, lambda i: (0,) * len(shape))
    return pl.pallas_call(
        body, name=name, grid=(T // tm,),
        in_specs=[row, row, pl.BlockSpec((tm, 1), lambda i: (i, 0)), full((C_CONV, W)), full((1, W)),
                  full((W, W)), full((1, W)), full((W, W)), full((1, W)), full((1, W))],
        out_specs=[row, row],
        out_shape=[jax.ShapeDtypeStruct((T, W), F32)] * 2,
        scratch_shapes=[pltpu.VMEM((tm + HIST, W), F32), pltpu.VMEM((W // LANES, tm, LANES), F32),
                        pltpu.VMEM((W // LANES, tm, LANES), F32), pltpu.VMEM((1, W), F32),
                        pltpu.VMEM((W // LANES, tm // SCAN_GROUP, LANES), F32),
                        pltpu.VMEM((W // LANES, tm // SCAN_GROUP, LANES), F32)],
        compiler_params=_params(("arbitrary",)),
    )(xc, gc, pos, cw, cb, wr, br, wi, bi, lam)


def _rglru_bwd(xc, gc, pos, hs, dout, cw, cb, wr, br, wi, bi, lam, *, name, tm=512):
    T, W = xc.shape
    tm = min(tm, T)
    nt = T // tm
    hb = tm // HIST

    def body(xc_ref, gc_ref, pos_ref, hs_ref, dout_ref, xch_ref, hsh_ref,
             cw_ref, cb_ref, wr_ref, br_ref, wi_ref, bi_ref, lam_ref,
             dxc_ref, dgc_ref, dcw_ref, dcb_ref, dwr_ref, dbr_ref, dwi_ref, dbi_ref, dlam_ref,
             xs, hsx, dys, asx, a_s, d_s, carry_s, grp_a, grp_h):
        i = pl.program_id(0)

        @pl.when(i == 0)
        def _():
            for r in (dcw_ref, dcb_ref, dwr_ref, dbr_ref, dwi_ref, dbi_ref, dlam_ref, carry_s):
                r[...] = jnp.zeros_like(r)
            dys[tm:tm + HIST, :] = jnp.zeros((HIST, W), F32)
            asx[tm:tm + HIST, :] = jnp.zeros((HIST, W), F32)

        has_prev = i < nt - 1
        xs[0:HIST, :] = jnp.where(has_prev, xch_ref[...], 0.0)
        hsx[0:HIST, :] = jnp.where(has_prev, hsh_ref[...], 0.0)
        xs[HIST:HIST + tm, :] = xc_ref[...]
        hs = hs_ref[...]
        hsx[HIST:HIST + tm, :] = hs
        y = _conv_fwd(xs, cw_ref, cb_ref, tm)
        yb, r, ig, sp, reset, a, mult = _rglru_gates(y, pos_ref, wr_ref, br_ref, wi_ref, bi_ref, lam_ref)

        gelu, dgelu = _gelu_parts(gc_ref[...])
        dout = dout_ref[...]
        dgc_ref[...] = dout * hs * dgelu
        asx[0:tm, :] = a
        a_up = asx[1:1 + tm, :]
        asx[tm:tm + HIST, :] = a[0:HIST, :]
        dh, carry_s[...] = _blocked_scan(a_up, dout * gelu, a_s, d_s, grp_a, grp_h, carry_s[...], reverse=True)
        hprev = hsx[HIST - 1:HIST - 1 + tm, :]
        igy = ig * y
        dmult = dh * igy
        digy = dh * mult
        dlog_a = jnp.where(reset, 0.0, dh * hprev * a - dmult * a * a / mult)
        dlam_ref[...] += jnp.sum(dlog_a * (C_EXP * r) * _sigmoid(-lam_ref[...]), axis=0, keepdims=True)
        dz_r = dlog_a * (-C_EXP * sp) * r * (1.0 - r)
        dz_i = digy * y * ig * (1.0 - ig)
        dzr_b = dz_r.astype(BF16)
        dzi_b = dz_i.astype(BF16)
        dy = digy * ig + _dot(dzr_b, wr_ref[...], NT_DIMS) + _dot(dzi_b, wi_ref[...], NT_DIMS)
        dwr_ref[...] += _dot(yb, dzr_b, TN_DIMS)
        dwi_ref[...] += _dot(yb, dzi_b, TN_DIMS)
        dbr_ref[...] += jnp.sum(dz_r, axis=0, keepdims=True)
        dbi_ref[...] += jnp.sum(dz_i, axis=0, keepdims=True)

        dys[0:tm, :] = dy
        dxc = cw_ref[0:1, :] * dy
        for j in range(1, C_CONV):
            dxc = dxc + cw_ref[j:j + 1, :] * dys[j:j + tm, :]
        dxc_ref[...] = dxc
        dys[tm:tm + HIST, :] = dy[0:HIST, :]
        dcb_ref[...] += jnp.sum(dy, axis=0, keepdims=True)
        for j in range(C_CONV):
            dcw_ref[j:j + 1, :] += jnp.sum(dy * xs[HIST - j:HIST - j + tm, :], axis=0, keepdims=True)

    row = pl.BlockSpec((tm, W), lambda i: (nt - 1 - i, 0))
    halo = pl.BlockSpec((HIST, W), lambda i: (jnp.maximum((nt - 1 - i) * hb - 1, 0), 0))
    full = lambda shape: pl.BlockSpec(shape, lambda i: (0,) * len(shape))
    out_specs = [row, row, full((C_CONV, W)), full((1, W)), full((W, W)), full((1, W)), full((W, W)), full((1, W)),
                 full((1, W))]
    out_shape = [jax.ShapeDtypeStruct((T, W), F32)] * 2
    out_shape += [jax.ShapeDtypeStruct(s, F32) for s in ((C_CONV, W), (1, W), (W, W), (1, W), (W, W), (1, W), (1, W))]
    return pl.pallas_call(
        body, name=name, grid=(nt,),
        in_specs=[row, row, pl.BlockSpec((tm, 1), lambda i: (nt - 1 - i, 0)), row, row, halo, halo,
                  full((C_CONV, W)), full((1, W)), full((W, W)), full((1, W)), full((W, W)), full((1, W)),
                  full((1, W))],
        out_specs=out_specs, out_shape=out_shape,
        scratch_shapes=[pltpu.VMEM((tm + HIST, W), F32), pltpu.VMEM((tm + HIST, W), F32),
                        pltpu.VMEM((tm + HIST, W), F32), pltpu.VMEM((tm + HIST, W), F32),
                        pltpu.VMEM((W // LANES, tm, LANES), F32), pltpu.VMEM((W // LANES, tm, LANES), F32),
                        pltpu.VMEM((1, W), F32), pltpu.VMEM((W // LANES, tm // SCAN_GROUP, LANES), F32),
                        pltpu.VMEM((W // LANES, tm // SCAN_GROUP, LANES), F32)],
        compiler_params=_params(("arbitrary",)),
    )(xc, gc, pos, hs, dout, xc, hs, cw, cb, wr, br, wi, bi, lam)


def _adam_math(w, g, m, v):
    m = ADAM_B1 * m + (1.0 - ADAM_B1) * g
    v = ADAM_B2 * v + (1.0 - ADAM_B2) * (g * g)
    m_hat = m / (1.0 - ADAM_B1 ** ADAM_STEP)
    v_hat = v / (1.0 - ADAM_B2 ** ADAM_STEP)
    delta = -ADAM_LR * (m_hat / (jnp.sqrt(v_hat) + ADAM_EPS) + ADAM_WD * w)
    return delta, m, v


def _pick_rows(R, cap=512, mult=16):
    for d in range(min(cap, R), 0, -1):
        if R % d == 0 and d % mult == 0:
            return d
    return R


def _adamw(parts, w, m, v, *, name, tr=None, part=0, prev=None):
    R, C = w.shape
    r = parts.shape[1]
    tr = _pick_rows(r) if tr is None else tr
    assert r % tr == 0 and R % r == 0, (name, R, r, tr)
    nt = r // tr

    def body(p_ref, w_ref, m_ref, v_ref, *rest):
        g_ref, d_ref, nm_ref, nv_ref = rest[-4:]
        g = p_ref[0].astype(F32)
        for d in range(1, N_DEV):
            g = g + p_ref[d].astype(F32)
        delta, nm, nv = _adam_math(w_ref[...], g, m_ref[...], v_ref[...])
        g_ref[...] = g
        d_ref[...] = delta
        nm_ref[...] = nm
        nv_ref[...] = nv

    row = pl.BlockSpec((tr, C), lambda i: (part * nt + i, 0))
    in_specs = [pl.BlockSpec((N_DEV, tr, C), lambda i: (0, i, 0)), row, row, row]
    operands = [parts, w, m, v]
    aliases = {}
    if prev is not None:
        in_specs += [pl.BlockSpec(memory_space=pl.ANY)] * 4
        operands += list(prev)
        aliases = {4 + i: i for i in range(4)}
    return pl.pallas_call(
        body, name=name, grid=(nt,), in_specs=in_specs,
        out_specs=[row] * 4, out_shape=[jax.ShapeDtypeStruct((R, C), F32)] * 4,
        input_output_aliases=aliases,
        compiler_params=_params(("parallel",)),
    )(*operands)


def _exchange(srcs, gather, *, name):
    n = len(srcs)
    out_shape = [jax.ShapeDtypeStruct((N_DEV,) + s.shape if gather else s.shape, s.dtype) for s in srcs]

    def body(*refs):
        ins, outs = refs[:n], refs[n:2 * n]
        send_sems, recv_sems, local_sems = refs[2 * n:]
        x, y, c = lax.axis_index("x"), lax.axis_index("y"), lax.axis_index("c")
        me = 4 * x + 2 * y + c
        local_copies, sends, arrivals = [], [], []
        for a in range(n):
            mine = ins[a] if gather else ins[a].at[me]
            local = pltpu.make_async_copy(mine, outs[a].at[me], local_sems.at[a])
            local.start()
            local_copies.append(local)
            for k in range(1, N_DEV):
                px, py, pc = x ^ ((k >> 2) & 1), y ^ ((k >> 1) & 1), c ^ (k & 1)
                peer = 4 * px + 2 * py + pc
                send = pltpu.make_async_remote_copy(
                    src_ref=ins[a] if gather else ins[a].at[peer], dst_ref=outs[a].at[me],
                    send_sem=send_sems.at[a * N_DEV + k], recv_sem=recv_sems.at[a * N_DEV + k],
                    device_id=(px, py, pc), device_id_type=pl.DeviceIdType.MESH)
                send.start()
                sends.append(send)
                arrivals.append(pltpu.make_async_remote_copy(
                    src_ref=mine, dst_ref=outs[a].at[peer],
                    send_sem=send_sems.at[a * N_DEV + k], recv_sem=recv_sems.at[a * N_DEV + k],
                    device_id=(px, py, pc), device_id_type=pl.DeviceIdType.MESH))
        for cp in sends:
            cp.wait_send()
        for cp in arrivals:
            cp.wait_recv()
        for cp in local_copies:
            cp.wait()

    return pl.pallas_call(
        body, name=name,
        in_specs=[pl.BlockSpec(memory_space=pl.ANY)] * n, out_specs=[pl.BlockSpec(memory_space=pl.ANY)] * n,
        out_shape=out_shape,
        scratch_shapes=[pltpu.SemaphoreType.DMA((n * N_DEV,)), pltpu.SemaphoreType.DMA((n * N_DEV,)),
                        pltpu.SemaphoreType.DMA((n,))],
    )(*srcs)


_HBM = pl.BlockSpec(memory_space=pltpu.HBM)
_SEM = pl.BlockSpec(memory_space=pltpu.SEMAPHORE)
_EFFECT = pltpu.SideEffectType.DATAFLOW_SIDE_EFFECTING


def _peers():
    x, y, c = lax.axis_index("x"), lax.axis_index("y"), lax.axis_index("c")
    out = []
    for k in range(1, N_DEV):
        px, py, pc = x ^ ((k >> 2) & 1), y ^ ((k >> 1) & 1), c ^ (k & 1)
        out.append((k, (px, py, pc), 4 * px + 2 * py + pc))
    return 4 * x + 2 * y + c, out


def _split_copies(src_refs, land_refs, send_sems, recv_sems, gather):
    me, peers = _peers()
    out = []
    for a, (src_ref, land_ref) in enumerate(zip(src_refs, land_refs)):
        for k, dev, blk in peers:
            common = dict(send_sem=send_sems.at[a * N_DEV + k], recv_sem=recv_sems.at[a * N_DEV + k], device_id=dev,
                          device_id_type=pl.DeviceIdType.MESH)
            src = src_ref if gather else src_ref.at[blk]
            out.append((pltpu.make_async_remote_copy(src_ref=src, dst_ref=land_ref.at[me], **common),
                        pltpu.make_async_remote_copy(src_ref=src, dst_ref=land_ref.at[blk], **common)))
    return out


def _exchange_start(srcs, gather, *, name, after=None):
    n = len(srcs)
    lands = [lax.empty((N_DEV,) + (s.shape if gather else s.shape[1:]), s.dtype) for s in srcs]

    def body(*refs):
        src_refs, land_refs = refs[:n], refs[n:2 * n]
        send_sems, recv_sems = refs[-2 * n - 3:-2 * n - 1]
        token = refs[-1]
        for outgoing, _ in _split_copies(src_refs, land_refs, send_sems, recv_sems, gather):
            outgoing.start()
        token[...] = jnp.zeros_like(token)

    res = pl.pallas_call(
        body, name=name,
        out_shape=(pltpu.SemaphoreType.DMA((n * N_DEV,)), pltpu.SemaphoreType.DMA((n * N_DEV,)),
                   *[pltpu.HBM(a.shape, a.dtype) for a in srcs + lands], jax.ShapeDtypeStruct((8, LANES), F32)),
        in_specs=(_HBM,) * (2 * n) + ((pl.BlockSpec(memory_space=pl.ANY),) if after is not None else ()),
        out_specs=(_SEM, _SEM) + (_HBM,) * (2 * n) + (pl.BlockSpec(memory_space=pltpu.VMEM),),
        input_output_aliases={i: i + 2 for i in range(2 * n)},
        compiler_params=pltpu.CompilerParams(has_side_effects=_EFFECT),
    )(*[pltpu.with_memory_space_constraint(a, pltpu.HBM) for a in srcs + lands],
      *([after] if after is not None else []))
    return res[0], res[1], list(res[2:2 + n]), list(res[2 + n:2 + 2 * n]), res[-1]


def _exchange_wait(started, after, gather, *, name):
    send_sems, recv_sems, srcs, lands, _ = started
    n = len(srcs)

    def body(*refs):
        src_refs, land_refs = refs[:n], refs[n:2 * n]
        send_sems, recv_sems = refs[2 * n:2 * n + 2]
        for outgoing, incoming in _split_copies(src_refs, land_refs, send_sems, recv_sems, gather):
            outgoing.wait_send()
            incoming.wait_recv()

    res = pl.pallas_call(
        body, name=name,
        out_shape=tuple(pltpu.HBM(a.shape, a.dtype) for a in srcs + lands),
        in_specs=(_HBM,) * (2 * n) + (_SEM, _SEM, pl.BlockSpec(memory_space=pl.ANY)), out_specs=(_HBM,) * (2 * n),
        input_output_aliases={i: i for i in range(2 * n)},
        compiler_params=pltpu.CompilerParams(has_side_effects=_EFFECT),
    )(*srcs, *lands, send_sems, recv_sems, after)
    return list(res[:n]), list(res[n:])


def _cols_to_blocks(g, *, name, tr=128):
    R, C = g.shape
    w = C // N_DEV
    tr = min(tr, R)

    def body(g_ref, o_ref):
        for p in range(N_DEV):
            o_ref[p] = g_ref[:, p * w:(p + 1) * w].astype(BF16)

    return pl.pallas_call(
        body, name=name, grid=(R // tr,),
        in_specs=[pl.BlockSpec((tr, C), lambda i: (i, 0))],
        out_specs=pl.BlockSpec((N_DEV, tr, w), lambda i: (0, i, 0)),
        out_shape=jax.ShapeDtypeStruct((N_DEV, R, w), BF16),
        compiler_params=_params(("parallel",)),
    )(g)


def _blocks_to_cols(b, *, name, tr=128):
    _, R, w = b.shape
    tr = min(tr, R)

    def body(b_ref, o_ref):
        o_ref[...] = jnp.concatenate([b_ref[p].astype(F32) for p in range(N_DEV)], axis=1).astype(o_ref.dtype)

    return pl.pallas_call(
        body, name=name, grid=(R // tr,),
        in_specs=[pl.BlockSpec((N_DEV, tr, w), lambda i: (0, i, 0))],
        out_specs=pl.BlockSpec((tr, N_DEV * w), lambda i: (i, 0)),
        out_shape=jax.ShapeDtypeStruct((R, N_DEV * w), b.dtype),
        compiler_params=_params(("parallel",)),
    )(b)


def _pack_rows(arrays, *, name, pick=None):
    B, _, w = arrays[0].shape
    rows = [a.shape[1] for a in arrays]
    first = 0
    if pick is not None:
        B, first = 1, pick

    def body(*refs):
        o_ref = refs[-1]
        r = 0
        for a_ref, n in zip(refs[:-1], rows):
            o_ref[0, r:r + n, :] = a_ref[0].astype(BF16)
            r += n

    return pl.pallas_call(
        body, name=name, grid=(B,),
        in_specs=[pl.BlockSpec((1, n, w), lambda b: (first + b, 0, 0)) for n in rows],
        out_specs=pl.BlockSpec((1, sum(rows), w), lambda b: (b, 0, 0)),
        out_shape=jax.ShapeDtypeStruct((B, sum(rows), w), BF16),
        compiler_params=_params(("parallel",)),
    )(*arrays)


def _unpack_rows(land, src, rows, *, name):
    _, R, w = land.shape
    src_spec = (pl.BlockSpec((1, R, w), lambda p: (p, 0, 0)) if src.ndim == 3
                else pl.BlockSpec((R, w), lambda p: (0, 0)))

    def body(land_ref, src_ref, *o_refs):
        me = 4 * lax.axis_index("x") + 2 * lax.axis_index("y") + lax.axis_index("c")
        mine = pl.program_id(0) == me
        r = 0
        for o_ref, n in zip(o_refs, rows):
            rows_i = slice(r, r + n)

            @pl.when(mine)
            def _(o_ref=o_ref, rows_i=rows_i):
                o_ref[0] = src_ref[0, rows_i, :] if src.ndim == 3 else src_ref[rows_i, :]

            @pl.when(jnp.logical_not(mine))
            def _(o_ref=o_ref, rows_i=rows_i):
                o_ref[0] = land_ref[0, rows_i, :]

            r += n

    return pl.pallas_call(
        body, name=name, grid=(N_DEV,),
        in_specs=[pl.BlockSpec((1, R, w), lambda p: (p, 0, 0)), src_spec],
        out_specs=[pl.BlockSpec((1, n, w), lambda p: (p, 0, 0)) for n in rows],
        out_shape=[jax.ShapeDtypeStruct((N_DEV, n, w), land.dtype) for n in rows],
        compiler_params=_params(("parallel",)),
    )(land, src)


def _to_blocks(w, axis):
    shape = w.shape
    k = shape[axis] // N_DEV
    w = w.reshape(shape[:axis] + (N_DEV, k) + shape[axis + 1:])
    return jnp.moveaxis(w, axis, 0)


def _from_blocks(wb, axis):
    w = jnp.moveaxis(wb, 0, axis)
    shape = w.shape
    return w.reshape(shape[:axis] + (shape[axis] * shape[axis + 1],) + shape[axis + 2:])


def _block_diag(w):
    n, k, _ = w.shape
    eye = jnp.eye(n, dtype=w.dtype)
    return (eye[:, None, :, None] * w[:, :, None, :]).reshape(n * k, n * k)


def _diag_blocks(wd):
    k = HEAD_DIM
    return jnp.stack([wd[h * k:(h + 1) * k, h * k:(h + 1) * k] for h in range(C_BLOCKS)])


def _pack(arrays, row_multiple=8):
    rows = []
    for a in arrays:
        flat = a.reshape(-1).astype(F32)
        pad = (-flat.shape[0]) % LANES
        rows.append(jnp.pad(flat, (0, pad)).reshape(-1, LANES))
    out = jnp.concatenate(rows, axis=0)
    return jnp.pad(out, ((0, (-out.shape[0]) % row_multiple), (0, 0)))


def _unpack(packed, shapes):
    outs, r = [], 0
    for s in shapes:
        size = math.prod(s)
        nrows = -(-size // LANES)
        outs.append(packed[r:r + nrows].reshape(-1)[:size].reshape(s))
        r += nrows
    return outs


def _rope_tables(positions):
    inv = 1.0 / (ROPE_THETA ** (jnp.arange(0, HEAD_DIM, 2, dtype=F32) / HEAD_DIM))
    ang = positions.astype(F32)[:, None] * inv
    cos, sin = jnp.cos(ang), jnp.sin(ang)
    return jnp.tile(cos, (1, 4)), jnp.tile(jnp.concatenate([-sin, sin], axis=1), (1, 2))


def _layer_fwd(l, x, pos, cos_t, sin_t, W, before_mixer=None):
    tag = f"l{l}"
    saved = {'x0': x}
    x1, a1, u1 = _ffn_fwd(x, W['norm_ffn1'][l], W['ffn1_gate'][l], W['ffn1_up'][l], W['ffn1_down'][l],
                          name=f"ffn1_fwd_{tag}")
    if before_mixer is not None:
        before_mixer(l, x1)
    h = _rms_fwd(x1, W['norm_mix'][l], name=f"mixnorm_fwd_{tag}")
    proj = _mm(h, W['w_in'][l], 'nn', name=f"proj_{tag}", tm=512, tn=IN_COLS, tk=h.shape[1])
    qa, ka2, va2, qb, kb, vb, xc, gc = _split_rope(proj, cos_t, sin_t, name=f"split_{tag}")
    oa, lsea = _attn_fwd(qa, ka2, va2, A_MAX_DIST, name=f"attn_a_fwd_{tag}")
    obs, lsebs = [], []
    for bi, (window, d) in enumerate(B_BRANCHES):
        o, lse = _attn_fwd(qb[bi], kb[bi], vb[bi], window // d, name=f"attn_b{bi}_fwd_{tag}")
        obs.append(o)
        lsebs.append(lse)
    oc, hs = _rglru_fwd(xc, gc, pos, W['conv_w'][l], W['conv_b'][l], W['rg_w_r'][l], W['rg_b_r'][l],
                        W['rg_w_i'][l], W['rg_b_i'][l], W['rg_lambda'][l], name=f"rglru_fwd_{tag}")
    mix = _combine_fwd(oa, lsea, W['sinks'][l], obs, lsebs, oc, name=f"combine_fwd_{tag}")
    x2 = _mm(mix, W['w_out'][l], 'nn', name=f"outproj_{tag}", tm=512, tn=x.shape[1], tk=MIX_WIDTH, res=x1)
    x3, a2, u2 = _ffn_fwd(x2, W['norm_ffn2'][l], W['ffn2_gate'][l], W['ffn2_up'][l], W['ffn2_down'][l],
                          name=f"ffn2_fwd_{tag}")
    saved.update(a1=a1, u1=u1, x1=x1, h=h, qa=qa, ka2=ka2, va2=va2, qb=qb, kb=kb, vb=vb, xc=xc, gc=gc, oa=oa,
                 lsea=lsea, obs=obs, lsebs=lsebs, hs=hs, mix=mix, x2=x2, a2=a2, u2=u2)
    return x3, saved


def _ffn_grads(tag, which, x, g, dy, a, u, wg, wu, wd):
    T, D = x.shape
    F = wg.shape[1]
    dx, dg, n, act, da, du = _ffn_bwd(x, g, dy, a, u, wg, wu, wd, name=f"{which}_bwd_{tag}")
    fc = _ffn_chunk(F)
    d_gate = _mm(n, da, 'tn', name=f"{which}_dgate_{tag}", tm=512, tn=fc, tk=4096)
    d_up = _mm(n, du, 'tn', name=f"{which}_dup_{tag}", tm=512, tn=fc, tk=4096)
    d_down = _mm(act, dy, 'tn', name=f"{which}_ddown_{tag}", tm=fc, tn=D, tk=1024, alpha=0.5)
    return dx, dg, d_gate, d_up, d_down


def _layer_bwd(l, dx3, pos, cos_t, sin_t, W, S, on_grads=None):
    tag = f"l{l}"
    G = {}
    dx2, G['norm_ffn2'], G['ffn2_gate'], G['ffn2_up'], G['ffn2_down'] = _ffn_grads(
        tag, 'ffn2', S['x2'], W['norm_ffn2'][l], dx3, S['a2'], S['u2'], W['ffn2_gate'][l], W['ffn2_up'][l],
        W['ffn2_down'][l])
    D = dx2.shape[1]
    dmix = _mm(dx2, W['w_out'][l], 'nt', name=f"outproj_dx_{tag}", tm=512, tn=MIX_WIDTH, tk=D)
    G['w_out'] = _mm(S['mix'], dx2, 'tn', name=f"outproj_dw_{tag}", tm=MIX_WIDTH, tn=D, tk=2048)
    doa, dla, dobs, dlbs, doc, dsink = _combine_bwd(dmix, S['oa'], S['lsea'], W['sinks'][l], S['obs'], S['lsebs'],
                                                    name=f"combine_bwd_{tag}")
    G['attn_sinks'] = dsink.reshape(A_WIDTH // HEAD_DIM, HEAD_DIM)[:, 0]
    dqa, dka2, dva2 = _attn_bwd(S['qa'], S['ka2'], S['va2'], doa, S['lsea'], dla, A_MAX_DIST,
                                name=f"attn_a_bwd_{tag}")
    dqb, dkb, dvb = [], [], []
    for bi, (window, d) in enumerate(B_BRANCHES):
        dq, dk, dv = _attn_bwd(S['qb'][bi], S['kb'][bi], S['vb'][bi], dobs[bi], S['lsebs'][bi], dlbs[bi], window // d,
                               name=f"attn_b{bi}_bwd_{tag}")
        dqb.append(dq)
        dkb.append(dk)
        dvb.append(dv)
    (dxc, dgc, G['conv_w'], G['conv_b'], dwr, G['rg_b_r'], dwi, G['rg_b_i'], G['rg_lambda']) = _rglru_bwd(
        S['xc'], S['gc'], pos, S['hs'], doc, W['conv_w'][l], W['conv_b'][l], W['rg_w_r'][l], W['rg_b_r'][l],
        W['rg_w_i'][l], W['rg_b_i'][l], W['rg_lambda'][l], name=f"rglru_bwd_{tag}")
    G['rg_w_r'] = _diag_blocks(dwr)
    G['rg_w_i'] = _diag_blocks(dwi)
    dproj = _merge_dproj(dqa, dka2, dva2, dqb, dkb, dvb, dxc, dgc, cos_t, sin_t, name=f"merge_{tag}")
    dh = _mm(dproj, W['w_in'][l], 'nt', name=f"proj_dx_{tag}", tm=512, tn=D, tk=IN_COLS)
    G['w_in'] = _mm(S['h'], dproj, 'tn', name=f"proj_dw_{tag}", tm=512, tn=IN_COLS, tk=2048)
    g_mix = W['norm_mix'][l]
    if on_grads is not None:
        g_mix = g_mix + on_grads(l, 0, G)
    dx1, G['norm_mix'] = _rms_bwd(S['x1'], g_mix, dh, dx2, name=f"mixnorm_bwd_{tag}")
    dx0, G['norm_ffn1'], G['ffn1_gate'], G['ffn1_up'], G['ffn1_down'] = _ffn_grads(
        tag, 'ffn1', S['x0'], W['norm_ffn1'][l], dx1, S['a1'], S['u1'], W['ffn1_gate'][l], W['ffn1_up'][l],
        W['ffn1_down'][l])
    if on_grads is not None:
        on_grads(l, 1, G)
    return dx0, G


def _device_step(x, positions, loss_target, W, before_layer=None, on_grads=None, before_mixer=None):
    T = x.shape[0]
    pos = positions.reshape(T, 1)
    cos_t, sin_t = _rope_tables(positions)
    saved = []
    for l in range(DEPTH):
        if before_layer is not None:
            before_layer(l, x)
        x, S = _layer_fwd(l, x, pos, cos_t, sin_t, W, before_mixer)
        saved.append(S)
    loss, dx, dg_final = _loss_head(x, W['norm_final'], loss_target, name="loss_head")
    grads = [None] * DEPTH
    for l in reversed(range(DEPTH)):
        dx, grads[l] = _layer_bwd(l, dx, pos, cos_t, sin_t, W, saved[l], on_grads)
    return loss, dx, grads, dg_final


SHARD_AXIS = {'ffn1_gate': 2, 'ffn1_up': 2, 'ffn1_down': 1, 'w_in': 2, 'w_out': 1, 'ffn2_gate': 2, 'ffn2_up': 2,
              'ffn2_down': 1, 'conv_w': 2}


def kernel(x, positions, norm_ffn1, ffn1_gate, ffn1_up, ffn1_down, norm_mix, w_in, attn_sinks, conv_w, conv_b, rg_w_r, rg_b_r, rg_w_i, rg_b_i, rg_lambda, w_out, norm_ffn2, ffn2_gate, ffn2_up, ffn2_down, norm_final, loss_target, m_norm_ffn1, m_ffn1_gate, m_ffn1_up, m_ffn1_down, m_norm_mix, m_w_in, m_attn_sinks, m_conv_w, m_conv_b, m_rg_w_r, m_rg_b_r, m_rg_w_i, m_rg_b_i, m_rg_lambda, m_w_out, m_norm_ffn2, m_ffn2_gate, m_ffn2_up, m_ffn2_down, m_norm_final, v_norm_ffn1, v_ffn1_gate, v_ffn1_up, v_ffn1_down, v_norm_mix, v_w_in, v_attn_sinks, v_conv_w, v_conv_b, v_rg_w_r, v_rg_b_r, v_rg_w_i, v_rg_b_i, v_rg_lambda, v_w_out, v_norm_ffn2, v_ffn2_gate, v_ffn2_up, v_ffn2_down, v_norm_final):
    given = dict(locals())
    me = 4 * lax.axis_index("x") + 2 * lax.axis_index("y") + lax.axis_index("c")

    def by_width(names):
        classes = {}
        for n in names:
            classes.setdefault(given[n].shape[2], []).append(n)
        return list(classes.values())

    def pack(names, get, tag, pick=None):
        return [_pack_rows([get(n) for n in cls], name=f"pack{ci}_{tag}", pick=pick)
                for ci, cls in enumerate(by_width(names))]

    def unpack(names, lands, srcs, tag):
        out = {}
        for ci, cls in enumerate(by_width(names)):
            arrays = _unpack_rows(lands[ci], srcs[ci], [given[n].shape[1] for n in cls], name=f"unpack{ci}_{tag}")
            out.update(zip(cls, arrays))
        return out

    def gather_start(l, names, tag, after=None):
        return _exchange_start([p[0] for p in pack(names, lambda n: given[n], f"w{tag}_l{l}", pick=l)], True,
                               name=f"gather_start{tag}_l{l}", after=after)

    def gather_wait(l, names, tag, started, after):
        srcs, lands = _exchange_wait(started, after, True, name=f"gather_wait{tag}_l{l}")
        blocks = unpack(names, lands, srcs, f"w{tag}_l{l}")
        for n in names:
            if SHARD_AXIS[n] == 2:
                W[n][l] = _blocks_to_cols(blocks[n], name=f"cols_{n}_l{l}")
            else:
                W[n][l] = blocks[n].reshape(-1, blocks[n].shape[2])
        return lands[0]

    W = {n: [None] * DEPTH for n in BIG_NAMES}
    ffn1_names, later_names = SCATTER_STAGES[1], SCATTER_STAGES[0]
    first = gather_start(0, ffn1_names, "a")
    conv_full = _exchange([conv_w], True, name="gather_conv_w")[0]
    landed = gather_wait(0, ffn1_names, "a", first, conv_full)
    second = gather_start(0, later_names, "b", after=landed)
    started = second[4][0, 0]
    gathers = [None] * DEPTH

    def before_layer(l, x_in):
        if l > 0:
            gather_wait(l, BIG_NAMES, "", gathers[l], x_in)

    def before_mixer(l, x1):
        if l == 0:
            landed = gather_wait(0, later_names, "b", second, x1)
            token = 0.0
            for k in range(1, DEPTH):
                gathers[k] = gather_start(k, BIG_NAMES, "", after=landed)
                token = token + gathers[k][4][0, 0]
            W['norm_mix'][0] = W['norm_mix'][0] + token

    scatters = {}

    def on_grads(l, stage, G):
        def blocks_of(n):
            if SHARD_AXIS[n] == 2:
                return _cols_to_blocks(G[n], name=f"blocks_{n}_l{l}")
            return G[n].reshape(N_DEV, -1, G[n].shape[1])

        scatters[l, stage] = _exchange_start(pack(SCATTER_STAGES[stage], blocks_of, f"g{stage}_l{l}"), False,
                                             name=f"scatter_start{stage}_l{l}")
        token = scatters[l, stage][4][0, 0]
        if stage == 1 and l > 0:
            W['norm_ffn2'][l - 1] = W['norm_ffn2'][l - 1] + token
        return token

    W['conv_w'] = [_from_blocks(conv_full[:, l], 1) for l in range(DEPTH)]
    for n in ('norm_ffn1', 'norm_mix', 'norm_ffn2', 'conv_b', 'rg_lambda'):
        W[n] = [given[n][l][None, :] for l in range(DEPTH)]
    W['norm_final'] = norm_final[None, :]
    W['sinks'] = [jnp.repeat(attn_sinks[l], HEAD_DIM)[None, :] for l in range(DEPTH)]
    for n in ('rg_w_r', 'rg_w_i'):
        W[n] = [_block_diag(given[n][l]).astype(BF16) for l in range(DEPTH)]
    for n in ('rg_b_r', 'rg_b_i'):
        W[n] = [given[n][l].reshape(1, C_WIDTH) for l in range(DEPTH)]

    W['norm_ffn1'][0] = W['norm_ffn1'][0] + started

    loss_part, grad_x, grads, dg_final = _device_step(x[0], positions[0], loss_target[0], W, before_layer, on_grads,
                                                      before_mixer)
    loss = lax.psum(loss_part[0, 0], ("x", "y", "c"))

    small_shapes = [given[n].shape for n in SMALL_NAMES] + [(DEPTH, C_CONV, C_WIDTH)]
    small_grads = []
    for n in SMALL_NAMES:
        if n == 'norm_final':
            small_grads.append(dg_final.reshape(-1))
        else:
            small_grads.append(jnp.stack([grads[l][n].reshape(given[n].shape[1:]) for l in range(DEPTH)]))
    small_grads.append(jnp.stack([grads[l]['conv_w'] for l in range(DEPTH)]))
    small_parts = _exchange([_pack(small_grads, SMALL_TILE)], True, name="gather_small_grads")[0]

    out = {}
    for stage in (0, 1):
        parts = {}
        for l in reversed(range(DEPTH)):
            last = stage == 1 and l == 0
            srcs, lands = _exchange_wait(scatters[l, stage], out['w_in'][1] if last else grad_x, False,
                                         name=f"scatter_wait{stage}_l{l}")
            parts[l] = unpack(SCATTER_STAGES[stage], lands, srcs, f"g{stage}_l{l}")
        for n in SCATTER_STAGES[stage]:
            shape = given[n].shape
            two_d = (shape[0] * shape[1], shape[2])
            res = None
            for l in reversed(range(DEPTH)):
                res = _adamw(parts[l][n], given[n].reshape(two_d), given['m_' + n].reshape(two_d),
                             given['v_' + n].reshape(two_d), name=f"adamw_{n}_l{l}", part=l, prev=res)
            out[n] = [r.reshape(shape) for r in res]

    w_small = [given[n] for n in SMALL_NAMES]
    m_small = [given['m_' + n] for n in SMALL_NAMES]
    v_small = [given['v_' + n] for n in SMALL_NAMES]
    zeros_cw = jnp.zeros((DEPTH, C_CONV, C_WIDTH), F32)
    res = _adamw(small_parts, _pack(w_small + [zeros_cw], SMALL_TILE), _pack(m_small + [zeros_cw], SMALL_TILE),
                 _pack(v_small + [zeros_cw], SMALL_TILE), name="adamw_small", tr=SMALL_TILE)
    unpacked = [_unpack(r, small_shapes) for r in res]
    for i, n in enumerate(SMALL_NAMES):
        out[n] = [u[i] for u in unpacked]

    k = conv_w.shape[2]
    g_cw = lax.dynamic_slice_in_dim(unpacked[0][-1], me * k, k, axis=2)
    zero_parts = jnp.zeros((N_DEV - 1,) + (8, LANES), F32)
    res = _adamw(jnp.concatenate([_pack([g_cw])[None], zero_parts]), _pack([conv_w]), _pack([m_conv_w]),
                 _pack([v_conv_w]), name="adamw_conv_w", tr=8)
    out['conv_w'] = [_unpack(r, [conv_w.shape])[0] for r in res]

    outputs = [loss, grad_x[None]]
    for i in range(4):
        outputs += [out[n][i] for n in WEIGHT_NAMES]
    return tuple(outputs)
```
